```python
import jax, jax.numpy as jnp
from jax import lax
import numpy as np

D_MODEL = 1024
BATCH = 16
SEQ = 256
DEPTH = 2
DEC_BATCH = 4
DEC_SEQ = 1024
PAST_LEN = 512

GRID_W = 64
MIX_W = D_MODEL
NORM_EPS = 1e-6
GLA_HEADS = 4
GLA_DK = 64
GLA_DV = 64
GLA_W = GLA_HEADS * GLA_DV
GLA_GATE_RANK = 16
GLA_TAU = 16.0
GLA_CHUNK = 64
CONV_W = 256
CONV_K = 31
MLA_HEADS = 8
MLA_NOPE = 64
MLA_ROPE = 32
MLA_V = 64
MLA_Q_RANK = 256
MLA_KV_RANK = 128
MLA_W = MLA_HEADS * MLA_V
Q_BLOCK = 128
ROPE_THETA = 10000.0
N_GROUPS = 4
EXPERTS_PER_GROUP = 8
N_EXPERTS = N_GROUPS * EXPERTS_PER_GROUP
TOP_K = 2
EXPERT_FF = 256
IN_SPLIT_SIZES = (GLA_HEADS * GLA_DK, GLA_HEADS * GLA_DK, GLA_W, GLA_W, GLA_GATE_RANK, GLA_GATE_RANK,
                  2 * CONV_W, MLA_Q_RANK, MLA_KV_RANK, MLA_ROPE)
IN_COLS = 1984

kernel_name = "hybrid_gla_conv_mla_hmoe_dit_step"


def rms_norm(x, w):
    xf = x.astype(jnp.float32)
    y = xf * lax.rsqrt(jnp.mean(xf * xf, axis=-1, keepdims=True) + NORM_EPS)
    return (y * w.astype(jnp.float32)).astype(x.dtype)


def layer_norm(x, w, b):
    xf = x.astype(jnp.float32)
    mu = jnp.mean(xf, axis=-1, keepdims=True)
    var = jnp.mean(jnp.square(xf - mu), axis=-1, keepdims=True)
    y = (xf - mu) * lax.rsqrt(var + NORM_EPS)
    return (y * w.astype(jnp.float32) + b.astype(jnp.float32)).astype(x.dtype)


def adaln_params(cond, ada_w, ada_b):
    m = jax.nn.silu(cond) @ ada_w + ada_b
    return jnp.split(m[:, None, :], 6, axis=-1)


def modulate(x, norm_w, shift, scale):
    return rms_norm(x, norm_w) * (1 + scale) + shift


def axial_angles(n_tok):
    rows = n_tok // GRID_W
    row = jnp.repeat(jnp.arange(rows, dtype=jnp.float32), GRID_W)
    col = jnp.tile(jnp.arange(GRID_W, dtype=jnp.float32), rows)
    n_freq = MLA_ROPE // 4
    inv_freq = ROPE_THETA ** (-jnp.arange(n_freq, dtype=jnp.float32) / n_freq)
    return jnp.stack([row[:, None] * inv_freq, col[:, None] * inv_freq], axis=1)


def apply_axial_rope(x, ang):
    n_freq = MLA_ROPE // 4
    xs = x.reshape(x.shape[:-1] + (2, 2, n_freq))
    x1, x2 = xs[..., 0, :], xs[..., 1, :]
    cos = jnp.cos(ang).astype(x.dtype)
    sin = jnp.sin(ang).astype(x.dtype)
    return jnp.stack([x1 * cos - x2 * sin, x1 * sin + x2 * cos], axis=-2).reshape(x.shape)


def split_projection(h, w_in):
    offs = np.cumsum(IN_SPLIT_SIZES)[:-1].tolist()
    return jnp.split(h @ w_in, offs, axis=-1)


def gla_chunk_scan(q, k, v, log_a, s0):
    bsz, nh, n, _ = q.shape
    nc = n // GLA_CHUNK

    def chunks(a):
        return jnp.moveaxis(a.reshape(bsz, nh, nc, GLA_CHUNK, a.shape[-1]), 2, 0)

    causal = jnp.tril(jnp.ones((GLA_CHUNK, GLA_CHUNK), dtype=bool))[:, :, None]

    def step(s, inp):
        qc, kc, vc, gc = inp
        b = jnp.cumsum(gc, axis=-2)
        inter = jnp.einsum('bhtd,bhde->bhte', qc * jnp.exp(b), s)
        diff = b[:, :, :, None, :] - b[:, :, None, :, :]
        decay = jnp.exp(jnp.where(causal, diff, -jnp.inf))
        att = jnp.einsum('bhtd,bhsd,bhtsd->bhts', qc, kc, decay)
        o = inter + jnp.einsum('bhts,bhse->bhte', att, vc)
        b_last = b[:, :, -1:, :]
        s_new = jnp.exp(b_last[:, :, 0, :, None]) * s + jnp.einsum('bhsd,bhse->bhde', kc * jnp.exp(b_last - b), vc)
        return s_new, o

    s_fin, o = lax.scan(step, s0, (chunks(q), chunks(k), chunks(v), chunks(log_a)))
    return jnp.moveaxis(o, 0, 2).reshape(bsz, nh, n, v.shape[-1]), s_fin


def gla_mixer(q, k, v, g, gf, gb, p, s0f, s0b):
    bsz, n, _ = q.shape

    def heads(a):
        return a.reshape(bsz, n, GLA_HEADS, -1).transpose(0, 2, 1, 3).astype(jnp.float32)

    qh = heads(q) * (GLA_DK ** -0.5)
    kh = heads(k)
    vh = heads(v)
    la_f = heads(jax.nn.log_sigmoid((gf @ p['gla_wg_f'] + p['gla_bg_f']).astype(jnp.float32))) / GLA_TAU
    la_b = heads(jax.nn.log_sigmoid((gb @ p['gla_wg_b'] + p['gla_bg_b']).astype(jnp.float32))) / GLA_TAU

    def flip(a):
        return a[:, :, ::-1]

    o_f, s_f = gla_chunk_scan(qh, kh, vh, la_f, s0f.astype(jnp.float32))
    o_b, s_b = gla_chunk_scan(flip(qh), flip(kh), flip(vh), flip(la_b), s0b.astype(jnp.float32))
    o = (o_f + flip(o_b)).transpose(0, 2, 1, 3)
    o = rms_norm(o, p['gla_norm_w'].reshape(GLA_HEADS, GLA_DV)).reshape(bsz, n, GLA_W).astype(q.dtype)
    return o * jax.nn.silu(g), s_f.astype(q.dtype), s_b.astype(q.dtype)


def conformer_conv(u, p):
    a, b = jnp.split(u, 2, axis=-1)
    z = a * jax.nn.sigmoid(b)
    z = lax.conv_general_dilated(z, p['conv_w'][:, None, :], window_strides=(1,),
                                 padding=[(CONV_K // 2, CONV_K // 2)],
                                 dimension_numbers=('NWC', 'WIO', 'NWC'),
                                 feature_group_count=CONV_W) + p['conv_b']
    return jax.nn.silu(layer_norm(z, p['conv_ln_w'], p['conv_ln_b']))


def mla_project(cq, ckv, krope, p, ang):
    bsz, n, _ = cq.shape
    q = (rms_norm(cq, p['mla_qnorm_w']) @ p['mla_wuq']).reshape(bsz, n, MLA_HEADS, MLA_NOPE + MLA_ROPE)
    q = q.transpose(0, 2, 1, 3)
    q_nope, q_rope = q[..., :MLA_NOPE], q[..., MLA_NOPE:]
    ckv_n = rms_norm(ckv, p['mla_kvnorm_w'])
    if ang is not None:
        q_rope = apply_axial_rope(q_rope, ang)
        krope = apply_axial_rope(krope, ang)
    return q_nope, q_rope, ckv_n, krope


def mla_expand(ckv_n, p):
    bsz, nk, _ = ckv_n.shape
    kv = (ckv_n @ p['mla_wukv']).reshape(bsz, nk, MLA_HEADS, MLA_NOPE + MLA_V).transpose(0, 2, 1, 3)
    return kv[..., :MLA_NOPE], kv[..., MLA_NOPE:]


def mla_attend(q_nope, q_rope, k_nope, k_rope, v):
    bsz, nh, nq, _ = q_nope.shape
    nb = nq // Q_BLOCK
    scale = (MLA_NOPE + MLA_ROPE) ** -0.5

    def blocks(a):
        return jnp.moveaxis(a.reshape(bsz, nh, nb, Q_BLOCK, a.shape[-1]), 2, 0)

    def one_block(args):
        qn, qr = args
        s = jnp.einsum('bhqd,bhkd->bhqk', qn, k_nope) + jnp.einsum('bhqr,bkr->bhqk', qr, k_rope)
        prob = jax.nn.softmax(s.astype(jnp.float32) * scale, axis=-1).astype(v.dtype)
        return jnp.einsum('bhqk,bhkd->bhqd', prob, v)

    o = lax.map(one_block, (blocks(q_nope), blocks(q_rope)))
    o = jnp.moveaxis(o, 0, 2).reshape(bsz, nh, nq, MLA_V)
    return o.transpose(0, 2, 1, 3).reshape(bsz, nq, MLA_W)


def hier_moe(h, p):
    lead = h.shape[:-1]
    gp = jax.nn.softmax((h @ p['moe_wg'] + p['moe_bg']).astype(jnp.float32), axis=-1)
    gw, gi = lax.top_k(gp, 1)
    el = (h @ p['moe_we'] + p['moe_be']).astype(jnp.float32).reshape(lead + (N_GROUPS, EXPERTS_PER_GROUP))
    el = jnp.take_along_axis(el, gi[..., None], axis=-2)[..., 0, :]
    ev, ei = lax.top_k(jax.nn.softmax(el, axis=-1), TOP_K)
    ev = ev / jnp.sum(ev, axis=-1, keepdims=True)
    in_group = jnp.sum(jax.nn.one_hot(ei, EXPERTS_PER_GROUP, dtype=jnp.float32) * ev[..., None], axis=-2)
    gate = (jax.nn.one_hot(gi[..., 0], N_GROUPS, dtype=jnp.float32)[..., None]
            * in_group[..., None, :] * gw[..., None]).reshape(lead + (N_EXPERTS,))
    hid = jax.nn.silu(jnp.einsum('bnd,edf->bnef', h, p['moe_w1'])) * jnp.einsum('bnd,edf->bnef', h, p['moe_w3'])
    return jnp.einsum('bnef,efd->bnd', hid * gate.astype(h.dtype)[..., None], p['moe_w2'])


def context_layer(x, c_ctx, p):
    sh1, sc1, g1, sh2, sc2, g2 = adaln_params(c_ctx[None, :], p['ada_w'], p['ada_b'])
    h = modulate(x, p['norm1_w'], sh1, sc1)
    q, k, v, g, gf, gb, u_conv, cq, ckv, krope = split_projection(h, p['w_in'])
    zero = jnp.zeros((x.shape[0], GLA_HEADS, GLA_DK, GLA_DV), x.dtype)
    o_gla, s_f, s_b = gla_mixer(q, k, v, g, gf, gb, p, zero, zero)
    o_conv = conformer_conv(u_conv, p)
    q_nope, q_rope, ckv_n, krope = mla_project(cq, ckv, krope, p, None)
    k_nope, vv = mla_expand(ckv_n, p)
    o_mla = mla_attend(q_nope, q_rope, k_nope, krope, vv)
    mix = jnp.concatenate([o_gla, o_conv, o_mla], axis=-1)
    x = x + g1 * (mix @ p['w_out'])
    x = x + g2 * hier_moe(modulate(x, p['norm2_w'], sh2, sc2), p)
    return x, ckv_n, krope, s_f, s_b


def latent_layer(x, c, ckv_ctx, krope_ctx, s0f, s0b, p):
    sh1, sc1, g1, sh2, sc2, g2 = adaln_params(c, p['ada_w'], p['ada_b'])
    h = modulate(x, p['norm1_w'], sh1, sc1)
    q, k, v, g, gf, gb, u_conv, cq, ckv, krope = split_projection(h, p['w_in'])
    ang = axial_angles(x.shape[1])
    o_gla, _, _ = gla_mixer(q, k, v, g, gf, gb, p, s0f, s0b)
    o_conv = conformer_conv(u_conv, p)
    q_nope, q_rope, ckv_n, krope = mla_project(cq, ckv, krope, p, ang)
    k_nope, vv = mla_expand(jnp.concatenate([ckv_ctx, ckv_n], axis=1), p)
    k_rope = jnp.concatenate([krope_ctx, krope], axis=1)
    o_mla = mla_attend(q_nope, q_rope, k_nope, k_rope, vv)
    mix = jnp.concatenate([o_gla, o_conv, o_mla], axis=-1)
    x = x + g1 * (mix @ p['w_out'])
    x = x + g2 * hier_moe(modulate(x, p['norm2_w'], sh2, sc2), p)
    return x


def setup_inputs(seed: int = 0) -> dict:
    key = jax.random.key(seed)
    L = DEPTH
    specs = [
        ("x_prompt", (BATCH, SEQ, D_MODEL), 1.0, 0.0),
        ("x_sample", (DEC_BATCH, DEC_SEQ, D_MODEL), 1.0, 0.0),
        ("cache_ckv", (DEC_BATCH, DEPTH, PAST_LEN, MLA_KV_RANK), 1.0, 0.0),
        ("cache_krope", (DEC_BATCH, DEPTH, PAST_LEN, MLA_ROPE), 1.0, 0.0),
        ("state_gla_fwd", (DEC_BATCH, DEPTH, GLA_HEADS, GLA_DK, GLA_DV), 1.0, 0.0),
        ("state_gla_bwd", (DEC_BATCH, DEPTH, GLA_HEADS, GLA_DK, GLA_DV), 1.0, 0.0),
        ("c", (DEC_BATCH, D_MODEL), 1.0, 0.0),
        ("c_ctx", (D_MODEL,), 1.0, 0.0),
        ("ada_w", (L, D_MODEL, 6 * D_MODEL), 0.5 * D_MODEL ** -0.5, 0.0),
        ("ada_b", (L, 6 * D_MODEL), 0.02, 0.0),
        ("norm1_w", (L, D_MODEL), 0.02, 1.0),
        ("norm2_w", (L, D_MODEL), 0.02, 1.0),
        ("w_in", (L, D_MODEL, IN_COLS), D_MODEL ** -0.5, 0.0),
        ("gla_wg_f", (L, GLA_GATE_RANK, GLA_HEADS * GLA_DK), GLA_GATE_RANK ** -0.5, 0.0),
        ("gla_bg_f", (L, GLA_HEADS * GLA_DK), 0.5, 0.0),
        ("gla_wg_b", (L, GLA_GATE_RANK, GLA_HEADS * GLA_DK), GLA_GATE_RANK ** -0.5, 0.0),
        ("gla_bg_b", (L, GLA_HEADS * GLA_DK), 0.5, 0.0),
        ("gla_norm_w", (L, GLA_W), 0.02, 1.0),
        ("conv_w", (L, CONV_K, CONV_W), CONV_K ** -0.5, 0.0),
        ("conv_b", (L, CONV_W), 0.01, 0.0),
        ("conv_ln_w", (L, CONV_W), 0.02, 1.0),
        ("conv_ln_b", (L, CONV_W), 0.01, 0.0),
        ("mla_qnorm_w", (L, MLA_Q_RANK), 0.02, 1.0),
        ("mla_wuq", (L, MLA_Q_RANK, MLA_HEADS * (MLA_NOPE + MLA_ROPE)), MLA_Q_RANK ** -0.5, 0.0),
        ("mla_kvnorm_w", (L, MLA_KV_RANK), 0.02, 1.0),
        ("mla_wukv", (L, MLA_KV_RANK, MLA_HEADS * (MLA_NOPE + MLA_V)), MLA_KV_RANK ** -0.5, 0.0),
        ("w_out", (L, MIX_W, D_MODEL), MIX_W ** -0.5, 0.0),
        ("moe_wg", (L, D_MODEL, N_GROUPS), D_MODEL ** -0.5, 0.0),
        ("moe_bg", (L, N_GROUPS), 0.01, 0.0),
        ("moe_we", (L, D_MODEL, N_EXPERTS), D_MODEL ** -0.5, 0.0),
        ("moe_be", (L, N_EXPERTS), 0.01, 0.0),
        ("moe_w1", (L, N_EXPERTS, D_MODEL, EXPERT_FF), D_MODEL ** -0.5, 0.0),
        ("moe_w3", (L, N_EXPERTS, D_MODEL, EXPERT_FF), D_MODEL ** -0.5, 0.0),
        ("moe_w2", (L, N_EXPERTS, EXPERT_FF, D_MODEL), EXPERT_FF ** -0.5, 0.0),
        ("final_norm_w", (D_MODEL,), 0.02, 1.0),
    ]
    keys = jax.random.split(key, len(specs))
    return {name: off + scale * jax.random.normal(keys[i], shape, jnp.float32)
            for i, (name, shape, scale, off) in enumerate(specs)}


def reference(x_prompt, x_sample, cache_ckv, cache_krope, state_gla_fwd, state_gla_bwd, c, c_ctx,
              ada_w, ada_b, norm1_w, norm2_w, w_in, gla_wg_f, gla_bg_f, gla_wg_b, gla_bg_b, gla_norm_w,
              conv_w, conv_b, conv_ln_w, conv_ln_b, mla_qnorm_w, mla_wuq, mla_kvnorm_w, mla_wukv, w_out,
              moe_wg, moe_bg, moe_we, moe_be, moe_w1, moe_w3, moe_w2, final_norm_w):
    xp, xs = x_prompt, x_sample
    ckv_list, krope_list, sf_list, sb_list = [], [], [], []
    for l in range(DEPTH):
        p = dict(ada_w=ada_w[l], ada_b=ada_b[l], norm1_w=norm1_w[l], norm2_w=norm2_w[l], w_in=w_in[l],
                 gla_wg_f=gla_wg_f[l], gla_bg_f=gla_bg_f[l], gla_wg_b=gla_wg_b[l], gla_bg_b=gla_bg_b[l],
                 gla_norm_w=gla_norm_w[l], conv_w=conv_w[l], conv_b=conv_b[l], conv_ln_w=conv_ln_w[l],
                 conv_ln_b=conv_ln_b[l], mla_qnorm_w=mla_qnorm_w[l], mla_wuq=mla_wuq[l],
                 mla_kvnorm_w=mla_kvnorm_w[l], mla_wukv=mla_wukv[l], w_out=w_out[l],
                 moe_wg=moe_wg[l], moe_bg=moe_bg[l], moe_we=moe_we[l], moe_be=moe_be[l],
                 moe_w1=moe_w1[l], moe_w3=moe_w3[l], moe_w2=moe_w2[l])
        xp, ckv_l, krope_l, sf_l, sb_l = context_layer(xp, c_ctx, p)
        ckv_list.append(ckv_l)
        krope_list.append(krope_l)
        sf_list.append(sf_l)
        sb_list.append(sb_l)
        xs = latent_layer(xs, c, cache_ckv[:, l], cache_krope[:, l], state_gla_fwd[:, l], state_gla_bwd[:, l], p)
    y_prompt = rms_norm(xp, final_norm_w)
    y_sample = rms_norm(xs, final_norm_w)
    new_cache_ckv = jnp.stack(ckv_list, axis=1)
    new_cache_krope = jnp.stack(krope_list, axis=1)
    new_state_gla_fwd = jnp.stack(sf_list, axis=1)
    new_state_gla_bwd = jnp.stack(sb_list, axis=1)
    return (y_prompt, y_sample, new_cache_ckv, new_cache_krope, new_state_gla_fwd, new_state_gla_bwd)
```

```python
import functools

import numpy as np
import jax
import jax.numpy as jnp
from jax import lax
from jax.experimental import pallas as pl
from jax.experimental.pallas import tpu as pltpu

F32 = jnp.float32
BF16 = jnp.bfloat16

D_MODEL = 1024
BATCH = 16
SEQ = 256
DEPTH = 2
DEC_BATCH = 4
DEC_SEQ = 1024
PAST_LEN = 512
GRID_W = 64
NORM_EPS = 1e-6
GLA_HEADS = 4
GLA_DK = 64
GLA_DV = 64
GLA_W = 256
GLA_GATE_RANK = 16
GLA_TAU = 16.0
CONV_W = 256
CONV_K = 31
MLA_HEADS = 8
MLA_NOPE = 64
MLA_ROPE = 32
MLA_V = 64
MLA_Q_RANK = 256
MLA_KV_RANK = 128
ROPE_THETA = 10000.0
N_GROUPS = 4
EXPERTS_PER_GROUP = 8
N_EXPERTS = 32
EXPERT_FF = 256

N_CTX = BATCH * SEQ
N_SMP = DEC_BATCH * DEC_SEQ
N_TOK = N_CTX + N_SMP
N_MOD = 8
TM = 256
P_COLS = 2048
MISC_BLK = 15
CHUNK = 64
SUB = 16
NEG = -1e30
VMEM_LIMIT = 56 * 1024 * 1024


def _cparams(sem):
    return pltpu.CompilerParams(dimension_semantics=sem, vmem_limit_bytes=VMEM_LIMIT)


def _dot(a, b):
    return jnp.dot(a, b, preferred_element_type=F32)


def _dot_nt(a, b):
    return lax.dot_general(a, b, (((1,), (1,)), ((), ())), preferred_element_type=F32)


def _split2(x):
    hi = x.astype(BF16)
    lo = (x - hi.astype(F32)).astype(BF16)
    return hi, lo


def _dot3(a, w):
    ah, al = _split2(a)
    wh, wl = _split2(w)
    return _dot(ah, wh) + _dot(al, wh) + _dot(ah, wl)


def _sigmoid(x):
    return 1.0 / (1.0 + jnp.exp(-x))


def _rms(x, w):
    ms = jnp.mean(x * x, axis=-1, keepdims=True)
    return x * lax.rsqrt(ms + NORM_EPS) * w


def _mod_index(i, tile):
    nctx = N_CTX // tile
    per = DEC_SEQ // tile
    return jnp.where(i < nctx, 0, 1 + (i - nctx) // per)


def _ada_kernel(c_ref, w_ref, b_ref, o_ref):
    c = c_ref[...]
    o_ref[...] = _dot3(c * _sigmoid(c), w_ref[...]) + b_ref[...]


def _ada_table(cond8, ada_w, ada_b):
    tn = 1024
    n6 = 6 * D_MODEL
    return pl.pallas_call(
        _ada_kernel,
        grid=(DEPTH, n6 // tn),
        in_specs=[
            pl.BlockSpec((N_MOD, D_MODEL), lambda l, j: (0, 0)),
            pl.BlockSpec((None, D_MODEL, tn), lambda l, j: (l, 0, j)),
            pl.BlockSpec((None, 1, tn), lambda l, j: (l, 0, j)),
        ],
        out_specs=pl.BlockSpec((None, N_MOD, tn), lambda l, j: (l, 0, j)),
        out_shape=jax.ShapeDtypeStruct((DEPTH, N_MOD, n6), F32),
        compiler_params=_cparams(("parallel", "parallel")),
        name="ada_table",
    )(cond8, ada_w, ada_b.reshape(DEPTH, 1, n6))


def _inproj_kernel(x_ref, mod_ref, nw_ref, w_ref, o_ref):
    mod = mod_ref[...]
    h = _rms(x_ref[...], nw_ref[...]) * (1.0 + mod[:, D_MODEL:2 * D_MODEL]) + mod[:, 0:D_MODEL]
    o_ref[...] = _dot(h.astype(BF16), w_ref[...])


def _inproj(x, mods4, l, norm_w, w_in_r):
    return pl.pallas_call(
        _inproj_kernel,
        grid=(N_TOK // TM,),
        in_specs=[
            pl.BlockSpec((TM, D_MODEL), lambda i: (i, 0)),
            pl.BlockSpec((None, None, 1, 6 * D_MODEL), lambda i: (l, _mod_index(i, TM), 0, 0)),
            pl.BlockSpec((1, D_MODEL), lambda i: (0, 0)),
            pl.BlockSpec((D_MODEL, P_COLS), lambda i: (0, 0)),
        ],
        out_specs=pl.BlockSpec((TM, P_COLS), lambda i: (i, 0)),
        out_shape=jax.ShapeDtypeStruct((N_TOK, P_COLS), F32),
        compiler_params=_cparams(("parallel",)),
        name="inproj",
    )(x, mods4, norm_w, w_in_r)


def _log_sigmoid(x):
    return jnp.minimum(x, 0.0) - jnp.log(1.0 + jnp.exp(-jnp.abs(x)))


def _split3(x):
    hi = x.astype(BF16)
    r = x - hi.astype(F32)
    mid = r.astype(BF16)
    lo = (r - mid.astype(F32)).astype(BF16)
    return hi, mid, lo


def _gla_consts(reverse):
    t = lax.broadcasted_iota(jnp.int32, (CHUNK, CHUNK), 0)
    s = lax.broadcasted_iota(jnp.int32, (CHUNK, CHUNK), 1)
    mid = jnp.bitwise_and(t, -SUB) + SUB // 2
    if reverse:
        cum = s >= t
        ref = s >= mid
    else:
        cum = s <= t
        ref = s <= mid
    cum_ref = jnp.concatenate([cum, ref], axis=0).astype(F32).astype(BF16)
    rt = jnp.bitwise_and(lax.broadcasted_iota(jnp.int32, (GLA_HEADS * SUB, CHUNK), 0), SUB - 1)
    cs = lax.broadcasted_iota(jnp.int32, (GLA_HEADS * SUB, CHUNK), 1)
    rowi = lax.broadcasted_iota(jnp.int32, (CHUNK, 1), 0)
    causal, valid = [], []
    for i in range(CHUNK // SUB):
        tq = rt + i * SUB
        causal.append((cs >= tq) if reverse else (cs <= tq))
        valid.append((rowi >= i * SUB) if reverse else (rowi < (i + 1) * SUB))
    return cum_ref, causal, valid


def _gla_chunk(r0, qkvg_ref, la_ref, st_ref, o_ref, consts, head_masks, block_mask, reverse):
    cum_ref, causal, valid = consts
    rows = pl.ds(r0, CHUNK)
    q = qkvg_ref[rows, 0:GLA_W]
    k = qkvg_ref[rows, GLA_W:2 * GLA_W]
    v = qkvg_ref[rows, 2 * GLA_W:3 * GLA_W]
    la = la_ref[rows, :]
    h0, h1, h2 = _split3(la)
    br = _dot(cum_ref, h0) + _dot(cum_ref, h1) + _dot(cum_ref, h2)
    b = br[0:CHUNK]
    rr = br[CHUNK:2 * CHUNK]
    qs = q * (GLA_DK ** -0.5)
    qhat = qs * jnp.exp(b - rr)
    vb = v.astype(BF16)
    parts = []
    for i in range(CHUNK // SUB):
        r_i = rr[i * SUB:i * SUB + 1]
        kt = (k * jnp.exp(jnp.where(valid[i], r_i - b, NEG))).astype(BF16)
        qi = qhat[i * SUB:(i + 1) * SUB]
        qbig = jnp.concatenate([jnp.where(m, qi, 0.0) for m in head_masks], axis=0).astype(BF16)
        a = jnp.where(causal[i], _dot_nt(qbig, kt), 0.0)
        ov = _dot(a.astype(BF16), vb)
        oi = jnp.where(head_masks[0], ov[0:SUB], 0.0)
        for h in range(1, GLA_HEADS):
            oi = oi + jnp.where(head_masks[h], ov[h * SUB:(h + 1) * SUB], 0.0)
        parts.append(oi)
    o_intra = jnp.concatenate(parts, axis=0)
    st = st_ref[...]
    o_inter = _dot_nt((qs * jnp.exp(b)).astype(BF16), st.astype(BF16))
    o_ref[rows, :] = o_inter + o_intra
    b_last = b[0:1] if reverse else b[CHUNK - 1:CHUNK]
    kl = (k * jnp.exp(b_last - b)).astype(BF16)
    upd = _dot(v.T.astype(BF16), kl)
    st_ref[...] = st * jnp.exp(b_last) + jnp.where(block_mask, upd, 0.0)


def _gla_kernel(*refs, n, has_state):
    if has_state:
        (qkvg_ref, misc_ref, wgf_ref, wgb_ref, bgf_ref, bgb_ref, nw_ref, s0f_ref, s0b_ref,
         o_ref, laf_s, lab_s, of_s, ob_s, stf_s, stb_s) = refs
    else:
        (qkvg_ref, misc_ref, wgf_ref, wgb_ref, bgf_ref, bgb_ref, nw_ref,
         o_ref, sf_ref, sb_ref, laf_s, lab_s, of_s, ob_s, stf_s, stb_s) = refs
    misc = misc_ref[...]
    laf_s[...] = _log_sigmoid(_dot3(misc, wgf_ref[...]) + bgf_ref[...]) * (1.0 / GLA_TAU)
    lab_s[...] = _log_sigmoid(_dot3(misc, wgb_ref[...]) + bgb_ref[...]) * (1.0 / GLA_TAU)
    if has_state:
        stf_s[...] = s0f_ref[...]
        stb_s[...] = s0b_ref[...]
    else:
        stf_s[...] = jnp.zeros((GLA_W, GLA_W), F32)
        stb_s[...] = jnp.zeros((GLA_W, GLA_W), F32)

    lane = lax.broadcasted_iota(jnp.int32, (1, GLA_W), 1)
    head_masks = [jnp.right_shift(lane, 6) == h for h in range(GLA_HEADS)]
    bi = jnp.right_shift(lax.broadcasted_iota(jnp.int32, (GLA_W, GLA_W), 0), 6)
    bj = jnp.right_shift(lax.broadcasted_iota(jnp.int32, (GLA_W, GLA_W), 1), 6)
    block_mask = bi == bj
    consts_f = _gla_consts(False)
    consts_b = _gla_consts(True)
    nc = n // CHUNK

    def body(j, carry):
        _gla_chunk(pl.multiple_of(j * CHUNK, CHUNK), qkvg_ref, laf_s, stf_s, of_s,
                   consts_f, head_masks, block_mask, False)
        _gla_chunk(pl.multiple_of((nc - 1 - j) * CHUNK, CHUNK), qkvg_ref, lab_s, stb_s, ob_s,
                   consts_b, head_masks, block_mask, True)
        return carry

    lax.fori_loop(0, nc, body, 0)

    if not has_state:
        sf_ref[...] = stf_s[...]
        sb_ref[...] = stb_s[...]

    ones_blk = block_mask.astype(F32).astype(BF16)
    rt = 256
    for r0 in range(0, n, rt):
        o = of_s[r0:r0 + rt, :] + ob_s[r0:r0 + rt, :]
        hi, lo = _split2(o * o)
        ms = (_dot(hi, ones_blk) + _dot(lo, ones_blk)) * (1.0 / GLA_DV)
        g = qkvg_ref[r0:r0 + rt, 3 * GLA_W:4 * GLA_W]
        o_ref[r0:r0 + rt, :] = o * lax.rsqrt(ms + NORM_EPS) * nw_ref[...] * (g * _sigmoid(g))


def _gla(p, wgf, wgb, bgf, bgb, nw, n, nb, row_blk0, s0f=None, s0b=None):
    has_state = s0f is not None
    full = lambda shape: pl.BlockSpec(shape, lambda b: (0,) * len(shape))
    in_specs = [
        pl.BlockSpec((n, 4 * GLA_W), lambda b: (row_blk0 + b, 0)),
        pl.BlockSpec((n, 128), lambda b: (row_blk0 + b, MISC_BLK)),
        full((128, GLA_W)), full((128, GLA_W)), full((1, GLA_W)), full((1, GLA_W)), full((1, GLA_W)),
    ]
    args = [p, p, wgf, wgb, bgf, bgb, nw]
    st_spec = pl.BlockSpec((None, GLA_W, GLA_W), lambda b: (b, 0, 0))
    out_specs = [pl.BlockSpec((n, GLA_W), lambda b: (b, 0))]
    out_shape = [jax.ShapeDtypeStruct((nb * n, GLA_W), F32)]
    if has_state:
        in_specs += [st_spec, st_spec]
        args += [s0f, s0b]
    else:
        out_specs += [st_spec, st_spec]
        out_shape += [jax.ShapeDtypeStruct((nb, GLA_W, GLA_W), F32)] * 2
    return pl.pallas_call(
        functools.partial(_gla_kernel, n=n, has_state=has_state),
        grid=(nb,),
        in_specs=in_specs,
        out_specs=out_specs,
        out_shape=out_shape,
        scratch_shapes=[pltpu.VMEM((n, GLA_W), F32)] * 4 + [pltpu.VMEM((GLA_W, GLA_W), F32)] * 2,
        compiler_params=_cparams(("parallel",)),
        name="gla_state" if has_state else "gla_ctx",
    )(*args)


CONV_PAD = 16
CONV_ROWS = 128


def _conv_kernel(u_ref, w_ref, b_ref, lnw_ref, lnb_ref, o_ref, zp_s, *, n):
    u = u_ref[...]
    zp_s[0:CONV_PAD, :] = jnp.zeros((CONV_PAD, CONV_W), F32)
    zp_s[CONV_PAD + n:2 * CONV_PAD + n, :] = jnp.zeros((CONV_PAD, CONV_W), F32)
    zp_s[CONV_PAD:CONV_PAD + n, :] = u[:, 0:CONV_W] * _sigmoid(u[:, CONV_W:2 * CONV_W])
    off = CONV_PAD - CONV_K // 2
    for r0 in range(0, n, CONV_ROWS):
        acc = zp_s[r0 + off:r0 + off + CONV_ROWS, :] * w_ref[0:1, :]
        for j in range(1, CONV_K):
            acc = acc + zp_s[r0 + off + j:r0 + off + j + CONV_ROWS, :] * w_ref[j:j + 1, :]
        acc = acc + b_ref[...]
        mu = jnp.mean(acc, axis=-1, keepdims=True)
        xc = acc - mu
        var = jnp.mean(xc * xc, axis=-1, keepdims=True)
        y = xc * lax.rsqrt(var + NORM_EPS) * lnw_ref[...] + lnb_ref[...]
        o_ref[r0:r0 + CONV_ROWS, :] = y * _sigmoid(y)


def _conv(p, cw, cb, lnw, lnb, n, nb, row_blk0):
    full = lambda shape: pl.BlockSpec(shape, lambda b: (0,) * len(shape))
    return pl.pallas_call(
        functools.partial(_conv_kernel, n=n),
        grid=(nb,),
        in_specs=[
            pl.BlockSpec((n, 2 * CONV_W), lambda b: (row_blk0 + b, 2)),
            full((32, CONV_W)), full((1, CONV_W)), full((1, CONV_W)), full((1, CONV_W)),
        ],
        out_specs=pl.BlockSpec((n, CONV_W), lambda b: (b, 0)),
        out_shape=jax.ShapeDtypeStruct((nb * n, CONV_W), F32),
        scratch_shapes=[pltpu.VMEM((n + 2 * CONV_PAD, CONV_W), F32)],
        compiler_params=_cparams(("parallel",)),
        name="conv%d" % n,
    )(p, cw, cb, lnw, lnb)


HEAD_BLK = 128


def _kv_expand(ckvn, t, wk_ref, wp_ref, wv_ref, k_ref, v_ref):
    cb = ckvn.astype(BF16)
    th, tl = _split2(t)
    k_ref[...] = (_dot(cb, wk_ref[...]) + _dot(th, wp_ref[...]) + _dot(tl, wp_ref[...])).astype(BF16)
    v_ref[...] = _dot(cb, wv_ref[...]).astype(BF16)


def _mla_prep_kernel(cq_ref, cm_ref, tqc_ref, tqs_ref, tk_ref, qnw_ref, kvw_ref, wa_ref, wb_ref,
                     wk_ref, wp_ref, wv_ref, q_ref, k_ref, v_ref, ckv_ref, kr_ref):
    qn = _rms(cq_ref[...], qnw_ref[...]).astype(BF16)
    tqc = jnp.concatenate([tqc_ref[...]] * MLA_HEADS, axis=1)
    tqs = jnp.concatenate([tqs_ref[...]] * MLA_HEADS, axis=1)
    scale = (MLA_NOPE + MLA_ROPE) ** -0.5
    q_ref[...] = ((_dot(qn, wa_ref[...]) * tqc + _dot(qn, wb_ref[...]) * tqs) * scale).astype(BF16)
    cm = cm_ref[...]
    ckvn = _rms(cm[:, 0:MLA_KV_RANK], kvw_ref[...])
    ckv_ref[...] = ckvn
    t = cm[:, MLA_KV_RANK:2 * MLA_KV_RANK] * tk_ref[...]
    kr_ref[...] = t
    _kv_expand(ckvn, t, wk_ref, wp_ref, wv_ref, k_ref, v_ref)


def _rope_tile(i):
    nctx = N_CTX // TM
    return jnp.where(i < nctx, 0, 1 + (i - nctx) % (DEC_SEQ // TM))


def _mla_prep(p, tabs, qnw, kvw, wa, wb, wk, wp, wv):
    tqc, tqs, tk = tabs
    full = lambda shape: pl.BlockSpec(shape, lambda i: (0,) * len(shape))
    tab_spec = pl.BlockSpec((TM, HEAD_BLK), lambda i: (_rope_tile(i), 0))
    hw = MLA_HEADS * HEAD_BLK
    return pl.pallas_call(
        _mla_prep_kernel,
        grid=(N_TOK // TM,),
        in_specs=[
            pl.BlockSpec((TM, MLA_Q_RANK), lambda i: (i, 6)),
            pl.BlockSpec((TM, 2 * MLA_KV_RANK), lambda i: (i, 7)),
            tab_spec, tab_spec, tab_spec,
            full((1, MLA_Q_RANK)), full((1, MLA_KV_RANK)),
            full((MLA_Q_RANK, hw)), full((MLA_Q_RANK, hw)),
            full((MLA_KV_RANK, hw)), full((128, hw)), full((MLA_KV_RANK, MLA_HEADS * MLA_V)),
        ],
        out_specs=[
            pl.BlockSpec((TM, hw), lambda i: (i, 0)),
            pl.BlockSpec((TM, hw), lambda i: (i, 0)),
            pl.BlockSpec((TM, MLA_HEADS * MLA_V), lambda i: (i, 0)),
            pl.BlockSpec((TM, MLA_KV_RANK), lambda i: (i, 0)),
            pl.BlockSpec((TM, 128), lambda i: (i, 0)),
        ],
        out_shape=[
            jax.ShapeDtypeStruct((N_TOK, hw), BF16),
            jax.ShapeDtypeStruct((N_TOK, hw), BF16),
            jax.ShapeDtypeStruct((N_TOK, MLA_HEADS * MLA_V), BF16),
            jax.ShapeDtypeStruct((N_TOK, MLA_KV_RANK), F32),
            jax.ShapeDtypeStruct((N_TOK, 128), F32),
        ],
        compiler_params=_cparams(("parallel",)),
        name="mla_prep",
    )(p, p, tqc, tqs, tk, qnw, kvw, wa, wb, wk, wp, wv)


def _cache_kv_kernel(ckv_ref, kr_ref, wk_ref, wp_ref, wv_ref, k_ref, v_ref):
    _kv_expand(ckv_ref[...], kr_ref[...], wk_ref, wp_ref, wv_ref, k_ref, v_ref)


def _cache_kv(ckv, kr, wk, wp, wv):
    n = DEC_BATCH * PAST_LEN
    full = lambda shape: pl.BlockSpec(shape, lambda i: (0,) * len(shape))
    hw = MLA_HEADS * HEAD_BLK
    return pl.pallas_call(
        _cache_kv_kernel,
        grid=(n // PAST_LEN,),
        in_specs=[
            pl.BlockSpec((PAST_LEN, MLA_KV_RANK), lambda i: (i, 0)),
            pl.BlockSpec((PAST_LEN, 128), lambda i: (i, 0)),
            full((MLA_KV_RANK, hw)), full((128, hw)), full((MLA_KV_RANK, MLA_HEADS * MLA_V)),
        ],
        out_specs=[
            pl.BlockSpec((PAST_LEN, hw), lambda i: (i, 0)),
            pl.BlockSpec((PAST_LEN, MLA_HEADS * MLA_V), lambda i: (i, 0)),
        ],
        out_shape=[
            jax.ShapeDtypeStruct((n, hw), BF16),
            jax.ShapeDtypeStruct((n, MLA_HEADS * MLA_V), BF16),
        ],
        compiler_params=_cparams(("parallel",)),
        name="cache_kv",
    )(ckv, kr, wk, wp, wv)


TQ = 256


def _attn_kernel(*refs, has_cache):
    if has_cache:
        q_ref, k_ref, v_ref, kc_ref, vc_ref, o_ref = refs
    else:
        q_ref, k_ref, v_ref, o_ref = refs
    outs = []
    for hh in range(2):
        cols = slice(hh * HEAD_BLK, (hh + 1) * HEAD_BLK)
        q = q_ref[:, cols]
        s = _dot_nt(q, k_ref[:, cols])
        m = jnp.max(s, axis=-1, keepdims=True)
        if has_cache:
            sc = _dot_nt(q, kc_ref[:, cols])
            m = jnp.maximum(m, jnp.max(sc, axis=-1, keepdims=True))
            pc = jnp.exp(sc - m)
        p = jnp.exp(s - m)
        den = jnp.sum(p, axis=-1, keepdims=True)
        o = _dot(p.astype(BF16), v_ref[...])
        if has_cache:
            den = den + jnp.sum(pc, axis=-1, keepdims=True)
            o = o + _dot(pc.astype(BF16), vc_ref[...])
        outs.append(o / den)
    lane = lax.broadcasted_iota(jnp.int32, (1, 2 * MLA_V), 1)
    o_ref[...] = jnp.where(lane < MLA_V, outs[0], outs[1])


def _attention(q, k, v, n, nb, row0, kc=None, vc=None):
    has_cache = kc is not None
    nq = n // TQ
    rb0 = row0 // TQ
    kb0 = row0 // n
    in_specs = [
        pl.BlockSpec((TQ, 2 * HEAD_BLK), lambda b, h, i: (rb0 + b * nq + i, h)),
        pl.BlockSpec((n, 2 * HEAD_BLK), lambda b, h, i: (kb0 + b, h)),
        pl.BlockSpec((n, 2 * MLA_V), lambda b, h, i: (kb0 + b, h)),
    ]
    args = [q, k, v]
    if has_cache:
        in_specs += [
            pl.BlockSpec((PAST_LEN, 2 * HEAD_BLK), lambda b, h, i: (b, h)),
            pl.BlockSpec((PAST_LEN, 2 * MLA_V), lambda b, h, i: (b, h)),
        ]
        args += [kc, vc]
    return pl.pallas_call(
        functools.partial(_attn_kernel, has_cache=has_cache),
        grid=(nb, MLA_HEADS // 2, nq),
        in_specs=in_specs,
        out_specs=pl.BlockSpec((TQ, 2 * MLA_V), lambda b, h, i: (b * nq + i, h)),
        out_shape=jax.ShapeDtypeStruct((nb * n, MLA_HEADS * MLA_V), F32),
        compiler_params=_cparams(("parallel", "parallel", "parallel")),
        name="attn_cache" if has_cache else "attn_ctx",
    )(*args)


def _outproj_kernel(x_ref, og_ref, oc_ref, om_ref, mod_ref, nw_ref, wo_ref, wr_ref, br_ref,
                    xo_ref, h_ref, gate_ref):
    attn = (_dot(og_ref[...].astype(BF16), wo_ref[0:GLA_W, :])
            + _dot(oc_ref[...].astype(BF16), wo_ref[GLA_W:GLA_W + CONV_W, :])
            + _dot(om_ref[...].astype(BF16), wo_ref[GLA_W + CONV_W:D_MODEL, :]))
    mod = mod_ref[...]
    xn = x_ref[...] + mod[:, 2 * D_MODEL:3 * D_MODEL] * attn
    xo_ref[...] = xn
    h = _rms(xn, nw_ref[...]) * (1.0 + mod[:, 4 * D_MODEL:5 * D_MODEL]) + mod[:, 3 * D_MODEL:4 * D_MODEL]
    h_ref[...] = h.astype(BF16)
    logits = _dot3(h, wr_ref[...]) + br_ref[...]
    lane = lax.broadcasted_iota(jnp.int32, logits.shape, 1).astype(F32)
    big = 1e9
    gl = jnp.where((lane >= N_EXPERTS) & (lane < N_EXPERTS + N_GROUPS), logits, NEG)
    gmax = jnp.max(gl, axis=-1, keepdims=True)
    gw = 1.0 / jnp.sum(jnp.exp(gl - gmax), axis=-1, keepdims=True)
    gi = jnp.min(jnp.where(gl == gmax, lane, big), axis=-1, keepdims=True) - N_EXPERTS
    lo = gi * EXPERTS_PER_GROUP
    el = jnp.where((lane >= lo) & (lane < lo + EXPERTS_PER_GROUP), logits, NEG)
    m1 = jnp.max(el, axis=-1, keepdims=True)
    i1 = jnp.min(jnp.where(el == m1, lane, big), axis=-1, keepdims=True)
    el2 = jnp.where(lane == i1, NEG, el)
    m2 = jnp.max(el2, axis=-1, keepdims=True)
    i2 = jnp.min(jnp.where(el2 == m2, lane, big), axis=-1, keepdims=True)
    e2 = jnp.exp(m2 - m1)
    p1 = 1.0 / (1.0 + e2)
    gate_ref[...] = jnp.where(lane == i1, gw * p1, jnp.where(lane == i2, gw * (e2 * p1), 0.0))


def _outproj(x, og, oc, om, mods4, l, norm_w, wo, wr, br):
    full = lambda shape: pl.BlockSpec(shape, lambda i: (0,) * len(shape))
    row = lambda w: pl.BlockSpec((TM, w), lambda i: (i, 0))
    return pl.pallas_call(
        _outproj_kernel,
        grid=(N_TOK // TM,),
        in_specs=[
            row(D_MODEL), row(GLA_W), row(CONV_W), row(MLA_HEADS * MLA_V),
            pl.BlockSpec((None, None, 1, 6 * D_MODEL), lambda i: (l, _mod_index(i, TM), 0, 0)),
            full((1, D_MODEL)), full((D_MODEL, D_MODEL)), full((D_MODEL, 128)), full((1, 128)),
        ],
        out_specs=[row(D_MODEL), row(D_MODEL), row(128)],
        out_shape=[
            jax.ShapeDtypeStruct((N_TOK, D_MODEL), F32),
            jax.ShapeDtypeStruct((N_TOK, D_MODEL), BF16),
            jax.ShapeDtypeStruct((N_TOK, 128), F32),
        ],
        compiler_params=_cparams(("parallel",)),
        name="outproj_router",
    )(x, og, oc, om, mods4, norm_w, wo, wr, br)


TT = 1024


def _moe_kernel(h_ref, gate_ref, x_ref, mod_ref, w1_ref, w3_ref, w2_ref, o_ref, acc_s):
    e = pl.program_id(1)

    @pl.when(e == 0)
    def _():
        acc_s[...] = jnp.zeros(acc_s.shape, F32)

    h = h_ref[...]
    a = _dot(h, w1_ref[...].astype(BF16))
    b = _dot(h, w3_ref[...].astype(BF16))
    gates = gate_ref[...]
    lane = lax.broadcasted_iota(jnp.int32, gates.shape, 1)
    g = jnp.sum(jnp.where(lane == e, gates, 0.0), axis=-1, keepdims=True)
    hid = a * _sigmoid(a) * b * g
    acc_s[...] += _dot(hid.astype(BF16), w2_ref[...].astype(BF16))

    @pl.when(e == N_EXPERTS - 1)
    def _():
        o_ref[...] = x_ref[...] + mod_ref[:, 5 * D_MODEL:6 * D_MODEL] * acc_s[...]


def _moe(h, gates, x, mods4, l, w1, w3, w2):
    row = lambda w: pl.BlockSpec((TT, w), lambda i, e: (i, 0))
    return pl.pallas_call(
        _moe_kernel,
        grid=(N_TOK // TT, N_EXPERTS),
        in_specs=[
            row(D_MODEL), row(128), row(D_MODEL),
            pl.BlockSpec((None, None, 1, 6 * D_MODEL), lambda i, e: (l, _mod_index(i, TT), 0, 0)),
            pl.BlockSpec((None, None, D_MODEL, EXPERT_FF), lambda i, e: (l, e, 0, 0)),
            pl.BlockSpec((None, None, D_MODEL, EXPERT_FF), lambda i, e: (l, e, 0, 0)),
            pl.BlockSpec((None, None, EXPERT_FF, D_MODEL), lambda i, e: (l, e, 0, 0)),
        ],
        out_specs=row(D_MODEL),
        out_shape=jax.ShapeDtypeStruct((N_TOK, D_MODEL), F32),
        scratch_shapes=[pltpu.VMEM((TT, D_MODEL), F32)],
        compiler_params=_cparams(("parallel", "arbitrary")),
        name="moe_dense",
    )(h, gates, x, mods4, w1, w3, w2)


def _final_kernel(x_ref, w_ref, o_ref):
    o_ref[...] = _rms(x_ref[...], w_ref[...])


def _final_norm(x, w):
    return pl.pallas_call(
        _final_kernel,
        grid=(N_TOK // TM,),
        in_specs=[pl.BlockSpec((TM, D_MODEL), lambda i: (i, 0)), pl.BlockSpec((1, D_MODEL), lambda i: (0, 0))],
        out_specs=pl.BlockSpec((TM, D_MODEL), lambda i: (i, 0)),
        out_shape=jax.ShapeDtypeStruct((N_TOK, D_MODEL), F32),
        compiler_params=_cparams(("parallel",)),
        name="final_norm",
    )(x, w)


def _rope_swap(w):
    nf = MLA_ROPE // 4
    g = w.reshape(w.shape[:-1] + (2, 2, nf))
    return jnp.stack([-g[..., 1, :], g[..., 0, :]], axis=-2).reshape(w.shape)


def _rope_tables():
    nf = MLA_ROPE // 4
    pos = np.arange(DEC_SEQ)
    row = (pos // GRID_W).astype(np.float32)
    col = (pos % GRID_W).astype(np.float32)
    inv = (np.float32(ROPE_THETA) ** (-np.arange(nf, dtype=np.float32) / np.float32(nf))).astype(np.float32)
    ar = (row[:, None] * inv).astype(np.float32)
    ac = (col[:, None] * inv).astype(np.float32)
    cos = np.concatenate([np.cos(ar), np.cos(ar), np.cos(ac), np.cos(ac)], axis=1)
    sin = np.concatenate([np.sin(ar), np.sin(ar), np.sin(ac), np.sin(ac)], axis=1)
    cos = np.concatenate([np.ones((TM, MLA_ROPE)), cos], axis=0).astype(np.float32)
    sin = np.concatenate([np.zeros((TM, MLA_ROPE)), sin], axis=0).astype(np.float32)
    n = cos.shape[0]
    z32 = np.zeros((n, 32), np.float32)
    tqc = np.concatenate([np.ones((n, MLA_NOPE), np.float32), cos, z32], axis=1)
    tqs = np.concatenate([np.zeros((n, MLA_NOPE), np.float32), sin, z32], axis=1)
    tk = np.concatenate([z32, cos, sin, z32], axis=1)
    return jnp.asarray(tqc), jnp.asarray(tqs), jnp.asarray(tk)


def _prep_layer(l, w_in, gla_wg_f, gla_wg_b, mla_wuq, mla_wukv, w_out, moe_wg, moe_bg, moe_we, moe_be):
    offs = np.cumsum([0, 256, 256, 256, 256, 16, 16, 512, 256, 128, 32])
    seg = lambda i: w_in[l][:, offs[i]:offs[i + 1]]
    kr = seg(9)
    w_in_r = jnp.concatenate(
        [seg(0), seg(1), seg(2), seg(3), seg(6), seg(7), seg(8), seg(4), seg(5), kr, _rope_swap(kr),
         jnp.zeros((D_MODEL, 32), F32)], axis=1).astype(BF16)
    wgf = jnp.zeros((128, GLA_W), F32).at[0:16].set(gla_wg_f[l])
    wgb = jnp.zeros((128, GLA_W), F32).at[16:32].set(gla_wg_b[l])
    wq = mla_wuq[l].reshape(MLA_Q_RANK, MLA_HEADS, MLA_NOPE + MLA_ROPE)
    nope, rope = wq[..., :MLA_NOPE], wq[..., MLA_NOPE:]
    z = lambda n: jnp.zeros((MLA_Q_RANK, MLA_HEADS, n), F32)
    hw = MLA_HEADS * HEAD_BLK
    wa = jnp.concatenate([nope, rope, z(32)], axis=-1).reshape(MLA_Q_RANK, hw).astype(BF16)
    wb = jnp.concatenate([z(MLA_NOPE), _rope_swap(rope), z(32)], axis=-1).reshape(MLA_Q_RANK, hw).astype(BF16)
    wkv = mla_wukv[l].reshape(MLA_KV_RANK, MLA_HEADS, MLA_NOPE + MLA_V)
    wk = jnp.concatenate([wkv[..., :MLA_NOPE], jnp.zeros((MLA_KV_RANK, MLA_HEADS, 64), F32)],
                         axis=-1).reshape(MLA_KV_RANK, hw).astype(BF16)
    wv = wkv[..., MLA_NOPE:].reshape(MLA_KV_RANK, MLA_HEADS * MLA_V).astype(BF16)
    wp_np = np.zeros((128, MLA_HEADS, HEAD_BLK), np.float32)
    for j in range(MLA_ROPE):
        wp_np[32 + j, :, MLA_NOPE + j] = 1.0
        wp_np[64 + j, :, MLA_NOPE + j] = 1.0
    wp = jnp.asarray(wp_np.reshape(128, hw)).astype(BF16)
    wo = w_out[l].astype(BF16)
    wr = jnp.concatenate([moe_we[l], moe_wg[l], jnp.zeros((D_MODEL, 128 - N_EXPERTS - N_GROUPS), F32)], axis=1)
    br = jnp.concatenate([moe_be[l], moe_bg[l], jnp.zeros((128 - N_EXPERTS - N_GROUPS,), F32)]).reshape(1, 128)
    return dict(w_in_r=w_in_r, wgf=wgf, wgb=wgb, wa=wa, wb=wb, wk=wk, wv=wv, wp=wp, wo=wo, wr=wr, br=br)


def _state_to_blockdiag(s):
    b = s.shape[0]
    st = jnp.transpose(s, (0, 1, 3, 2))
    eye = jnp.eye(GLA_HEADS, dtype=s.dtype)
    out = st[:, :, :, None, :] * eye[None, :, None, :, None]
    return out.reshape(b, GLA_W, GLA_W)


def _blockdiag_to_state(st):
    b = st.shape[0]
    blocks = st.reshape(b, GLA_HEADS, GLA_DV, GLA_HEADS, GLA_DK)
    diag = jnp.stack([blocks[:, h, :, h, :] for h in range(GLA_HEADS)], axis=1)
    return jnp.transpose(diag, (0, 1, 3, 2))


def kernel(x_prompt, x_sample, cache_ckv, cache_krope, state_gla_fwd, state_gla_bwd, c, c_ctx, ada_w, ada_b,
           norm1_w, norm2_w, w_in, gla_wg_f, gla_bg_f, gla_wg_b, gla_bg_b, gla_norm_w, conv_w, conv_b,
           conv_ln_w, conv_ln_b, mla_qnorm_w, mla_wuq, mla_kvnorm_w, mla_wukv, w_out, moe_wg, moe_bg, moe_we,
           moe_be, moe_w1, moe_w3, moe_w2, final_norm_w):
    x = jnp.concatenate([x_prompt.reshape(N_CTX, D_MODEL), x_sample.reshape(N_SMP, D_MODEL)], axis=0)
    cond8 = jnp.concatenate([c_ctx[None, :], c, jnp.zeros((N_MOD - 1 - DEC_BATCH, D_MODEL), F32)], axis=0)
    mods4 = _ada_table(cond8, ada_w, ada_b).reshape(DEPTH, N_MOD, 1, 6 * D_MODEL)
    tabs = _rope_tables()
    row1 = lambda a: a.reshape(1, -1)

    ckv_new, kr_new, sf_new, sb_new = [], [], [], []
    for l in range(DEPTH):
        w = _prep_layer(l, w_in, gla_wg_f, gla_wg_b, mla_wuq, mla_wukv, w_out, moe_wg, moe_bg, moe_we, moe_be)
        p = _inproj(x, mods4, l, row1(norm1_w[l]), w["w_in_r"])

        gla_args = (p, w["wgf"], w["wgb"], row1(gla_bg_f[l]), row1(gla_bg_b[l]), row1(gla_norm_w[l]))
        og_c, sf, sb = _gla(*gla_args, SEQ, BATCH, 0)
        og_s = _gla(*gla_args, DEC_SEQ, DEC_BATCH, N_CTX // DEC_SEQ,
                    _state_to_blockdiag(state_gla_fwd[:, l]), _state_to_blockdiag(state_gla_bwd[:, l]))[0]
        og = jnp.concatenate([og_c, og_s], axis=0)

        cw = jnp.concatenate([conv_w[l], jnp.zeros((1, CONV_W), F32)], axis=0)
        conv_args = (p, cw, row1(conv_b[l]), row1(conv_ln_w[l]), row1(conv_ln_b[l]))
        oc = jnp.concatenate([_conv(*conv_args, SEQ, BATCH, 0),
                              _conv(*conv_args, DEC_SEQ, DEC_BATCH, N_CTX // DEC_SEQ)], axis=0)

        q, k, v, ckvn, kr = _mla_prep(p, tabs, row1(mla_qnorm_w[l]), row1(mla_kvnorm_w[l]),
                                      w["wa"], w["wb"], w["wk"], w["wp"], w["wv"])
        cc = cache_ckv[:, l].reshape(DEC_BATCH * PAST_LEN, MLA_KV_RANK)
        ckr = cache_krope[:, l].reshape(DEC_BATCH * PAST_LEN, MLA_ROPE)
        ckr = jnp.pad(ckr, ((0, 0), (32, 64)))
        kc, vc = _cache_kv(cc, ckr, w["wk"], w["wp"], w["wv"])
        om = jnp.concatenate([_attention(q, k, v, SEQ, BATCH, 0),
                              _attention(q, k, v, DEC_SEQ, DEC_BATCH, N_CTX, kc, vc)], axis=0)

        xn, h2, gates = _outproj(x, og, oc, om, mods4, l, row1(norm2_w[l]), w["wo"], w["wr"], w["br"])
        x = _moe(h2, gates, xn, mods4, l, moe_w1, moe_w3, moe_w2)

        ckv_new.append(ckvn[:N_CTX].reshape(BATCH, SEQ, MLA_KV_RANK))
        kr_new.append(kr[:N_CTX, 32:32 + MLA_ROPE].reshape(BATCH, SEQ, MLA_ROPE))
        sf_new.append(_blockdiag_to_state(sf))
        sb_new.append(_blockdiag_to_state(sb))

    y = _final_norm(x, row1(final_norm_w))
    return (y[:N_CTX].reshape(BATCH, SEQ, D_MODEL), y[N_CTX:].reshape(DEC_BATCH, DEC_SEQ, D_MODEL),
            jnp.stack(ckv_new, axis=1), jnp.stack(kr_new, axis=1),
            jnp.stack(sf_new, axis=1), jnp.stack(sb_new, axis=1))
```

```python
import functools

import numpy as np
import jax
import jax.numpy as jnp
from jax import lax
from jax.experimental import pallas as pl
from jax.experimental.pallas import tpu as pltpu

F32 = jnp.float32
BF16 = jnp.bfloat16

D_MODEL = 1024
BATCH = 16
SEQ = 256
DEPTH = 2
DEC_BATCH = 4
DEC_SEQ = 1024
PAST_LEN = 512
GRID_W = 64
NORM_EPS = 1e-6
GLA_HEADS = 4
GLA_DK = 64
GLA_DV = 64
GLA_W = 256
GLA_GATE_RANK = 16
GLA_TAU = 16.0
CONV_W = 256
CONV_K = 31
MLA_HEADS = 8
MLA_NOPE = 64
MLA_ROPE = 32
MLA_V = 64
MLA_Q_RANK = 256
MLA_KV_RANK = 128
ROPE_THETA = 10000.0
N_GROUPS = 4
EXPERTS_PER_GROUP = 8
N_EXPERTS = 32
EXPERT_FF = 256

N_CTX = BATCH * SEQ
N_SMP = DEC_BATCH * DEC_SEQ
N_TOK = N_CTX + N_SMP
N_MOD = 8
TM = 256
P_COLS = 2048
MISC_BLK = 15
CHUNK = 64
SUB = 16
NEG = -1e30
VMEM_LIMIT = 56 * 1024 * 1024


def _cparams(sem):
    return pltpu.CompilerParams(dimension_semantics=sem, vmem_limit_bytes=VMEM_LIMIT)


def _dot(a, b):
    return jnp.dot(a, b, preferred_element_type=F32)


def _dot_nt(a, b):
    return lax.dot_general(a, b, (((1,), (1,)), ((), ())), preferred_element_type=F32)


def _split2(x):
    hi = x.astype(BF16)
    lo = (x - hi.astype(F32)).astype(BF16)
    return hi, lo


def _dot3(a, w):
    ah, al = _split2(a)
    wh, wl = _split2(w)
    return _dot(ah, wh) + _dot(al, wh) + _dot(ah, wl)


def _sigmoid(x):
    return 1.0 / (1.0 + jnp.exp(-x))


def _rms(x, w):
    ms = jnp.mean(x * x, axis=-1, keepdims=True)
    return x * lax.rsqrt(ms + NORM_EPS) * w


def _mod_index(i, tile):
    nctx = N_CTX // tile
    per = DEC_SEQ // tile
    return jnp.where(i < nctx, 0, 1 + (i - nctx) // per)


def _ada_kernel(c_ref, w_ref, b_ref, o_ref):
    c = c_ref[...]
    o_ref[...] = _dot3(c * _sigmoid(c), w_ref[...]) + b_ref[...]


def _ada_table(cond8, ada_w, ada_b):
    tn = 1024
    n6 = 6 * D_MODEL
    return pl.pallas_call(
        _ada_kernel,
        grid=(DEPTH, n6 // tn),
        in_specs=[
            pl.BlockSpec((N_MOD, D_MODEL), lambda l, j: (0, 0)),
            pl.BlockSpec((None, D_MODEL, tn), lambda l, j: (l, 0, j)),
            pl.BlockSpec((None, 1, tn), lambda l, j: (l, 0, j)),
        ],
        out_specs=pl.BlockSpec((None, N_MOD, tn), lambda l, j: (l, 0, j)),
        out_shape=jax.ShapeDtypeStruct((DEPTH, N_MOD, n6), F32),
        compiler_params=_cparams(("parallel", "parallel")),
        name="ada_table",
    )(cond8, ada_w, ada_b.reshape(DEPTH, 1, n6))


def _inproj_kernel(x_ref, mod_ref, nw_ref, w_ref, o_ref):
    mod = mod_ref[...]
    h = _rms(x_ref[...], nw_ref[...]) * (1.0 + mod[:, D_MODEL:2 * D_MODEL]) + mod[:, 0:D_MODEL]
    o_ref[...] = _dot(h.astype(BF16), w_ref[...])


def _inproj(x, mods4, l, norm_w, w_in_r):
    return pl.pallas_call(
        _inproj_kernel,
        grid=(N_TOK // TM,),
        in_specs=[
            pl.BlockSpec((TM, D_MODEL), lambda i: (i, 0)),
            pl.BlockSpec((None, None, 1, 6 * D_MODEL), lambda i: (l, _mod_index(i, TM), 0, 0)),
            pl.BlockSpec((1, D_MODEL), lambda i: (0, 0)),
            pl.BlockSpec((D_MODEL, P_COLS), lambda i: (0, 0)),
        ],
        out_specs=pl.BlockSpec((TM, P_COLS), lambda i: (i, 0)),
        out_shape=jax.ShapeDtypeStruct((N_TOK, P_COLS), F32),
        compiler_params=_cparams(("parallel",)),
        name="inproj",
    )(x, mods4, norm_w, w_in_r)


def _log_sigmoid(x):
    return jnp.minimum(x, 0.0) - jnp.log(1.0 + jnp.exp(-jnp.abs(x)))


def _split3(x):
    hi = x.astype(BF16)
    r = x - hi.astype(F32)
    mid = r.astype(BF16)
    lo = (r - mid.astype(F32)).astype(BF16)
    return hi, mid, lo


def _gla_consts(reverse):
    t = lax.broadcasted_iota(jnp.int32, (CHUNK, CHUNK), 0)
    s = lax.broadcasted_iota(jnp.int32, (CHUNK, CHUNK), 1)
    mid = jnp.bitwise_and(t, -SUB) + SUB // 2
    if reverse:
        cum = s >= t
        ref = s >= mid
    else:
        cum = s <= t
        ref = s <= mid
    cum_ref = jnp.concatenate([cum, ref], axis=0).astype(F32).astype(BF16)
    rt = jnp.bitwise_and(lax.broadcasted_iota(jnp.int32, (GLA_HEADS * SUB, CHUNK), 0), SUB - 1)
    cs = lax.broadcasted_iota(jnp.int32, (GLA_HEADS * SUB, CHUNK), 1)
    rowi = lax.broadcasted_iota(jnp.int32, (CHUNK, 1), 0)
    causal, valid = [], []
    for i in range(CHUNK // SUB):
        tq = rt + i * SUB
        causal.append((cs >= tq) if reverse else (cs <= tq))
        valid.append((rowi >= i * SUB) if reverse else (rowi < (i + 1) * SUB))
    return cum_ref, causal, valid


def _gla_chunk(r0, qkvg_ref, la_ref, st_ref, o_ref, consts, head_masks, block_mask, reverse):
    cum_ref, causal, valid = consts
    rows = pl.ds(r0, CHUNK)
    q = qkvg_ref[rows, 0:GLA_W]
    k = qkvg_ref[rows, GLA_W:2 * GLA_W]
    v = qkvg_ref[rows, 2 * GLA_W:3 * GLA_W]
    la = la_ref[rows, :]
    h0, h1, h2 = _split3(la)
    br = _dot(cum_ref, h0) + _dot(cum_ref, h1) + _dot(cum_ref, h2)
    b = br[0:CHUNK]
    rr = br[CHUNK:2 * CHUNK]
    qs = q * (GLA_DK ** -0.5)
    qhat = qs * jnp.exp(b - rr)
    vb = v.astype(BF16)
    parts = []
    for i in range(CHUNK // SUB):
        r_i = rr[i * SUB:i * SUB + 1]
        kt = (k * jnp.exp(jnp.where(valid[i], r_i - b, NEG))).astype(BF16)
        qi = qhat[i * SUB:(i + 1) * SUB]
        qbig = jnp.concatenate([jnp.where(m, qi, 0.0) for m in head_masks], axis=0).astype(BF16)
        a = jnp.where(causal[i], _dot_nt(qbig, kt), 0.0)
        ov = _dot(a.astype(BF16), vb)
        oi = jnp.where(head_masks[0], ov[0:SUB], 0.0)
        for h in range(1, GLA_HEADS):
            oi = oi + jnp.where(head_masks[h], ov[h * SUB:(h + 1) * SUB], 0.0)
        parts.append(oi)
    o_intra = jnp.concatenate(parts, axis=0)
    st = st_ref[...]
    o_inter = _dot_nt((qs * jnp.exp(b)).astype(BF16), st.astype(BF16))
    o_ref[rows, :] = o_inter + o_intra
    b_last = b[0:1] if reverse else b[CHUNK - 1:CHUNK]
    kl = (k * jnp.exp(b_last - b)).astype(BF16)
    upd = _dot(v.T.astype(BF16), kl)
    st_ref[...] = st * jnp.exp(b_last) + jnp.where(block_mask, upd, 0.0)


def _gla_kernel(*refs, n, has_state):
    if has_state:
        (qkvg_ref, misc_ref, wgf_ref, wgb_ref, bgf_ref, bgb_ref, nw_ref, s0f_ref, s0b_ref,
         o_ref, laf_s, lab_s, of_s, ob_s, stf_s, stb_s) = refs
    else:
        (qkvg_ref, misc_ref, wgf_ref, wgb_ref, bgf_ref, bgb_ref, nw_ref,
         o_ref, sf_ref, sb_ref, laf_s, lab_s, of_s, ob_s, stf_s, stb_s) = refs
    misc = misc_ref[...]
    laf_s[...] = _log_sigmoid(_dot3(misc, wgf_ref[...]) + bgf_ref[...]) * (1.0 / GLA_TAU)
    lab_s[...] = _log_sigmoid(_dot3(misc, wgb_ref[...]) + bgb_ref[...]) * (1.0 / GLA_TAU)
    if has_state:
        stf_s[...] = s0f_ref[...]
        stb_s[...] = s0b_ref[...]
    else:
        stf_s[...] = jnp.zeros((GLA_W, GLA_W), F32)
        stb_s[...] = jnp.zeros((GLA_W, GLA_W), F32)

    lane = lax.broadcasted_iota(jnp.int32, (1, GLA_W), 1)
    head_masks = [jnp.right_shift(lane, 6) == h for h in range(GLA_HEADS)]
    bi = jnp.right_shift(lax.broadcasted_iota(jnp.int32, (GLA_W, GLA_W), 0), 6)
    bj = jnp.right_shift(lax.broadcasted_iota(jnp.int32, (GLA_W, GLA_W), 1), 6)
    block_mask = bi == bj
    consts_f = _gla_consts(False)
    consts_b = _gla_consts(True)
    nc = n // CHUNK

    def body(j, carry):
        _gla_chunk(pl.multiple_of(j * CHUNK, CHUNK), qkvg_ref, laf_s, stf_s, of_s,
                   consts_f, head_masks, block_mask, False)
        _gla_chunk(pl.multiple_of((nc - 1 - j) * CHUNK, CHUNK), qkvg_ref, lab_s, stb_s, ob_s,
                   consts_b, head_masks, block_mask, True)
        return carry

    lax.fori_loop(0, nc, body, 0)

    if not has_state:
        sf_ref[...] = stf_s[...]
        sb_ref[...] = stb_s[...]

    ones_blk = block_mask.astype(F32).astype(BF16)
    rt = 256
    for r0 in range(0, n, rt):
        o = of_s[r0:r0 + rt, :] + ob_s[r0:r0 + rt, :]
        hi, lo = _split2(o * o)
        ms = (_dot(hi, ones_blk) + _dot(lo, ones_blk)) * (1.0 / GLA_DV)
        g = qkvg_ref[r0:r0 + rt, 3 * GLA_W:4 * GLA_W]
        o_ref[r0:r0 + rt, :] = o * lax.rsqrt(ms + NORM_EPS) * nw_ref[...] * (g * _sigmoid(g))


def _gla(p, wgf, wgb, bgf, bgb, nw, n, nb, row_blk0, s0f=None, s0b=None):
    has_state = s0f is not None
    full = lambda shape: pl.BlockSpec(shape, lambda b: (0,) * len(shape))
    in_specs = [
        pl.BlockSpec((n, 4 * GLA_W), lambda b: (row_blk0 + b, 0)),
        pl.BlockSpec((n, 128), lambda b: (row_blk0 + b, MISC_BLK)),
        full((128, GLA_W)), full((128, GLA_W)), full((1, GLA_W)), full((1, GLA_W)), full((1, GLA_W)),
    ]
    args = [p, p, wgf, wgb, bgf, bgb, nw]
    st_spec = pl.BlockSpec((None, GLA_W, GLA_W), lambda b: (b, 0, 0))
    out_specs = [pl.BlockSpec((n, GLA_W), lambda b: (b, 0))]
    out_shape = [jax.ShapeDtypeStruct((nb * n, GLA_W), F32)]
    if has_state:
        in_specs += [st_spec, st_spec]
        args += [s0f, s0b]
    else:
        out_specs += [st_spec, st_spec]
        out_shape += [jax.ShapeDtypeStruct((nb, GLA_W, GLA_W), F32)] * 2
    return pl.pallas_call(
        functools.partial(_gla_kernel, n=n, has_state=has_state),
        grid=(nb,),
        in_specs=in_specs,
        out_specs=out_specs,
        out_shape=out_shape,
        scratch_shapes=[pltpu.VMEM((n, GLA_W), F32)] * 4 + [pltpu.VMEM((GLA_W, GLA_W), F32)] * 2,
        compiler_params=_cparams(("parallel",)),
        name="gla_state" if has_state else "gla_ctx",
    )(*args)


CONV_PAD = 16
CONV_ROWS = 128


def _conv_kernel(u_ref, w_ref, b_ref, lnw_ref, lnb_ref, o_ref, zp_s, *, n):
    u = u_ref[...]
    zp_s[0:CONV_PAD, :] = jnp.zeros((CONV_PAD, CONV_W), F32)
    zp_s[CONV_PAD + n:2 * CONV_PAD + n, :] = jnp.zeros((CONV_PAD, CONV_W), F32)
    zp_s[CONV_PAD:CONV_PAD + n, :] = u[:, 0:CONV_W] * _sigmoid(u[:, CONV_W:2 * CONV_W])
    off = CONV_PAD - CONV_K // 2
    for r0 in range(0, n, CONV_ROWS):
        acc = zp_s[r0 + off:r0 + off + CONV_ROWS, :] * w_ref[0:1, :]
        for j in range(1, CONV_K):
            acc = acc + zp_s[r0 + off + j:r0 + off + j + CONV_ROWS, :] * w_ref[j:j + 1, :]
        acc = acc + b_ref[...]
        mu = jnp.mean(acc, axis=-1, keepdims=True)
        xc = acc - mu
        var = jnp.mean(xc * xc, axis=-1, keepdims=True)
        y = xc * lax.rsqrt(var + NORM_EPS) * lnw_ref[...] + lnb_ref[...]
        o_ref[r0:r0 + CONV_ROWS, :] = y * _sigmoid(y)


def _conv(p, cw, cb, lnw, lnb, n, nb, row_blk0):
    full = lambda shape: pl.BlockSpec(shape, lambda b: (0,) * len(shape))
    return pl.pallas_call(
        functools.partial(_conv_kernel, n=n),
        grid=(nb,),
        in_specs=[
            pl.BlockSpec((n, 2 * CONV_W), lambda b: (row_blk0 + b, 2)),
            full((32, CONV_W)), full((1, CONV_W)), full((1, CONV_W)), full((1, CONV_W)),
        ],
        out_specs=pl.BlockSpec((n, CONV_W), lambda b: (b, 0)),
        out_shape=jax.ShapeDtypeStruct((nb * n, CONV_W), F32),
        scratch_shapes=[pltpu.VMEM((n + 2 * CONV_PAD, CONV_W), F32)],
        compiler_params=_cparams(("parallel",)),
        name="conv%d" % n,
    )(p, cw, cb, lnw, lnb)


HEAD_BLK = 128


def _kv_expand(ckvn, t, wk_ref, wp_ref, wv_ref, k_ref, v_ref):
    cb = ckvn.astype(BF16)
    th, tl = _split2(t)
    k_ref[...] = (_dot(cb, wk_ref[...]) + _dot(th, wp_ref[...]) + _dot(tl, wp_ref[...])).astype(BF16)
    v_ref[...] = _dot(cb, wv_ref[...]).astype(BF16)


def _mla_prep_kernel(cq_ref, cm_ref, tqc_ref, tqs_ref, tk_ref, qnw_ref, kvw_ref, wa_ref, wb_ref,
                     wk_ref, wp_ref, wv_ref, q_ref, k_ref, v_ref, ckv_ref, kr_ref):
    qn = _rms(cq_ref[...], qnw_ref[...]).astype(BF16)
    tqc = jnp.concatenate([tqc_ref[...]] * MLA_HEADS, axis=1)
    tqs = jnp.concatenate([tqs_ref[...]] * MLA_HEADS, axis=1)
    scale = (MLA_NOPE + MLA_ROPE) ** -0.5
    q_ref[...] = ((_dot(qn, wa_ref[...]) * tqc + _dot(qn, wb_ref[...]) * tqs) * scale).astype(BF16)
    cm = cm_ref[...]
    ckvn = _rms(cm[:, 0:MLA_KV_RANK], kvw_ref[...])
    ckv_ref[...] = ckvn
    t = cm[:, MLA_KV_RANK:2 * MLA_KV_RANK] * tk_ref[...]
    kr_ref[...] = t
    _kv_expand(ckvn, t, wk_ref, wp_ref, wv_ref, k_ref, v_ref)


def _rope_tile(i):
    nctx = N_CTX // TM
    return jnp.where(i < nctx, 0, 1 + (i - nctx) % (DEC_SEQ // TM))


def _mla_prep(p, tabs, qnw, kvw, wa, wb, wk, wp, wv):
    tqc, tqs, tk = tabs
    full = lambda shape: pl.BlockSpec(shape, lambda i: (0,) * len(shape))
    tab_spec = pl.BlockSpec((TM, HEAD_BLK), lambda i: (_rope_tile(i), 0))
    hw = MLA_HEADS * HEAD_BLK
    return pl.pallas_call(
        _mla_prep_kernel,
        grid=(N_TOK // TM,),
        in_specs=[
            pl.BlockSpec((TM, MLA_Q_RANK), lambda i: (i, 6)),
            pl.BlockSpec((TM, 2 * MLA_KV_RANK), lambda i: (i, 7)),
            tab_spec, tab_spec, tab_spec,
            full((1, MLA_Q_RANK)), full((1, MLA_KV_RANK)),
            full((MLA_Q_RANK, hw)), full((MLA_Q_RANK, hw)),
            full((MLA_KV_RANK, hw)), full((128, hw)), full((MLA_KV_RANK, MLA_HEADS * MLA_V)),
        ],
        out_specs=[
            pl.BlockSpec((TM, hw), lambda i: (i, 0)),
            pl.BlockSpec((TM, hw), lambda i: (i, 0)),
            pl.BlockSpec((TM, MLA_HEADS * MLA_V), lambda i: (i, 0)),
            pl.BlockSpec((TM, MLA_KV_RANK), lambda i: (i, 0)),
            pl.BlockSpec((TM, 128), lambda i: (i, 0)),
        ],
        out_shape=[
            jax.ShapeDtypeStruct((N_TOK, hw), BF16),
            jax.ShapeDtypeStruct((N_TOK, hw), BF16),
            jax.ShapeDtypeStruct((N_TOK, MLA_HEADS * MLA_V), BF16),
            jax.ShapeDtypeStruct((N_TOK, MLA_KV_RANK), F32),
            jax.ShapeDtypeStruct((N_TOK, 128), F32),
        ],
        compiler_params=_cparams(("parallel",)),
        name="mla_prep",
    )(p, p, tqc, tqs, tk, qnw, kvw, wa, wb, wk, wp, wv)


def _cache_kv_kernel(ckv_ref, kr_ref, wk_ref, wp_ref, wv_ref, k_ref, v_ref):
    _kv_expand(ckv_ref[...], kr_ref[...], wk_ref, wp_ref, wv_ref, k_ref, v_ref)


def _cache_kv(ckv, kr, wk, wp, wv):
    n = DEC_BATCH * PAST_LEN
    full = lambda shape: pl.BlockSpec(shape, lambda i: (0,) * len(shape))
    hw = MLA_HEADS * HEAD_BLK
    return pl.pallas_call(
        _cache_kv_kernel,
        grid=(n // PAST_LEN,),
        in_specs=[
            pl.BlockSpec((PAST_LEN, MLA_KV_RANK), lambda i: (i, 0)),
            pl.BlockSpec((PAST_LEN, 128), lambda i: (i, 0)),
            full((MLA_KV_RANK, hw)), full((128, hw)), full((MLA_KV_RANK, MLA_HEADS * MLA_V)),
        ],
        out_specs=[
            pl.BlockSpec((PAST_LEN, hw), lambda i: (i, 0)),
            pl.BlockSpec((PAST_LEN, MLA_HEADS * MLA_V), lambda i: (i, 0)),
        ],
        out_shape=[
            jax.ShapeDtypeStruct((n, hw), BF16),
            jax.ShapeDtypeStruct((n, MLA_HEADS * MLA_V), BF16),
        ],
        compiler_params=_cparams(("parallel",)),
        name="cache_kv",
    )(ckv, kr, wk, wp, wv)


TQ = 256


def _attn_kernel(*refs, has_cache):
    if has_cache:
        q_ref, k_ref, v_ref, kc_ref, vc_ref, o_ref = refs
    else:
        q_ref, k_ref, v_ref, o_ref = refs
    outs = []
    for hh in range(2):
        cols = slice(hh * HEAD_BLK, (hh + 1) * HEAD_BLK)
        q = q_ref[:, cols]
        s = _dot_nt(q, k_ref[:, cols])
        m = jnp.max(s, axis=-1, keepdims=True)
        if has_cache:
            sc = _dot_nt(q, kc_ref[:, cols])
            m = jnp.maximum(m, jnp.max(sc, axis=-1, keepdims=True))
            pc = jnp.exp(sc - m)
        p = jnp.exp(s - m)
        den = jnp.sum(p, axis=-1, keepdims=True)
        o = _dot(p.astype(BF16), v_ref[...])
        if has_cache:
            den = den + jnp.sum(pc, axis=-1, keepdims=True)
            o = o + _dot(pc.astype(BF16), vc_ref[...])
        outs.append(o / den)
    lane = lax.broadcasted_iota(jnp.int32, (1, 2 * MLA_V), 1)
    o_ref[...] = jnp.where(lane < MLA_V, outs[0], outs[1])


def _attention(q, k, v, n, nb, row0, kc=None, vc=None):
    has_cache = kc is not None
    nq = n // TQ
    rb0 = row0 // TQ
    kb0 = row0 // n
    in_specs = [
        pl.BlockSpec((TQ, 2 * HEAD_BLK), lambda b, h, i: (rb0 + b * nq + i, h)),
        pl.BlockSpec((n, 2 * HEAD_BLK), lambda b, h, i: (kb0 + b, h)),
        pl.BlockSpec((n, 2 * MLA_V), lambda b, h, i: (kb0 + b, h)),
    ]
    args = [q, k, v]
    if has_cache:
        in_specs += [
            pl.BlockSpec((PAST_LEN, 2 * HEAD_BLK), lambda b, h, i: (b, h)),
            pl.BlockSpec((PAST_LEN, 2 * MLA_V), lambda b, h, i: (b, h)),
        ]
        args += [kc, vc]
    return pl.pallas_call(
        functools.partial(_attn_kernel, has_cache=has_cache),
        grid=(nb, MLA_HEADS // 2, nq),
        in_specs=in_specs,
        out_specs=pl.BlockSpec((TQ, 2 * MLA_V), lambda b, h, i: (b * nq + i, h)),
        out_shape=jax.ShapeDtypeStruct((nb * n, MLA_HEADS * MLA_V), F32),
        compiler_params=_cparams(("parallel", "parallel", "parallel")),
        name="attn_cache" if has_cache else "attn_ctx",
    )(*args)


def _outproj_kernel(x_ref, og_ref, oc_ref, om_ref, mod_ref, nw_ref, wo_ref, wr_ref, br_ref,
                    xo_ref, h_ref, route_ref, cnt_ref, carry_s):
    @pl.when(pl.program_id(0) == 0)
    def _():
        carry_s[...] = jnp.zeros(carry_s.shape, F32)

    attn = (_dot(og_ref[...].astype(BF16), wo_ref[0:GLA_W, :])
            + _dot(oc_ref[...].astype(BF16), wo_ref[GLA_W:GLA_W + CONV_W, :])
            + _dot(om_ref[...].astype(BF16), wo_ref[GLA_W + CONV_W:D_MODEL, :]))
    mod = mod_ref[...]
    xn = x_ref[...] + mod[:, 2 * D_MODEL:3 * D_MODEL] * attn
    xo_ref[...] = xn
    h = _rms(xn, nw_ref[...]) * (1.0 + mod[:, 4 * D_MODEL:5 * D_MODEL]) + mod[:, 3 * D_MODEL:4 * D_MODEL]
    h_ref[...] = h
    logits = _dot3(h, wr_ref[...]) + br_ref[...]
    lane = lax.broadcasted_iota(jnp.int32, logits.shape, 1).astype(F32)
    big = 1e9
    gl = jnp.where((lane >= N_EXPERTS) & (lane < N_EXPERTS + N_GROUPS), logits, NEG)
    gmax = jnp.max(gl, axis=-1, keepdims=True)
    gw = 1.0 / jnp.sum(jnp.exp(gl - gmax), axis=-1, keepdims=True)
    gi = jnp.min(jnp.where(gl == gmax, lane, big), axis=-1, keepdims=True) - N_EXPERTS
    lo = gi * EXPERTS_PER_GROUP
    el = jnp.where((lane >= lo) & (lane < lo + EXPERTS_PER_GROUP), logits, NEG)
    m1 = jnp.max(el, axis=-1, keepdims=True)
    i1 = jnp.min(jnp.where(el == m1, lane, big), axis=-1, keepdims=True)
    el2 = jnp.where(lane == i1, NEG, el)
    m2 = jnp.max(el2, axis=-1, keepdims=True)
    i2 = jnp.min(jnp.where(el2 == m2, lane, big), axis=-1, keepdims=True)
    e2 = jnp.exp(m2 - m1)
    p1 = 1.0 / (1.0 + e2)
    onehot = jnp.where((lane == i1) | (lane == i2), 1.0, 0.0)
    tr = lax.broadcasted_iota(jnp.int32, (TM, TM), 0)
    tc = lax.broadcasted_iota(jnp.int32, (TM, TM), 1)
    before = (tc < tr).astype(F32).astype(BF16)
    seen = _dot(before, onehot.astype(BF16)) + carry_s[0:1, :]
    rank1 = jnp.sum(jnp.where(lane == i1, seen, 0.0), axis=-1, keepdims=True)
    rank2 = jnp.sum(jnp.where(lane == i2, seen, 0.0), axis=-1, keepdims=True)
    carry_s[...] = carry_s[...] + jnp.sum(onehot, axis=0, keepdims=True)
    cnt_ref[...] = carry_s[...]
    cols = (i1, i2, gw * p1, gw * (e2 * p1), rank1, rank2)
    route = jnp.zeros(logits.shape, F32)
    for j, col in enumerate(cols):
        route = jnp.where(lane == j, col, route)
    route_ref[...] = route


def _outproj(x, og, oc, om, mods4, l, norm_w, wo, wr, br):
    full = lambda shape: pl.BlockSpec(shape, lambda i: (0,) * len(shape))
    row = lambda w: pl.BlockSpec((TM, w), lambda i: (i, 0))
    return pl.pallas_call(
        _outproj_kernel,
        grid=(N_TOK // TM,),
        in_specs=[
            row(D_MODEL), row(GLA_W), row(CONV_W), row(MLA_HEADS * MLA_V),
            pl.BlockSpec((None, None, 1, 6 * D_MODEL), lambda i: (l, _mod_index(i, TM), 0, 0)),
            full((1, D_MODEL)), full((D_MODEL, D_MODEL)), full((D_MODEL, 128)), full((1, 128)),
        ],
        out_specs=[row(D_MODEL), row(D_MODEL), row(128), full((8, 128))],
        out_shape=[
            jax.ShapeDtypeStruct((N_TOK, D_MODEL), F32),
            jax.ShapeDtypeStruct((N_TOK, D_MODEL), F32),
            jax.ShapeDtypeStruct((N_TOK, 128), F32),
            jax.ShapeDtypeStruct((8, 128), F32),
        ],
        scratch_shapes=[pltpu.VMEM((8, 128), F32)],
        compiler_params=_cparams(("arbitrary",)),
        name="outproj_router",
    )(x, og, oc, om, mods4, norm_w, wo, wr, br)


N_ASSIGN = 2 * N_TOK
TE = 256
N_ETILE = N_ASSIGN // TE
N_WORK = N_ETILE + N_EXPERTS


def _lane_col(x, lane, j):
    return jnp.sum(jnp.where(lane == j, x, 0.0), axis=-1, keepdims=True)


def _plan_kernel(cnt_s, cnt_ref, route_ref, pos_ref, wt_ref, we_ref, wlo_ref, whi_ref, wf_ref, off_s):
    cnt = cnt_ref[...]
    chi = jnp.floor(cnt * (1.0 / 128.0))
    clo = cnt - chi * 128.0
    r = lax.broadcasted_iota(jnp.int32, (128, 128), 0)
    c = lax.broadcasted_iota(jnp.int32, (128, 128), 1)
    below = (r < c).astype(F32).astype(BF16)
    off = (128.0 * _dot(chi.astype(BF16), below) + _dot(clo.astype(BF16), below))[0:1, :]
    route = route_ref[...]
    lane = lax.broadcasted_iota(jnp.int32, route.shape, 1).astype(F32)
    pos1 = _lane_col(off, lane, _lane_col(route, lane, 0.0)) + _lane_col(route, lane, 4.0)
    pos2 = _lane_col(off, lane, _lane_col(route, lane, 1.0)) + _lane_col(route, lane, 5.0)
    pos_ref[...] = jnp.where(lane == 0.0, pos1, jnp.where(lane == 1.0, pos2, 0.0))

    @pl.when(pl.program_id(0) == 0)
    def _():
        def offs(e, acc):
            off_s[e] = acc
            return acc + cnt_s[e]

        off_s[N_EXPERTS] = lax.fori_loop(0, N_EXPERTS, offs, jnp.int32(0))

        def item(w, st):
            j, e, first = st
            active = j < N_ETILE
            jj = jnp.minimum(j, N_ETILE - 1)
            ee = jnp.minimum(e, N_EXPERTS - 1)
            tlo = jj * TE
            thi = tlo + TE
            end = off_s[ee + 1]
            wt_ref[w] = jj
            we_ref[w] = ee
            wlo_ref[w] = jnp.where(active, jnp.maximum(off_s[ee], tlo) - tlo, 0)
            whi_ref[w] = jnp.where(active, jnp.minimum(end, thi) - tlo, 0)
            wf_ref[w] = jnp.where(active, first, 0)
            adv_j = (active & (end >= thi)).astype(jnp.int32)
            adv_e = (active & (end <= thi)).astype(jnp.int32)
            return j + adv_j, e + adv_e, adv_j

        lax.fori_loop(0, N_WORK, item, (jnp.int32(0), jnp.int32(0), jnp.int32(1)))


def _plan(cnt_i, cnt, route):
    smem = pl.BlockSpec(memory_space=pltpu.SMEM)
    work = jax.ShapeDtypeStruct((N_WORK,), jnp.int32)
    return pl.pallas_call(
        _plan_kernel,
        grid=(N_TOK // TM,),
        in_specs=[smem, pl.BlockSpec((8, 128), lambda i: (0, 0)), pl.BlockSpec((TM, 128), lambda i: (i, 0))],
        out_specs=[pl.BlockSpec((TM, 128), lambda i: (i, 0)), smem, smem, smem, smem, smem],
        out_shape=[jax.ShapeDtypeStruct((N_TOK, 128), F32), work, work, work, work, work],
        scratch_shapes=[pltpu.SMEM((N_EXPERTS + 1,), jnp.int32)],
        compiler_params=_cparams(("arbitrary",)),
        name="moe_plan",
    )(cnt_i, cnt, route)


def _dispatch_kernel(pos_s, h_hbm, xs_hbm, sem):
    base = pl.program_id(0) * TM

    def copy(i, k):
        t = base + i
        return pltpu.make_async_copy(h_hbm.at[pl.ds(t, 1)], xs_hbm.at[pl.ds(pos_s[2 * t + k], 1)], sem)

    def start(i, carry):
        copy(i, 0).start()
        copy(i, 1).start()
        return carry

    def wait(i, carry):
        copy(i, 0).wait()
        copy(i, 1).wait()
        return carry

    lax.fori_loop(0, TM, start, 0, unroll=8)
    lax.fori_loop(0, TM, wait, 0, unroll=8)


def _dispatch(pos, h):
    any_spec = pl.BlockSpec(memory_space=pl.ANY)
    return pl.pallas_call(
        _dispatch_kernel,
        grid_spec=pltpu.PrefetchScalarGridSpec(
            num_scalar_prefetch=1, grid=(N_TOK // TM,), in_specs=[any_spec], out_specs=any_spec,
            scratch_shapes=[pltpu.SemaphoreType.DMA(())]),
        out_shape=jax.ShapeDtypeStruct((N_ASSIGN, D_MODEL), F32),
        compiler_params=_cparams(("arbitrary",)),
        name="moe_dispatch",
    )(pos, h)


def _experts_kernel(wt_s, we_s, wlo_s, whi_s, wf_s, xs_ref, w1_ref, w3_ref, w2_ref, o_ref):
    w = pl.program_id(0)
    x = xs_ref[...].astype(BF16)
    a = _dot(x, w1_ref[...].astype(BF16))
    b = _dot(x, w3_ref[...].astype(BF16))
    y = _dot((a * _sigmoid(a) * b).astype(BF16), w2_ref[...].astype(BF16))
    row = lax.broadcasted_iota(jnp.int32, (TE, 1), 0)
    mine = (row >= wlo_s[w]) & (row < whi_s[w])

    @pl.when(wf_s[w] == 1)
    def _():
        o_ref[...] = jnp.where(mine, y, 0.0)

    @pl.when(wf_s[w] == 0)
    def _():
        o_ref[...] = jnp.where(mine, y, o_ref[...])


def _experts(work, xs, l, w1, w3, w2):
    tile = lambda w, wt, we, wlo, whi, wf: (wt[w], 0)
    wspec = lambda shape: pl.BlockSpec((None, None) + shape, lambda w, wt, we, wlo, whi, wf: (l, we[w], 0, 0))
    return pl.pallas_call(
        _experts_kernel,
        grid_spec=pltpu.PrefetchScalarGridSpec(
            num_scalar_prefetch=5, grid=(N_WORK,),
            in_specs=[pl.BlockSpec((TE, D_MODEL), tile), wspec((D_MODEL, EXPERT_FF)),
                      wspec((D_MODEL, EXPERT_FF)), wspec((EXPERT_FF, D_MODEL))],
            out_specs=pl.BlockSpec((TE, D_MODEL), tile)),
        out_shape=jax.ShapeDtypeStruct((N_ASSIGN, D_MODEL), F32),
        compiler_params=_cparams(("arbitrary",)),
        name="moe_experts",
    )(*work, xs, w1, w3, w2)


def _combine_kernel(pos_s, x_ref, route_ref, mod_ref, fw_ref, ys_hbm, o_ref, y1_s, y2_s, sem, *, final):
    base = pl.program_id(0) * TM

    def copy(i, k):
        dst = y1_s if k == 0 else y2_s
        return pltpu.make_async_copy(ys_hbm.at[pl.ds(pos_s[2 * (base + i) + k], 1)], dst.at[pl.ds(i, 1)], sem)

    def start(i, carry):
        copy(i, 0).start()
        copy(i, 1).start()
        return carry

    def wait(i, carry):
        copy(i, 0).wait()
        copy(i, 1).wait()
        return carry

    lax.fori_loop(0, TM, start, 0, unroll=8)
    lax.fori_loop(0, TM, wait, 0, unroll=8)
    route = route_ref[...]
    lane = lax.broadcasted_iota(jnp.int32, route.shape, 1)
    moe = _lane_col(route, lane, 2) * y1_s[...] + _lane_col(route, lane, 3) * y2_s[...]
    x = x_ref[...] + mod_ref[:, 5 * D_MODEL:6 * D_MODEL] * moe
    o_ref[...] = _rms(x, fw_ref[...]) if final else x


def _combine(pos, x, route, mods4, l, fw, ys, final):
    row = lambda w: pl.BlockSpec((TM, w), lambda i, p: (i, 0))
    return pl.pallas_call(
        functools.partial(_combine_kernel, final=final),
        grid_spec=pltpu.PrefetchScalarGridSpec(
            num_scalar_prefetch=1, grid=(N_TOK // TM,),
            in_specs=[row(D_MODEL), row(128),
                      pl.BlockSpec((None, None, 1, 6 * D_MODEL), lambda i, p: (l, _mod_index(i, TM), 0, 0)),
                      pl.BlockSpec((1, D_MODEL), lambda i, p: (0, 0)),
                      pl.BlockSpec(memory_space=pl.ANY)],
            out_specs=row(D_MODEL),
            scratch_shapes=[pltpu.VMEM((TM, D_MODEL), F32), pltpu.VMEM((TM, D_MODEL), F32),
                            pltpu.SemaphoreType.DMA(())]),
        out_shape=jax.ShapeDtypeStruct((N_TOK, D_MODEL), F32),
        compiler_params=_cparams(("arbitrary",)),
        name="moe_combine_final" if final else "moe_combine",
    )(pos, x, route, mods4, fw, ys)


def _rope_swap(w):
    nf = MLA_ROPE // 4
    g = w.reshape(w.shape[:-1] + (2, 2, nf))
    return jnp.stack([-g[..., 1, :], g[..., 0, :]], axis=-2).reshape(w.shape)


def _rope_tables():
    nf = MLA_ROPE // 4
    pos = np.arange(DEC_SEQ)
    row = (pos // GRID_W).astype(np.float32)
    col = (pos % GRID_W).astype(np.float32)
    inv = (np.float32(ROPE_THETA) ** (-np.arange(nf, dtype=np.float32) / np.float32(nf))).astype(np.float32)
    ar = (row[:, None] * inv).astype(np.float32)
    ac = (col[:, None] * inv).astype(np.float32)
    cos = np.concatenate([np.cos(ar), np.cos(ar), np.cos(ac), np.cos(ac)], axis=1)
    sin = np.concatenate([np.sin(ar), np.sin(ar), np.sin(ac), np.sin(ac)], axis=1)
    cos = np.concatenate([np.ones((TM, MLA_ROPE)), cos], axis=0).astype(np.float32)
    sin = np.concatenate([np.zeros((TM, MLA_ROPE)), sin], axis=0).astype(np.float32)
    n = cos.shape[0]
    z32 = np.zeros((n, 32), np.float32)
    tqc = np.concatenate([np.ones((n, MLA_NOPE), np.float32), cos, z32], axis=1)
    tqs = np.concatenate([np.zeros((n, MLA_NOPE), np.float32), sin, z32], axis=1)
    tk = np.concatenate([z32, cos, sin, z32], axis=1)
    return jnp.asarray(tqc), jnp.asarray(tqs), jnp.asarray(tk)


def _prep_layer(l, w_in, gla_wg_f, gla_wg_b, mla_wuq, mla_wukv, w_out, moe_wg, moe_bg, moe_we, moe_be):
    offs = np.cumsum([0, 256, 256, 256, 256, 16, 16, 512, 256, 128, 32])
    seg = lambda i: w_in[l][:, offs[i]:offs[i + 1]]
    kr = seg(9)
    w_in_r = jnp.concatenate(
        [seg(0), seg(1), seg(2), seg(3), seg(6), seg(7), seg(8), seg(4), seg(5), kr, _rope_swap(kr),
         jnp.zeros((D_MODEL, 32), F32)], axis=1).astype(BF16)
    wgf = jnp.zeros((128, GLA_W), F32).at[0:16].set(gla_wg_f[l])
    wgb = jnp.zeros((128, GLA_W), F32).at[16:32].set(gla_wg_b[l])
    wq = mla_wuq[l].reshape(MLA_Q_RANK, MLA_HEADS, MLA_NOPE + MLA_ROPE)
    nope, rope = wq[..., :MLA_NOPE], wq[..., MLA_NOPE:]
    z = lambda n: jnp.zeros((MLA_Q_RANK, MLA_HEADS, n), F32)
    hw = MLA_HEADS * HEAD_BLK
    wa = jnp.concatenate([nope, rope, z(32)], axis=-1).reshape(MLA_Q_RANK, hw).astype(BF16)
    wb = jnp.concatenate([z(MLA_NOPE), _rope_swap(rope), z(32)], axis=-1).reshape(MLA_Q_RANK, hw).astype(BF16)
    wkv = mla_wukv[l].reshape(MLA_KV_RANK, MLA_HEADS, MLA_NOPE + MLA_V)
    wk = jnp.concatenate([wkv[..., :MLA_NOPE], jnp.zeros((MLA_KV_RANK, MLA_HEADS, 64), F32)],
                         axis=-1).reshape(MLA_KV_RANK, hw).astype(BF16)
    wv = wkv[..., MLA_NOPE:].reshape(MLA_KV_RANK, MLA_HEADS * MLA_V).astype(BF16)
    wp_np = np.zeros((128, MLA_HEADS, HEAD_BLK), np.float32)
    for j in range(MLA_ROPE):
        wp_np[32 + j, :, MLA_NOPE + j] = 1.0
        wp_np[64 + j, :, MLA_NOPE + j] = 1.0
    wp = jnp.asarray(wp_np.reshape(128, hw)).astype(BF16)
    wo = w_out[l].astype(BF16)
    wr = jnp.concatenate([moe_we[l], moe_wg[l], jnp.zeros((D_MODEL, 128 - N_EXPERTS - N_GROUPS), F32)], axis=1)
    br = jnp.concatenate([moe_be[l], moe_bg[l], jnp.zeros((128 - N_EXPERTS - N_GROUPS,), F32)]).reshape(1, 128)
    return dict(w_in_r=w_in_r, wgf=wgf, wgb=wgb, wa=wa, wb=wb, wk=wk, wv=wv, wp=wp, wo=wo, wr=wr, br=br)


def _state_to_blockdiag(s):
    b = s.shape[0]
    st = jnp.transpose(s, (0, 1, 3, 2))
    eye = jnp.eye(GLA_HEADS, dtype=s.dtype)
    out = st[:, :, :, None, :] * eye[None, :, None, :, None]
    return out.reshape(b, GLA_W, GLA_W)


def _blockdiag_to_state(st):
    b = st.shape[0]
    blocks = st.reshape(b, GLA_HEADS, GLA_DV, GLA_HEADS, GLA_DK)
    diag = jnp.stack([blocks[:, h, :, h, :] for h in range(GLA_HEADS)], axis=1)
    return jnp.transpose(diag, (0, 1, 3, 2))


def kernel(x_prompt, x_sample, cache_ckv, cache_krope, state_gla_fwd, state_gla_bwd, c, c_ctx, ada_w, ada_b,
           norm1_w, norm2_w, w_in, gla_wg_f, gla_bg_f, gla_wg_b, gla_bg_b, gla_norm_w, conv_w, conv_b,
           conv_ln_w, conv_ln_b, mla_qnorm_w, mla_wuq, mla_kvnorm_w, mla_wukv, w_out, moe_wg, moe_bg, moe_we,
           moe_be, moe_w1, moe_w3, moe_w2, final_norm_w):
    x = jnp.concatenate([x_prompt.reshape(N_CTX, D_MODEL), x_sample.reshape(N_SMP, D_MODEL)], axis=0)
    cond8 = jnp.concatenate([c_ctx[None, :], c, jnp.zeros((N_MOD - 1 - DEC_BATCH, D_MODEL), F32)], axis=0)
    mods4 = _ada_table(cond8, ada_w, ada_b).reshape(DEPTH, N_MOD, 1, 6 * D_MODEL)
    tabs = _rope_tables()
    row1 = lambda a: a.reshape(1, -1)

    ckv_new, kr_new, sf_new, sb_new = [], [], [], []
    for l in range(DEPTH):
        w = _prep_layer(l, w_in, gla_wg_f, gla_wg_b, mla_wuq, mla_wukv, w_out, moe_wg, moe_bg, moe_we, moe_be)
        p = _inproj(x, mods4, l, row1(norm1_w[l]), w["w_in_r"])

        gla_args = (p, w["wgf"], w["wgb"], row1(gla_bg_f[l]), row1(gla_bg_b[l]), row1(gla_norm_w[l]))
        og_c, sf, sb = _gla(*gla_args, SEQ, BATCH, 0)
        og_s = _gla(*gla_args, DEC_SEQ, DEC_BATCH, N_CTX // DEC_SEQ,
                    _state_to_blockdiag(state_gla_fwd[:, l]), _state_to_blockdiag(state_gla_bwd[:, l]))[0]
        og = jnp.concatenate([og_c, og_s], axis=0)

        cw = jnp.concatenate([conv_w[l], jnp.zeros((1, CONV_W), F32)], axis=0)
        conv_args = (p, cw, row1(conv_b[l]), row1(conv_ln_w[l]), row1(conv_ln_b[l]))
        oc = jnp.concatenate([_conv(*conv_args, SEQ, BATCH, 0),
                              _conv(*conv_args, DEC_SEQ, DEC_BATCH, N_CTX // DEC_SEQ)], axis=0)

        q, k, v, ckvn, kr = _mla_prep(p, tabs, row1(mla_qnorm_w[l]), row1(mla_kvnorm_w[l]),
                                      w["wa"], w["wb"], w["wk"], w["wp"], w["wv"])
        cc = cache_ckv[:, l].reshape(DEC_BATCH * PAST_LEN, MLA_KV_RANK)
        ckr = cache_krope[:, l].reshape(DEC_BATCH * PAST_LEN, MLA_ROPE)
        ckr = jnp.pad(ckr, ((0, 0), (32, 64)))
        kc, vc = _cache_kv(cc, ckr, w["wk"], w["wp"], w["wv"])
        om = jnp.concatenate([_attention(q, k, v, SEQ, BATCH, 0),
                              _attention(q, k, v, DEC_SEQ, DEC_BATCH, N_CTX, kc, vc)], axis=0)

        xn, h2, route, cnt = _outproj(x, og, oc, om, mods4, l, row1(norm2_w[l]), w["wo"], w["wr"], w["br"])
        posf, *work = _plan(cnt[0, :N_EXPERTS].astype(jnp.int32), cnt, route)
        pos = posf[:, 0:2].astype(jnp.int32).reshape(N_ASSIGN)
        ys = _experts(work, _dispatch(pos, h2), l, moe_w1, moe_w3, moe_w2)
        x = _combine(pos, xn, route, mods4, l, row1(final_norm_w), ys, l == DEPTH - 1)

        ckv_new.append(ckvn[:N_CTX].reshape(BATCH, SEQ, MLA_KV_RANK))
        kr_new.append(kr[:N_CTX, 32:32 + MLA_ROPE].reshape(BATCH, SEQ, MLA_ROPE))
        sf_new.append(_blockdiag_to_state(sf))
        sb_new.append(_blockdiag_to_state(sb))

    y = x
    return (y[:N_CTX].reshape(BATCH, SEQ, D_MODEL), y[N_CTX:].reshape(DEC_BATCH, DEC_SEQ, D_MODEL),
            jnp.stack(ckv_new, axis=1), jnp.stack(kr_new, axis=1),
            jnp.stack(sf_new, axis=1), jnp.stack(sb_new, axis=1))
```

```python
import functools

import numpy as np
import jax
import jax.numpy as jnp
from jax import lax
from jax.experimental import pallas as pl
from jax.experimental.pallas import tpu as pltpu

F32 = jnp.float32
BF16 = jnp.bfloat16

D_MODEL = 1024
BATCH = 16
SEQ = 256
DEPTH = 2
DEC_BATCH = 4
DEC_SEQ = 1024
PAST_LEN = 512
GRID_W = 64
NORM_EPS = 1e-6
GLA_HEADS = 4
GLA_DK = 64
GLA_DV = 64
GLA_W = 256
GLA_GATE_RANK = 16
GLA_TAU = 16.0
CONV_W = 256
CONV_K = 31
MLA_HEADS = 8
MLA_NOPE = 64
MLA_ROPE = 32
MLA_V = 64
MLA_Q_RANK = 256
MLA_KV_RANK = 128
ROPE_THETA = 10000.0
N_GROUPS = 4
EXPERTS_PER_GROUP = 8
N_EXPERTS = 32
EXPERT_FF = 256

N_CTX = BATCH * SEQ
N_SMP = DEC_BATCH * DEC_SEQ
N_TOK = N_CTX + N_SMP
N_MOD = 8
TM = 256
P_COLS = 2048
MISC_BLK = 15
CHUNK = 64
SUB = 16
NEG = -1e30
VMEM_LIMIT = 56 * 1024 * 1024


def _cparams(sem):
    return pltpu.CompilerParams(dimension_semantics=sem, vmem_limit_bytes=VMEM_LIMIT)


def _dot(a, b):
    return jnp.dot(a, b, preferred_element_type=F32)


def _dot_nt(a, b):
    return lax.dot_general(a, b, (((1,), (1,)), ((), ())), preferred_element_type=F32)


def _split2(x):
    hi = x.astype(BF16)
    lo = (x - hi.astype(F32)).astype(BF16)
    return hi, lo


def _dot3(a, w):
    ah, al = _split2(a)
    wh, wl = _split2(w)
    return _dot(ah, wh) + _dot(al, wh) + _dot(ah, wl)


def _sigmoid(x):
    return 1.0 / (1.0 + jnp.exp(-x))


def _rms(x, w):
    ms = jnp.mean(x * x, axis=-1, keepdims=True)
    return x * lax.rsqrt(ms + NORM_EPS) * w


def _mod_index(i, tile):
    nctx = N_CTX // tile
    per = DEC_SEQ // tile
    return jnp.where(i < nctx, 0, 1 + (i - nctx) // per)


def _ada_kernel(c_ref, w_ref, b_ref, o_ref):
    c = c_ref[...]
    o_ref[...] = _dot3(c * _sigmoid(c), w_ref[...]) + b_ref[...]


def _ada_table(cond8, ada_w, ada_b):
    tn = 1024
    n6 = 6 * D_MODEL
    return pl.pallas_call(
        _ada_kernel,
        grid=(DEPTH, n6 // tn),
        in_specs=[
            pl.BlockSpec((N_MOD, D_MODEL), lambda l, j: (0, 0)),
            pl.BlockSpec((None, D_MODEL, tn), lambda l, j: (l, 0, j)),
            pl.BlockSpec((None, 1, tn), lambda l, j: (l, 0, j)),
        ],
        out_specs=pl.BlockSpec((None, N_MOD, tn), lambda l, j: (l, 0, j)),
        out_shape=jax.ShapeDtypeStruct((DEPTH, N_MOD, n6), F32),
        compiler_params=_cparams(("parallel", "parallel")),
        name="ada_table",
    )(cond8, ada_w, ada_b.reshape(DEPTH, 1, n6))


def _inproj_kernel(x_ref, mod_ref, nw_ref, w_ref, o_ref):
    mod = mod_ref[...]
    h = _rms(x_ref[...], nw_ref[...]) * (1.0 + mod[:, D_MODEL:2 * D_MODEL]) + mod[:, 0:D_MODEL]
    o_ref[...] = _dot(h.astype(BF16), w_ref[...])


def _inproj(x, mods4, l, norm_w, w_in_r):
    return pl.pallas_call(
        _inproj_kernel,
        grid=(N_TOK // TM,),
        in_specs=[
            pl.BlockSpec((TM, D_MODEL), lambda i: (i, 0)),
            pl.BlockSpec((None, None, 1, 6 * D_MODEL), lambda i: (l, _mod_index(i, TM), 0, 0)),
            pl.BlockSpec((1, D_MODEL), lambda i: (0, 0)),
            pl.BlockSpec((D_MODEL, P_COLS), lambda i: (0, 0)),
        ],
        out_specs=pl.BlockSpec((TM, P_COLS), lambda i: (i, 0)),
        out_shape=jax.ShapeDtypeStruct((N_TOK, P_COLS), F32),
        compiler_params=_cparams(("parallel",)),
        name="inproj",
    )(x, mods4, norm_w, w_in_r)


def _log_sigmoid(x):
    return jnp.minimum(x, 0.0) - jnp.log(1.0 + jnp.exp(-jnp.abs(x)))


def _split3(x):
    hi = x.astype(BF16)
    r = x - hi.astype(F32)
    mid = r.astype(BF16)
    lo = (r - mid.astype(F32)).astype(BF16)
    return hi, mid, lo


def _gla_consts(reverse):
    t = lax.broadcasted_iota(jnp.int32, (CHUNK, CHUNK), 0)
    s = lax.broadcasted_iota(jnp.int32, (CHUNK, CHUNK), 1)
    mid = jnp.bitwise_and(t, -SUB) + SUB // 2
    if reverse:
        cum = s >= t
        ref = s >= mid
    else:
        cum = s <= t
        ref = s <= mid
    cum_ref = jnp.concatenate([cum, ref], axis=0).astype(F32).astype(BF16)
    rt = jnp.bitwise_and(lax.broadcasted_iota(jnp.int32, (GLA_HEADS * SUB, CHUNK), 0), SUB - 1)
    cs = lax.broadcasted_iota(jnp.int32, (GLA_HEADS * SUB, CHUNK), 1)
    rowi = lax.broadcasted_iota(jnp.int32, (CHUNK, 1), 0)
    causal, valid = [], []
    for i in range(CHUNK // SUB):
        tq = rt + i * SUB
        causal.append((cs >= tq) if reverse else (cs <= tq))
        valid.append((rowi >= i * SUB) if reverse else (rowi < (i + 1) * SUB))
    return cum_ref, causal, valid


def _gla_chunk(r0, qkvg_ref, la_ref, st_ref, o_ref, consts, head_masks, block_mask, reverse):
    cum_ref, causal, valid = consts
    rows = pl.ds(r0, CHUNK)
    q = qkvg_ref[rows, 0:GLA_W]
    k = qkvg_ref[rows, GLA_W:2 * GLA_W]
    v = qkvg_ref[rows, 2 * GLA_W:3 * GLA_W]
    la = la_ref[rows, :]
    h0, h1, h2 = _split3(la)
    br = _dot(cum_ref, h0) + _dot(cum_ref, h1) + _dot(cum_ref, h2)
    b = br[0:CHUNK]
    rr = br[CHUNK:2 * CHUNK]
    qs = q * (GLA_DK ** -0.5)
    qhat = qs * jnp.exp(b - rr)
    vb = v.astype(BF16)
    parts = []
    for i in range(CHUNK // SUB):
        r_i = rr[i * SUB:i * SUB + 1]
        kt = (k * jnp.exp(jnp.where(valid[i], r_i - b, NEG))).astype(BF16)
        qi = qhat[i * SUB:(i + 1) * SUB]
        qbig = jnp.concatenate([jnp.where(m, qi, 0.0) for m in head_masks], axis=0).astype(BF16)
        a = jnp.where(causal[i], _dot_nt(qbig, kt), 0.0)
        ov = _dot(a.astype(BF16), vb)
        oi = jnp.where(head_masks[0], ov[0:SUB], 0.0)
        for h in range(1, GLA_HEADS):
            oi = oi + jnp.where(head_masks[h], ov[h * SUB:(h + 1) * SUB], 0.0)
        parts.append(oi)
    o_intra = jnp.concatenate(parts, axis=0)
    st = st_ref[...]
    o_inter = _dot_nt((qs * jnp.exp(b)).astype(BF16), st.astype(BF16))
    o_ref[rows, :] = o_inter + o_intra
    b_last = b[0:1] if reverse else b[CHUNK - 1:CHUNK]
    kl = (k * jnp.exp(b_last - b)).astype(BF16)
    upd = _dot(v.T.astype(BF16), kl)
    st_ref[...] = st * jnp.exp(b_last) + jnp.where(block_mask, upd, 0.0)


def _gla_kernel(*refs, n, has_state):
    if has_state:
        (qkvg_ref, misc_ref, wgf_ref, wgb_ref, bgf_ref, bgb_ref, nw_ref, s0f_ref, s0b_ref,
         o_ref, laf_s, lab_s, of_s, ob_s, stf_s, stb_s) = refs
    else:
        (qkvg_ref, misc_ref, wgf_ref, wgb_ref, bgf_ref, bgb_ref, nw_ref,
         o_ref, sf_ref, sb_ref, laf_s, lab_s, of_s, ob_s, stf_s, stb_s) = refs
    misc = misc_ref[...]
    laf_s[...] = _log_sigmoid(_dot3(misc, wgf_ref[...]) + bgf_ref[...]) * (1.0 / GLA_TAU)
    lab_s[...] = _log_sigmoid(_dot3(misc, wgb_ref[...]) + bgb_ref[...]) * (1.0 / GLA_TAU)
    if has_state:
        stf_s[...] = s0f_ref[...]
        stb_s[...] = s0b_ref[...]
    else:
        stf_s[...] = jnp.zeros((GLA_W, GLA_W), F32)
        stb_s[...] = jnp.zeros((GLA_W, GLA_W), F32)

    lane = lax.broadcasted_iota(jnp.int32, (1, GLA_W), 1)
    head_masks = [jnp.right_shift(lane, 6) == h for h in range(GLA_HEADS)]
    bi = jnp.right_shift(lax.broadcasted_iota(jnp.int32, (GLA_W, GLA_W), 0), 6)
    bj = jnp.right_shift(lax.broadcasted_iota(jnp.int32, (GLA_W, GLA_W), 1), 6)
    block_mask = bi == bj
    consts_f = _gla_consts(False)
    consts_b = _gla_consts(True)
    nc = n // CHUNK

    def body(j, carry):
        _gla_chunk(pl.multiple_of(j * CHUNK, CHUNK), qkvg_ref, laf_s, stf_s, of_s,
                   consts_f, head_masks, block_mask, False)
        _gla_chunk(pl.multiple_of((nc - 1 - j) * CHUNK, CHUNK), qkvg_ref, lab_s, stb_s, ob_s,
                   consts_b, head_masks, block_mask, True)
        return carry

    lax.fori_loop(0, nc, body, 0)

    if not has_state:
        sf_ref[...] = stf_s[...]
        sb_ref[...] = stb_s[...]

    ones_blk = block_mask.astype(F32).astype(BF16)
    rt = 256
    for r0 in range(0, n, rt):
        o = of_s[r0:r0 + rt, :] + ob_s[r0:r0 + rt, :]
        hi, lo = _split2(o * o)
        ms = (_dot(hi, ones_blk) + _dot(lo, ones_blk)) * (1.0 / GLA_DV)
        g = qkvg_ref[r0:r0 + rt, 3 * GLA_W:4 * GLA_W]
        o_ref[r0:r0 + rt, :] = o * lax.rsqrt(ms + NORM_EPS) * nw_ref[...] * (g * _sigmoid(g))


def _gla(p, wgf, wgb, bgf, bgb, nw, n, nb, row_blk0, s0f=None, s0b=None):
    has_state = s0f is not None
    full = lambda shape: pl.BlockSpec(shape, lambda b: (0,) * len(shape))
    in_specs = [
        pl.BlockSpec((n, 4 * GLA_W), lambda b: (row_blk0 + b, 0)),
        pl.BlockSpec((n, 128), lambda b: (row_blk0 + b, MISC_BLK)),
        full((128, GLA_W)), full((128, GLA_W)), full((1, GLA_W)), full((1, GLA_W)), full((1, GLA_W)),
    ]
    args = [p, p, wgf, wgb, bgf, bgb, nw]
    st_spec = pl.BlockSpec((None, GLA_W, GLA_W), lambda b: (b, 0, 0))
    out_specs = [pl.BlockSpec((n, GLA_W), lambda b: (b, 0))]
    out_shape = [jax.ShapeDtypeStruct((nb * n, GLA_W), F32)]
    if has_state:
        in_specs += [st_spec, st_spec]
        args += [s0f, s0b]
    else:
        out_specs += [st_spec, st_spec]
        out_shape += [jax.ShapeDtypeStruct((nb, GLA_W, GLA_W), F32)] * 2
    return pl.pallas_call(
        functools.partial(_gla_kernel, n=n, has_state=has_state),
        grid=(nb,),
        in_specs=in_specs,
        out_specs=out_specs,
        out_shape=out_shape,
        scratch_shapes=[pltpu.VMEM((n, GLA_W), F32)] * 4 + [pltpu.VMEM((GLA_W, GLA_W), F32)] * 2,
        compiler_params=_cparams(("parallel",)),
        name="gla_state" if has_state else "gla_ctx",
    )(*args)


CONV_PAD = 16
CONV_ROWS = 128


def _conv_kernel(u_ref, w_ref, b_ref, lnw_ref, lnb_ref, o_ref, zp_s, *, n):
    u = u_ref[...]
    zp_s[0:CONV_PAD, :] = jnp.zeros((CONV_PAD, CONV_W), F32)
    zp_s[CONV_PAD + n:2 * CONV_PAD + n, :] = jnp.zeros((CONV_PAD, CONV_W), F32)
    zp_s[CONV_PAD:CONV_PAD + n, :] = u[:, 0:CONV_W] * _sigmoid(u[:, CONV_W:2 * CONV_W])
    off = CONV_PAD - CONV_K // 2
    for r0 in range(0, n, CONV_ROWS):
        acc = zp_s[r0 + off:r0 + off + CONV_ROWS, :] * w_ref[0:1, :]
        for j in range(1, CONV_K):
            acc = acc + zp_s[r0 + off + j:r0 + off + j + CONV_ROWS, :] * w_ref[j:j + 1, :]
        acc = acc + b_ref[...]
        mu = jnp.mean(acc, axis=-1, keepdims=True)
        xc = acc - mu
        var = jnp.mean(xc * xc, axis=-1, keepdims=True)
        y = xc * lax.rsqrt(var + NORM_EPS) * lnw_ref[...] + lnb_ref[...]
        o_ref[r0:r0 + CONV_ROWS, :] = y * _sigmoid(y)


def _conv(p, cw, cb, lnw, lnb, n, nb, row_blk0):
    full = lambda shape: pl.BlockSpec(shape, lambda b: (0,) * len(shape))
    return pl.pallas_call(
        functools.partial(_conv_kernel, n=n),
        grid=(nb,),
        in_specs=[
            pl.BlockSpec((n, 2 * CONV_W), lambda b: (row_blk0 + b, 2)),
            full((32, CONV_W)), full((1, CONV_W)), full((1, CONV_W)), full((1, CONV_W)),
        ],
        out_specs=pl.BlockSpec((n, CONV_W), lambda b: (b, 0)),
        out_shape=jax.ShapeDtypeStruct((nb * n, CONV_W), F32),
        scratch_shapes=[pltpu.VMEM((n + 2 * CONV_PAD, CONV_W), F32)],
        compiler_params=_cparams(("parallel",)),
        name="conv%d" % n,
    )(p, cw, cb, lnw, lnb)


HEAD_BLK = 128


def _kv_expand(ckvn, t, wk_ref, wp_ref, wv_ref, k_ref, v_ref):
    cb = ckvn.astype(BF16)
    th, tl = _split2(t)
    k_ref[...] = (_dot(cb, wk_ref[...]) + _dot(th, wp_ref[...]) + _dot(tl, wp_ref[...])).astype(BF16)
    v_ref[...] = _dot(cb, wv_ref[...]).astype(BF16)


def _mla_prep_kernel(cq_ref, cm_ref, tqc_ref, tqs_ref, tk_ref, qnw_ref, kvw_ref, wa_ref, wb_ref,
                     wk_ref, wp_ref, wv_ref, q_ref, k_ref, v_ref, ckv_ref, kr_ref):
    qn = _rms(cq_ref[...], qnw_ref[...]).astype(BF16)
    tqc = jnp.concatenate([tqc_ref[...]] * MLA_HEADS, axis=1)
    tqs = jnp.concatenate([tqs_ref[...]] * MLA_HEADS, axis=1)
    scale = (MLA_NOPE + MLA_ROPE) ** -0.5
    q_ref[...] = ((_dot(qn, wa_ref[...]) * tqc + _dot(qn, wb_ref[...]) * tqs) * scale).astype(BF16)
    cm = cm_ref[...]
    ckvn = _rms(cm[:, 0:MLA_KV_RANK], kvw_ref[...])
    ckv_ref[...] = ckvn
    t = cm[:, MLA_KV_RANK:2 * MLA_KV_RANK] * tk_ref[...]
    kr_ref[...] = t
    _kv_expand(ckvn, t, wk_ref, wp_ref, wv_ref, k_ref, v_ref)


def _rope_tile(i):
    nctx = N_CTX // TM
    return jnp.where(i < nctx, 0, 1 + (i - nctx) % (DEC_SEQ // TM))


def _mla_prep(p, tabs, qnw, kvw, wa, wb, wk, wp, wv):
    tqc, tqs, tk = tabs
    full = lambda shape: pl.BlockSpec(shape, lambda i: (0,) * len(shape))
    tab_spec = pl.BlockSpec((TM, HEAD_BLK), lambda i: (_rope_tile(i), 0))
    hw = MLA_HEADS * HEAD_BLK
    return pl.pallas_call(
        _mla_prep_kernel,
        grid=(N_TOK // TM,),
        in_specs=[
            pl.BlockSpec((TM, MLA_Q_RANK), lambda i: (i, 6)),
            pl.BlockSpec((TM, 2 * MLA_KV_RANK), lambda i: (i, 7)),
            tab_spec, tab_spec, tab_spec,
            full((1, MLA_Q_RANK)), full((1, MLA_KV_RANK)),
            full((MLA_Q_RANK, hw)), full((MLA_Q_RANK, hw)),
            full((MLA_KV_RANK, hw)), full((128, hw)), full((MLA_KV_RANK, MLA_HEADS * MLA_V)),
        ],
        out_specs=[
            pl.BlockSpec((TM, hw), lambda i: (i, 0)),
            pl.BlockSpec((TM, hw), lambda i: (i, 0)),
            pl.BlockSpec((TM, MLA_HEADS * MLA_V), lambda i: (i, 0)),
            pl.BlockSpec((TM, MLA_KV_RANK), lambda i: (i, 0)),
            pl.BlockSpec((TM, 128), lambda i: (i, 0)),
        ],
        out_shape=[
            jax.ShapeDtypeStruct((N_TOK, hw), BF16),
            jax.ShapeDtypeStruct((N_TOK, hw), BF16),
            jax.ShapeDtypeStruct((N_TOK, MLA_HEADS * MLA_V), BF16),
            jax.ShapeDtypeStruct((N_TOK, MLA_KV_RANK), F32),
            jax.ShapeDtypeStruct((N_TOK, 128), F32),
        ],
        compiler_params=_cparams(("parallel",)),
        name="mla_prep",
    )(p, p, tqc, tqs, tk, qnw, kvw, wa, wb, wk, wp, wv)


def _cache_kv_kernel(ckv_ref, kr_ref, wk_ref, wp_ref, wv_ref, k_ref, v_ref):
    _kv_expand(ckv_ref[...], kr_ref[...], wk_ref, wp_ref, wv_ref, k_ref, v_ref)


def _cache_kv(ckv, kr, wk, wp, wv):
    n = DEC_BATCH * PAST_LEN
    full = lambda shape: pl.BlockSpec(shape, lambda i: (0,) * len(shape))
    hw = MLA_HEADS * HEAD_BLK
    return pl.pallas_call(
        _cache_kv_kernel,
        grid=(n // PAST_LEN,),
        in_specs=[
            pl.BlockSpec((PAST_LEN, MLA_KV_RANK), lambda i: (i, 0)),
            pl.BlockSpec((PAST_LEN, 128), lambda i: (i, 0)),
            full((MLA_KV_RANK, hw)), full((128, hw)), full((MLA_KV_RANK, MLA_HEADS * MLA_V)),
        ],
        out_specs=[
            pl.BlockSpec((PAST_LEN, hw), lambda i: (i, 0)),
            pl.BlockSpec((PAST_LEN, MLA_HEADS * MLA_V), lambda i: (i, 0)),
        ],
        out_shape=[
            jax.ShapeDtypeStruct((n, hw), BF16),
            jax.ShapeDtypeStruct((n, MLA_HEADS * MLA_V), BF16),
        ],
        compiler_params=_cparams(("parallel",)),
        name="cache_kv",
    )(ckv, kr, wk, wp, wv)


TQ = 256


def _attn_kernel(*refs, has_cache):
    if has_cache:
        q_ref, k_ref, v_ref, kc_ref, vc_ref, o_ref = refs
    else:
        q_ref, k_ref, v_ref, o_ref = refs
    outs = []
    for hh in range(2):
        cols = slice(hh * HEAD_BLK, (hh + 1) * HEAD_BLK)
        q = q_ref[:, cols]
        s = _dot_nt(q, k_ref[:, cols])
        m = jnp.max(s, axis=-1, keepdims=True)
        if has_cache:
            sc = _dot_nt(q, kc_ref[:, cols])
            m = jnp.maximum(m, jnp.max(sc, axis=-1, keepdims=True))
            pc = jnp.exp(sc - m)
        p = jnp.exp(s - m)
        den = jnp.sum(p, axis=-1, keepdims=True)
        o = _dot(p.astype(BF16), v_ref[...])
        if has_cache:
            den = den + jnp.sum(pc, axis=-1, keepdims=True)
            o = o + _dot(pc.astype(BF16), vc_ref[...])
        outs.append(o / den)
    lane = lax.broadcasted_iota(jnp.int32, (1, 2 * MLA_V), 1)
    o_ref[...] = jnp.where(lane < MLA_V, outs[0], outs[1])


def _attention(q, k, v, n, nb, row0, kc=None, vc=None):
    has_cache = kc is not None
    nq = n // TQ
    rb0 = row0 // TQ
    kb0 = row0 // n
    in_specs = [
        pl.BlockSpec((TQ, 2 * HEAD_BLK), lambda b, h, i: (rb0 + b * nq + i, h)),
        pl.BlockSpec((n, 2 * HEAD_BLK), lambda b, h, i: (kb0 + b, h)),
        pl.BlockSpec((n, 2 * MLA_V), lambda b, h, i: (kb0 + b, h)),
    ]
    args = [q, k, v]
    if has_cache:
        in_specs += [
            pl.BlockSpec((PAST_LEN, 2 * HEAD_BLK), lambda b, h, i: (b, h)),
            pl.BlockSpec((PAST_LEN, 2 * MLA_V), lambda b, h, i: (b, h)),
        ]
        args += [kc, vc]
    return pl.pallas_call(
        functools.partial(_attn_kernel, has_cache=has_cache),
        grid=(nb, MLA_HEADS // 2, nq),
        in_specs=in_specs,
        out_specs=pl.BlockSpec((TQ, 2 * MLA_V), lambda b, h, i: (b * nq + i, h)),
        out_shape=jax.ShapeDtypeStruct((nb * n, MLA_HEADS * MLA_V), F32),
        compiler_params=_cparams(("parallel", "parallel", "parallel")),
        name="attn_cache" if has_cache else "attn_ctx",
    )(*args)


def _outproj_kernel(x_ref, og_ref, oc_ref, om_ref, mod_ref, nw_ref, wo_ref, wr_ref, br_ref,
                    xo_ref, h_ref, route_ref, cnt_ref, carry_s):
    @pl.when(pl.program_id(0) == 0)
    def _():
        carry_s[...] = jnp.zeros(carry_s.shape, F32)

    attn = (_dot(og_ref[...].astype(BF16), wo_ref[0:GLA_W, :])
            + _dot(oc_ref[...].astype(BF16), wo_ref[GLA_W:GLA_W + CONV_W, :])
            + _dot(om_ref[...].astype(BF16), wo_ref[GLA_W + CONV_W:D_MODEL, :]))
    mod = mod_ref[...]
    xn = x_ref[...] + mod[:, 2 * D_MODEL:3 * D_MODEL] * attn
    xo_ref[...] = xn
    h = _rms(xn, nw_ref[...]) * (1.0 + mod[:, 4 * D_MODEL:5 * D_MODEL]) + mod[:, 3 * D_MODEL:4 * D_MODEL]
    h_ref[...] = h
    logits = _dot3(h, wr_ref[...]) + br_ref[...]
    lane = lax.broadcasted_iota(jnp.int32, logits.shape, 1).astype(F32)
    big = 1e9
    gl = jnp.where((lane >= N_EXPERTS) & (lane < N_EXPERTS + N_GROUPS), logits, NEG)
    gmax = jnp.max(gl, axis=-1, keepdims=True)
    gw = 1.0 / jnp.sum(jnp.exp(gl - gmax), axis=-1, keepdims=True)
    gi = jnp.min(jnp.where(gl == gmax, lane, big), axis=-1, keepdims=True) - N_EXPERTS
    lo = gi * EXPERTS_PER_GROUP
    el = jnp.where((lane >= lo) & (lane < lo + EXPERTS_PER_GROUP), logits, NEG)
    m1 = jnp.max(el, axis=-1, keepdims=True)
    i1 = jnp.min(jnp.where(el == m1, lane, big), axis=-1, keepdims=True)
    el2 = jnp.where(lane == i1, NEG, el)
    m2 = jnp.max(el2, axis=-1, keepdims=True)
    i2 = jnp.min(jnp.where(el2 == m2, lane, big), axis=-1, keepdims=True)
    e2 = jnp.exp(m2 - m1)
    p1 = 1.0 / (1.0 + e2)
    onehot = jnp.where((lane == i1) | (lane == i2), 1.0, 0.0)
    tr = lax.broadcasted_iota(jnp.int32, (TM, TM), 0)
    tc = lax.broadcasted_iota(jnp.int32, (TM, TM), 1)
    before = (tc < tr).astype(F32).astype(BF16)
    seen = _dot(before, onehot.astype(BF16)) + carry_s[0:1, :]
    rank1 = jnp.sum(jnp.where(lane == i1, seen, 0.0), axis=-1, keepdims=True)
    rank2 = jnp.sum(jnp.where(lane == i2, seen, 0.0), axis=-1, keepdims=True)
    carry_s[...] = carry_s[...] + jnp.sum(onehot, axis=0, keepdims=True)
    cnt_ref[...] = carry_s[...]
    cols = (i1, i2, gw * p1, gw * (e2 * p1), rank1, rank2)
    route = jnp.zeros(logits.shape, F32)
    for j, col in enumerate(cols):
        route = jnp.where(lane == j, col, route)
    route_ref[...] = route


def _outproj(x, og, oc, om, mods4, l, norm_w, wo, wr, br):
    full = lambda shape: pl.BlockSpec(shape, lambda i: (0,) * len(shape))
    row = lambda w: pl.BlockSpec((TM, w), lambda i: (i, 0))
    return pl.pallas_call(
        _outproj_kernel,
        grid=(N_TOK // TM,),
        in_specs=[
            row(D_MODEL), row(GLA_W), row(CONV_W), row(MLA_HEADS * MLA_V),
            pl.BlockSpec((None, None, 1, 6 * D_MODEL), lambda i: (l, _mod_index(i, TM), 0, 0)),
            full((1, D_MODEL)), full((D_MODEL, D_MODEL)), full((D_MODEL, 128)), full((1, 128)),
        ],
        out_specs=[row(D_MODEL), row(D_MODEL), row(128), full((8, 128))],
        out_shape=[
            jax.ShapeDtypeStruct((N_TOK, D_MODEL), F32),
            jax.ShapeDtypeStruct((N_TOK, D_MODEL), F32),
            jax.ShapeDtypeStruct((N_TOK, 128), F32),
            jax.ShapeDtypeStruct((8, 128), F32),
        ],
        scratch_shapes=[pltpu.VMEM((8, 128), F32)],
        compiler_params=_cparams(("arbitrary",)),
        name="outproj_router",
    )(x, og, oc, om, mods4, norm_w, wo, wr, br)


N_ASSIGN = 2 * N_TOK
TE = 256
N_ETILE = N_ASSIGN // TE
N_WORK = N_ETILE + N_EXPERTS


def _lane_col(x, lane, j):
    return jnp.sum(jnp.where(lane == j, x, 0.0), axis=-1, keepdims=True)


def _plan_kernel(cnt_s, cnt_ref, route_ref, pos_ref, wt_ref, we_ref, wlo_ref, whi_ref, wf_ref, off_s):
    cnt = cnt_ref[...]
    chi = jnp.floor(cnt * (1.0 / 128.0))
    clo = cnt - chi * 128.0
    r = lax.broadcasted_iota(jnp.int32, (128, 128), 0)
    c = lax.broadcasted_iota(jnp.int32, (128, 128), 1)
    below = (r < c).astype(F32).astype(BF16)
    off = (128.0 * _dot(chi.astype(BF16), below) + _dot(clo.astype(BF16), below))[0:1, :]
    route = route_ref[...]
    lane = lax.broadcasted_iota(jnp.int32, route.shape, 1).astype(F32)
    pos1 = _lane_col(off, lane, _lane_col(route, lane, 0.0)) + _lane_col(route, lane, 4.0)
    pos2 = _lane_col(off, lane, _lane_col(route, lane, 1.0)) + _lane_col(route, lane, 5.0)
    pos_ref[...] = jnp.where(lane == 0.0, pos1, jnp.where(lane == 1.0, pos2, 0.0))

    @pl.when(pl.program_id(0) == 0)
    def _():
        def offs(e, acc):
            off_s[e] = acc
            return acc + cnt_s[e]

        off_s[N_EXPERTS] = lax.fori_loop(0, N_EXPERTS, offs, jnp.int32(0))

        def item(w, st):
            j, e, first = st
            active = j < N_ETILE
            jj = jnp.minimum(j, N_ETILE - 1)
            ee = jnp.minimum(e, N_EXPERTS - 1)
            tlo = jj * TE
            thi = tlo + TE
            end = off_s[ee + 1]
            wt_ref[w] = jj
            we_ref[w] = ee
            wlo_ref[w] = jnp.where(active, jnp.maximum(off_s[ee], tlo) - tlo, 0)
            whi_ref[w] = jnp.where(active, jnp.minimum(end, thi) - tlo, 0)
            wf_ref[w] = jnp.where(active, first, 0)
            adv_j = (active & (end >= thi)).astype(jnp.int32)
            adv_e = (active & (end <= thi)).astype(jnp.int32)
            return j + adv_j, e + adv_e, adv_j

        lax.fori_loop(0, N_WORK, item, (jnp.int32(0), jnp.int32(0), jnp.int32(1)))


def _plan(cnt_i, cnt, route):
    smem = pl.BlockSpec(memory_space=pltpu.SMEM)
    work = jax.ShapeDtypeStruct((N_WORK,), jnp.int32)
    return pl.pallas_call(
        _plan_kernel,
        grid=(N_TOK // TM,),
        in_specs=[smem, pl.BlockSpec((8, 128), lambda i: (0, 0)), pl.BlockSpec((TM, 128), lambda i: (i, 0))],
        out_specs=[pl.BlockSpec((TM, 128), lambda i: (i, 0)), smem, smem, smem, smem, smem],
        out_shape=[jax.ShapeDtypeStruct((N_TOK, 128), F32), work, work, work, work, work],
        scratch_shapes=[pltpu.SMEM((N_EXPERTS + 1,), jnp.int32)],
        compiler_params=_cparams(("arbitrary",)),
        name="moe_plan",
    )(cnt_i, cnt, route)


def _dispatch_kernel(pos_s, h_ref, xs_hbm, sem):
    base = pl.program_id(0) * TM

    def copy(i, k):
        return pltpu.make_async_copy(h_ref.at[pl.ds(i, 1)], xs_hbm.at[pl.ds(pos_s[2 * (base + i) + k], 1)], sem)

    def start(i, carry):
        copy(i, 0).start()
        copy(i, 1).start()
        return carry

    def wait(i, carry):
        copy(i, 0).wait()
        copy(i, 1).wait()
        return carry

    lax.fori_loop(0, TM, start, 0, unroll=8)
    lax.fori_loop(0, TM, wait, 0, unroll=8)


def _dispatch(pos, h):
    any_spec = pl.BlockSpec(memory_space=pl.ANY)
    return pl.pallas_call(
        _dispatch_kernel,
        grid_spec=pltpu.PrefetchScalarGridSpec(
            num_scalar_prefetch=1, grid=(N_TOK // TM,),
            in_specs=[pl.BlockSpec((TM, D_MODEL), lambda i, p: (i, 0))], out_specs=any_spec,
            scratch_shapes=[pltpu.SemaphoreType.DMA(())]),
        out_shape=jax.ShapeDtypeStruct((N_ASSIGN, D_MODEL), F32),
        compiler_params=_cparams(("arbitrary",)),
        name="moe_dispatch",
    )(pos, h)


def _experts_kernel(wt_s, we_s, wlo_s, whi_s, wf_s, xs_ref, w1_ref, w3_ref, w2_ref, o_ref):
    w = pl.program_id(0)
    x = xs_ref[...].astype(BF16)
    a = _dot(x, w1_ref[...].astype(BF16))
    b = _dot(x, w3_ref[...].astype(BF16))
    y = _dot((a * _sigmoid(a) * b).astype(BF16), w2_ref[...].astype(BF16))
    row = lax.broadcasted_iota(jnp.int32, (TE, 1), 0)
    mine = (row >= wlo_s[w]) & (row < whi_s[w])

    @pl.when(wf_s[w] == 1)
    def _():
        o_ref[...] = jnp.where(mine, y, 0.0)

    @pl.when(wf_s[w] == 0)
    def _():
        o_ref[...] = jnp.where(mine, y, o_ref[...])


def _experts(work, xs, l, w1, w3, w2):
    tile = lambda w, wt, we, wlo, whi, wf: (wt[w], 0)
    wspec = lambda shape: pl.BlockSpec((None, None) + shape, lambda w, wt, we, wlo, whi, wf: (l, we[w], 0, 0))
    return pl.pallas_call(
        _experts_kernel,
        grid_spec=pltpu.PrefetchScalarGridSpec(
            num_scalar_prefetch=5, grid=(N_WORK,),
            in_specs=[pl.BlockSpec((TE, D_MODEL), tile), wspec((D_MODEL, EXPERT_FF)),
                      wspec((D_MODEL, EXPERT_FF)), wspec((EXPERT_FF, D_MODEL))],
            out_specs=pl.BlockSpec((TE, D_MODEL), tile)),
        out_shape=jax.ShapeDtypeStruct((N_ASSIGN, D_MODEL), F32),
        compiler_params=_cparams(("arbitrary",)),
        name="moe_experts",
    )(*work, xs, w1, w3, w2)


def _combine_kernel(pos_s, x_ref, route_ref, mod_ref, fw_ref, ys_hbm, o_ref, y1_s, y2_s, sem, *, final):
    base = pl.program_id(0) * TM

    def copy(i, k):
        dst = y1_s if k == 0 else y2_s
        return pltpu.make_async_copy(ys_hbm.at[pl.ds(pos_s[2 * (base + i) + k], 1)], dst.at[pl.ds(i, 1)], sem)

    def start(i, carry):
        copy(i, 0).start()
        copy(i, 1).start()
        return carry

    def wait(i, carry):
        copy(i, 0).wait()
        copy(i, 1).wait()
        return carry

    lax.fori_loop(0, TM, start, 0, unroll=8)
    lax.fori_loop(0, TM, wait, 0, unroll=8)
    route = route_ref[...]
    lane = lax.broadcasted_iota(jnp.int32, route.shape, 1)
    moe = _lane_col(route, lane, 2) * y1_s[...] + _lane_col(route, lane, 3) * y2_s[...]
    x = x_ref[...] + mod_ref[:, 5 * D_MODEL:6 * D_MODEL] * moe
    o_ref[...] = _rms(x, fw_ref[...]) if final else x


def _combine(pos, x, route, mods4, l, fw, ys, final):
    row = lambda w: pl.BlockSpec((TM, w), lambda i, p: (i, 0))
    return pl.pallas_call(
        functools.partial(_combine_kernel, final=final),
        grid_spec=pltpu.PrefetchScalarGridSpec(
            num_scalar_prefetch=1, grid=(N_TOK // TM,),
            in_specs=[row(D_MODEL), row(128),
                      pl.BlockSpec((None, None, 1, 6 * D_MODEL), lambda i, p: (l, _mod_index(i, TM), 0, 0)),
                      pl.BlockSpec((1, D_MODEL), lambda i, p: (0, 0)),
                      pl.BlockSpec(memory_space=pl.ANY)],
            out_specs=row(D_MODEL),
            scratch_shapes=[pltpu.VMEM((TM, D_MODEL), F32), pltpu.VMEM((TM, D_MODEL), F32),
                            pltpu.SemaphoreType.DMA(())]),
        out_shape=jax.ShapeDtypeStruct((N_TOK, D_MODEL), F32),
        compiler_params=_cparams(("arbitrary",)),
        name="moe_combine_final" if final else "moe_combine",
    )(pos, x, route, mods4, fw, ys)


def _rope_swap(w):
    nf = MLA_ROPE // 4
    g = w.reshape(w.shape[:-1] + (2, 2, nf))
    return jnp.stack([-g[..., 1, :], g[..., 0, :]], axis=-2).reshape(w.shape)


def _rope_tables():
    nf = MLA_ROPE // 4
    pos = np.arange(DEC_SEQ)
    row = (pos // GRID_W).astype(np.float32)
    col = (pos % GRID_W).astype(np.float32)
    inv = (np.float32(ROPE_THETA) ** (-np.arange(nf, dtype=np.float32) / np.float32(nf))).astype(np.float32)
    ar = (row[:, None] * inv).astype(np.float32)
    ac = (col[:, None] * inv).astype(np.float32)
    cos = np.concatenate([np.cos(ar), np.cos(ar), np.cos(ac), np.cos(ac)], axis=1)
    sin = np.concatenate([np.sin(ar), np.sin(ar), np.sin(ac), np.sin(ac)], axis=1)
    cos = np.concatenate([np.ones((TM, MLA_ROPE)), cos], axis=0).astype(np.float32)
    sin = np.concatenate([np.zeros((TM, MLA_ROPE)), sin], axis=0).astype(np.float32)
    n = cos.shape[0]
    z32 = np.zeros((n, 32), np.float32)
    tqc = np.concatenate([np.ones((n, MLA_NOPE), np.float32), cos, z32], axis=1)
    tqs = np.concatenate([np.zeros((n, MLA_NOPE), np.float32), sin, z32], axis=1)
    tk = np.concatenate([z32, cos, sin, z32], axis=1)
    return jnp.asarray(tqc), jnp.asarray(tqs), jnp.asarray(tk)


def _prep_layer(l, w_in, gla_wg_f, gla_wg_b, mla_wuq, mla_wukv, w_out, moe_wg, moe_bg, moe_we, moe_be):
    offs = np.cumsum([0, 256, 256, 256, 256, 16, 16, 512, 256, 128, 32])
    seg = lambda i: w_in[l][:, offs[i]:offs[i + 1]]
    kr = seg(9)
    w_in_r = jnp.concatenate(
        [seg(0), seg(1), seg(2), seg(3), seg(6), seg(7), seg(8), seg(4), seg(5), kr, _rope_swap(kr),
         jnp.zeros((D_MODEL, 32), F32)], axis=1).astype(BF16)
    wgf = jnp.zeros((128, GLA_W), F32).at[0:16].set(gla_wg_f[l])
    wgb = jnp.zeros((128, GLA_W), F32).at[16:32].set(gla_wg_b[l])
    wq = mla_wuq[l].reshape(MLA_Q_RANK, MLA_HEADS, MLA_NOPE + MLA_ROPE)
    nope, rope = wq[..., :MLA_NOPE], wq[..., MLA_NOPE:]
    z = lambda n: jnp.zeros((MLA_Q_RANK, MLA_HEADS, n), F32)
    hw = MLA_HEADS * HEAD_BLK
    wa = jnp.concatenate([nope, rope, z(32)], axis=-1).reshape(MLA_Q_RANK, hw).astype(BF16)
    wb = jnp.concatenate([z(MLA_NOPE), _rope_swap(rope), z(32)], axis=-1).reshape(MLA_Q_RANK, hw).astype(BF16)
    wkv = mla_wukv[l].reshape(MLA_KV_RANK, MLA_HEADS, MLA_NOPE + MLA_V)
    wk = jnp.concatenate([wkv[..., :MLA_NOPE], jnp.zeros((MLA_KV_RANK, MLA_HEADS, 64), F32)],
                         axis=-1).reshape(MLA_KV_RANK, hw).astype(BF16)
    wv = wkv[..., MLA_NOPE:].reshape(MLA_KV_RANK, MLA_HEADS * MLA_V).astype(BF16)
    wp_np = np.zeros((128, MLA_HEADS, HEAD_BLK), np.float32)
    for j in range(MLA_ROPE):
        wp_np[32 + j, :, MLA_NOPE + j] = 1.0
        wp_np[64 + j, :, MLA_NOPE + j] = 1.0
    wp = jnp.asarray(wp_np.reshape(128, hw)).astype(BF16)
    wo = w_out[l].astype(BF16)
    wr = jnp.concatenate([moe_we[l], moe_wg[l], jnp.zeros((D_MODEL, 128 - N_EXPERTS - N_GROUPS), F32)], axis=1)
    br = jnp.concatenate([moe_be[l], moe_bg[l], jnp.zeros((128 - N_EXPERTS - N_GROUPS,), F32)]).reshape(1, 128)
    return dict(w_in_r=w_in_r, wgf=wgf, wgb=wgb, wa=wa, wb=wb, wk=wk, wv=wv, wp=wp, wo=wo, wr=wr, br=br)


def _state_to_blockdiag(s):
    b = s.shape[0]
    st = jnp.transpose(s, (0, 1, 3, 2))
    eye = jnp.eye(GLA_HEADS, dtype=s.dtype)
    out = st[:, :, :, None, :] * eye[None, :, None, :, None]
    return out.reshape(b, GLA_W, GLA_W)


def _blockdiag_to_state(st):
    b = st.shape[0]
    blocks = st.reshape(b, GLA_HEADS, GLA_DV, GLA_HEADS, GLA_DK)
    diag = jnp.stack([blocks[:, h, :, h, :] for h in range(GLA_HEADS)], axis=1)
    return jnp.transpose(diag, (0, 1, 3, 2))


def kernel(x_prompt, x_sample, cache_ckv, cache_krope, state_gla_fwd, state_gla_bwd, c, c_ctx, ada_w, ada_b,
           norm1_w, norm2_w, w_in, gla_wg_f, gla_bg_f, gla_wg_b, gla_bg_b, gla_norm_w, conv_w, conv_b,
           conv_ln_w, conv_ln_b, mla_qnorm_w, mla_wuq, mla_kvnorm_w, mla_wukv, w_out, moe_wg, moe_bg, moe_we,
           moe_be, moe_w1, moe_w3, moe_w2, final_norm_w):
    x = jnp.concatenate([x_prompt.reshape(N_CTX, D_MODEL), x_sample.reshape(N_SMP, D_MODEL)], axis=0)
    cond8 = jnp.concatenate([c_ctx[None, :], c, jnp.zeros((N_MOD - 1 - DEC_BATCH, D_MODEL), F32)], axis=0)
    mods4 = _ada_table(cond8, ada_w, ada_b).reshape(DEPTH, N_MOD, 1, 6 * D_MODEL)
    tabs = _rope_tables()
    row1 = lambda a: a.reshape(1, -1)

    ckv_new, kr_new, sf_new, sb_new = [], [], [], []
    for l in range(DEPTH):
        w = _prep_layer(l, w_in, gla_wg_f, gla_wg_b, mla_wuq, mla_wukv, w_out, moe_wg, moe_bg, moe_we, moe_be)
        p = _inproj(x, mods4, l, row1(norm1_w[l]), w["w_in_r"])

        gla_args = (p, w["wgf"], w["wgb"], row1(gla_bg_f[l]), row1(gla_bg_b[l]), row1(gla_norm_w[l]))
        og_c, sf, sb = _gla(*gla_args, SEQ, BATCH, 0)
        og_s = _gla(*gla_args, DEC_SEQ, DEC_BATCH, N_CTX // DEC_SEQ,
                    _state_to_blockdiag(state_gla_fwd[:, l]), _state_to_blockdiag(state_gla_bwd[:, l]))[0]
        og = jnp.concatenate([og_c, og_s], axis=0)

        cw = jnp.concatenate([conv_w[l], jnp.zeros((1, CONV_W), F32)], axis=0)
        conv_args = (p, cw, row1(conv_b[l]), row1(conv_ln_w[l]), row1(conv_ln_b[l]))
        oc = jnp.concatenate([_conv(*conv_args, SEQ, BATCH, 0),
                              _conv(*conv_args, DEC_SEQ, DEC_BATCH, N_CTX // DEC_SEQ)], axis=0)

        q, k, v, ckvn, kr = _mla_prep(p, tabs, row1(mla_qnorm_w[l]), row1(mla_kvnorm_w[l]),
                                      w["wa"], w["wb"], w["wk"], w["wp"], w["wv"])
        cc = cache_ckv[:, l].reshape(DEC_BATCH * PAST_LEN, MLA_KV_RANK)
        ckr = cache_krope[:, l].reshape(DEC_BATCH * PAST_LEN, MLA_ROPE)
        ckr = jnp.pad(ckr, ((0, 0), (32, 64)))
        kc, vc = _cache_kv(cc, ckr, w["wk"], w["wp"], w["wv"])
        om = jnp.concatenate([_attention(q, k, v, SEQ, BATCH, 0),
                              _attention(q, k, v, DEC_SEQ, DEC_BATCH, N_CTX, kc, vc)], axis=0)

        xn, h2, route, cnt = _outproj(x, og, oc, om, mods4, l, row1(norm2_w[l]), w["wo"], w["wr"], w["br"])
        posf, *work = _plan(cnt[0, :N_EXPERTS].astype(jnp.int32), cnt, route)
        pos = posf[:, 0:2].astype(jnp.int32).reshape(N_ASSIGN)
        ys = _experts(work, _dispatch(pos, h2), l, moe_w1, moe_w3, moe_w2)
        x = _combine(pos, xn, route, mods4, l, row1(final_norm_w), ys, l == DEPTH - 1)

        ckv_new.append(ckvn[:N_CTX].reshape(BATCH, SEQ, MLA_KV_RANK))
        kr_new.append(kr[:N_CTX, 32:32 + MLA_ROPE].reshape(BATCH, SEQ, MLA_ROPE))
        sf_new.append(_blockdiag_to_state(sf))
        sb_new.append(_blockdiag_to_state(sb))

    y = x
    return (y[:N_CTX].reshape(BATCH, SEQ, D_MODEL), y[N_CTX:].reshape(DEC_BATCH, DEC_SEQ, D_MODEL),
            jnp.stack(ckv_new, axis=1), jnp.stack(kr_new, axis=1),
            jnp.stack(sf_new, axis=1), jnp.stack(sb_new, axis=1))
```

```python
import functools

import numpy as np
import jax
import jax.numpy as jnp
from jax import lax
from jax.experimental import pallas as pl
from jax.experimental.pallas import tpu as pltpu

F32 = jnp.float32
BF16 = jnp.bfloat16

D_MODEL = 1024
BATCH = 16
SEQ = 256
DEPTH = 2
DEC_BATCH = 4
DEC_SEQ = 1024
PAST_LEN = 512
GRID_W = 64
NORM_EPS = 1e-6
GLA_HEADS = 4
GLA_DK = 64
GLA_DV = 64
GLA_W = 256
GLA_GATE_RANK = 16
GLA_TAU = 16.0
CONV_W = 256
CONV_K = 31
MLA_HEADS = 8
MLA_NOPE = 64
MLA_ROPE = 32
MLA_V = 64
MLA_Q_RANK = 256
MLA_KV_RANK = 128
ROPE_THETA = 10000.0
N_GROUPS = 4
EXPERTS_PER_GROUP = 8
N_EXPERTS = 32
EXPERT_FF = 256

N_CTX = BATCH * SEQ
N_SMP = DEC_BATCH * DEC_SEQ
N_TOK = N_CTX + N_SMP
N_MOD = 8
TM = 256
P_COLS = 2048
MISC_BLK = 15
CHUNK = 64
SUB = 16
NEG = -1e30
VMEM_LIMIT = 56 * 1024 * 1024


def _cparams(sem):
    return pltpu.CompilerParams(dimension_semantics=sem, vmem_limit_bytes=VMEM_LIMIT)


def _dot(a, b):
    return jnp.dot(a, b, preferred_element_type=F32)


def _dot_nt(a, b):
    return lax.dot_general(a, b, (((1,), (1,)), ((), ())), preferred_element_type=F32)


def _split2(x):
    hi = x.astype(BF16)
    lo = (x - hi.astype(F32)).astype(BF16)
    return hi, lo


def _dot3(a, w):
    ah, al = _split2(a)
    wh, wl = _split2(w)
    return _dot(ah, wh) + _dot(al, wh) + _dot(ah, wl)


def _sigmoid(x):
    return 1.0 / (1.0 + jnp.exp(-x))


def _rms(x, w):
    ms = jnp.mean(x * x, axis=-1, keepdims=True)
    return x * lax.rsqrt(ms + NORM_EPS) * w


def _mod_index(i, tile):
    nctx = N_CTX // tile
    per = DEC_SEQ // tile
    return jnp.where(i < nctx, 0, 1 + (i - nctx) // per)


def _ada_kernel(c_ref, w_ref, b_ref, o_ref):
    c = c_ref[...]
    o_ref[...] = _dot3(c * _sigmoid(c), w_ref[...]) + b_ref[...]


def _ada_table(cond8, ada_w, ada_b):
    tn = 1024
    n6 = 6 * D_MODEL
    return pl.pallas_call(
        _ada_kernel,
        grid=(DEPTH, n6 // tn),
        in_specs=[
            pl.BlockSpec((N_MOD, D_MODEL), lambda l, j: (0, 0)),
            pl.BlockSpec((None, D_MODEL, tn), lambda l, j: (l, 0, j)),
            pl.BlockSpec((None, 1, tn), lambda l, j: (l, 0, j)),
        ],
        out_specs=pl.BlockSpec((None, N_MOD, tn), lambda l, j: (l, 0, j)),
        out_shape=jax.ShapeDtypeStruct((DEPTH, N_MOD, n6), F32),
        compiler_params=_cparams(("parallel", "parallel")),
        name="ada_table",
    )(cond8, ada_w, ada_b.reshape(DEPTH, 1, n6))


def _pair_specs(tile, width, split):
    nctx = N_CTX // tile
    offb = 0 if split else nctx
    return [pl.BlockSpec((tile, width), lambda i, *_: (jnp.minimum(i, nctx - 1), 0)),
            pl.BlockSpec((tile, width), lambda i, *_: (jnp.maximum(i - nctx, 0) + offb, 0))]


def _pair(x):
    return x if isinstance(x, tuple) else (x, x)


def _pick(tile, a_ref, b_ref):
    return jnp.where(pl.program_id(0) < N_CTX // tile, a_ref[...], b_ref[...])


TM_IN = 512


def _inproj_kernel(xa_ref, xb_ref, mod_ref, nw_ref, w_ref, o_ref):
    mod = mod_ref[...]
    x = _pick(TM_IN, xa_ref, xb_ref)
    h = _rms(x, nw_ref[...]) * (1.0 + mod[:, D_MODEL:2 * D_MODEL]) + mod[:, 0:D_MODEL]
    o_ref[...] = _dot(h.astype(BF16), w_ref[...])


def _inproj(x, mods4, l, norm_w, w_in_r):
    return pl.pallas_call(
        _inproj_kernel,
        grid=(N_TOK // TM_IN,),
        in_specs=_pair_specs(TM_IN, D_MODEL, isinstance(x, tuple)) + [
            pl.BlockSpec((None, None, 1, 6 * D_MODEL), lambda i: (l, _mod_index(i, TM_IN), 0, 0)),
            pl.BlockSpec((1, D_MODEL), lambda i: (0, 0)),
            pl.BlockSpec((D_MODEL, P_COLS), lambda i: (0, 0)),
        ],
        out_specs=pl.BlockSpec((TM_IN, P_COLS), lambda i: (i, 0)),
        out_shape=jax.ShapeDtypeStruct((N_TOK, P_COLS), F32),
        compiler_params=_cparams(("parallel",)),
        name="inproj",
    )(*_pair(x), mods4, norm_w, w_in_r)


def _log_sigmoid(x):
    return jnp.minimum(x, 0.0) - jnp.log(1.0 + jnp.exp(-jnp.abs(x)))


def _split3(x):
    hi = x.astype(BF16)
    r = x - hi.astype(F32)
    mid = r.astype(BF16)
    lo = (r - mid.astype(F32)).astype(BF16)
    return hi, mid, lo


def _gla_consts(reverse):
    t = lax.broadcasted_iota(jnp.int32, (CHUNK, CHUNK), 0)
    s = lax.broadcasted_iota(jnp.int32, (CHUNK, CHUNK), 1)
    mid = jnp.bitwise_and(t, -SUB) + SUB // 2
    if reverse:
        cum = s >= t
        ref = s >= mid
    else:
        cum = s <= t
        ref = s <= mid
    cum_ref = jnp.concatenate([cum, ref], axis=0).astype(F32).astype(BF16)
    rt = jnp.bitwise_and(lax.broadcasted_iota(jnp.int32, (GLA_HEADS * SUB, CHUNK), 0), SUB - 1)
    cs = lax.broadcasted_iota(jnp.int32, (GLA_HEADS * SUB, CHUNK), 1)
    rowi = lax.broadcasted_iota(jnp.int32, (CHUNK, 1), 0)
    causal, valid = [], []
    for i in range(CHUNK // SUB):
        tq = rt + i * SUB
        causal.append((cs >= tq) if reverse else (cs <= tq))
        valid.append((rowi >= i * SUB) if reverse else (rowi < (i + 1) * SUB))
    return cum_ref, causal, valid


def _gla_chunk(r0, qkvg_ref, la_ref, st_ref, o_ref, consts, head_masks, block_mask, reverse):
    cum_ref, causal, valid = consts
    rows = pl.ds(r0, CHUNK)
    q = qkvg_ref[rows, 0:GLA_W]
    k = qkvg_ref[rows, GLA_W:2 * GLA_W]
    v = qkvg_ref[rows, 2 * GLA_W:3 * GLA_W]
    la = la_ref[rows, :]
    h0, h1, h2 = _split3(la)
    br = _dot(cum_ref, h0) + _dot(cum_ref, h1) + _dot(cum_ref, h2)
    b = br[0:CHUNK]
    rr = br[CHUNK:2 * CHUNK]
    qs = q * (GLA_DK ** -0.5)
    qhat = qs * jnp.exp(b - rr)
    vb = v.astype(BF16)
    parts = []
    for i in range(CHUNK // SUB):
        r_i = rr[i * SUB:i * SUB + 1]
        kt = (k * jnp.exp(jnp.where(valid[i], r_i - b, NEG))).astype(BF16)
        qi = qhat[i * SUB:(i + 1) * SUB]
        qbig = jnp.concatenate([jnp.where(m, qi, 0.0) for m in head_masks], axis=0).astype(BF16)
        a = jnp.where(causal[i], _dot_nt(qbig, kt), 0.0)
        ov = _dot(a.astype(BF16), vb)
        oi = jnp.where(head_masks[0], ov[0:SUB], 0.0)
        for h in range(1, GLA_HEADS):
            oi = oi + jnp.where(head_masks[h], ov[h * SUB:(h + 1) * SUB], 0.0)
        parts.append(oi)
    o_intra = jnp.concatenate(parts, axis=0)
    st = st_ref[...]
    o_inter = _dot_nt((qs * jnp.exp(b)).astype(BF16), st.astype(BF16))
    o_ref[rows, :] = o_inter + o_intra
    b_last = b[0:1] if reverse else b[CHUNK - 1:CHUNK]
    kl = (k * jnp.exp(b_last - b)).astype(BF16)
    upd = _dot(v.T.astype(BF16), kl)
    st_ref[...] = st * jnp.exp(b_last) + jnp.where(block_mask, upd, 0.0)


def _gla_kernel(*refs, n, has_state):
    if has_state:
        (qkvg_ref, misc_ref, wgf_ref, wgb_ref, bgf_ref, bgb_ref, nw_ref, s0f_ref, s0b_ref,
         o_ref, laf_s, lab_s, of_s, ob_s, stf_s, stb_s) = refs
    else:
        (qkvg_ref, misc_ref, wgf_ref, wgb_ref, bgf_ref, bgb_ref, nw_ref,
         o_ref, sf_ref, sb_ref, laf_s, lab_s, of_s, ob_s, stf_s, stb_s) = refs
    misc = misc_ref[...]
    laf_s[...] = _log_sigmoid(_dot3(misc, wgf_ref[...]) + bgf_ref[...]) * (1.0 / GLA_TAU)
    lab_s[...] = _log_sigmoid(_dot3(misc, wgb_ref[...]) + bgb_ref[...]) * (1.0 / GLA_TAU)
    stf_s[...] = jnp.zeros((GLA_W, GLA_W), F32)
    stb_s[...] = jnp.zeros((GLA_W, GLA_W), F32)
    if has_state:
        for st_s, s0_ref in ((stf_s, s0f_ref), (stb_s, s0b_ref)):
            for h in range(GLA_HEADS):
                st_s[h * GLA_DK:(h + 1) * GLA_DK, h * GLA_DV:(h + 1) * GLA_DV] = s0_ref[h]
            st_s[...] = st_s[...].T

    lane = lax.broadcasted_iota(jnp.int32, (1, GLA_W), 1)
    head_masks = [jnp.right_shift(lane, 6) == h for h in range(GLA_HEADS)]
    bi = jnp.right_shift(lax.broadcasted_iota(jnp.int32, (GLA_W, GLA_W), 0), 6)
    bj = jnp.right_shift(lax.broadcasted_iota(jnp.int32, (GLA_W, GLA_W), 1), 6)
    block_mask = bi == bj
    consts_f = _gla_consts(False)
    consts_b = _gla_consts(True)
    nc = n // CHUNK

    def body(j, carry):
        _gla_chunk(pl.multiple_of(j * CHUNK, CHUNK), qkvg_ref, laf_s, stf_s, of_s,
                   consts_f, head_masks, block_mask, False)
        _gla_chunk(pl.multiple_of((nc - 1 - j) * CHUNK, CHUNK), qkvg_ref, lab_s, stb_s, ob_s,
                   consts_b, head_masks, block_mask, True)
        return carry

    lax.fori_loop(0, nc, body, 0)

    if not has_state:
        for st_s, s_ref in ((stf_s, sf_ref), (stb_s, sb_ref)):
            s = st_s[...].T
            for h in range(GLA_HEADS):
                s_ref[h] = s[h * GLA_DK:(h + 1) * GLA_DK, h * GLA_DV:(h + 1) * GLA_DV]

    ones_blk = block_mask.astype(F32).astype(BF16)
    rt = 256
    for r0 in range(0, n, rt):
        o = of_s[r0:r0 + rt, :] + ob_s[r0:r0 + rt, :]
        hi, lo = _split2(o * o)
        ms = (_dot(hi, ones_blk) + _dot(lo, ones_blk)) * (1.0 / GLA_DV)
        g = qkvg_ref[r0:r0 + rt, 3 * GLA_W:4 * GLA_W]
        o_ref[r0:r0 + rt, :] = o * lax.rsqrt(ms + NORM_EPS) * nw_ref[...] * (g * _sigmoid(g))


def _gla(p, wgf, wgb, bgf, bgb, nw, n, nb, row_blk0, s0f=None, s0b=None):
    has_state = s0f is not None
    full = lambda shape: pl.BlockSpec(shape, lambda b: (0,) * len(shape))
    in_specs = [
        pl.BlockSpec((n, 4 * GLA_W), lambda b: (row_blk0 + b, 0)),
        pl.BlockSpec((n, 128), lambda b: (row_blk0 + b, MISC_BLK)),
        full((128, GLA_W)), full((128, GLA_W)), full((1, GLA_W)), full((1, GLA_W)), full((1, GLA_W)),
    ]
    args = [p, p, wgf, wgb, bgf, bgb, nw]
    out_specs = [pl.BlockSpec((n, GLA_W), lambda b: (b, 0))]
    out_shape = [jax.ShapeDtypeStruct((nb * n, GLA_W), F32)]
    if has_state:
        layer = s0f[1]
        st_spec = pl.BlockSpec((None, None, GLA_HEADS, GLA_DK, GLA_DV), lambda b: (b, layer, 0, 0, 0))
        in_specs += [st_spec, st_spec]
        args += [s0f[0], s0b[0]]
    else:
        st_spec = pl.BlockSpec((None, GLA_HEADS, GLA_DK, GLA_DV), lambda b: (b, 0, 0, 0))
        out_specs += [st_spec, st_spec]
        out_shape += [jax.ShapeDtypeStruct((nb, GLA_HEADS, GLA_DK, GLA_DV), F32)] * 2
    return pl.pallas_call(
        functools.partial(_gla_kernel, n=n, has_state=has_state),
        grid=(nb,),
        in_specs=in_specs,
        out_specs=out_specs,
        out_shape=out_shape,
        scratch_shapes=[pltpu.VMEM((n, GLA_W), F32)] * 4 + [pltpu.VMEM((GLA_W, GLA_W), F32)] * 2,
        compiler_params=_cparams(("parallel",)),
        name="gla_state" if has_state else "gla_ctx",
    )(*args)


CONV_PAD = 16
CONV_ROWS = 128


def _conv_kernel(u_ref, w_ref, b_ref, lnw_ref, lnb_ref, o_ref, zp_s, *, n):
    u = u_ref[...]
    zp_s[0:CONV_PAD, :] = jnp.zeros((CONV_PAD, CONV_W), F32)
    zp_s[CONV_PAD + n:2 * CONV_PAD + n, :] = jnp.zeros((CONV_PAD, CONV_W), F32)
    zp_s[CONV_PAD:CONV_PAD + n, :] = u[:, 0:CONV_W] * _sigmoid(u[:, CONV_W:2 * CONV_W])
    off = CONV_PAD - CONV_K // 2
    for r0 in range(0, n, CONV_ROWS):
        acc = zp_s[r0 + off:r0 + off + CONV_ROWS, :] * w_ref[0:1, :]
        for j in range(1, CONV_K):
            acc = acc + zp_s[r0 + off + j:r0 + off + j + CONV_ROWS, :] * w_ref[j:j + 1, :]
        acc = acc + b_ref[...]
        mu = jnp.mean(acc, axis=-1, keepdims=True)
        xc = acc - mu
        var = jnp.mean(xc * xc, axis=-1, keepdims=True)
        y = xc * lax.rsqrt(var + NORM_EPS) * lnw_ref[...] + lnb_ref[...]
        o_ref[r0:r0 + CONV_ROWS, :] = y * _sigmoid(y)


def _conv(p, cw, cb, lnw, lnb, n, nb, row_blk0):
    full = lambda shape: pl.BlockSpec(shape, lambda b: (0,) * len(shape))
    return pl.pallas_call(
        functools.partial(_conv_kernel, n=n),
        grid=(nb,),
        in_specs=[
            pl.BlockSpec((n, 2 * CONV_W), lambda b: (row_blk0 + b, 2)),
            full((32, CONV_W)), full((1, CONV_W)), full((1, CONV_W)), full((1, CONV_W)),
        ],
        out_specs=pl.BlockSpec((n, CONV_W), lambda b: (b, 0)),
        out_shape=jax.ShapeDtypeStruct((nb * n, CONV_W), F32),
        scratch_shapes=[pltpu.VMEM((n + 2 * CONV_PAD, CONV_W), F32)],
        compiler_params=_cparams(("parallel",)),
        name="conv%d" % n,
    )(p, cw, cb, lnw, lnb)


HEAD_BLK = 128


def _kv_expand(ckvn, t, wk_ref, wp_ref, wv_ref, k_ref, v_ref):
    cb = ckvn.astype(BF16)
    th, tl = _split2(t)
    k_ref[...] = (_dot(cb, wk_ref[...]) + _dot(th, wp_ref[...]) + _dot(tl, wp_ref[...])).astype(BF16)
    v_ref[...] = _dot(cb, wv_ref[...]).astype(BF16)


def _mla_prep_kernel(cq_ref, cm_ref, tqc_ref, tqs_ref, tk_ref, qnw_ref, kvw_ref, wa_ref, wb_ref,
                     wk_ref, wp_ref, wv_ref, q_ref, k_ref, v_ref, ckv_ref, kr_ref):
    qn = _rms(cq_ref[...], qnw_ref[...]).astype(BF16)
    tqc = jnp.concatenate([tqc_ref[...]] * MLA_HEADS, axis=1)
    tqs = jnp.concatenate([tqs_ref[...]] * MLA_HEADS, axis=1)
    scale = (MLA_NOPE + MLA_ROPE) ** -0.5
    q_ref[...] = ((_dot(qn, wa_ref[...]) * tqc + _dot(qn, wb_ref[...]) * tqs) * scale).astype(BF16)
    cm = cm_ref[...]
    ckvn = _rms(cm[:, 0:MLA_KV_RANK], kvw_ref[...])
    ckv_ref[...] = ckvn
    t = cm[:, MLA_KV_RANK:2 * MLA_KV_RANK] * tk_ref[...]
    kr_ref[...] = t
    _kv_expand(ckvn, t, wk_ref, wp_ref, wv_ref, k_ref, v_ref)


def _rope_tile(i):
    nctx = N_CTX // TM
    return jnp.where(i < nctx, 0, 1 + (i - nctx) % (DEC_SEQ // TM))


def _mla_prep(p, tabs, qnw, kvw, wa, wb, wk, wp, wv):
    tqc, tqs, tk = tabs
    full = lambda shape: pl.BlockSpec(shape, lambda i: (0,) * len(shape))
    tab_spec = pl.BlockSpec((TM, HEAD_BLK), lambda i: (_rope_tile(i), 0))
    hw = MLA_HEADS * HEAD_BLK
    return pl.pallas_call(
        _mla_prep_kernel,
        grid=(N_TOK // TM,),
        in_specs=[
            pl.BlockSpec((TM, MLA_Q_RANK), lambda i: (i, 6)),
            pl.BlockSpec((TM, 2 * MLA_KV_RANK), lambda i: (i, 7)),
            tab_spec, tab_spec, tab_spec,
            full((1, MLA_Q_RANK)), full((1, MLA_KV_RANK)),
            full((MLA_Q_RANK, hw)), full((MLA_Q_RANK, hw)),
            full((MLA_KV_RANK, hw)), full((128, hw)), full((MLA_KV_RANK, MLA_HEADS * MLA_V)),
        ],
        out_specs=[
            pl.BlockSpec((TM, hw), lambda i: (i, 0)),
            pl.BlockSpec((TM, hw), lambda i: (i, 0)),
            pl.BlockSpec((TM, MLA_HEADS * MLA_V), lambda i: (i, 0)),
            pl.BlockSpec((TM, MLA_KV_RANK), lambda i: (i, 0)),
            pl.BlockSpec((TM, 128), lambda i: (i, 0)),
        ],
        out_shape=[
            jax.ShapeDtypeStruct((N_TOK, hw), BF16),
            jax.ShapeDtypeStruct((N_TOK, hw), BF16),
            jax.ShapeDtypeStruct((N_TOK, MLA_HEADS * MLA_V), BF16),
            jax.ShapeDtypeStruct((N_TOK, MLA_KV_RANK), F32),
            jax.ShapeDtypeStruct((N_TOK, 128), F32),
        ],
        compiler_params=_cparams(("parallel",)),
        name="mla_prep",
    )(p, p, tqc, tqs, tk, qnw, kvw, wa, wb, wk, wp, wv)


def _cache_kv_kernel(ckv_ref, kr_ref, wk_ref, wp_ref, wv_ref, k_ref, v_ref):
    _kv_expand(ckv_ref[...], kr_ref[...], wk_ref, wp_ref, wv_ref, k_ref, v_ref)


def _cache_kv(ckv, kr, wk, wp, wv):
    n = DEC_BATCH * PAST_LEN
    full = lambda shape: pl.BlockSpec(shape, lambda i: (0,) * len(shape))
    hw = MLA_HEADS * HEAD_BLK
    return pl.pallas_call(
        _cache_kv_kernel,
        grid=(n // PAST_LEN,),
        in_specs=[
            pl.BlockSpec((PAST_LEN, MLA_KV_RANK), lambda i: (i, 0)),
            pl.BlockSpec((PAST_LEN, 128), lambda i: (i, 0)),
            full((MLA_KV_RANK, hw)), full((128, hw)), full((MLA_KV_RANK, MLA_HEADS * MLA_V)),
        ],
        out_specs=[
            pl.BlockSpec((PAST_LEN, hw), lambda i: (i, 0)),
            pl.BlockSpec((PAST_LEN, MLA_HEADS * MLA_V), lambda i: (i, 0)),
        ],
        out_shape=[
            jax.ShapeDtypeStruct((n, hw), BF16),
            jax.ShapeDtypeStruct((n, MLA_HEADS * MLA_V), BF16),
        ],
        compiler_params=_cparams(("parallel",)),
        name="cache_kv",
    )(ckv, kr, wk, wp, wv)


TQ = 256


def _attn_kernel(*refs, has_cache):
    if has_cache:
        q_ref, k_ref, v_ref, kc_ref, vc_ref, o_ref = refs
    else:
        q_ref, k_ref, v_ref, o_ref = refs
    outs = []
    for hh in range(2):
        cols = slice(hh * HEAD_BLK, (hh + 1) * HEAD_BLK)
        q = q_ref[:, cols]
        s = _dot_nt(q, k_ref[:, cols])
        m = jnp.max(s, axis=-1, keepdims=True)
        if has_cache:
            sc = _dot_nt(q, kc_ref[:, cols])
            m = jnp.maximum(m, jnp.max(sc, axis=-1, keepdims=True))
            pc = jnp.exp(sc - m)
        p = jnp.exp(s - m)
        den = jnp.sum(p, axis=-1, keepdims=True)
        o = _dot(p.astype(BF16), v_ref[...])
        if has_cache:
            den = den + jnp.sum(pc, axis=-1, keepdims=True)
            o = o + _dot(pc.astype(BF16), vc_ref[...])
        outs.append(o / den)
    lane = lax.broadcasted_iota(jnp.int32, (1, 2 * MLA_V), 1)
    o_ref[...] = jnp.where(lane < MLA_V, outs[0], outs[1])


def _attention(q, k, v, n, nb, row0, kc=None, vc=None):
    has_cache = kc is not None
    nq = n // TQ
    rb0 = row0 // TQ
    kb0 = row0 // n
    in_specs = [
        pl.BlockSpec((TQ, 2 * HEAD_BLK), lambda b, h, i: (rb0 + b * nq + i, h)),
        pl.BlockSpec((n, 2 * HEAD_BLK), lambda b, h, i: (kb0 + b, h)),
        pl.BlockSpec((n, 2 * MLA_V), lambda b, h, i: (kb0 + b, h)),
    ]
    args = [q, k, v]
    if has_cache:
        in_specs += [
            pl.BlockSpec((PAST_LEN, 2 * HEAD_BLK), lambda b, h, i: (b, h)),
            pl.BlockSpec((PAST_LEN, 2 * MLA_V), lambda b, h, i: (b, h)),
        ]
        args += [kc, vc]
    return pl.pallas_call(
        functools.partial(_attn_kernel, has_cache=has_cache),
        grid=(nb, MLA_HEADS // 2, nq),
        in_specs=in_specs,
        out_specs=pl.BlockSpec((TQ, 2 * MLA_V), lambda b, h, i: (b * nq + i, h)),
        out_shape=jax.ShapeDtypeStruct((nb * n, MLA_HEADS * MLA_V), F32),
        compiler_params=_cparams(("parallel", "parallel", "parallel")),
        name="attn_cache" if has_cache else "attn_ctx",
    )(*args)


def _outproj_kernel(xa_ref, xb_ref, oga_ref, ogb_ref, oca_ref, ocb_ref, oma_ref, omb_ref, mod_ref, nw_ref,
                    wo_ref, wr_ref, br_ref, xo_ref, h_ref, route_ref, cnt_ref, carry_s, wrh_s, wrl_s):
    @pl.when(pl.program_id(0) == 0)
    def _():
        carry_s[...] = jnp.zeros(carry_s.shape, F32)
        wrh, wrl = _split2(wr_ref[...])
        wrh_s[...] = wrh
        wrl_s[...] = wrl

    attn = (_dot(_pick(TM, oga_ref, ogb_ref).astype(BF16), wo_ref[0:GLA_W, :])
            + _dot(_pick(TM, oca_ref, ocb_ref).astype(BF16), wo_ref[GLA_W:GLA_W + CONV_W, :])
            + _dot(_pick(TM, oma_ref, omb_ref).astype(BF16), wo_ref[GLA_W + CONV_W:D_MODEL, :]))
    mod = mod_ref[...]
    xn = _pick(TM, xa_ref, xb_ref) + mod[:, 2 * D_MODEL:3 * D_MODEL] * attn
    xo_ref[...] = xn
    h = _rms(xn, nw_ref[...]) * (1.0 + mod[:, 4 * D_MODEL:5 * D_MODEL]) + mod[:, 3 * D_MODEL:4 * D_MODEL]
    h_ref[...] = h
    hh, hl = _split2(h)
    logits = _dot(hh, wrh_s[...]) + _dot(hl, wrh_s[...]) + _dot(hh, wrl_s[...]) + br_ref[...]
    lane = lax.broadcasted_iota(jnp.int32, logits.shape, 1).astype(F32)
    big = 1e9
    gl = jnp.where((lane >= N_EXPERTS) & (lane < N_EXPERTS + N_GROUPS), logits, NEG)
    gmax = jnp.max(gl, axis=-1, keepdims=True)
    gw = 1.0 / jnp.sum(jnp.exp(gl - gmax), axis=-1, keepdims=True)
    gi = jnp.min(jnp.where(gl == gmax, lane, big), axis=-1, keepdims=True) - N_EXPERTS
    lo = gi * EXPERTS_PER_GROUP
    el = jnp.where((lane >= lo) & (lane < lo + EXPERTS_PER_GROUP), logits, NEG)
    m1 = jnp.max(el, axis=-1, keepdims=True)
    i1 = jnp.min(jnp.where(el == m1, lane, big), axis=-1, keepdims=True)
    el2 = jnp.where(lane == i1, NEG, el)
    m2 = jnp.max(el2, axis=-1, keepdims=True)
    i2 = jnp.min(jnp.where(el2 == m2, lane, big), axis=-1, keepdims=True)
    e2 = jnp.exp(m2 - m1)
    p1 = 1.0 / (1.0 + e2)
    onehot = jnp.where((lane == i1) | (lane == i2), 1.0, 0.0)
    tr = lax.broadcasted_iota(jnp.int32, (TM, TM), 0)
    tc = lax.broadcasted_iota(jnp.int32, (TM, TM), 1)
    before = (tc < tr).astype(F32).astype(BF16)
    seen = _dot(before, onehot.astype(BF16)) + carry_s[0:1, :]
    rank1 = jnp.sum(jnp.where(lane == i1, seen, 0.0), axis=-1, keepdims=True)
    rank2 = jnp.sum(jnp.where(lane == i2, seen, 0.0), axis=-1, keepdims=True)
    carry_s[...] = carry_s[...] + jnp.sum(onehot, axis=0, keepdims=True)
    cnt_ref[...] = carry_s[...]
    cols = (i1, i2, gw * p1, gw * (e2 * p1), rank1, rank2)
    route = jnp.zeros(logits.shape, F32)
    for j, col in enumerate(cols):
        route = jnp.where(lane == j, col, route)
    route_ref[...] = route


def _outproj(x, og, oc, om, mods4, l, norm_w, wo, wr, br):
    full = lambda shape: pl.BlockSpec(shape, lambda i: (0,) * len(shape))
    row = lambda w: pl.BlockSpec((TM, w), lambda i: (i, 0))
    return pl.pallas_call(
        _outproj_kernel,
        grid=(N_TOK // TM,),
        in_specs=(_pair_specs(TM, D_MODEL, isinstance(x, tuple)) + _pair_specs(TM, GLA_W, True)
                  + _pair_specs(TM, CONV_W, True) + _pair_specs(TM, MLA_HEADS * MLA_V, True) + [
            pl.BlockSpec((None, None, 1, 6 * D_MODEL), lambda i: (l, _mod_index(i, TM), 0, 0)),
            full((1, D_MODEL)), full((D_MODEL, D_MODEL)), full((D_MODEL, 128)), full((1, 128)),
        ]),
        out_specs=[row(D_MODEL), row(D_MODEL), row(128), full((8, 128))],
        out_shape=[
            jax.ShapeDtypeStruct((N_TOK, D_MODEL), F32),
            jax.ShapeDtypeStruct((N_TOK, D_MODEL), F32),
            jax.ShapeDtypeStruct((N_TOK, 128), F32),
            jax.ShapeDtypeStruct((8, 128), F32),
        ],
        scratch_shapes=[pltpu.VMEM((8, 128), F32), pltpu.VMEM((D_MODEL, 128), BF16),
                        pltpu.VMEM((D_MODEL, 128), BF16)],
        compiler_params=_cparams(("arbitrary",)),
        name="outproj_router",
    )(*_pair(x), *og, *oc, *om, mods4, norm_w, wo, wr, br)


N_ASSIGN = 2 * N_TOK
TE = 256
N_ETILE = N_ASSIGN // TE
N_WORK = N_ETILE + N_EXPERTS


def _lane_col(x, lane, j):
    return jnp.sum(jnp.where(lane == j, x, 0.0), axis=-1, keepdims=True)


def _plan_kernel(cnt_s, cnt_ref, route_ref, pos_ref, wt_ref, we_ref, wlo_ref, whi_ref, wf_ref, off_s):
    cnt = cnt_ref[...]
    chi = jnp.floor(cnt * (1.0 / 128.0))
    clo = cnt - chi * 128.0
    r = lax.broadcasted_iota(jnp.int32, (128, 128), 0)
    c = lax.broadcasted_iota(jnp.int32, (128, 128), 1)
    below = (r < c).astype(F32).astype(BF16)
    off = (128.0 * _dot(chi.astype(BF16), below) + _dot(clo.astype(BF16), below))[0:1, :]
    route = route_ref[...]
    lane = lax.broadcasted_iota(jnp.int32, route.shape, 1).astype(F32)
    pos1 = _lane_col(off, lane, _lane_col(route, lane, 0.0)) + _lane_col(route, lane, 4.0)
    pos2 = _lane_col(off, lane, _lane_col(route, lane, 1.0)) + _lane_col(route, lane, 5.0)
    pos_ref[...] = jnp.where(lane == 0.0, pos1, jnp.where(lane == 1.0, pos2, 0.0))

    @pl.when(pl.program_id(0) == 0)
    def _():
        def offs(e, acc):
            off_s[e] = acc
            return acc + cnt_s[e]

        off_s[N_EXPERTS] = lax.fori_loop(0, N_EXPERTS, offs, jnp.int32(0))

        def item(w, st):
            j, e, first = st
            active = j < N_ETILE
            jj = jnp.minimum(j, N_ETILE - 1)
            ee = jnp.minimum(e, N_EXPERTS - 1)
            tlo = jj * TE
            thi = tlo + TE
            end = off_s[ee + 1]
            wt_ref[w] = jj
            we_ref[w] = ee
            wlo_ref[w] = jnp.where(active, jnp.maximum(off_s[ee], tlo) - tlo, 0)
            whi_ref[w] = jnp.where(active, jnp.minimum(end, thi) - tlo, 0)
            wf_ref[w] = jnp.where(active, first, 0)
            adv_j = (active & (end >= thi)).astype(jnp.int32)
            adv_e = (active & (end <= thi)).astype(jnp.int32)
            return j + adv_j, e + adv_e, adv_j

        lax.fori_loop(0, N_WORK, item, (jnp.int32(0), jnp.int32(0), jnp.int32(1)))


def _plan(cnt_i, cnt, route):
    smem = pl.BlockSpec(memory_space=pltpu.SMEM)
    work = jax.ShapeDtypeStruct((N_WORK,), jnp.int32)
    return pl.pallas_call(
        _plan_kernel,
        grid=(1,),
        in_specs=[smem, pl.BlockSpec((8, 128), lambda i: (0, 0)), pl.BlockSpec((N_TOK, 128), lambda i: (0, 0))],
        out_specs=[pl.BlockSpec((N_TOK, 128), lambda i: (0, 0)), smem, smem, smem, smem, smem],
        out_shape=[jax.ShapeDtypeStruct((N_TOK, 128), F32), work, work, work, work, work],
        scratch_shapes=[pltpu.SMEM((N_EXPERTS + 1,), jnp.int32)],
        compiler_params=_cparams(("arbitrary",)),
        name="moe_plan",
    )(cnt_i, cnt, route)


def _dispatch_kernel(pos_s, h_ref, xs_hbm, sem):
    base = pl.program_id(0) * TM

    def copy(i, k):
        return pltpu.make_async_copy(h_ref.at[pl.ds(i, 1)], xs_hbm.at[pl.ds(pos_s[2 * (base + i) + k], 1)], sem)

    def start(i, carry):
        copy(i, 0).start(priority=0)
        copy(i, 1).start(priority=1)
        return carry

    def wait(i, carry):
        copy(i, 0).wait()
        copy(i, 1).wait()
        return carry

    lax.fori_loop(0, TM, start, 0, unroll=8)
    lax.fori_loop(0, TM, wait, 0, unroll=8)


def _dispatch(pos, h):
    any_spec = pl.BlockSpec(memory_space=pl.ANY)
    return pl.pallas_call(
        _dispatch_kernel,
        grid_spec=pltpu.PrefetchScalarGridSpec(
            num_scalar_prefetch=1, grid=(N_TOK // TM,),
            in_specs=[pl.BlockSpec((TM, D_MODEL), lambda i, p: (i, 0))], out_specs=any_spec,
            scratch_shapes=[pltpu.SemaphoreType.DMA(())]),
        out_shape=jax.ShapeDtypeStruct((N_ASSIGN, D_MODEL), F32),
        compiler_params=_cparams(("arbitrary",)),
        name="moe_dispatch",
    )(pos, h)


def _experts_kernel(wt_s, we_s, wlo_s, whi_s, wf_s, xs_ref, w1_ref, w3_ref, w2_ref, o_ref, w1_s, w3_s, w2_s):
    w = pl.program_id(0)
    first = wf_s[w] == 1
    nonempty = whi_s[w] > wlo_s[w]

    @pl.when((w == 0) | (we_s[w] != we_s[jnp.maximum(w - 1, 0)]))
    def _():
        w1_s[...] = w1_ref[...].astype(BF16)
        w3_s[...] = w3_ref[...].astype(BF16)
        w2_s[...] = w2_ref[...].astype(BF16)

    @pl.when(first & jnp.logical_not(nonempty))
    def _():
        o_ref[...] = jnp.zeros(o_ref.shape, F32)

    def expert_rows():
        x = xs_ref[...].astype(BF16)
        a = _dot(x, w1_s[...])
        b = _dot(x, w3_s[...])
        y = _dot((a * _sigmoid(a) * b).astype(BF16), w2_s[...])
        row = lax.broadcasted_iota(jnp.int32, (TE, 1), 0)
        return (row >= wlo_s[w]) & (row < whi_s[w]), y

    @pl.when(nonempty & first)
    def _():
        mine, y = expert_rows()
        o_ref[...] = jnp.where(mine, y, 0.0)

    @pl.when(nonempty & jnp.logical_not(first))
    def _():
        mine, y = expert_rows()
        o_ref[...] = jnp.where(mine, y, o_ref[...])


def _experts(work, xs, l, w1, w3, w2):
    tile = lambda w, wt, we, wlo, whi, wf: (wt[w], 0)
    wspec = lambda shape: pl.BlockSpec((None, None) + shape, lambda w, wt, we, wlo, whi, wf: (l, we[w], 0, 0))
    return pl.pallas_call(
        _experts_kernel,
        grid_spec=pltpu.PrefetchScalarGridSpec(
            num_scalar_prefetch=5, grid=(N_WORK,),
            in_specs=[pl.BlockSpec((TE, D_MODEL), tile), wspec((D_MODEL, EXPERT_FF)),
                      wspec((D_MODEL, EXPERT_FF)), wspec((EXPERT_FF, D_MODEL))],
            out_specs=pl.BlockSpec((TE, D_MODEL), tile),
            scratch_shapes=[pltpu.VMEM((D_MODEL, EXPERT_FF), BF16), pltpu.VMEM((D_MODEL, EXPERT_FF), BF16),
                            pltpu.VMEM((EXPERT_FF, D_MODEL), BF16)]),
        out_shape=jax.ShapeDtypeStruct((N_ASSIGN, D_MODEL), F32),
        compiler_params=_cparams(("arbitrary",)),
        name="moe_experts",
    )(*work, xs, w1, w3, w2)


def _combine_kernel(pos_s, x_ref, route_ref, mod_ref, fw_ref, ys_hbm, *refs, final):
    if final:
        oa_ref, ob_ref, y1_s, y2_s, sem = refs
    else:
        o_ref, y1_s, y2_s, sem = refs
    base = pl.program_id(0) * TM

    def copy(i, k):
        dst = y1_s if k == 0 else y2_s
        return pltpu.make_async_copy(ys_hbm.at[pl.ds(pos_s[2 * (base + i) + k], 1)], dst.at[pl.ds(i, 1)], sem)

    def start(i, carry):
        copy(i, 0).start(priority=0)
        copy(i, 1).start(priority=1)
        return carry

    def wait(i, carry):
        copy(i, 0).wait()
        copy(i, 1).wait()
        return carry

    lax.fori_loop(0, TM, start, 0, unroll=8)
    lax.fori_loop(0, TM, wait, 0, unroll=8)
    route = route_ref[...]
    lane = lax.broadcasted_iota(jnp.int32, route.shape, 1)
    moe = _lane_col(route, lane, 2) * y1_s[...] + _lane_col(route, lane, 3) * y2_s[...]
    x = x_ref[...] + mod_ref[:, 5 * D_MODEL:6 * D_MODEL] * moe
    if final:
        y = _rms(x, fw_ref[...])
        is_ctx = pl.program_id(0) < N_CTX // TM

        @pl.when(is_ctx)
        def _():
            oa_ref[...] = y

        @pl.when(jnp.logical_not(is_ctx))
        def _():
            ob_ref[...] = y
    else:
        o_ref[...] = x


def _combine(pos, x, route, mods4, l, fw, ys, final):
    row = lambda w: pl.BlockSpec((TM, w), lambda i, p: (i, 0))
    if final:
        out_specs = _pair_specs(TM, D_MODEL, True)
        out_shape = [jax.ShapeDtypeStruct((N_CTX, D_MODEL), F32), jax.ShapeDtypeStruct((N_SMP, D_MODEL), F32)]
    else:
        out_specs = row(D_MODEL)
        out_shape = jax.ShapeDtypeStruct((N_TOK, D_MODEL), F32)
    return pl.pallas_call(
        functools.partial(_combine_kernel, final=final),
        grid_spec=pltpu.PrefetchScalarGridSpec(
            num_scalar_prefetch=1, grid=(N_TOK // TM,),
            in_specs=[row(D_MODEL), row(128),
                      pl.BlockSpec((None, None, 1, 6 * D_MODEL), lambda i, p: (l, _mod_index(i, TM), 0, 0)),
                      pl.BlockSpec((1, D_MODEL), lambda i, p: (0, 0)),
                      pl.BlockSpec(memory_space=pl.ANY)],
            out_specs=out_specs,
            scratch_shapes=[pltpu.VMEM((TM, D_MODEL), F32), pltpu.VMEM((TM, D_MODEL), F32),
                            pltpu.SemaphoreType.DMA(())]),
        out_shape=out_shape,
        compiler_params=_cparams(("arbitrary",)),
        name="moe_combine_final" if final else "moe_combine",
    )(pos, x, route, mods4, fw, ys)


def _rope_swap(w):
    nf = MLA_ROPE // 4
    g = w.reshape(w.shape[:-1] + (2, 2, nf))
    return jnp.stack([-g[..., 1, :], g[..., 0, :]], axis=-2).reshape(w.shape)


def _rope_tables():
    nf = MLA_ROPE // 4
    pos = np.arange(DEC_SEQ)
    row = (pos // GRID_W).astype(np.float32)
    col = (pos % GRID_W).astype(np.float32)
    inv = (np.float32(ROPE_THETA) ** (-np.arange(nf, dtype=np.float32) / np.float32(nf))).astype(np.float32)
    ar = (row[:, None] * inv).astype(np.float32)
    ac = (col[:, None] * inv).astype(np.float32)
    cos = np.concatenate([np.cos(ar), np.cos(ar), np.cos(ac), np.cos(ac)], axis=1)
    sin = np.concatenate([np.sin(ar), np.sin(ar), np.sin(ac), np.sin(ac)], axis=1)
    cos = np.concatenate([np.ones((TM, MLA_ROPE)), cos], axis=0).astype(np.float32)
    sin = np.concatenate([np.zeros((TM, MLA_ROPE)), sin], axis=0).astype(np.float32)
    n = cos.shape[0]
    z32 = np.zeros((n, 32), np.float32)
    tqc = np.concatenate([np.ones((n, MLA_NOPE), np.float32), cos, z32], axis=1)
    tqs = np.concatenate([np.zeros((n, MLA_NOPE), np.float32), sin, z32], axis=1)
    tk = np.concatenate([z32, cos, sin, z32], axis=1)
    return jnp.asarray(tqc), jnp.asarray(tqs), jnp.asarray(tk)


def _prep_layer(l, w_in, gla_wg_f, gla_wg_b, mla_wuq, mla_wukv, w_out, moe_wg, moe_bg, moe_we, moe_be):
    offs = np.cumsum([0, 256, 256, 256, 256, 16, 16, 512, 256, 128, 32])
    seg = lambda i: w_in[l][:, offs[i]:offs[i + 1]]
    kr = seg(9)
    w_in_r = jnp.concatenate(
        [seg(0), seg(1), seg(2), seg(3), seg(6), seg(7), seg(8), seg(4), seg(5), kr, _rope_swap(kr),
         jnp.zeros((D_MODEL, 32), F32)], axis=1).astype(BF16)
    wgf = jnp.zeros((128, GLA_W), F32).at[0:16].set(gla_wg_f[l])
    wgb = jnp.zeros((128, GLA_W), F32).at[16:32].set(gla_wg_b[l])
    wq = mla_wuq[l].reshape(MLA_Q_RANK, MLA_HEADS, MLA_NOPE + MLA_ROPE)
    nope, rope = wq[..., :MLA_NOPE], wq[..., MLA_NOPE:]
    z = lambda n: jnp.zeros((MLA_Q_RANK, MLA_HEADS, n), F32)
    hw = MLA_HEADS * HEAD_BLK
    wa = jnp.concatenate([nope, rope, z(32)], axis=-1).reshape(MLA_Q_RANK, hw).astype(BF16)
    wb = jnp.concatenate([z(MLA_NOPE), _rope_swap(rope), z(32)], axis=-1).reshape(MLA_Q_RANK, hw).astype(BF16)
    wkv = mla_wukv[l].reshape(MLA_KV_RANK, MLA_HEADS, MLA_NOPE + MLA_V)
    wk = jnp.concatenate([wkv[..., :MLA_NOPE], jnp.zeros((MLA_KV_RANK, MLA_HEADS, 64), F32)],
                         axis=-1).reshape(MLA_KV_RANK, hw).astype(BF16)
    wv = wkv[..., MLA_NOPE:].reshape(MLA_KV_RANK, MLA_HEADS * MLA_V).astype(BF16)
    wp_np = np.zeros((128, MLA_HEADS, HEAD_BLK), np.float32)
    for j in range(MLA_ROPE):
        wp_np[32 + j, :, MLA_NOPE + j] = 1.0
        wp_np[64 + j, :, MLA_NOPE + j] = 1.0
    wp = jnp.asarray(wp_np.reshape(128, hw)).astype(BF16)
    wo = w_out[l].astype(BF16)
    wr = jnp.concatenate([moe_we[l], moe_wg[l], jnp.zeros((D_MODEL, 128 - N_EXPERTS - N_GROUPS), F32)], axis=1)
    br = jnp.concatenate([moe_be[l], moe_bg[l], jnp.zeros((128 - N_EXPERTS - N_GROUPS,), F32)]).reshape(1, 128)
    return dict(w_in_r=w_in_r, wgf=wgf, wgb=wgb, wa=wa, wb=wb, wk=wk, wv=wv, wp=wp, wo=wo, wr=wr, br=br)


def kernel(x_prompt, x_sample, cache_ckv, cache_krope, state_gla_fwd, state_gla_bwd, c, c_ctx, ada_w, ada_b,
           norm1_w, norm2_w, w_in, gla_wg_f, gla_bg_f, gla_wg_b, gla_bg_b, gla_norm_w, conv_w, conv_b,
           conv_ln_w, conv_ln_b, mla_qnorm_w, mla_wuq, mla_kvnorm_w, mla_wukv, w_out, moe_wg, moe_bg, moe_we,
           moe_be, moe_w1, moe_w3, moe_w2, final_norm_w):
    x = (x_prompt.reshape(N_CTX, D_MODEL), x_sample.reshape(N_SMP, D_MODEL))
    cond8 = jnp.concatenate([c_ctx[None, :], c, jnp.zeros((N_MOD - 1 - DEC_BATCH, D_MODEL), F32)], axis=0)
    mods4 = _ada_table(cond8, ada_w, ada_b).reshape(DEPTH, N_MOD, 1, 6 * D_MODEL)
    tabs = _rope_tables()
    row1 = lambda a: a.reshape(1, -1)

    ckv_new, kr_new, sf_new, sb_new = [], [], [], []
    for l in range(DEPTH):
        w = _prep_layer(l, w_in, gla_wg_f, gla_wg_b, mla_wuq, mla_wukv, w_out, moe_wg, moe_bg, moe_we, moe_be)
        p = _inproj(x, mods4, l, row1(norm1_w[l]), w["w_in_r"])

        gla_args = (p, w["wgf"], w["wgb"], row1(gla_bg_f[l]), row1(gla_bg_b[l]), row1(gla_norm_w[l]))
        og_c, sf, sb = _gla(*gla_args, SEQ, BATCH, 0)
        og_s = _gla(*gla_args, DEC_SEQ, DEC_BATCH, N_CTX // DEC_SEQ, (state_gla_fwd, l), (state_gla_bwd, l))[0]
        og = (og_c, og_s)

        cw = jnp.concatenate([conv_w[l], jnp.zeros((1, CONV_W), F32)], axis=0)
        conv_args = (p, cw, row1(conv_b[l]), row1(conv_ln_w[l]), row1(conv_ln_b[l]))
        oc = (_conv(*conv_args, SEQ, BATCH, 0), _conv(*conv_args, DEC_SEQ, DEC_BATCH, N_CTX // DEC_SEQ))

        q, k, v, ckvn, kr = _mla_prep(p, tabs, row1(mla_qnorm_w[l]), row1(mla_kvnorm_w[l]),
                                      w["wa"], w["wb"], w["wk"], w["wp"], w["wv"])
        cc = cache_ckv[:, l].reshape(DEC_BATCH * PAST_LEN, MLA_KV_RANK)
        ckr = cache_krope[:, l].reshape(DEC_BATCH * PAST_LEN, MLA_ROPE)
        ckr = jnp.pad(ckr, ((0, 0), (32, 64)))
        kc, vc = _cache_kv(cc, ckr, w["wk"], w["wp"], w["wv"])
        om = (_attention(q, k, v, SEQ, BATCH, 0), _attention(q, k, v, DEC_SEQ, DEC_BATCH, N_CTX, kc, vc))

        xn, h2, route, cnt = _outproj(x, og, oc, om, mods4, l, row1(norm2_w[l]), w["wo"], w["wr"], w["br"])
        posf, *work = _plan(cnt[0, :N_EXPERTS].astype(jnp.int32), cnt, route)
        pos = posf[:, 0:2].astype(jnp.int32).reshape(N_ASSIGN)
        ys = _experts(work, _dispatch(pos, h2), l, moe_w1, moe_w3, moe_w2)
        x = _combine(pos, xn, route, mods4, l, row1(final_norm_w), ys, l == DEPTH - 1)

        ckv_new.append(ckvn[:N_CTX].reshape(BATCH, SEQ, MLA_KV_RANK))
        kr_new.append(kr[:N_CTX, 32:32 + MLA_ROPE].reshape(BATCH, SEQ, MLA_ROPE))
        sf_new.append(sf)
        sb_new.append(sb)

    y_ctx, y_smp = x
    return (y_ctx.reshape(BATCH, SEQ, D_MODEL), y_smp.reshape(DEC_BATCH, DEC_SEQ, D_MODEL),
            jnp.stack(ckv_new, axis=1), jnp.stack(kr_new, axis=1),
            jnp.stack(sf_new, axis=1), jnp.stack(sb_new, axis=1))
```

```python
import functools

import numpy as np
import jax
import jax.numpy as jnp
from jax import lax
from jax.experimental import pallas as pl
from jax.experimental.pallas import tpu as pltpu

F32 = jnp.float32
BF16 = jnp.bfloat16

D_MODEL = 1024
BATCH = 16
SEQ = 256
DEPTH = 2
DEC_BATCH = 4
DEC_SEQ = 1024
PAST_LEN = 512
GRID_W = 64
NORM_EPS = 1e-6
GLA_HEADS = 4
GLA_DK = 64
GLA_DV = 64
GLA_W = 256
GLA_GATE_RANK = 16
GLA_TAU = 16.0
CONV_W = 256
CONV_K = 31
MLA_HEADS = 8
MLA_NOPE = 64
MLA_ROPE = 32
MLA_V = 64
MLA_Q_RANK = 256
MLA_KV_RANK = 128
ROPE_THETA = 10000.0
N_GROUPS = 4
EXPERTS_PER_GROUP = 8
N_EXPERTS = 32
EXPERT_FF = 256

N_CTX = BATCH * SEQ
N_SMP = DEC_BATCH * DEC_SEQ
N_TOK = N_CTX + N_SMP
N_MOD = 8
TM = 256
P_COLS = 2048
MISC_BLK = 15
CHUNK = 64
SUB = 16
NEG = -1e30
VMEM_LIMIT = 56 * 1024 * 1024


def _cparams(sem):
    return pltpu.CompilerParams(dimension_semantics=sem, vmem_limit_bytes=VMEM_LIMIT)


def _dot(a, b):
    return jnp.dot(a, b, preferred_element_type=F32)


def _dot_nt(a, b):
    return lax.dot_general(a, b, (((1,), (1,)), ((), ())), preferred_element_type=F32)


def _split2(x):
    hi = x.astype(BF16)
    lo = (x - hi.astype(F32)).astype(BF16)
    return hi, lo


def _dot3(a, w):
    ah, al = _split2(a)
    wh, wl = _split2(w)
    return _dot(ah, wh) + _dot(al, wh) + _dot(ah, wl)


def _sigmoid(x):
    return 1.0 / (1.0 + jnp.exp(-x))


def _rms(x, w):
    ms = jnp.mean(x * x, axis=-1, keepdims=True)
    return x * lax.rsqrt(ms + NORM_EPS) * w


def _mod_index(i, tile):
    nctx = N_CTX // tile
    per = DEC_SEQ // tile
    return jnp.where(i < nctx, 0, 1 + (i - nctx) // per)


def _ada_kernel(c_ref, w_ref, b_ref, o_ref):
    c = c_ref[...]
    o_ref[...] = _dot3(c * _sigmoid(c), w_ref[...]) + b_ref[...]


def _ada_table(cond8, ada_w, ada_b):
    tn = 1024
    n6 = 6 * D_MODEL
    return pl.pallas_call(
        _ada_kernel,
        grid=(DEPTH, n6 // tn),
        in_specs=[
            pl.BlockSpec((N_MOD, D_MODEL), lambda l, j: (0, 0)),
            pl.BlockSpec((None, D_MODEL, tn), lambda l, j: (l, 0, j)),
            pl.BlockSpec((None, 1, tn), lambda l, j: (l, 0, j)),
        ],
        out_specs=pl.BlockSpec((None, N_MOD, tn), lambda l, j: (l, 0, j)),
        out_shape=jax.ShapeDtypeStruct((DEPTH, N_MOD, n6), F32),
        compiler_params=_cparams(("parallel", "parallel")),
        name="ada_table",
    )(cond8, ada_w, ada_b.reshape(DEPTH, 1, n6))


def _pair_specs(tile, width, split):
    nctx = N_CTX // tile
    offb = 0 if split else nctx
    return [pl.BlockSpec((tile, width), lambda i, *_: (jnp.minimum(i, nctx - 1), 0)),
            pl.BlockSpec((tile, width), lambda i, *_: (jnp.maximum(i - nctx, 0) + offb, 0))]


def _pair(x):
    return x if isinstance(x, tuple) else (x, x)


def _pick(tile, a_ref, b_ref):
    return jnp.where(pl.program_id(0) < N_CTX // tile, a_ref[...], b_ref[...])


TM_IN = 512


def _inproj_kernel(xa_ref, xb_ref, mod_ref, nw_ref, w_ref, o_ref):
    mod = mod_ref[...]
    x = _pick(TM_IN, xa_ref, xb_ref)
    h = _rms(x, nw_ref[...]) * (1.0 + mod[:, D_MODEL:2 * D_MODEL]) + mod[:, 0:D_MODEL]
    o_ref[...] = _dot(h.astype(BF16), w_ref[...])


def _inproj(x, mods4, l, norm_w, w_in_r):
    return pl.pallas_call(
        _inproj_kernel,
        grid=(N_TOK // TM_IN,),
        in_specs=_pair_specs(TM_IN, D_MODEL, isinstance(x, tuple)) + [
            pl.BlockSpec((None, None, 1, 6 * D_MODEL), lambda i: (l, _mod_index(i, TM_IN), 0, 0)),
            pl.BlockSpec((1, D_MODEL), lambda i: (0, 0)),
            pl.BlockSpec((D_MODEL, P_COLS), lambda i: (0, 0)),
        ],
        out_specs=pl.BlockSpec((TM_IN, P_COLS), lambda i: (i, 0)),
        out_shape=jax.ShapeDtypeStruct((N_TOK, P_COLS), F32),
        compiler_params=_cparams(("parallel",)),
        name="inproj",
    )(*_pair(x), mods4, norm_w, w_in_r)


def _log_sigmoid(x):
    return jnp.minimum(x, 0.0) - jnp.log(1.0 + jnp.exp(-jnp.abs(x)))


def _split3(x):
    hi = x.astype(BF16)
    r = x - hi.astype(F32)
    mid = r.astype(BF16)
    lo = (r - mid.astype(F32)).astype(BF16)
    return hi, mid, lo


def _gla_consts(reverse):
    t = lax.broadcasted_iota(jnp.int32, (CHUNK, CHUNK), 0)
    s = lax.broadcasted_iota(jnp.int32, (CHUNK, CHUNK), 1)
    mid = jnp.bitwise_and(t, -SUB) + SUB // 2
    if reverse:
        cum = s >= t
        ref = s >= mid
    else:
        cum = s <= t
        ref = s <= mid
    cum_ref = jnp.concatenate([cum, ref], axis=0).astype(F32).astype(BF16)
    rt = jnp.bitwise_and(lax.broadcasted_iota(jnp.int32, (GLA_HEADS * SUB, CHUNK), 0), SUB - 1)
    cs = lax.broadcasted_iota(jnp.int32, (GLA_HEADS * SUB, CHUNK), 1)
    rowi = lax.broadcasted_iota(jnp.int32, (CHUNK, 1), 0)
    causal, valid = [], []
    for i in range(CHUNK // SUB):
        tq = rt + i * SUB
        causal.append((cs >= tq) if reverse else (cs <= tq))
        valid.append((rowi >= i * SUB) if reverse else (rowi < (i + 1) * SUB))
    return cum_ref, causal, valid


def _gla_local(jobs, nc, o_s, qe_s, u_s, dec_s, head_masks, block_mask):
    nsub = CHUNK // SUB
    rows = [pl.ds(pl.multiple_of(c * CHUNK, CHUNK), CHUNK) for c, *_ in jobs]
    bs = []
    for (c, d, qs, k, vb, vt, la_ref, consts), r in zip(jobs, rows):
        h0, h1, h2 = _split3(la_ref[r, :])
        br = _dot(consts[0], h0) + _dot(consts[0], h1) + _dot(consts[0], h2)
        bs.append((br[0:CHUNK], br[CHUNK:2 * CHUNK]))
    ops = []
    for (c, d, qs, k, vb, vt, la_ref, consts), (b, rr) in zip(jobs, bs):
        qhat = qs * jnp.exp(b - rr)
        for i in range(nsub):
            r_i = rr[i * SUB:i * SUB + 1]
            kt = (k * jnp.exp(jnp.where(consts[2][i], r_i - b, NEG))).astype(BF16)
            qi = qhat[i * SUB:(i + 1) * SUB]
            ops.append((jnp.concatenate([jnp.where(m, qi, 0.0) for m in head_masks], axis=0).astype(BF16), kt))
    scores = [_dot_nt(qbig, kt) for qbig, kt in ops]
    outs = []
    for j, (c, d, qs, k, vb, vt, la_ref, consts) in enumerate(jobs):
        for i in range(nsub):
            a = jnp.where(consts[1][i], scores[j * nsub + i], 0.0)
            outs.append(_dot(a.astype(BF16), vb))
    for j, ((c, d, qs, k, vb, vt, la_ref, consts), (b, rr), r) in enumerate(zip(jobs, bs, rows)):
        parts = []
        for i in range(nsub):
            ov = outs[j * nsub + i]
            oi = jnp.where(head_masks[0], ov[0:SUB], 0.0)
            for h in range(1, GLA_HEADS):
                oi = oi + jnp.where(head_masks[h], ov[h * SUB:(h + 1) * SUB], 0.0)
            parts.append(oi)
        o_s[d, r, :] = jnp.concatenate(parts, axis=0)
        qe_s[d, r, :] = (qs * jnp.exp(b)).astype(BF16)
        b_last = b[0:1] if d == 1 else b[CHUNK - 1:CHUNK]
        kl = (k * jnp.exp(b_last - b)).astype(BF16)
        idx = d * nc + c
        u_s[idx] = jnp.where(block_mask, _dot(vt, kl), 0.0)
        dec_s[pl.ds(pl.multiple_of(idx * 8, 8), 8), :] = jnp.broadcast_to(jnp.exp(b_last), (8, GLA_W))


def _gla_kernel(*refs, n, has_state):
    if has_state:
        (qkvg_ref, misc_ref, wgf_ref, wgb_ref, bgf_ref, bgb_ref, nw_ref, s0f_ref, s0b_ref,
         o_ref, laf_s, lab_s, o_s, qe_s, u_s, dec_s, sts_s, stf_s, stb_s) = refs
    else:
        (qkvg_ref, misc_ref, wgf_ref, wgb_ref, bgf_ref, bgb_ref, nw_ref,
         o_ref, sf_ref, sb_ref, laf_s, lab_s, o_s, qe_s, u_s, dec_s, sts_s, stf_s, stb_s) = refs
    misc = misc_ref[...]
    laf_s[...] = _log_sigmoid(_dot3(misc, wgf_ref[...]) + bgf_ref[...]) * (1.0 / GLA_TAU)
    lab_s[...] = _log_sigmoid(_dot3(misc, wgb_ref[...]) + bgb_ref[...]) * (1.0 / GLA_TAU)
    stf_s[...] = jnp.zeros((GLA_W, GLA_W), F32)
    stb_s[...] = jnp.zeros((GLA_W, GLA_W), F32)
    if has_state:
        for st_s, s0_ref in ((stf_s, s0f_ref), (stb_s, s0b_ref)):
            for h in range(GLA_HEADS):
                st_s[h * GLA_DK:(h + 1) * GLA_DK, h * GLA_DV:(h + 1) * GLA_DV] = s0_ref[h]
            st_s[...] = st_s[...].T

    lane = lax.broadcasted_iota(jnp.int32, (1, GLA_W), 1)
    head_masks = [jnp.right_shift(lane, 6) == h for h in range(GLA_HEADS)]
    bi = jnp.right_shift(lax.broadcasted_iota(jnp.int32, (GLA_W, GLA_W), 0), 6)
    bj = jnp.right_shift(lax.broadcasted_iota(jnp.int32, (GLA_W, GLA_W), 1), 6)
    block_mask = bi == bj
    consts_f = _gla_consts(False)
    consts_b = _gla_consts(True)
    nc = n // CHUNK

    def local(cc, carry):
        jobs = []
        for c in (2 * cc, 2 * cc + 1):
            rows = pl.ds(pl.multiple_of(c * CHUNK, CHUNK), CHUNK)
            qs = qkvg_ref[rows, 0:GLA_W] * (GLA_DK ** -0.5)
            k = qkvg_ref[rows, GLA_W:2 * GLA_W]
            v = qkvg_ref[rows, 2 * GLA_W:3 * GLA_W]
            vb = v.astype(BF16)
            vt = v.T.astype(BF16)
            jobs += [(c, 0, qs, k, vb, vt, laf_s, consts_f), (c, 1, qs, k, vb, vt, lab_s, consts_b)]
        _gla_local(jobs, nc, o_s, qe_s, u_s, dec_s, head_masks, block_mask)
        return carry

    lax.fori_loop(0, nc // 2, local, 0)

    def scan(j, carry):
        for d, st_s in enumerate((stf_s, stb_s)):
            idx = d * nc + (j if d == 0 else nc - 1 - j)
            st = st_s[...]
            sts_s[idx] = st.astype(BF16)
            st_s[...] = st * dec_s[pl.ds(pl.multiple_of(idx * 8, 8), 1), :] + u_s[idx]
        return carry

    lax.fori_loop(0, nc, scan, 0)

    def inter(cc, carry):
        jobs = [(c, d, pl.ds(pl.multiple_of(c * CHUNK, CHUNK), CHUNK))
                for c in (2 * cc, 2 * cc + 1) for d in range(2)]
        res = [_dot_nt(qe_s[d, r, :], sts_s[d * nc + c]) for c, d, r in jobs]
        for (c, d, r), o in zip(jobs, res):
            o_s[d, r, :] += o
        return carry

    lax.fori_loop(0, nc // 2, inter, 0)

    if not has_state:
        for st_s, s_ref in ((stf_s, sf_ref), (stb_s, sb_ref)):
            s = st_s[...].T
            for h in range(GLA_HEADS):
                s_ref[h] = s[h * GLA_DK:(h + 1) * GLA_DK, h * GLA_DV:(h + 1) * GLA_DV]

    ones_blk = block_mask.astype(F32).astype(BF16)
    rt = 256
    for r0 in range(0, n, rt):
        o = o_s[0, r0:r0 + rt, :] + o_s[1, r0:r0 + rt, :]
        hi, lo = _split2(o * o)
        ms = (_dot(hi, ones_blk) + _dot(lo, ones_blk)) * (1.0 / GLA_DV)
        g = qkvg_ref[r0:r0 + rt, 3 * GLA_W:4 * GLA_W]
        o_ref[r0:r0 + rt, :] = o * lax.rsqrt(ms + NORM_EPS) * nw_ref[...] * (g * _sigmoid(g))


def _gla(p, wgf, wgb, bgf, bgb, nw, n, nb, row_blk0, s0f=None, s0b=None):
    has_state = s0f is not None
    full = lambda shape: pl.BlockSpec(shape, lambda b: (0,) * len(shape))
    in_specs = [
        pl.BlockSpec((n, 4 * GLA_W), lambda b: (row_blk0 + b, 0)),
        pl.BlockSpec((n, 128), lambda b: (row_blk0 + b, MISC_BLK)),
        full((128, GLA_W)), full((128, GLA_W)), full((1, GLA_W)), full((1, GLA_W)), full((1, GLA_W)),
    ]
    args = [p, p, wgf, wgb, bgf, bgb, nw]
    out_specs = [pl.BlockSpec((n, GLA_W), lambda b: (b, 0))]
    out_shape = [jax.ShapeDtypeStruct((nb * n, GLA_W), F32)]
    if has_state:
        layer = s0f[1]
        st_spec = pl.BlockSpec((None, None, GLA_HEADS, GLA_DK, GLA_DV), lambda b: (b, layer, 0, 0, 0))
        in_specs += [st_spec, st_spec]
        args += [s0f[0], s0b[0]]
    else:
        st_spec = pl.BlockSpec((None, GLA_HEADS, GLA_DK, GLA_DV), lambda b: (b, 0, 0, 0))
        out_specs += [st_spec, st_spec]
        out_shape += [jax.ShapeDtypeStruct((nb, GLA_HEADS, GLA_DK, GLA_DV), F32)] * 2
    return pl.pallas_call(
        functools.partial(_gla_kernel, n=n, has_state=has_state),
        grid=(nb,),
        in_specs=in_specs,
        out_specs=out_specs,
        out_shape=out_shape,
        scratch_shapes=[
            pltpu.VMEM((n, GLA_W), F32), pltpu.VMEM((n, GLA_W), F32),
            pltpu.VMEM((2, n, GLA_W), F32),
            pltpu.VMEM((2, n, GLA_W), BF16),
            pltpu.VMEM((2 * n // CHUNK, GLA_W, GLA_W), F32),
            pltpu.VMEM((2 * n // CHUNK * 8, GLA_W), F32),
            pltpu.VMEM((2 * n // CHUNK, GLA_W, GLA_W), BF16),
            pltpu.VMEM((GLA_W, GLA_W), F32), pltpu.VMEM((GLA_W, GLA_W), F32),
        ],
        compiler_params=_cparams(("parallel",)),
        name="gla_state" if has_state else "gla_ctx",
    )(*args)


CONV_PAD = 16
CONV_ROWS = 128


def _conv_kernel(u_ref, w_ref, b_ref, lnw_ref, lnb_ref, o_ref, zp_s, *, n):
    u = u_ref[...]
    zp_s[0:CONV_PAD, :] = jnp.zeros((CONV_PAD, CONV_W), F32)
    zp_s[CONV_PAD + n:2 * CONV_PAD + n, :] = jnp.zeros((CONV_PAD, CONV_W), F32)
    zp_s[CONV_PAD:CONV_PAD + n, :] = u[:, 0:CONV_W] * _sigmoid(u[:, CONV_W:2 * CONV_W])
    off = CONV_PAD - CONV_K // 2
    for r0 in range(0, n, CONV_ROWS):
        acc = zp_s[r0 + off:r0 + off + CONV_ROWS, :] * w_ref[0:1, :]
        for j in range(1, CONV_K):
            acc = acc + zp_s[r0 + off + j:r0 + off + j + CONV_ROWS, :] * w_ref[j:j + 1, :]
        acc = acc + b_ref[...]
        mu = jnp.mean(acc, axis=-1, keepdims=True)
        xc = acc - mu
        var = jnp.mean(xc * xc, axis=-1, keepdims=True)
        y = xc * lax.rsqrt(var + NORM_EPS) * lnw_ref[...] + lnb_ref[...]
        o_ref[r0:r0 + CONV_ROWS, :] = y * _sigmoid(y)


def _conv(p, cw, cb, lnw, lnb, n, nb, row_blk0):
    full = lambda shape: pl.BlockSpec(shape, lambda b: (0,) * len(shape))
    return pl.pallas_call(
        functools.partial(_conv_kernel, n=n),
        grid=(nb,),
        in_specs=[
            pl.BlockSpec((n, 2 * CONV_W), lambda b: (row_blk0 + b, 2)),
            full((32, CONV_W)), full((1, CONV_W)), full((1, CONV_W)), full((1, CONV_W)),
        ],
        out_specs=pl.BlockSpec((n, CONV_W), lambda b: (b, 0)),
        out_shape=jax.ShapeDtypeStruct((nb * n, CONV_W), F32),
        scratch_shapes=[pltpu.VMEM((n + 2 * CONV_PAD, CONV_W), F32)],
        compiler_params=_cparams(("parallel",)),
        name="conv%d" % n,
    )(p, cw, cb, lnw, lnb)


HEAD_BLK = 128


def _kv_expand(ckvn, t, wk_ref, wp_ref, wv_ref, k_ref, v_ref):
    cb = ckvn.astype(BF16)
    th, tl = _split2(t)
    k_ref[...] = (_dot(cb, wk_ref[...]) + _dot(th, wp_ref[...]) + _dot(tl, wp_ref[...])).astype(BF16)
    v_ref[...] = _dot(cb, wv_ref[...]).astype(BF16)


def _mla_prep_kernel(cq_ref, cm_ref, tqc_ref, tqs_ref, tk_ref, qnw_ref, kvw_ref, wa_ref, wb_ref,
                     wk_ref, wp_ref, wv_ref, q_ref, k_ref, v_ref, ckv_ref, kr_ref):
    qn = _rms(cq_ref[...], qnw_ref[...]).astype(BF16)
    tqc = jnp.concatenate([tqc_ref[...]] * MLA_HEADS, axis=1)
    tqs = jnp.concatenate([tqs_ref[...]] * MLA_HEADS, axis=1)
    scale = (MLA_NOPE + MLA_ROPE) ** -0.5
    q_ref[...] = ((_dot(qn, wa_ref[...]) * tqc + _dot(qn, wb_ref[...]) * tqs) * scale).astype(BF16)
    cm = cm_ref[...]
    ckvn = _rms(cm[:, 0:MLA_KV_RANK], kvw_ref[...])
    ckv_ref[...] = ckvn
    t = cm[:, MLA_KV_RANK:2 * MLA_KV_RANK] * tk_ref[...]
    kr_ref[...] = t
    _kv_expand(ckvn, t, wk_ref, wp_ref, wv_ref, k_ref, v_ref)


def _rope_tile(i):
    nctx = N_CTX // TM
    return jnp.where(i < nctx, 0, 1 + (i - nctx) % (DEC_SEQ // TM))


def _mla_prep(p, tabs, qnw, kvw, wa, wb, wk, wp, wv):
    tqc, tqs, tk = tabs
    full = lambda shape: pl.BlockSpec(shape, lambda i: (0,) * len(shape))
    tab_spec = pl.BlockSpec((TM, HEAD_BLK), lambda i: (_rope_tile(i), 0))
    hw = MLA_HEADS * HEAD_BLK
    return pl.pallas_call(
        _mla_prep_kernel,
        grid=(N_TOK // TM,),
        in_specs=[
            pl.BlockSpec((TM, MLA_Q_RANK), lambda i: (i, 6)),
            pl.BlockSpec((TM, 2 * MLA_KV_RANK), lambda i: (i, 7)),
            tab_spec, tab_spec, tab_spec,
            full((1, MLA_Q_RANK)), full((1, MLA_KV_RANK)),
            full((MLA_Q_RANK, hw)), full((MLA_Q_RANK, hw)),
            full((MLA_KV_RANK, hw)), full((128, hw)), full((MLA_KV_RANK, MLA_HEADS * MLA_V)),
        ],
        out_specs=[
            pl.BlockSpec((TM, hw), lambda i: (i, 0)),
            pl.BlockSpec((TM, hw), lambda i: (i, 0)),
            pl.BlockSpec((TM, MLA_HEADS * MLA_V), lambda i: (i, 0)),
            pl.BlockSpec((TM, MLA_KV_RANK), lambda i: (i, 0)),
            pl.BlockSpec((TM, 128), lambda i: (i, 0)),
        ],
        out_shape=[
            jax.ShapeDtypeStruct((N_TOK, hw), BF16),
            jax.ShapeDtypeStruct((N_TOK, hw), BF16),
            jax.ShapeDtypeStruct((N_TOK, MLA_HEADS * MLA_V), BF16),
            jax.ShapeDtypeStruct((N_TOK, MLA_KV_RANK), F32),
            jax.ShapeDtypeStruct((N_TOK, 128), F32),
        ],
        compiler_params=_cparams(("parallel",)),
        name="mla_prep",
    )(p, p, tqc, tqs, tk, qnw, kvw, wa, wb, wk, wp, wv)


def _cache_kv_kernel(ckv_ref, kr_ref, wk_ref, wp_ref, wv_ref, k_ref, v_ref):
    _kv_expand(ckv_ref[...], kr_ref[...], wk_ref, wp_ref, wv_ref, k_ref, v_ref)


def _cache_kv(ckv, kr, wk, wp, wv):
    n = DEC_BATCH * PAST_LEN
    full = lambda shape: pl.BlockSpec(shape, lambda i: (0,) * len(shape))
    hw = MLA_HEADS * HEAD_BLK
    return pl.pallas_call(
        _cache_kv_kernel,
        grid=(n // PAST_LEN,),
        in_specs=[
            pl.BlockSpec((PAST_LEN, MLA_KV_RANK), lambda i: (i, 0)),
            pl.BlockSpec((PAST_LEN, 128), lambda i: (i, 0)),
            full((MLA_KV_RANK, hw)), full((128, hw)), full((MLA_KV_RANK, MLA_HEADS * MLA_V)),
        ],
        out_specs=[
            pl.BlockSpec((PAST_LEN, hw), lambda i: (i, 0)),
            pl.BlockSpec((PAST_LEN, MLA_HEADS * MLA_V), lambda i: (i, 0)),
        ],
        out_shape=[
            jax.ShapeDtypeStruct((n, hw), BF16),
            jax.ShapeDtypeStruct((n, MLA_HEADS * MLA_V), BF16),
        ],
        compiler_params=_cparams(("parallel",)),
        name="cache_kv",
    )(ckv, kr, wk, wp, wv)


TQ = 256


def _attn_kernel(*refs, has_cache):
    if has_cache:
        q_ref, k_ref, v_ref, kc_ref, vc_ref, o_ref = refs
    else:
        q_ref, k_ref, v_ref, o_ref = refs
    cols = [slice(hh * HEAD_BLK, (hh + 1) * HEAD_BLK) for hh in range(2)]
    s = [_dot_nt(q_ref[:, c], k_ref[:, c]) for c in cols]
    m = [jnp.max(x, axis=-1, keepdims=True) for x in s]
    if has_cache:
        sc = [_dot_nt(q_ref[:, c], kc_ref[:, c]) for c in cols]
        m = [jnp.maximum(a, jnp.max(x, axis=-1, keepdims=True)) for a, x in zip(m, sc)]
        pc = [jnp.exp(x - a) for x, a in zip(sc, m)]
    p = [jnp.exp(x - a) for x, a in zip(s, m)]
    den = [jnp.sum(x, axis=-1, keepdims=True) for x in p]
    o = [_dot(x.astype(BF16), v_ref[...]) for x in p]
    if has_cache:
        den = [a + jnp.sum(x, axis=-1, keepdims=True) for a, x in zip(den, pc)]
        o = [a + _dot(x.astype(BF16), vc_ref[...]) for a, x in zip(o, pc)]
    outs = [a / b for a, b in zip(o, den)]
    lane = lax.broadcasted_iota(jnp.int32, (1, 2 * MLA_V), 1)
    o_ref[...] = jnp.where(lane < MLA_V, outs[0], outs[1])


def _attention(q, k, v, n, nb, row0, kc=None, vc=None):
    has_cache = kc is not None
    nq = n // TQ
    rb0 = row0 // TQ
    kb0 = row0 // n
    in_specs = [
        pl.BlockSpec((TQ, 2 * HEAD_BLK), lambda b, h, i: (rb0 + b * nq + i, h)),
        pl.BlockSpec((n, 2 * HEAD_BLK), lambda b, h, i: (kb0 + b, h)),
        pl.BlockSpec((n, 2 * MLA_V), lambda b, h, i: (kb0 + b, h)),
    ]
    args = [q, k, v]
    if has_cache:
        in_specs += [
            pl.BlockSpec((PAST_LEN, 2 * HEAD_BLK), lambda b, h, i: (b, h)),
            pl.BlockSpec((PAST_LEN, 2 * MLA_V), lambda b, h, i: (b, h)),
        ]
        args += [kc, vc]
    return pl.pallas_call(
        functools.partial(_attn_kernel, has_cache=has_cache),
        grid=(nb, MLA_HEADS // 2, nq),
        in_specs=in_specs,
        out_specs=pl.BlockSpec((TQ, 2 * MLA_V), lambda b, h, i: (b * nq + i, h)),
        out_shape=jax.ShapeDtypeStruct((nb * n, MLA_HEADS * MLA_V), F32),
        compiler_params=_cparams(("parallel", "parallel", "parallel")),
        name="attn_cache" if has_cache else "attn_ctx",
    )(*args)


def _outproj_kernel(xa_ref, xb_ref, oga_ref, ogb_ref, oca_ref, ocb_ref, oma_ref, omb_ref, mod_ref, nw_ref,
                    wo_ref, wr_ref, br_ref, xo_ref, h_ref, route_ref, cnt_ref, carry_s, wrh_s, wrl_s):
    @pl.when(pl.program_id(0) == 0)
    def _():
        carry_s[...] = jnp.zeros(carry_s.shape, F32)
        wrh, wrl = _split2(wr_ref[...])
        wrh_s[...] = wrh
        wrl_s[...] = wrl

    attn = (_dot(_pick(TM, oga_ref, ogb_ref).astype(BF16), wo_ref[0:GLA_W, :])
            + _dot(_pick(TM, oca_ref, ocb_ref).astype(BF16), wo_ref[GLA_W:GLA_W + CONV_W, :])
            + _dot(_pick(TM, oma_ref, omb_ref).astype(BF16), wo_ref[GLA_W + CONV_W:D_MODEL, :]))
    mod = mod_ref[...]
    xn = _pick(TM, xa_ref, xb_ref) + mod[:, 2 * D_MODEL:3 * D_MODEL] * attn
    xo_ref[...] = xn
    h = _rms(xn, nw_ref[...]) * (1.0 + mod[:, 4 * D_MODEL:5 * D_MODEL]) + mod[:, 3 * D_MODEL:4 * D_MODEL]
    _rows_to_tiles(h_ref, h)
    hh, hl = _split2(h)
    logits = _dot(hh, wrh_s[...]) + _dot(hl, wrh_s[...]) + _dot(hh, wrl_s[...]) + br_ref[...]
    lane = lax.broadcasted_iota(jnp.int32, logits.shape, 1).astype(F32)
    big = 1e9
    gl = jnp.where((lane >= N_EXPERTS) & (lane < N_EXPERTS + N_GROUPS), logits, NEG)
    gmax = jnp.max(gl, axis=-1, keepdims=True)
    gw = 1.0 / jnp.sum(jnp.exp(gl - gmax), axis=-1, keepdims=True)
    gi = jnp.min(jnp.where(gl == gmax, lane, big), axis=-1, keepdims=True) - N_EXPERTS
    lo = gi * EXPERTS_PER_GROUP
    el = jnp.where((lane >= lo) & (lane < lo + EXPERTS_PER_GROUP), logits, NEG)
    m1 = jnp.max(el, axis=-1, keepdims=True)
    i1 = jnp.min(jnp.where(el == m1, lane, big), axis=-1, keepdims=True)
    el2 = jnp.where(lane == i1, NEG, el)
    m2 = jnp.max(el2, axis=-1, keepdims=True)
    i2 = jnp.min(jnp.where(el2 == m2, lane, big), axis=-1, keepdims=True)
    e2 = jnp.exp(m2 - m1)
    p1 = 1.0 / (1.0 + e2)
    onehot = jnp.where((lane == i1) | (lane == i2), 1.0, 0.0)
    tr = lax.broadcasted_iota(jnp.int32, (TM, TM), 0)
    tc = lax.broadcasted_iota(jnp.int32, (TM, TM), 1)
    before = (tc < tr).astype(F32).astype(BF16)
    seen = _dot(before, onehot.astype(BF16)) + carry_s[0:1, :]
    rank1 = jnp.sum(jnp.where(lane == i1, seen, 0.0), axis=-1, keepdims=True)
    rank2 = jnp.sum(jnp.where(lane == i2, seen, 0.0), axis=-1, keepdims=True)
    carry_s[...] = carry_s[...] + jnp.sum(onehot, axis=0, keepdims=True)
    cnt_ref[...] = carry_s[...]
    cols = (i1, i2, gw * p1, gw * (e2 * p1), rank1, rank2)
    route = jnp.zeros(logits.shape, F32)
    for j, col in enumerate(cols):
        route = jnp.where(lane == j, col, route)
    route_ref[...] = route


def _outproj(x, og, oc, om, mods4, l, norm_w, wo, wr, br):
    full = lambda shape: pl.BlockSpec(shape, lambda i: (0,) * len(shape))
    row = lambda w: pl.BlockSpec((TM, w), lambda i: (i, 0))
    return pl.pallas_call(
        _outproj_kernel,
        grid=(N_TOK // TM,),
        in_specs=(_pair_specs(TM, D_MODEL, isinstance(x, tuple)) + _pair_specs(TM, GLA_W, True)
                  + _pair_specs(TM, CONV_W, True) + _pair_specs(TM, MLA_HEADS * MLA_V, True) + [
            pl.BlockSpec((None, None, 1, 6 * D_MODEL), lambda i: (l, _mod_index(i, TM), 0, 0)),
            full((1, D_MODEL)), full((D_MODEL, D_MODEL)), full((D_MODEL, 128)), full((1, 128)),
        ]),
        out_specs=[row(D_MODEL), pl.BlockSpec(_tiled(TM), lambda i: (i, 0)), row(128), full((8, 128))],
        out_shape=[
            jax.ShapeDtypeStruct((N_TOK, D_MODEL), F32),
            jax.ShapeDtypeStruct(_tiled(N_TOK), F32),
            jax.ShapeDtypeStruct((N_TOK, 128), F32),
            jax.ShapeDtypeStruct((8, 128), F32),
        ],
        scratch_shapes=[pltpu.VMEM((8, 128), F32), pltpu.VMEM((D_MODEL, 128), BF16),
                        pltpu.VMEM((D_MODEL, 128), BF16)],
        compiler_params=_cparams(("arbitrary",)),
        name="outproj_router",
    )(*_pair(x), *og, *oc, *om, mods4, norm_w, wo, wr, br)


N_ASSIGN = 2 * N_TOK
TE = 256
N_ETILE = N_ASSIGN // TE
N_WORK = N_ETILE + N_EXPERTS


ROW_SUB = 8


def _tiled(n):
    return (n * ROW_SUB, D_MODEL // ROW_SUB)


def _rows_to_tiles(ref, x):
    n = x.shape[0]
    for s in range(ROW_SUB):
        ref[pl.ds(s, n, stride=ROW_SUB), :] = x[:, s * 128:(s + 1) * 128]


def _tiles_to_rows(ref):
    n = ref.shape[0] // ROW_SUB
    return jnp.concatenate([ref[pl.ds(s, n, stride=ROW_SUB), :] for s in range(ROW_SUB)], axis=1)


def _row_tile(ref, t):
    return ref.at[pl.ds(pl.multiple_of(t * ROW_SUB, ROW_SUB), ROW_SUB)]


def _lane_col(x, lane, j):
    return jnp.sum(jnp.where(lane == j, x, 0.0), axis=-1, keepdims=True)


def _plan_kernel(cnt_s, cnt_ref, route_ref, pos_ref, wt_ref, we_ref, wlo_ref, whi_ref, wf_ref, off_s):
    cnt = cnt_ref[...]
    chi = jnp.floor(cnt * (1.0 / 128.0))
    clo = cnt - chi * 128.0
    r = lax.broadcasted_iota(jnp.int32, (128, 128), 0)
    c = lax.broadcasted_iota(jnp.int32, (128, 128), 1)
    below = (r < c).astype(F32).astype(BF16)
    off = (128.0 * _dot(chi.astype(BF16), below) + _dot(clo.astype(BF16), below))[0:1, :]
    route = route_ref[...]
    lane = lax.broadcasted_iota(jnp.int32, route.shape, 1).astype(F32)
    pos1 = _lane_col(off, lane, _lane_col(route, lane, 0.0)) + _lane_col(route, lane, 4.0)
    pos2 = _lane_col(off, lane, _lane_col(route, lane, 1.0)) + _lane_col(route, lane, 5.0)
    pos_ref[...] = jnp.where(lane == 0.0, pos1, jnp.where(lane == 1.0, pos2, 0.0))

    @pl.when(pl.program_id(0) == 0)
    def _():
        def offs(e, acc):
            off_s[e] = acc
            return acc + cnt_s[e]

        off_s[N_EXPERTS] = lax.fori_loop(0, N_EXPERTS, offs, jnp.int32(0))

        def item(w, st):
            j, e, first = st
            active = j < N_ETILE
            jj = jnp.minimum(j, N_ETILE - 1)
            ee = jnp.minimum(e, N_EXPERTS - 1)
            tlo = jj * TE
            thi = tlo + TE
            end = off_s[ee + 1]
            wt_ref[w] = jj
            we_ref[w] = ee
            wlo_ref[w] = jnp.where(active, jnp.maximum(off_s[ee], tlo) - tlo, 0)
            whi_ref[w] = jnp.where(active, jnp.minimum(end, thi) - tlo, 0)
            wf_ref[w] = jnp.where(active, first, 0)
            adv_j = (active & (end >= thi)).astype(jnp.int32)
            adv_e = (active & (end <= thi)).astype(jnp.int32)
            return j + adv_j, e + adv_e, adv_j

        lax.fori_loop(0, N_WORK, item, (jnp.int32(0), jnp.int32(0), jnp.int32(1)))


def _plan(cnt_i, cnt, route):
    smem = pl.BlockSpec(memory_space=pltpu.SMEM)
    work = jax.ShapeDtypeStruct((N_WORK,), jnp.int32)
    return pl.pallas_call(
        _plan_kernel,
        grid=(1,),
        in_specs=[smem, pl.BlockSpec((8, 128), lambda i: (0, 0)), pl.BlockSpec((N_TOK, 128), lambda i: (0, 0))],
        out_specs=[pl.BlockSpec((N_TOK, 128), lambda i: (0, 0)), smem, smem, smem, smem, smem],
        out_shape=[jax.ShapeDtypeStruct((N_TOK, 128), F32), work, work, work, work, work],
        scratch_shapes=[pltpu.SMEM((N_EXPERTS + 1,), jnp.int32)],
        compiler_params=_cparams(("arbitrary",)),
        name="moe_plan",
    )(cnt_i, cnt, route)


def _dispatch_kernel(pos_s, h_ref, xs_hbm, sem):
    base = pl.program_id(0) * TM

    def copy(i, k):
        return pltpu.make_async_copy(_row_tile(h_ref, i), _row_tile(xs_hbm, pos_s[2 * (base + i) + k]), sem)

    def start(i, carry):
        copy(i, 0).start(priority=0)
        copy(i, 1).start(priority=1)
        return carry

    def wait(i, carry):
        copy(i, 0).wait()
        copy(i, 1).wait()
        return carry

    lax.fori_loop(0, TM, start, 0, unroll=8)
    lax.fori_loop(0, TM, wait, 0, unroll=8)


def _dispatch(pos, h):
    any_spec = pl.BlockSpec(memory_space=pl.ANY)
    return pl.pallas_call(
        _dispatch_kernel,
        grid_spec=pltpu.PrefetchScalarGridSpec(
            num_scalar_prefetch=1, grid=(N_TOK // TM,),
            in_specs=[pl.BlockSpec(_tiled(TM), lambda i, p: (i, 0))], out_specs=any_spec,
            scratch_shapes=[pltpu.SemaphoreType.DMA(())]),
        out_shape=jax.ShapeDtypeStruct(_tiled(N_ASSIGN), F32),
        compiler_params=_cparams(("arbitrary",)),
        name="moe_dispatch",
    )(pos, h)


def _experts_kernel(wt_s, we_s, wlo_s, whi_s, wf_s, xs_ref, w1_ref, w3_ref, w2_ref, o_ref, w1_s, w3_s, w2_s):
    w = pl.program_id(0)
    first = wf_s[w] == 1
    nonempty = whi_s[w] > wlo_s[w]

    @pl.when((w == 0) | (we_s[w] != we_s[jnp.maximum(w - 1, 0)]))
    def _():
        w1_s[...] = w1_ref[...].astype(BF16)
        w3_s[...] = w3_ref[...].astype(BF16)
        w2_s[...] = w2_ref[...].astype(BF16)

    @pl.when(first & jnp.logical_not(nonempty))
    def _():
        o_ref[...] = jnp.zeros(o_ref.shape, F32)

    def expert_rows():
        x = _tiles_to_rows(xs_ref).astype(BF16)
        a = _dot(x, w1_s[...])
        b = _dot(x, w3_s[...])
        y = _dot((a * _sigmoid(a) * b).astype(BF16), w2_s[...])
        row = lax.broadcasted_iota(jnp.int32, (TE, 1), 0)
        return (row >= wlo_s[w]) & (row < whi_s[w]), y

    @pl.when(nonempty & first)
    def _():
        mine, y = expert_rows()
        _rows_to_tiles(o_ref, jnp.where(mine, y, 0.0))

    @pl.when(nonempty & jnp.logical_not(first))
    def _():
        mine, y = expert_rows()
        _rows_to_tiles(o_ref, jnp.where(mine, y, _tiles_to_rows(o_ref)))


def _experts(work, xs, l, w1, w3, w2):
    tile = pl.BlockSpec(_tiled(TE), lambda w, wt, we, wlo, whi, wf: (wt[w], 0))
    wspec = lambda shape: pl.BlockSpec((None, None) + shape, lambda w, wt, we, wlo, whi, wf: (l, we[w], 0, 0))
    return pl.pallas_call(
        _experts_kernel,
        grid_spec=pltpu.PrefetchScalarGridSpec(
            num_scalar_prefetch=5, grid=(N_WORK,),
            in_specs=[tile, wspec((D_MODEL, EXPERT_FF)), wspec((D_MODEL, EXPERT_FF)), wspec((EXPERT_FF, D_MODEL))],
            out_specs=tile,
            scratch_shapes=[pltpu.VMEM((D_MODEL, EXPERT_FF), BF16), pltpu.VMEM((D_MODEL, EXPERT_FF), BF16),
                            pltpu.VMEM((EXPERT_FF, D_MODEL), BF16)]),
        out_shape=jax.ShapeDtypeStruct(_tiled(N_ASSIGN), F32),
        compiler_params=_cparams(("arbitrary",)),
        name="moe_experts",
    )(*work, xs, w1, w3, w2)


def _combine_kernel(pos_s, x_ref, route_ref, mod_ref, fw_ref, ys_hbm, *refs, final):
    if final:
        oa_ref, ob_ref, y1_s, y2_s, sem = refs
    else:
        o_ref, y1_s, y2_s, sem = refs
    base = pl.program_id(0) * TM

    def copy(i, k):
        dst = y1_s if k == 0 else y2_s
        return pltpu.make_async_copy(_row_tile(ys_hbm, pos_s[2 * (base + i) + k]), _row_tile(dst, i), sem)

    def start(i, carry):
        copy(i, 0).start(priority=0)
        copy(i, 1).start(priority=1)
        return carry

    def wait(i, carry):
        copy(i, 0).wait()
        copy(i, 1).wait()
        return carry

    lax.fori_loop(0, TM, start, 0, unroll=8)
    lax.fori_loop(0, TM, wait, 0, unroll=8)
    route = route_ref[...]
    lane = lax.broadcasted_iota(jnp.int32, route.shape, 1)
    moe = _lane_col(route, lane, 2) * _tiles_to_rows(y1_s) + _lane_col(route, lane, 3) * _tiles_to_rows(y2_s)
    x = x_ref[...] + mod_ref[:, 5 * D_MODEL:6 * D_MODEL] * moe
    if final:
        y = _rms(x, fw_ref[...])
        is_ctx = pl.program_id(0) < N_CTX // TM

        @pl.when(is_ctx)
        def _():
            oa_ref[...] = y

        @pl.when(jnp.logical_not(is_ctx))
        def _():
            ob_ref[...] = y
    else:
        o_ref[...] = x


def _combine(pos, x, route, mods4, l, fw, ys, final):
    row = lambda w: pl.BlockSpec((TM, w), lambda i, p: (i, 0))
    if final:
        out_specs = _pair_specs(TM, D_MODEL, True)
        out_shape = [jax.ShapeDtypeStruct((N_CTX, D_MODEL), F32), jax.ShapeDtypeStruct((N_SMP, D_MODEL), F32)]
    else:
        out_specs = row(D_MODEL)
        out_shape = jax.ShapeDtypeStruct((N_TOK, D_MODEL), F32)
    return pl.pallas_call(
        functools.partial(_combine_kernel, final=final),
        grid_spec=pltpu.PrefetchScalarGridSpec(
            num_scalar_prefetch=1, grid=(N_TOK // TM,),
            in_specs=[row(D_MODEL), row(128),
                      pl.BlockSpec((None, None, 1, 6 * D_MODEL), lambda i, p: (l, _mod_index(i, TM), 0, 0)),
                      pl.BlockSpec((1, D_MODEL), lambda i, p: (0, 0)),
                      pl.BlockSpec(memory_space=pl.ANY)],
            out_specs=out_specs,
            scratch_shapes=[pltpu.VMEM(_tiled(TM), F32), pltpu.VMEM(_tiled(TM), F32),
                            pltpu.SemaphoreType.DMA(())]),
        out_shape=out_shape,
        compiler_params=_cparams(("arbitrary",)),
        name="moe_combine_final" if final else "moe_combine",
    )(pos, x, route, mods4, fw, ys)


def _rope_swap(w):
    nf = MLA_ROPE // 4
    g = w.reshape(w.shape[:-1] + (2, 2, nf))
    return jnp.stack([-g[..., 1, :], g[..., 0, :]], axis=-2).reshape(w.shape)


def _rope_tables():
    nf = MLA_ROPE // 4
    pos = np.arange(DEC_SEQ)
    row = (pos // GRID_W).astype(np.float32)
    col = (pos % GRID_W).astype(np.float32)
    inv = (np.float32(ROPE_THETA) ** (-np.arange(nf, dtype=np.float32) / np.float32(nf))).astype(np.float32)
    ar = (row[:, None] * inv).astype(np.float32)
    ac = (col[:, None] * inv).astype(np.float32)
    cos = np.concatenate([np.cos(ar), np.cos(ar), np.cos(ac), np.cos(ac)], axis=1)
    sin = np.concatenate([np.sin(ar), np.sin(ar), np.sin(ac), np.sin(ac)], axis=1)
    cos = np.concatenate([np.ones((TM, MLA_ROPE)), cos], axis=0).astype(np.float32)
    sin = np.concatenate([np.zeros((TM, MLA_ROPE)), sin], axis=0).astype(np.float32)
    n = cos.shape[0]
    z32 = np.zeros((n, 32), np.float32)
    tqc = np.concatenate([np.ones((n, MLA_NOPE), np.float32), cos, z32], axis=1)
    tqs = np.concatenate([np.zeros((n, MLA_NOPE), np.float32), sin, z32], axis=1)
    tk = np.concatenate([z32, cos, sin, z32], axis=1)
    return jnp.asarray(tqc), jnp.asarray(tqs), jnp.asarray(tk)


def _prep_layer(l, w_in, gla_wg_f, gla_wg_b, mla_wuq, mla_wukv, w_out, moe_wg, moe_bg, moe_we, moe_be):
    offs = np.cumsum([0, 256, 256, 256, 256, 16, 16, 512, 256, 128, 32])
    seg = lambda i: w_in[l][:, offs[i]:offs[i + 1]]
    kr = seg(9)
    w_in_r = jnp.concatenate(
        [seg(0), seg(1), seg(2), seg(3), seg(6), seg(7), seg(8), seg(4), seg(5), kr, _rope_swap(kr),
         jnp.zeros((D_MODEL, 32), F32)], axis=1).astype(BF16)
    wgf = jnp.zeros((128, GLA_W), F32).at[0:16].set(gla_wg_f[l])
    wgb = jnp.zeros((128, GLA_W), F32).at[16:32].set(gla_wg_b[l])
    wq = mla_wuq[l].reshape(MLA_Q_RANK, MLA_HEADS, MLA_NOPE + MLA_ROPE)
    nope, rope = wq[..., :MLA_NOPE], wq[..., MLA_NOPE:]
    z = lambda n: jnp.zeros((MLA_Q_RANK, MLA_HEADS, n), F32)
    hw = MLA_HEADS * HEAD_BLK
    wa = jnp.concatenate([nope, rope, z(32)], axis=-1).reshape(MLA_Q_RANK, hw).astype(BF16)
    wb = jnp.concatenate([z(MLA_NOPE), _rope_swap(rope), z(32)], axis=-1).reshape(MLA_Q_RANK, hw).astype(BF16)
    wkv = mla_wukv[l].reshape(MLA_KV_RANK, MLA_HEADS, MLA_NOPE + MLA_V)
    wk = jnp.concatenate([wkv[..., :MLA_NOPE], jnp.zeros((MLA_KV_RANK, MLA_HEADS, 64), F32)],
                         axis=-1).reshape(MLA_KV_RANK, hw).astype(BF16)
    wv = wkv[..., MLA_NOPE:].reshape(MLA_KV_RANK, MLA_HEADS * MLA_V).astype(BF16)
    wp_np = np.zeros((128, MLA_HEADS, HEAD_BLK), np.float32)
    for j in range(MLA_ROPE):
        wp_np[32 + j, :, MLA_NOPE + j] = 1.0
        wp_np[64 + j, :, MLA_NOPE + j] = 1.0
    wp = jnp.asarray(wp_np.reshape(128, hw)).astype(BF16)
    wo = w_out[l].astype(BF16)
    wr = jnp.concatenate([moe_we[l], moe_wg[l], jnp.zeros((D_MODEL, 128 - N_EXPERTS - N_GROUPS), F32)], axis=1)
    br = jnp.concatenate([moe_be[l], moe_bg[l], jnp.zeros((128 - N_EXPERTS - N_GROUPS,), F32)]).reshape(1, 128)
    return dict(w_in_r=w_in_r, wgf=wgf, wgb=wgb, wa=wa, wb=wb, wk=wk, wv=wv, wp=wp, wo=wo, wr=wr, br=br)


def kernel(x_prompt, x_sample, cache_ckv, cache_krope, state_gla_fwd, state_gla_bwd, c, c_ctx, ada_w, ada_b,
           norm1_w, norm2_w, w_in, gla_wg_f, gla_bg_f, gla_wg_b, gla_bg_b, gla_norm_w, conv_w, conv_b,
           conv_ln_w, conv_ln_b, mla_qnorm_w, mla_wuq, mla_kvnorm_w, mla_wukv, w_out, moe_wg, moe_bg, moe_we,
           moe_be, moe_w1, moe_w3, moe_w2, final_norm_w):
    x = (x_prompt.reshape(N_CTX, D_MODEL), x_sample.reshape(N_SMP, D_MODEL))
    cond8 = jnp.concatenate([c_ctx[None, :], c, jnp.zeros((N_MOD - 1 - DEC_BATCH, D_MODEL), F32)], axis=0)
    mods4 = _ada_table(cond8, ada_w, ada_b).reshape(DEPTH, N_MOD, 1, 6 * D_MODEL)
    tabs = _rope_tables()
    row1 = lambda a: a.reshape(1, -1)

    ckv_new, kr_new, sf_new, sb_new = [], [], [], []
    for l in range(DEPTH):
        w = _prep_layer(l, w_in, gla_wg_f, gla_wg_b, mla_wuq, mla_wukv, w_out, moe_wg, moe_bg, moe_we, moe_be)
        p = _inproj(x, mods4, l, row1(norm1_w[l]), w["w_in_r"])

        gla_args = (p, w["wgf"], w["wgb"], row1(gla_bg_f[l]), row1(gla_bg_b[l]), row1(gla_norm_w[l]))
        og_c, sf, sb = _gla(*gla_args, SEQ, BATCH, 0)
        og_s = _gla(*gla_args, DEC_SEQ, DEC_BATCH, N_CTX // DEC_SEQ, (state_gla_fwd, l), (state_gla_bwd, l))[0]
        og = (og_c, og_s)

        cw = jnp.concatenate([conv_w[l], jnp.zeros((1, CONV_W), F32)], axis=0)
        conv_args = (p, cw, row1(conv_b[l]), row1(conv_ln_w[l]), row1(conv_ln_b[l]))
        oc = (_conv(*conv_args, SEQ, BATCH, 0), _conv(*conv_args, DEC_SEQ, DEC_BATCH, N_CTX // DEC_SEQ))

        q, k, v, ckvn, kr = _mla_prep(p, tabs, row1(mla_qnorm_w[l]), row1(mla_kvnorm_w[l]),
                                      w["wa"], w["wb"], w["wk"], w["wp"], w["wv"])
        cc = cache_ckv[:, l].reshape(DEC_BATCH * PAST_LEN, MLA_KV_RANK)
        ckr = cache_krope[:, l].reshape(DEC_BATCH * PAST_LEN, MLA_ROPE)
        ckr = jnp.pad(ckr, ((0, 0), (32, 64)))
        kc, vc = _cache_kv(cc, ckr, w["wk"], w["wp"], w["wv"])
        om = (_attention(q, k, v, SEQ, BATCH, 0), _attention(q, k, v, DEC_SEQ, DEC_BATCH, N_CTX, kc, vc))

        xn, h2, route, cnt = _outproj(x, og, oc, om, mods4, l, row1(norm2_w[l]), w["wo"], w["wr"], w["br"])
        posf, *work = _plan(cnt[0, :N_EXPERTS].astype(jnp.int32), cnt, route)
        pos = posf[:, 0:2].astype(jnp.int32).reshape(N_ASSIGN)
        ys = _experts(work, _dispatch(pos, h2), l, moe_w1, moe_w3, moe_w2)
        x = _combine(pos, xn, route, mods4, l, row1(final_norm_w), ys, l == DEPTH - 1)

        ckv_new.append(ckvn[:N_CTX].reshape(BATCH, SEQ, MLA_KV_RANK))
        kr_new.append(kr[:N_CTX, 32:32 + MLA_ROPE].reshape(BATCH, SEQ, MLA_ROPE))
        sf_new.append(sf)
        sb_new.append(sb)

    y_ctx, y_smp = x
    return (y_ctx.reshape(BATCH, SEQ, D_MODEL), y_smp.reshape(DEC_BATCH, DEC_SEQ, D_MODEL),
            jnp.stack(ckv_new, axis=1), jnp.stack(kr_new, axis=1),
            jnp.stack(sf_new, axis=1), jnp.stack(sb_new, axis=1))
```

```python
import functools

import numpy as np
import jax
import jax.numpy as jnp
from jax import lax
from jax.experimental import pallas as pl
from jax.experimental.pallas import tpu as pltpu

F32 = jnp.float32
BF16 = jnp.bfloat16

D_MODEL = 1024
BATCH = 16
SEQ = 256
DEPTH = 2
DEC_BATCH = 4
DEC_SEQ = 1024
PAST_LEN = 512
GRID_W = 64
NORM_EPS = 1e-6
GLA_HEADS = 4
GLA_DK = 64
GLA_DV = 64
GLA_W = 256
GLA_GATE_RANK = 16
GLA_TAU = 16.0
CONV_W = 256
CONV_K = 31
MLA_HEADS = 8
MLA_NOPE = 64
MLA_ROPE = 32
MLA_V = 64
MLA_Q_RANK = 256
MLA_KV_RANK = 128
ROPE_THETA = 10000.0
N_GROUPS = 4
EXPERTS_PER_GROUP = 8
N_EXPERTS = 32
EXPERT_FF = 256

N_CTX = BATCH * SEQ
N_SMP = DEC_BATCH * DEC_SEQ
N_TOK = N_CTX + N_SMP
N_MOD = 8
TM = 256
P_COLS = 2048
MISC_BLK = 15
CHUNK = 64
SUB = 16
NEG = -1e30
VMEM_LIMIT = 56 * 1024 * 1024


def _cparams(sem):
    return pltpu.CompilerParams(dimension_semantics=sem, vmem_limit_bytes=VMEM_LIMIT)


def _dot(a, b):
    return jnp.dot(a, b, preferred_element_type=F32)


def _dot_nt(a, b):
    return lax.dot_general(a, b, (((1,), (1,)), ((), ())), preferred_element_type=F32)


def _split2(x):
    hi = x.astype(BF16)
    lo = (x - hi.astype(F32)).astype(BF16)
    return hi, lo


def _dot3(a, w):
    ah, al = _split2(a)
    wh, wl = _split2(w)
    return _dot(ah, wh) + _dot(al, wh) + _dot(ah, wl)


def _sigmoid(x):
    return 1.0 / (1.0 + jnp.exp(-x))


def _rms(x, w):
    ms = jnp.mean(x * x, axis=-1, keepdims=True)
    return x * lax.rsqrt(ms + NORM_EPS) * w


def _mod_index(i, tile):
    nctx = N_CTX // tile
    per = DEC_SEQ // tile
    return jnp.where(i < nctx, 0, 1 + (i - nctx) // per)


def _ada_kernel(c_ref, w_ref, b_ref, o_ref):
    c = c_ref[...]
    o_ref[...] = _dot3(c * _sigmoid(c), w_ref[...]) + b_ref[...]


def _ada_table(cond8, ada_w, ada_b):
    tn = 1024
    n6 = 6 * D_MODEL
    return pl.pallas_call(
        _ada_kernel,
        grid=(DEPTH, n6 // tn),
        in_specs=[
            pl.BlockSpec((N_MOD, D_MODEL), lambda l, j: (0, 0)),
            pl.BlockSpec((None, D_MODEL, tn), lambda l, j: (l, 0, j)),
            pl.BlockSpec((None, 1, tn), lambda l, j: (l, 0, j)),
        ],
        out_specs=pl.BlockSpec((None, N_MOD, tn), lambda l, j: (l, 0, j)),
        out_shape=jax.ShapeDtypeStruct((DEPTH, N_MOD, n6), F32),
        compiler_params=_cparams(("parallel", "parallel")),
        name="ada_table",
    )(cond8, ada_w, ada_b.reshape(DEPTH, 1, n6))


def _pair_specs(tile, width, split):
    nctx = N_CTX // tile
    offb = 0 if split else nctx
    return [pl.BlockSpec((tile, width), lambda i, *_: (jnp.minimum(i, nctx - 1), 0)),
            pl.BlockSpec((tile, width), lambda i, *_: (jnp.maximum(i - nctx, 0) + offb, 0))]


def _pair(x):
    return x if isinstance(x, tuple) else (x, x)


TM_IN = 512


def _inproj_kernel(xa_ref, xb_ref, mod_ref, nw_ref, w_ref, o_ref):
    mod = mod_ref[...]

    def project(x_ref):
        h = _rms(x_ref[...], nw_ref[...]) * (1.0 + mod[:, D_MODEL:2 * D_MODEL]) + mod[:, 0:D_MODEL]
        o_ref[...] = _dot(h.astype(BF16), w_ref[...])

    is_ctx = pl.program_id(0) < N_CTX // TM_IN
    pl.when(is_ctx)(lambda: project(xa_ref))
    pl.when(jnp.logical_not(is_ctx))(lambda: project(xb_ref))


def _inproj(x, mods4, l, norm_w, w_in_r):
    return pl.pallas_call(
        _inproj_kernel,
        grid=(N_TOK // TM_IN,),
        in_specs=_pair_specs(TM_IN, D_MODEL, isinstance(x, tuple)) + [
            pl.BlockSpec((None, None, 1, 6 * D_MODEL), lambda i: (l, _mod_index(i, TM_IN), 0, 0)),
            pl.BlockSpec((1, D_MODEL), lambda i: (0, 0)),
            pl.BlockSpec((D_MODEL, P_COLS), lambda i: (0, 0)),
        ],
        out_specs=pl.BlockSpec((TM_IN, P_COLS), lambda i: (i, 0)),
        out_shape=jax.ShapeDtypeStruct((N_TOK, P_COLS), F32),
        compiler_params=_cparams(("parallel",)),
        name="inproj",
    )(*_pair(x), mods4, norm_w, w_in_r)


def _log_sigmoid(x):
    return jnp.minimum(x, 0.0) - jnp.log(1.0 + jnp.exp(-jnp.abs(x)))


def _split3(x):
    hi = x.astype(BF16)
    r = x - hi.astype(F32)
    mid = r.astype(BF16)
    lo = (r - mid.astype(F32)).astype(BF16)
    return hi, mid, lo


def _gla_consts(reverse):
    t = lax.broadcasted_iota(jnp.int32, (CHUNK, CHUNK), 0)
    s = lax.broadcasted_iota(jnp.int32, (CHUNK, CHUNK), 1)
    mid = jnp.bitwise_and(t, -SUB) + SUB // 2
    if reverse:
        cum = s >= t
        ref = s >= mid
    else:
        cum = s <= t
        ref = s <= mid
    cum_ref = jnp.concatenate([cum, ref], axis=0).astype(F32).astype(BF16)
    rt = jnp.bitwise_and(lax.broadcasted_iota(jnp.int32, (GLA_HEADS * SUB, CHUNK), 0), SUB - 1)
    cs = lax.broadcasted_iota(jnp.int32, (GLA_HEADS * SUB, CHUNK), 1)
    rowi = lax.broadcasted_iota(jnp.int32, (CHUNK, 1), 0)
    causal, valid = [], []
    for i in range(CHUNK // SUB):
        tq = rt + i * SUB
        causal.append((cs >= tq) if reverse else (cs <= tq))
        valid.append((rowi >= i * SUB) if reverse else (rowi < (i + 1) * SUB))
    return cum_ref, causal, valid


def _gla_local(jobs, nc, o_s, qe_s, u_s, dec_s, head_masks, block_mask):
    nsub = CHUNK // SUB
    rows = [pl.ds(pl.multiple_of(c * CHUNK, CHUNK), CHUNK) for c, *_ in jobs]
    bs = []
    for (c, d, qs, k, vb, vt, la_ref, consts), r in zip(jobs, rows):
        h0, h1, h2 = _split3(la_ref[r, :])
        br = _dot(consts[0], h0) + _dot(consts[0], h1) + _dot(consts[0], h2)
        bs.append((br[0:CHUNK], br[CHUNK:2 * CHUNK]))
    ops = []
    for (c, d, qs, k, vb, vt, la_ref, consts), (b, rr) in zip(jobs, bs):
        qhat = qs * jnp.exp(b - rr)
        for i in range(nsub):
            r_i = rr[i * SUB:i * SUB + 1]
            kt = (k * jnp.exp(jnp.where(consts[2][i], r_i - b, NEG))).astype(BF16)
            qi = qhat[i * SUB:(i + 1) * SUB]
            ops.append((jnp.concatenate([jnp.where(m, qi, 0.0) for m in head_masks], axis=0).astype(BF16), kt))
    scores = [_dot_nt(qbig, kt) for qbig, kt in ops]
    outs = []
    for j, (c, d, qs, k, vb, vt, la_ref, consts) in enumerate(jobs):
        for i in range(nsub):
            a = jnp.where(consts[1][i], scores[j * nsub + i], 0.0)
            outs.append(_dot(a.astype(BF16), vb))
    for j, ((c, d, qs, k, vb, vt, la_ref, consts), (b, rr), r) in enumerate(zip(jobs, bs, rows)):
        parts = []
        for i in range(nsub):
            ov = outs[j * nsub + i]
            oi = jnp.where(head_masks[0], ov[0:SUB], 0.0)
            for h in range(1, GLA_HEADS):
                oi = oi + jnp.where(head_masks[h], ov[h * SUB:(h + 1) * SUB], 0.0)
            parts.append(oi)
        o_s[d, r, :] = jnp.concatenate(parts, axis=0)
        qe_s[d, r, :] = (qs * jnp.exp(b)).astype(BF16)
        b_last = b[0:1] if d == 1 else b[CHUNK - 1:CHUNK]
        kl = (k * jnp.exp(b_last - b)).astype(BF16)
        idx = d * nc + c
        u_s[idx] = jnp.where(block_mask, _dot(vt, kl), 0.0)
        dec_s[pl.ds(pl.multiple_of(idx * 8, 8), 8), :] = jnp.broadcast_to(jnp.exp(b_last), (8, GLA_W))


def _gla_kernel(*refs, n, has_state):
    if has_state:
        (qkvg_ref, misc_ref, wgf_ref, wgb_ref, bgf_ref, bgb_ref, nw_ref, s0f_ref, s0b_ref,
         o_ref, laf_s, lab_s, o_s, qe_s, u_s, dec_s, sts_s, stf_s, stb_s) = refs
    else:
        (qkvg_ref, misc_ref, wgf_ref, wgb_ref, bgf_ref, bgb_ref, nw_ref,
         o_ref, sf_ref, sb_ref, laf_s, lab_s, o_s, qe_s, u_s, dec_s, sts_s, stf_s, stb_s) = refs
    misc = misc_ref[...]
    laf_s[...] = _log_sigmoid(_dot3(misc, wgf_ref[...]) + bgf_ref[...]) * (1.0 / GLA_TAU)
    lab_s[...] = _log_sigmoid(_dot3(misc, wgb_ref[...]) + bgb_ref[...]) * (1.0 / GLA_TAU)
    stf_s[...] = jnp.zeros((GLA_W, GLA_W), F32)
    stb_s[...] = jnp.zeros((GLA_W, GLA_W), F32)
    if has_state:
        for st_s, s0_ref in ((stf_s, s0f_ref), (stb_s, s0b_ref)):
            for h in range(GLA_HEADS):
                st_s[h * GLA_DK:(h + 1) * GLA_DK, h * GLA_DV:(h + 1) * GLA_DV] = s0_ref[h]
            st_s[...] = st_s[...].T

    lane = lax.broadcasted_iota(jnp.int32, (1, GLA_W), 1)
    head_masks = [jnp.right_shift(lane, 6) == h for h in range(GLA_HEADS)]
    bi = jnp.right_shift(lax.broadcasted_iota(jnp.int32, (GLA_W, GLA_W), 0), 6)
    bj = jnp.right_shift(lax.broadcasted_iota(jnp.int32, (GLA_W, GLA_W), 1), 6)
    block_mask = bi == bj
    consts_f = _gla_consts(False)
    consts_b = _gla_consts(True)
    nc = n // CHUNK

    def local(cc, carry):
        jobs = []
        for c in (2 * cc, 2 * cc + 1):
            rows = pl.ds(pl.multiple_of(c * CHUNK, CHUNK), CHUNK)
            qs = qkvg_ref[rows, 0:GLA_W] * (GLA_DK ** -0.5)
            k = qkvg_ref[rows, GLA_W:2 * GLA_W]
            v = qkvg_ref[rows, 2 * GLA_W:3 * GLA_W]
            vb = v.astype(BF16)
            vt = v.T.astype(BF16)
            jobs += [(c, 0, qs, k, vb, vt, laf_s, consts_f), (c, 1, qs, k, vb, vt, lab_s, consts_b)]
        _gla_local(jobs, nc, o_s, qe_s, u_s, dec_s, head_masks, block_mask)
        return carry

    lax.fori_loop(0, nc // 2, local, 0)

    def scan(j, carry):
        for d, st_s in enumerate((stf_s, stb_s)):
            idx = d * nc + (j if d == 0 else nc - 1 - j)
            st = st_s[...]
            sts_s[idx] = st.astype(BF16)
            st_s[...] = st * dec_s[pl.ds(pl.multiple_of(idx * 8, 8), 1), :] + u_s[idx]
        return carry

    lax.fori_loop(0, nc, scan, 0)

    def inter(cc, carry):
        jobs = [(c, d, pl.ds(pl.multiple_of(c * CHUNK, CHUNK), CHUNK))
                for c in (2 * cc, 2 * cc + 1) for d in range(2)]
        res = [_dot_nt(qe_s[d, r, :], sts_s[d * nc + c]) for c, d, r in jobs]
        for (c, d, r), o in zip(jobs, res):
            o_s[d, r, :] += o
        return carry

    lax.fori_loop(0, nc // 2, inter, 0)

    if not has_state:
        for st_s, s_ref in ((stf_s, sf_ref), (stb_s, sb_ref)):
            s = st_s[...].T
            for h in range(GLA_HEADS):
                s_ref[h] = s[h * GLA_DK:(h + 1) * GLA_DK, h * GLA_DV:(h + 1) * GLA_DV]

    ones_blk = block_mask.astype(F32).astype(BF16)
    rt = 256
    for r0 in range(0, n, rt):
        o = o_s[0, r0:r0 + rt, :] + o_s[1, r0:r0 + rt, :]
        hi, lo = _split2(o * o)
        ms = (_dot(hi, ones_blk) + _dot(lo, ones_blk)) * (1.0 / GLA_DV)
        g = qkvg_ref[r0:r0 + rt, 3 * GLA_W:4 * GLA_W]
        o_ref[r0:r0 + rt, :] = o * lax.rsqrt(ms + NORM_EPS) * nw_ref[...] * (g * _sigmoid(g))


def _gla(p, wgf, wgb, bgf, bgb, nw, n, nb, row_blk0, s0f=None, s0b=None):
    has_state = s0f is not None
    full = lambda shape: pl.BlockSpec(shape, lambda b: (0,) * len(shape))
    in_specs = [
        pl.BlockSpec((n, 4 * GLA_W), lambda b: (row_blk0 + b, 0)),
        pl.BlockSpec((n, 128), lambda b: (row_blk0 + b, MISC_BLK)),
        full((128, GLA_W)), full((128, GLA_W)), full((1, GLA_W)), full((1, GLA_W)), full((1, GLA_W)),
    ]
    args = [p, p, wgf, wgb, bgf, bgb, nw]
    out_specs = [pl.BlockSpec((n, GLA_W), lambda b: (b, 0))]
    out_shape = [jax.ShapeDtypeStruct((nb * n, GLA_W), F32)]
    if has_state:
        layer = s0f[1]
        st_spec = pl.BlockSpec((None, None, GLA_HEADS, GLA_DK, GLA_DV), lambda b: (b, layer, 0, 0, 0))
        in_specs += [st_spec, st_spec]
        args += [s0f[0], s0b[0]]
    else:
        st_spec = pl.BlockSpec((None, GLA_HEADS, GLA_DK, GLA_DV), lambda b: (b, 0, 0, 0))
        out_specs += [st_spec, st_spec]
        out_shape += [jax.ShapeDtypeStruct((nb, GLA_HEADS, GLA_DK, GLA_DV), F32)] * 2
    return pl.pallas_call(
        functools.partial(_gla_kernel, n=n, has_state=has_state),
        grid=(nb,),
        in_specs=in_specs,
        out_specs=out_specs,
        out_shape=out_shape,
        scratch_shapes=[
            pltpu.VMEM((n, GLA_W), F32), pltpu.VMEM((n, GLA_W), F32),
            pltpu.VMEM((2, n, GLA_W), F32),
            pltpu.VMEM((2, n, GLA_W), BF16),
            pltpu.VMEM((2 * n // CHUNK, GLA_W, GLA_W), F32),
            pltpu.VMEM((2 * n // CHUNK * 8, GLA_W), F32),
            pltpu.VMEM((2 * n // CHUNK, GLA_W, GLA_W), BF16),
            pltpu.VMEM((GLA_W, GLA_W), F32), pltpu.VMEM((GLA_W, GLA_W), F32),
        ],
        compiler_params=_cparams(("parallel",)),
        name="gla_state" if has_state else "gla_ctx",
    )(*args)


CONV_PAD = 16
CONV_ROWS = 128
CONV_SHIFT_ROWS = 24


def _conv_kernel(u_ref, w_ref, b_ref, lnw_ref, lnb_ref, o_ref, zp_s, zsh_s, *, n):
    u = u_ref[...]
    zp_s[0:CONV_PAD, :] = jnp.zeros((CONV_PAD, CONV_W), F32)
    zp_s[CONV_PAD + n:2 * CONV_PAD + n, :] = jnp.zeros((CONV_PAD, CONV_W), F32)
    zp_s[CONV_PAD:CONV_PAD + n, :] = u[:, 0:CONV_W] * _sigmoid(u[:, CONV_W:2 * CONV_W])
    m = n + CONV_SHIFT_ROWS
    for s in range(1, 8):
        zsh_s[s - 1] = zp_s[s:s + m, :]
    off = CONV_PAD - CONV_K // 2

    def tap(r0, j):
        s = (off + j) % 8
        base = r0 + off + j - s
        return zp_s[base:base + CONV_ROWS, :] if s == 0 else zsh_s[s - 1, base:base + CONV_ROWS, :]

    for r0 in range(0, n, CONV_ROWS):
        acc = tap(r0, 0) * w_ref[0:1, :]
        for j in range(1, CONV_K):
            acc = acc + tap(r0, j) * w_ref[j:j + 1, :]
        acc = acc + b_ref[...]
        mu = jnp.mean(acc, axis=-1, keepdims=True)
        xc = acc - mu
        var = jnp.mean(xc * xc, axis=-1, keepdims=True)
        y = xc * lax.rsqrt(var + NORM_EPS) * lnw_ref[...] + lnb_ref[...]
        o_ref[r0:r0 + CONV_ROWS, :] = y * _sigmoid(y)


def _conv(p, cw, cb, lnw, lnb, n, nb, row_blk0):
    full = lambda shape: pl.BlockSpec(shape, lambda b: (0,) * len(shape))
    return pl.pallas_call(
        functools.partial(_conv_kernel, n=n),
        grid=(nb,),
        in_specs=[
            pl.BlockSpec((n, 2 * CONV_W), lambda b: (row_blk0 + b, 2)),
            full((32, CONV_W)), full((1, CONV_W)), full((1, CONV_W)), full((1, CONV_W)),
        ],
        out_specs=pl.BlockSpec((n, CONV_W), lambda b: (b, 0)),
        out_shape=jax.ShapeDtypeStruct((nb * n, CONV_W), F32),
        scratch_shapes=[pltpu.VMEM((n + 2 * CONV_PAD, CONV_W), F32),
                        pltpu.VMEM((7, n + CONV_SHIFT_ROWS, CONV_W), F32)],
        compiler_params=_cparams(("parallel",)),
        name="conv%d" % n,
    )(p, cw, cb, lnw, lnb)


HEAD_BLK = 128


def _kv_expand(ckvn, t, wk_ref, wp_ref, wv_ref, k_ref, v_ref):
    cb = ckvn.astype(BF16)
    th, tl = _split2(t)
    k_ref[...] = (_dot(cb, wk_ref[...]) + _dot(th, wp_ref[...]) + _dot(tl, wp_ref[...])).astype(BF16)
    v_ref[...] = _dot(cb, wv_ref[...]).astype(BF16)


def _mla_prep_kernel(cq_ref, cm_ref, tqc_ref, tqs_ref, tk_ref, qnw_ref, kvw_ref, wa_ref, wb_ref,
                     wk_ref, wp_ref, wv_ref, q_ref, k_ref, v_ref, ckv_ref, kr_ref):
    qn = _rms(cq_ref[...], qnw_ref[...]).astype(BF16)
    tqc = jnp.concatenate([tqc_ref[...]] * MLA_HEADS, axis=1)
    tqs = jnp.concatenate([tqs_ref[...]] * MLA_HEADS, axis=1)
    scale = (MLA_NOPE + MLA_ROPE) ** -0.5
    q_ref[...] = ((_dot(qn, wa_ref[...]) * tqc + _dot(qn, wb_ref[...]) * tqs) * scale).astype(BF16)
    cm = cm_ref[...]
    ckvn = _rms(cm[:, 0:MLA_KV_RANK], kvw_ref[...])
    ckv_ref[...] = ckvn
    t = cm[:, MLA_KV_RANK:2 * MLA_KV_RANK] * tk_ref[...]
    kr_ref[...] = t
    _kv_expand(ckvn, t, wk_ref, wp_ref, wv_ref, k_ref, v_ref)


def _rope_tile(i):
    nctx = N_CTX // TM
    return jnp.where(i < nctx, 0, 1 + (i - nctx) % (DEC_SEQ // TM))


def _mla_prep(p, tabs, qnw, kvw, wa, wb, wk, wp, wv):
    tqc, tqs, tk = tabs
    full = lambda shape: pl.BlockSpec(shape, lambda i: (0,) * len(shape))
    tab_spec = pl.BlockSpec((TM, HEAD_BLK), lambda i: (_rope_tile(i), 0))
    hw = MLA_HEADS * HEAD_BLK
    return pl.pallas_call(
        _mla_prep_kernel,
        grid=(N_TOK // TM,),
        in_specs=[
            pl.BlockSpec((TM, MLA_Q_RANK), lambda i: (i, 6)),
            pl.BlockSpec((TM, 2 * MLA_KV_RANK), lambda i: (i, 7)),
            tab_spec, tab_spec, tab_spec,
            full((1, MLA_Q_RANK)), full((1, MLA_KV_RANK)),
            full((MLA_Q_RANK, hw)), full((MLA_Q_RANK, hw)),
            full((MLA_KV_RANK, hw)), full((128, hw)), full((MLA_KV_RANK, MLA_HEADS * MLA_V)),
        ],
        out_specs=[
            pl.BlockSpec((TM, hw), lambda i: (i, 0)),
            pl.BlockSpec((TM, hw), lambda i: (i, 0)),
            pl.BlockSpec((TM, MLA_HEADS * MLA_V), lambda i: (i, 0)),
            pl.BlockSpec((TM, MLA_KV_RANK), lambda i: (i, 0)),
            pl.BlockSpec((TM, 128), lambda i: (i, 0)),
        ],
        out_shape=[
            jax.ShapeDtypeStruct((N_TOK, hw), BF16),
            jax.ShapeDtypeStruct((N_TOK, hw), BF16),
            jax.ShapeDtypeStruct((N_TOK, MLA_HEADS * MLA_V), BF16),
            jax.ShapeDtypeStruct((N_TOK, MLA_KV_RANK), F32),
            jax.ShapeDtypeStruct((N_TOK, 128), F32),
        ],
        compiler_params=_cparams(("parallel",)),
        name="mla_prep",
    )(p, p, tqc, tqs, tk, qnw, kvw, wa, wb, wk, wp, wv)


def _cache_kv_kernel(ckv_ref, kr_ref, wk_ref, wp_ref, wv_ref, k_ref, v_ref):
    _kv_expand(ckv_ref[...], kr_ref[...], wk_ref, wp_ref, wv_ref, k_ref, v_ref)


def _cache_kv(ckv, kr, wk, wp, wv):
    n = DEC_BATCH * PAST_LEN
    full = lambda shape: pl.BlockSpec(shape, lambda i: (0,) * len(shape))
    hw = MLA_HEADS * HEAD_BLK
    return pl.pallas_call(
        _cache_kv_kernel,
        grid=(n // PAST_LEN,),
        in_specs=[
            pl.BlockSpec((PAST_LEN, MLA_KV_RANK), lambda i: (i, 0)),
            pl.BlockSpec((PAST_LEN, 128), lambda i: (i, 0)),
            full((MLA_KV_RANK, hw)), full((128, hw)), full((MLA_KV_RANK, MLA_HEADS * MLA_V)),
        ],
        out_specs=[
            pl.BlockSpec((PAST_LEN, hw), lambda i: (i, 0)),
            pl.BlockSpec((PAST_LEN, MLA_HEADS * MLA_V), lambda i: (i, 0)),
        ],
        out_shape=[
            jax.ShapeDtypeStruct((n, hw), BF16),
            jax.ShapeDtypeStruct((n, MLA_HEADS * MLA_V), BF16),
        ],
        compiler_params=_cparams(("parallel",)),
        name="cache_kv",
    )(ckv, kr, wk, wp, wv)


TQ = 256


def _attn_kernel(*refs, has_cache):
    if has_cache:
        q_ref, k_ref, v_ref, kc_ref, vc_ref, o_ref = refs
    else:
        q_ref, k_ref, v_ref, o_ref = refs
    nh = q_ref.shape[1] // HEAD_BLK
    cols = [slice(h * HEAD_BLK, (h + 1) * HEAD_BLK) for h in range(nh)]
    vcols = [slice((h // 2) * 2 * MLA_V, (h // 2 + 1) * 2 * MLA_V) for h in range(nh)]
    s = [_dot_nt(q_ref[:, c], k_ref[:, c]) for c in cols]
    m = [jnp.max(x, axis=-1, keepdims=True) for x in s]
    if has_cache:
        sc = [_dot_nt(q_ref[:, c], kc_ref[:, c]) for c in cols]
        m = [jnp.maximum(a, jnp.max(x, axis=-1, keepdims=True)) for a, x in zip(m, sc)]
        pc = [jnp.exp(x - a) for x, a in zip(sc, m)]
    p = [jnp.exp(x - a) for x, a in zip(s, m)]
    den = [jnp.sum(x, axis=-1, keepdims=True) for x in p]
    o = [_dot(x.astype(BF16), v_ref[:, c]) for x, c in zip(p, vcols)]
    if has_cache:
        den = [a + jnp.sum(x, axis=-1, keepdims=True) for a, x in zip(den, pc)]
        o = [a + _dot(x.astype(BF16), vc_ref[:, c]) for a, x, c in zip(o, pc, vcols)]
    outs = [a / b for a, b in zip(o, den)]
    lane = lax.broadcasted_iota(jnp.int32, (1, 2 * MLA_V), 1)
    for j in range(nh // 2):
        o_ref[:, j * 2 * MLA_V:(j + 1) * 2 * MLA_V] = jnp.where(lane < MLA_V, outs[2 * j], outs[2 * j + 1])


def _attention(q, k, v, n, nb, row0, tq, npair, kc=None, vc=None):
    has_cache = kc is not None
    nq = n // tq
    rb0 = row0 // tq
    kb0 = row0 // n
    in_specs = [
        pl.BlockSpec((tq, npair * 2 * HEAD_BLK), lambda b, h, i: (rb0 + b * nq + i, h)),
        pl.BlockSpec((n, npair * 2 * HEAD_BLK), lambda b, h, i: (kb0 + b, h)),
        pl.BlockSpec((n, npair * 2 * MLA_V), lambda b, h, i: (kb0 + b, h)),
    ]
    args = [q, k, v]
    if has_cache:
        in_specs += [
            pl.BlockSpec((PAST_LEN, npair * 2 * HEAD_BLK), lambda b, h, i: (b, h)),
            pl.BlockSpec((PAST_LEN, npair * 2 * MLA_V), lambda b, h, i: (b, h)),
        ]
        args += [kc, vc]
    return pl.pallas_call(
        functools.partial(_attn_kernel, has_cache=has_cache),
        grid=(nb, MLA_HEADS // (2 * npair), nq),
        in_specs=in_specs,
        out_specs=pl.BlockSpec((tq, npair * 2 * MLA_V), lambda b, h, i: (b * nq + i, h)),
        out_shape=jax.ShapeDtypeStruct((nb * n, MLA_HEADS * MLA_V), F32),
        compiler_params=_cparams(("parallel", "parallel", "parallel")),
        name="attn_cache" if has_cache else "attn_ctx",
    )(*args)


def _outproj_kernel(xa_ref, xb_ref, oga_ref, ogb_ref, oca_ref, ocb_ref, oma_ref, omb_ref, mod_ref, nw_ref,
                    wo_ref, wr_ref, br_ref, xo_ref, h_ref, route_ref, cnt_ref, carry_s, wrh_s, wrl_s):
    @pl.when(pl.program_id(0) == 0)
    def _():
        carry_s[...] = jnp.zeros(carry_s.shape, F32)
        wrh, wrl = _split2(wr_ref[...])
        wrh_s[...] = wrh
        wrl_s[...] = wrl

    mod = mod_ref[...]

    def residual(x_ref, og_ref, oc_ref, om_ref):
        attn = (_dot(og_ref[...].astype(BF16), wo_ref[0:GLA_W, :])
                + _dot(oc_ref[...].astype(BF16), wo_ref[GLA_W:GLA_W + CONV_W, :])
                + _dot(om_ref[...].astype(BF16), wo_ref[GLA_W + CONV_W:D_MODEL, :]))
        xo_ref[...] = x_ref[...] + mod[:, 2 * D_MODEL:3 * D_MODEL] * attn

    is_ctx = pl.program_id(0) < N_CTX // TM
    pl.when(is_ctx)(lambda: residual(xa_ref, oga_ref, oca_ref, oma_ref))
    pl.when(jnp.logical_not(is_ctx))(lambda: residual(xb_ref, ogb_ref, ocb_ref, omb_ref))
    xn = xo_ref[...]
    h = _rms(xn, nw_ref[...]) * (1.0 + mod[:, 4 * D_MODEL:5 * D_MODEL]) + mod[:, 3 * D_MODEL:4 * D_MODEL]
    _rows_to_tiles(h_ref, h)
    hh, hl = _split2(h)
    logits = _dot(hh, wrh_s[...]) + _dot(hl, wrh_s[...]) + _dot(hh, wrl_s[...]) + br_ref[...]
    lane = lax.broadcasted_iota(jnp.int32, logits.shape, 1).astype(F32)
    big = 1e9
    gl = jnp.where((lane >= N_EXPERTS) & (lane < N_EXPERTS + N_GROUPS), logits, NEG)
    gmax = jnp.max(gl, axis=-1, keepdims=True)
    gw = 1.0 / jnp.sum(jnp.exp(gl - gmax), axis=-1, keepdims=True)
    gi = jnp.min(jnp.where(gl == gmax, lane, big), axis=-1, keepdims=True) - N_EXPERTS
    lo = gi * EXPERTS_PER_GROUP
    el = jnp.where((lane >= lo) & (lane < lo + EXPERTS_PER_GROUP), logits, NEG)
    m1 = jnp.max(el, axis=-1, keepdims=True)
    i1 = jnp.min(jnp.where(el == m1, lane, big), axis=-1, keepdims=True)
    el2 = jnp.where(lane == i1, NEG, el)
    m2 = jnp.max(el2, axis=-1, keepdims=True)
    i2 = jnp.min(jnp.where(el2 == m2, lane, big), axis=-1, keepdims=True)
    e2 = jnp.exp(m2 - m1)
    p1 = 1.0 / (1.0 + e2)
    onehot = jnp.where((lane == i1) | (lane == i2), 1.0, 0.0)
    tr = lax.broadcasted_iota(jnp.int32, (TM, TM), 0)
    tc = lax.broadcasted_iota(jnp.int32, (TM, TM), 1)
    before = (tc < tr).astype(F32).astype(BF16)
    seen = _dot(before, onehot.astype(BF16)) + carry_s[0:1, :]
    rank1 = jnp.sum(jnp.where(lane == i1, seen, 0.0), axis=-1, keepdims=True)
    rank2 = jnp.sum(jnp.where(lane == i2, seen, 0.0), axis=-1, keepdims=True)
    carry_s[...] = carry_s[...] + jnp.sum(onehot, axis=0, keepdims=True)
    cnt_ref[...] = carry_s[...]
    cols = (i1, i2, gw * p1, gw * (e2 * p1), rank1, rank2)
    route = jnp.zeros(logits.shape, F32)
    for j, col in enumerate(cols):
        route = jnp.where(lane == j, col, route)
    route_ref[...] = route


def _outproj(x, og, oc, om, mods4, l, norm_w, wo, wr, br):
    full = lambda shape: pl.BlockSpec(shape, lambda i: (0,) * len(shape))
    row = lambda w: pl.BlockSpec((TM, w), lambda i: (i, 0))
    return pl.pallas_call(
        _outproj_kernel,
        grid=(N_TOK // TM,),
        in_specs=(_pair_specs(TM, D_MODEL, isinstance(x, tuple)) + _pair_specs(TM, GLA_W, True)
                  + _pair_specs(TM, CONV_W, True) + _pair_specs(TM, MLA_HEADS * MLA_V, True) + [
            pl.BlockSpec((None, None, 1, 6 * D_MODEL), lambda i: (l, _mod_index(i, TM), 0, 0)),
            full((1, D_MODEL)), full((D_MODEL, D_MODEL)), full((D_MODEL, 128)), full((1, 128)),
        ]),
        out_specs=[row(D_MODEL), pl.BlockSpec(_tiled(TM), lambda i: (i, 0)), row(128), full((8, 128))],
        out_shape=[
            jax.ShapeDtypeStruct((N_TOK, D_MODEL), F32),
            jax.ShapeDtypeStruct(_tiled(N_TOK), F32),
            jax.ShapeDtypeStruct((N_TOK, 128), F32),
            jax.ShapeDtypeStruct((8, 128), F32),
        ],
        scratch_shapes=[pltpu.VMEM((8, 128), F32), pltpu.VMEM((D_MODEL, 128), BF16),
                        pltpu.VMEM((D_MODEL, 128), BF16)],
        compiler_params=_cparams(("arbitrary",)),
        name="outproj_router",
    )(*_pair(x), *og, *oc, *om, mods4, norm_w, wo, wr, br)


N_ASSIGN = 2 * N_TOK
TE = 256
N_ETILE = N_ASSIGN // TE
N_WORK = N_ETILE + N_EXPERTS


ROW_SUB = 8


def _tiled(n):
    return (n * ROW_SUB, D_MODEL // ROW_SUB)


def _rows_to_tiles(ref, x, r0=0):
    n = x.shape[0]
    for s in range(ROW_SUB):
        ref[pl.ds(r0 * ROW_SUB + s, n, stride=ROW_SUB), :] = x[:, s * 128:(s + 1) * 128]


def _tiles_to_rows(ref, r0=0, n=None):
    n = ref.shape[0] // ROW_SUB if n is None else n
    return jnp.concatenate([ref[pl.ds(r0 * ROW_SUB + s, n, stride=ROW_SUB), :] for s in range(ROW_SUB)], axis=1)


def _row_tile(ref, t):
    return ref.at[pl.ds(pl.multiple_of(t * ROW_SUB, ROW_SUB), ROW_SUB)]


def _lane_col(x, lane, j):
    return jnp.sum(jnp.where(lane == j, x, 0.0), axis=-1, keepdims=True)


def _plan_kernel(cnt_s, cnt_ref, route_ref, pos_ref, wt_ref, we_ref, wlo_ref, whi_ref, wf_ref, off_s):
    cnt = cnt_ref[...]
    chi = jnp.floor(cnt * (1.0 / 128.0))
    clo = cnt - chi * 128.0
    r = lax.broadcasted_iota(jnp.int32, (128, 128), 0)
    c = lax.broadcasted_iota(jnp.int32, (128, 128), 1)
    below = (r < c).astype(F32).astype(BF16)
    off = (128.0 * _dot(chi.astype(BF16), below) + _dot(clo.astype(BF16), below))[0:1, :]
    route = route_ref[...]
    lane = lax.broadcasted_iota(jnp.int32, route.shape, 1).astype(F32)
    pos1 = _lane_col(off, lane, _lane_col(route, lane, 0.0)) + _lane_col(route, lane, 4.0)
    pos2 = _lane_col(off, lane, _lane_col(route, lane, 1.0)) + _lane_col(route, lane, 5.0)
    pos_ref[...] = jnp.where(lane == 0.0, pos1, jnp.where(lane == 1.0, pos2, 0.0))

    @pl.when(pl.program_id(0) == 0)
    def _():
        def offs(e, acc):
            off_s[e] = acc
            return acc + cnt_s[e]

        off_s[N_EXPERTS] = lax.fori_loop(0, N_EXPERTS, offs, jnp.int32(0))

        def item(w, st):
            j, e, first = st
            active = j < N_ETILE
            jj = jnp.minimum(j, N_ETILE - 1)
            ee = jnp.minimum(e, N_EXPERTS - 1)
            tlo = jj * TE
            thi = tlo + TE
            end = off_s[ee + 1]
            wt_ref[w] = jj
            we_ref[w] = ee
            wlo_ref[w] = jnp.where(active, jnp.maximum(off_s[ee], tlo) - tlo, 0)
            whi_ref[w] = jnp.where(active, jnp.minimum(end, thi) - tlo, 0)
            wf_ref[w] = jnp.where(active, first, 0)
            adv_j = (active & (end >= thi)).astype(jnp.int32)
            adv_e = (active & (end <= thi)).astype(jnp.int32)
            return j + adv_j, e + adv_e, adv_j

        lax.fori_loop(0, N_WORK, item, (jnp.int32(0), jnp.int32(0), jnp.int32(1)))


def _plan(cnt_i, cnt, route):
    smem = pl.BlockSpec(memory_space=pltpu.SMEM)
    work = jax.ShapeDtypeStruct((N_WORK,), jnp.int32)
    return pl.pallas_call(
        _plan_kernel,
        grid=(1,),
        in_specs=[smem, pl.BlockSpec((8, 128), lambda i: (0, 0)), pl.BlockSpec((N_TOK, 128), lambda i: (0, 0))],
        out_specs=[pl.BlockSpec((N_TOK, 128), lambda i: (0, 0)), smem, smem, smem, smem, smem],
        out_shape=[jax.ShapeDtypeStruct((N_TOK, 128), F32), work, work, work, work, work],
        scratch_shapes=[pltpu.SMEM((N_EXPERTS + 1,), jnp.int32)],
        compiler_params=_cparams(("arbitrary",)),
        name="moe_plan",
    )(cnt_i, cnt, route)


def _dispatch_kernel(pos_s, h_ref, xs_hbm, sem):
    base = pl.program_id(0) * TM

    def copy(i, k):
        return pltpu.make_async_copy(_row_tile(h_ref, i), _row_tile(xs_hbm, pos_s[2 * (base + i) + k]), sem)

    def start(i, carry):
        copy(i, 0).start(priority=0)
        copy(i, 1).start(priority=1)
        return carry

    def wait(i, carry):
        copy(i, 0).wait()
        copy(i, 1).wait()
        return carry

    lax.fori_loop(0, TM, start, 0, unroll=8)
    lax.fori_loop(0, TM, wait, 0, unroll=8)


def _dispatch(pos, h):
    any_spec = pl.BlockSpec(memory_space=pl.ANY)
    return pl.pallas_call(
        _dispatch_kernel,
        grid_spec=pltpu.PrefetchScalarGridSpec(
            num_scalar_prefetch=1, grid=(N_TOK // TM,),
            in_specs=[pl.BlockSpec(_tiled(TM), lambda i, p: (i, 0))], out_specs=any_spec,
            scratch_shapes=[pltpu.SemaphoreType.DMA(())]),
        out_shape=jax.ShapeDtypeStruct(_tiled(N_ASSIGN), F32),
        compiler_params=_cparams(("arbitrary",)),
        name="moe_dispatch",
    )(pos, h)


def _experts_kernel(wt_s, we_s, wlo_s, whi_s, wf_s, xs_ref, w1_ref, w3_ref, w2_ref, o_ref, w1_s, w3_s, w2_s):
    w = pl.program_id(0)
    first = wf_s[w] == 1
    nonempty = whi_s[w] > wlo_s[w]

    @pl.when((w == 0) | (we_s[w] != we_s[jnp.maximum(w - 1, 0)]))
    def _():
        w1_s[...] = w1_ref[...].astype(BF16)
        w3_s[...] = w3_ref[...].astype(BF16)
        w2_s[...] = w2_ref[...].astype(BF16)

    @pl.when(first & jnp.logical_not(nonempty))
    def _():
        o_ref[...] = jnp.zeros(o_ref.shape, F32)

    nsplit = 2
    hrows = TE // nsplit

    def expert_rows(merge):
        x = [_tiles_to_rows(xs_ref, j * hrows, hrows).astype(BF16) for j in range(nsplit)]
        a = [_dot(v, w1_s[...]) for v in x]
        b = [_dot(v, w3_s[...]) for v in x]
        hid = [(u * _sigmoid(u) * v).astype(BF16) for u, v in zip(a, b)]
        y = [_dot(v, w2_s[...]) for v in hid]
        for j in range(nsplit):
            row = lax.broadcasted_iota(jnp.int32, (hrows, 1), 0) + j * hrows
            mine = (row >= wlo_s[w]) & (row < whi_s[w])
            other = _tiles_to_rows(o_ref, j * hrows, hrows) if merge else 0.0
            _rows_to_tiles(o_ref, jnp.where(mine, y[j], other), j * hrows)

    @pl.when(nonempty & first)
    def _():
        expert_rows(False)

    @pl.when(nonempty & jnp.logical_not(first))
    def _():
        expert_rows(True)


def _experts(work, xs, l, w1, w3, w2):
    tile = pl.BlockSpec(_tiled(TE), lambda w, wt, we, wlo, whi, wf: (wt[w], 0))
    wspec = lambda shape: pl.BlockSpec((None, None) + shape, lambda w, wt, we, wlo, whi, wf: (l, we[w], 0, 0))
    return pl.pallas_call(
        _experts_kernel,
        grid_spec=pltpu.PrefetchScalarGridSpec(
            num_scalar_prefetch=5, grid=(N_WORK,),
            in_specs=[tile, wspec((D_MODEL, EXPERT_FF)), wspec((D_MODEL, EXPERT_FF)), wspec((EXPERT_FF, D_MODEL))],
            out_specs=tile,
            scratch_shapes=[pltpu.VMEM((D_MODEL, EXPERT_FF), BF16), pltpu.VMEM((D_MODEL, EXPERT_FF), BF16),
                            pltpu.VMEM((EXPERT_FF, D_MODEL), BF16)]),
        out_shape=jax.ShapeDtypeStruct(_tiled(N_ASSIGN), F32),
        compiler_params=_cparams(("arbitrary",)),
        name="moe_experts",
    )(*work, xs, w1, w3, w2)


def _combine_kernel(pos_s, x_ref, route_ref, mod_ref, fw_ref, ys_hbm, *refs, final):
    if final:
        oa_ref, ob_ref, y_s, sem = refs
    else:
        o_ref, y_s, sem = refs
    step = pl.program_id(0)
    nsteps = pl.num_programs(0)

    def copy(t, i, k):
        slot = t % 2
        src = _row_tile(ys_hbm, pos_s[2 * (t * TM + i) + k])
        return pltpu.make_async_copy(src, _row_tile(y_s.at[slot, k], i), sem.at[slot])

    def gather(t):
        def start(i, carry):
            copy(t, i, 0).start(priority=0)
            copy(t, i, 1).start(priority=1)
            return carry

        lax.fori_loop(0, TM, start, 0, unroll=8)

    def wait(i, carry):
        copy(step, i, 0).wait()
        copy(step, i, 1).wait()
        return carry

    pl.when(step == 0)(lambda: gather(step))
    pl.when(step + 1 < nsteps)(lambda: gather(step + 1))
    lax.fori_loop(0, TM, wait, 0, unroll=8)
    route = route_ref[...]
    lane = lax.broadcasted_iota(jnp.int32, route.shape, 1)
    slot = step % 2
    moe = (_lane_col(route, lane, 2) * _tiles_to_rows(y_s.at[slot, 0])
           + _lane_col(route, lane, 3) * _tiles_to_rows(y_s.at[slot, 1]))
    x = x_ref[...] + mod_ref[:, 5 * D_MODEL:6 * D_MODEL] * moe
    if final:
        y = _rms(x, fw_ref[...])
        is_ctx = pl.program_id(0) < N_CTX // TM

        @pl.when(is_ctx)
        def _():
            oa_ref[...] = y

        @pl.when(jnp.logical_not(is_ctx))
        def _():
            ob_ref[...] = y
    else:
        o_ref[...] = x


def _combine(pos, x, route, mods4, l, fw, ys, final):
    row = lambda w: pl.BlockSpec((TM, w), lambda i, p: (i, 0))
    if final:
        out_specs = _pair_specs(TM, D_MODEL, True)
        out_shape = [jax.ShapeDtypeStruct((N_CTX, D_MODEL), F32), jax.ShapeDtypeStruct((N_SMP, D_MODEL), F32)]
    else:
        out_specs = row(D_MODEL)
        out_shape = jax.ShapeDtypeStruct((N_TOK, D_MODEL), F32)
    return pl.pallas_call(
        functools.partial(_combine_kernel, final=final),
        grid_spec=pltpu.PrefetchScalarGridSpec(
            num_scalar_prefetch=1, grid=(N_TOK // TM,),
            in_specs=[row(D_MODEL), row(128),
                      pl.BlockSpec((None, None, 1, 6 * D_MODEL), lambda i, p: (l, _mod_index(i, TM), 0, 0)),
                      pl.BlockSpec((1, D_MODEL), lambda i, p: (0, 0)),
                      pl.BlockSpec(memory_space=pl.ANY)],
            out_specs=out_specs,
            scratch_shapes=[pltpu.VMEM((2, 2) + _tiled(TM), F32), pltpu.SemaphoreType.DMA((2,))]),
        out_shape=out_shape,
        compiler_params=_cparams(("arbitrary",)),
        name="moe_combine_final" if final else "moe_combine",
    )(pos, x, route, mods4, fw, ys)


def _rope_swap(w):
    nf = MLA_ROPE // 4
    g = w.reshape(w.shape[:-1] + (2, 2, nf))
    return jnp.stack([-g[..., 1, :], g[..., 0, :]], axis=-2).reshape(w.shape)


def _rope_tables():
    nf = MLA_ROPE // 4
    pos = np.arange(DEC_SEQ)
    row = (pos // GRID_W).astype(np.float32)
    col = (pos % GRID_W).astype(np.float32)
    inv = (np.float32(ROPE_THETA) ** (-np.arange(nf, dtype=np.float32) / np.float32(nf))).astype(np.float32)
    ar = (row[:, None] * inv).astype(np.float32)
    ac = (col[:, None] * inv).astype(np.float32)
    cos = np.concatenate([np.cos(ar), np.cos(ar), np.cos(ac), np.cos(ac)], axis=1)
    sin = np.concatenate([np.sin(ar), np.sin(ar), np.sin(ac), np.sin(ac)], axis=1)
    cos = np.concatenate([np.ones((TM, MLA_ROPE)), cos], axis=0).astype(np.float32)
    sin = np.concatenate([np.zeros((TM, MLA_ROPE)), sin], axis=0).astype(np.float32)
    n = cos.shape[0]
    z32 = np.zeros((n, 32), np.float32)
    tqc = np.concatenate([np.ones((n, MLA_NOPE), np.float32), cos, z32], axis=1)
    tqs = np.concatenate([np.zeros((n, MLA_NOPE), np.float32), sin, z32], axis=1)
    tk = np.concatenate([z32, cos, sin, z32], axis=1)
    return jnp.asarray(tqc), jnp.asarray(tqs), jnp.asarray(tk)


def _prep_layer(l, w_in, gla_wg_f, gla_wg_b, mla_wuq, mla_wukv, w_out, moe_wg, moe_bg, moe_we, moe_be):
    offs = np.cumsum([0, 256, 256, 256, 256, 16, 16, 512, 256, 128, 32])
    seg = lambda i: w_in[l][:, offs[i]:offs[i + 1]]
    kr = seg(9)
    w_in_r = jnp.concatenate(
        [seg(0), seg(1), seg(2), seg(3), seg(6), seg(7), seg(8), seg(4), seg(5), kr, _rope_swap(kr),
         jnp.zeros((D_MODEL, 32), F32)], axis=1).astype(BF16)
    wgf = jnp.zeros((128, GLA_W), F32).at[0:16].set(gla_wg_f[l])
    wgb = jnp.zeros((128, GLA_W), F32).at[16:32].set(gla_wg_b[l])
    wq = mla_wuq[l].reshape(MLA_Q_RANK, MLA_HEADS, MLA_NOPE + MLA_ROPE)
    nope, rope = wq[..., :MLA_NOPE], wq[..., MLA_NOPE:]
    z = lambda n: jnp.zeros((MLA_Q_RANK, MLA_HEADS, n), F32)
    hw = MLA_HEADS * HEAD_BLK
    wa = jnp.concatenate([nope, rope, z(32)], axis=-1).reshape(MLA_Q_RANK, hw).astype(BF16)
    wb = jnp.concatenate([z(MLA_NOPE), _rope_swap(rope), z(32)], axis=-1).reshape(MLA_Q_RANK, hw).astype(BF16)
    wkv = mla_wukv[l].reshape(MLA_KV_RANK, MLA_HEADS, MLA_NOPE + MLA_V)
    wk = jnp.concatenate([wkv[..., :MLA_NOPE], jnp.zeros((MLA_KV_RANK, MLA_HEADS, 64), F32)],
                         axis=-1).reshape(MLA_KV_RANK, hw).astype(BF16)
    wv = wkv[..., MLA_NOPE:].reshape(MLA_KV_RANK, MLA_HEADS * MLA_V).astype(BF16)
    wp_np = np.zeros((128, MLA_HEADS, HEAD_BLK), np.float32)
    for j in range(MLA_ROPE):
        wp_np[32 + j, :, MLA_NOPE + j] = 1.0
        wp_np[64 + j, :, MLA_NOPE + j] = 1.0
    wp = jnp.asarray(wp_np.reshape(128, hw)).astype(BF16)
    wo = w_out[l].astype(BF16)
    wr = jnp.concatenate([moe_we[l], moe_wg[l], jnp.zeros((D_MODEL, 128 - N_EXPERTS - N_GROUPS), F32)], axis=1)
    br = jnp.concatenate([moe_be[l], moe_bg[l], jnp.zeros((128 - N_EXPERTS - N_GROUPS,), F32)]).reshape(1, 128)
    return dict(w_in_r=w_in_r, wgf=wgf, wgb=wgb, wa=wa, wb=wb, wk=wk, wv=wv, wp=wp, wo=wo, wr=wr, br=br)


def kernel(x_prompt, x_sample, cache_ckv, cache_krope, state_gla_fwd, state_gla_bwd, c, c_ctx, ada_w, ada_b,
           norm1_w, norm2_w, w_in, gla_wg_f, gla_bg_f, gla_wg_b, gla_bg_b, gla_norm_w, conv_w, conv_b,
           conv_ln_w, conv_ln_b, mla_qnorm_w, mla_wuq, mla_kvnorm_w, mla_wukv, w_out, moe_wg, moe_bg, moe_we,
           moe_be, moe_w1, moe_w3, moe_w2, final_norm_w):
    x = (x_prompt.reshape(N_CTX, D_MODEL), x_sample.reshape(N_SMP, D_MODEL))
    cond8 = jnp.concatenate([c_ctx[None, :], c, jnp.zeros((N_MOD - 1 - DEC_BATCH, D_MODEL), F32)], axis=0)
    mods4 = _ada_table(cond8, ada_w, ada_b).reshape(DEPTH, N_MOD, 1, 6 * D_MODEL)
    tabs = _rope_tables()
    row1 = lambda a: a.reshape(1, -1)

    ckv_new, kr_new, sf_new, sb_new = [], [], [], []
    for l in range(DEPTH):
        w = _prep_layer(l, w_in, gla_wg_f, gla_wg_b, mla_wuq, mla_wukv, w_out, moe_wg, moe_bg, moe_we, moe_be)
        p = _inproj(x, mods4, l, row1(norm1_w[l]), w["w_in_r"])

        gla_args = (p, w["wgf"], w["wgb"], row1(gla_bg_f[l]), row1(gla_bg_b[l]), row1(gla_norm_w[l]))
        og_c, sf, sb = _gla(*gla_args, SEQ, BATCH, 0)
        og_s = _gla(*gla_args, DEC_SEQ, DEC_BATCH, N_CTX // DEC_SEQ, (state_gla_fwd, l), (state_gla_bwd, l))[0]
        og = (og_c, og_s)

        cw = jnp.concatenate([conv_w[l], jnp.zeros((1, CONV_W), F32)], axis=0)
        conv_args = (p, cw, row1(conv_b[l]), row1(conv_ln_w[l]), row1(conv_ln_b[l]))
        oc = (_conv(*conv_args, SEQ, BATCH, 0), _conv(*conv_args, DEC_SEQ, DEC_BATCH, N_CTX // DEC_SEQ))

        q, k, v, ckvn, kr = _mla_prep(p, tabs, row1(mla_qnorm_w[l]), row1(mla_kvnorm_w[l]),
                                      w["wa"], w["wb"], w["wk"], w["wp"], w["wv"])
        cc = cache_ckv[:, l].reshape(DEC_BATCH * PAST_LEN, MLA_KV_RANK)
        ckr = cache_krope[:, l].reshape(DEC_BATCH * PAST_LEN, MLA_ROPE)
        ckr = jnp.pad(ckr, ((0, 0), (32, 64)))
        kc, vc = _cache_kv(cc, ckr, w["wk"], w["wp"], w["wv"])
        om = (_attention(q, k, v, SEQ, BATCH, 0, SEQ, MLA_HEADS // 2),
              _attention(q, k, v, DEC_SEQ, DEC_BATCH, N_CTX, 512, 1, kc, vc))

        xn, h2, route, cnt = _outproj(x, og, oc, om, mods4, l, row1(norm2_w[l]), w["wo"], w["wr"], w["br"])
        posf, *work = _plan(cnt[0, :N_EXPERTS].astype(jnp.int32), cnt, route)
        pos = posf[:, 0:2].astype(jnp.int32).reshape(N_ASSIGN)
        ys = _experts(work, _dispatch(pos, h2), l, moe_w1, moe_w3, moe_w2)
        x = _combine(pos, xn, route, mods4, l, row1(final_norm_w), ys, l == DEPTH - 1)

        ckv_new.append(ckvn[:N_CTX].reshape(BATCH, SEQ, MLA_KV_RANK))
        kr_new.append(kr[:N_CTX, 32:32 + MLA_ROPE].reshape(BATCH, SEQ, MLA_ROPE))
        sf_new.append(sf)
        sb_new.append(sb)

    y_ctx, y_smp = x
    return (y_ctx.reshape(BATCH, SEQ, D_MODEL), y_smp.reshape(DEC_BATCH, DEC_SEQ, D_MODEL),
            jnp.stack(ckv_new, axis=1), jnp.stack(kr_new, axis=1),
            jnp.stack(sf_new, axis=1), jnp.stack(sb_new, axis=1))
```

```python
import functools

import numpy as np
import jax
import jax.numpy as jnp
from jax import lax
from jax.experimental import pallas as pl
from jax.experimental.pallas import tpu as pltpu

F32 = jnp.float32
BF16 = jnp.bfloat16

D_MODEL = 1024
BATCH = 16
SEQ = 256
DEPTH = 2
DEC_BATCH = 4
DEC_SEQ = 1024
PAST_LEN = 512
GRID_W = 64
NORM_EPS = 1e-6
GLA_HEADS = 4
GLA_DK = 64
GLA_DV = 64
GLA_W = 256
GLA_GATE_RANK = 16
GLA_TAU = 16.0
CONV_W = 256
CONV_K = 31
MLA_HEADS = 8
MLA_NOPE = 64
MLA_ROPE = 32
MLA_V = 64
MLA_Q_RANK = 256
MLA_KV_RANK = 128
ROPE_THETA = 10000.0
N_GROUPS = 4
EXPERTS_PER_GROUP = 8
N_EXPERTS = 32
EXPERT_FF = 256

N_CTX = BATCH * SEQ
N_SMP = DEC_BATCH * DEC_SEQ
N_TOK = N_CTX + N_SMP
N_MOD = 8
TM = 512
P_COLS = 2048
MISC_BLK = 15
CHUNK = 64
SUB = 16
NEG = -1e30
VMEM_LIMIT = 56 * 1024 * 1024


def _cparams(sem):
    return pltpu.CompilerParams(dimension_semantics=sem, vmem_limit_bytes=VMEM_LIMIT)


def _dot(a, b):
    return jnp.dot(a, b, preferred_element_type=F32)


def _dot_nt(a, b):
    return lax.dot_general(a, b, (((1,), (1,)), ((), ())), preferred_element_type=F32)


def _split2(x):
    hi = x.astype(BF16)
    lo = (x - hi.astype(F32)).astype(BF16)
    return hi, lo


def _dot3(a, w):
    ah, al = _split2(a)
    wh, wl = _split2(w)
    return _dot(ah, wh) + _dot(al, wh) + _dot(ah, wl)


def _sigmoid(x):
    return 1.0 / (1.0 + jnp.exp(-x))


def _rms(x, w):
    ms = jnp.mean(x * x, axis=-1, keepdims=True)
    return x * lax.rsqrt(ms + NORM_EPS) * w


def _mod_index(i, tile):
    nctx = N_CTX // tile
    per = DEC_SEQ // tile
    return jnp.where(i < nctx, 0, 1 + (i - nctx) // per)


def _ada_kernel(c_ref, w_ref, b_ref, o_ref):
    c = c_ref[...]
    o_ref[...] = _dot3(c * _sigmoid(c), w_ref[...]) + b_ref[...]


def _ada_table(cond8, ada_w, ada_b):
    tn = 1024
    n6 = 6 * D_MODEL
    return pl.pallas_call(
        _ada_kernel,
        grid=(DEPTH, n6 // tn),
        in_specs=[
            pl.BlockSpec((N_MOD, D_MODEL), lambda l, j: (0, 0)),
            pl.BlockSpec((None, D_MODEL, tn), lambda l, j: (l, 0, j)),
            pl.BlockSpec((None, 1, tn), lambda l, j: (l, 0, j)),
        ],
        out_specs=pl.BlockSpec((None, N_MOD, tn), lambda l, j: (l, 0, j)),
        out_shape=jax.ShapeDtypeStruct((DEPTH, N_MOD, n6), F32),
        compiler_params=_cparams(("parallel", "parallel")),
        name="ada_table",
    )(cond8, ada_w, ada_b.reshape(DEPTH, 1, n6))


def _pair_specs(tile, width, split):
    nctx = N_CTX // tile
    offb = 0 if split else nctx
    return [pl.BlockSpec((tile, width), lambda i, *_: (jnp.minimum(i, nctx - 1), 0)),
            pl.BlockSpec((tile, width), lambda i, *_: (jnp.maximum(i - nctx, 0) + offb, 0))]


def _pair(x):
    return x if isinstance(x, tuple) else (x, x)


TM_IN = 512


def _inproj_kernel(xa_ref, xb_ref, mod_ref, nw_ref, w_ref, o_ref):
    mod = mod_ref[...]

    def project(x_ref):
        h = _rms(x_ref[...], nw_ref[...]) * (1.0 + mod[:, D_MODEL:2 * D_MODEL]) + mod[:, 0:D_MODEL]
        o_ref[...] = _dot(h.astype(BF16), w_ref[...])

    is_ctx = pl.program_id(0) < N_CTX // TM_IN
    pl.when(is_ctx)(lambda: project(xa_ref))
    pl.when(jnp.logical_not(is_ctx))(lambda: project(xb_ref))


def _inproj(x, mods4, l, norm_w, w_in_r):
    return pl.pallas_call(
        _inproj_kernel,
        grid=(N_TOK // TM_IN,),
        in_specs=_pair_specs(TM_IN, D_MODEL, isinstance(x, tuple)) + [
            pl.BlockSpec((None, None, 1, 6 * D_MODEL), lambda i: (l, _mod_index(i, TM_IN), 0, 0)),
            pl.BlockSpec((1, D_MODEL), lambda i: (0, 0)),
            pl.BlockSpec((D_MODEL, P_COLS), lambda i: (0, 0)),
        ],
        out_specs=pl.BlockSpec((TM_IN, P_COLS), lambda i: (i, 0)),
        out_shape=jax.ShapeDtypeStruct((N_TOK, P_COLS), F32),
        compiler_params=_cparams(("parallel",)),
        name="inproj",
    )(*_pair(x), mods4, norm_w, w_in_r)


def _log_sigmoid(x):
    return jnp.minimum(x, 0.0) - jnp.log(1.0 + jnp.exp(-jnp.abs(x)))


def _split3(x):
    hi = x.astype(BF16)
    r = x - hi.astype(F32)
    mid = r.astype(BF16)
    lo = (r - mid.astype(F32)).astype(BF16)
    return hi, mid, lo


def _gla_consts(reverse):
    t = lax.broadcasted_iota(jnp.int32, (CHUNK, CHUNK), 0)
    s = lax.broadcasted_iota(jnp.int32, (CHUNK, CHUNK), 1)
    mid = jnp.bitwise_and(t, -SUB) + SUB // 2
    if reverse:
        cum = s >= t
        ref = s >= mid
    else:
        cum = s <= t
        ref = s <= mid
    cum_ref = jnp.concatenate([cum, ref], axis=0).astype(F32).astype(BF16)
    rt = jnp.bitwise_and(lax.broadcasted_iota(jnp.int32, (GLA_HEADS * SUB, CHUNK), 0), SUB - 1)
    cs = lax.broadcasted_iota(jnp.int32, (GLA_HEADS * SUB, CHUNK), 1)
    rowi = lax.broadcasted_iota(jnp.int32, (CHUNK, 1), 0)
    causal, valid = [], []
    for i in range(CHUNK // SUB):
        tq = rt + i * SUB
        causal.append((cs >= tq) if reverse else (cs <= tq))
        valid.append((rowi >= i * SUB) if reverse else (rowi < (i + 1) * SUB))
    return cum_ref, causal, valid


def _gla_local(jobs, nc, o_s, qe_s, u_s, dec_s, head_masks, block_mask):
    nsub = CHUNK // SUB
    rows = [pl.ds(pl.multiple_of(c * CHUNK, CHUNK), CHUNK) for c, *_ in jobs]
    bs = []
    for (c, d, qs, k, vb, vt, la_ref, consts), r in zip(jobs, rows):
        h0, h1, h2 = _split3(la_ref[r, :])
        br = _dot(consts[0], h0) + _dot(consts[0], h1) + _dot(consts[0], h2)
        bs.append((br[0:CHUNK], br[CHUNK:2 * CHUNK]))
    ops = []
    for (c, d, qs, k, vb, vt, la_ref, consts), (b, rr) in zip(jobs, bs):
        qhat = qs * jnp.exp(b - rr)
        for i in range(nsub):
            r_i = rr[i * SUB:i * SUB + 1]
            kt = (k * jnp.exp(jnp.where(consts[2][i], r_i - b, NEG))).astype(BF16)
            qi = qhat[i * SUB:(i + 1) * SUB]
            ops.append((jnp.concatenate([jnp.where(m, qi, 0.0) for m in head_masks], axis=0).astype(BF16), kt))
    scores = [_dot_nt(qbig, kt) for qbig, kt in ops]
    outs = []
    for j, (c, d, qs, k, vb, vt, la_ref, consts) in enumerate(jobs):
        for i in range(nsub):
            a = jnp.where(consts[1][i], scores[j * nsub + i], 0.0)
            outs.append(_dot(a.astype(BF16), vb))
    for j, ((c, d, qs, k, vb, vt, la_ref, consts), (b, rr), r) in enumerate(zip(jobs, bs, rows)):
        parts = []
        for i in range(nsub):
            ov = outs[j * nsub + i]
            oi = jnp.where(head_masks[0], ov[0:SUB], 0.0)
            for h in range(1, GLA_HEADS):
                oi = oi + jnp.where(head_masks[h], ov[h * SUB:(h + 1) * SUB], 0.0)
            parts.append(oi)
        o_s[d, r, :] = jnp.concatenate(parts, axis=0)
        qe_s[d, r, :] = (qs * jnp.exp(b)).astype(BF16)
        b_last = b[0:1] if d == 1 else b[CHUNK - 1:CHUNK]
        kl = (k * jnp.exp(b_last - b)).astype(BF16)
        idx = d * nc + c
        u_s[idx] = jnp.where(block_mask, _dot(vt, kl), 0.0)
        dec_s[pl.ds(pl.multiple_of(idx * 8, 8), 8), :] = jnp.broadcast_to(jnp.exp(b_last), (8, GLA_W))


def _gla_kernel(*refs, n, has_state):
    if has_state:
        (qkvg_ref, misc_ref, wgf_ref, wgb_ref, bgf_ref, bgb_ref, nw_ref, s0f_ref, s0b_ref,
         o_ref, laf_s, lab_s, o_s, qe_s, u_s, dec_s, sts_s, stf_s, stb_s) = refs
    else:
        (qkvg_ref, misc_ref, wgf_ref, wgb_ref, bgf_ref, bgb_ref, nw_ref,
         o_ref, sf_ref, sb_ref, laf_s, lab_s, o_s, qe_s, u_s, dec_s, sts_s, stf_s, stb_s) = refs
    misc = misc_ref[...]
    laf_s[...] = _log_sigmoid(_dot3(misc, wgf_ref[...]) + bgf_ref[...]) * (1.0 / GLA_TAU)
    lab_s[...] = _log_sigmoid(_dot3(misc, wgb_ref[...]) + bgb_ref[...]) * (1.0 / GLA_TAU)
    stf_s[...] = jnp.zeros((GLA_W, GLA_W), F32)
    stb_s[...] = jnp.zeros((GLA_W, GLA_W), F32)
    if has_state:
        for st_s, s0_ref in ((stf_s, s0f_ref), (stb_s, s0b_ref)):
            for h in range(GLA_HEADS):
                st_s[h * GLA_DK:(h + 1) * GLA_DK, h * GLA_DV:(h + 1) * GLA_DV] = s0_ref[h]
            st_s[...] = st_s[...].T

    lane = lax.broadcasted_iota(jnp.int32, (1, GLA_W), 1)
    head_masks = [jnp.right_shift(lane, 6) == h for h in range(GLA_HEADS)]
    bi = jnp.right_shift(lax.broadcasted_iota(jnp.int32, (GLA_W, GLA_W), 0), 6)
    bj = jnp.right_shift(lax.broadcasted_iota(jnp.int32, (GLA_W, GLA_W), 1), 6)
    block_mask = bi == bj
    consts_f = _gla_consts(False)
    consts_b = _gla_consts(True)
    nc = n // CHUNK

    def local(cc, carry):
        jobs = []
        for c in (2 * cc, 2 * cc + 1):
            rows = pl.ds(pl.multiple_of(c * CHUNK, CHUNK), CHUNK)
            qs = qkvg_ref[rows, 0:GLA_W] * (GLA_DK ** -0.5)
            k = qkvg_ref[rows, GLA_W:2 * GLA_W]
            v = qkvg_ref[rows, 2 * GLA_W:3 * GLA_W]
            vb = v.astype(BF16)
            vt = v.T.astype(BF16)
            jobs += [(c, 0, qs, k, vb, vt, laf_s, consts_f), (c, 1, qs, k, vb, vt, lab_s, consts_b)]
        _gla_local(jobs, nc, o_s, qe_s, u_s, dec_s, head_masks, block_mask)
        return carry

    lax.fori_loop(0, nc // 2, local, 0)

    def scan(j, carry):
        for d, st_s in enumerate((stf_s, stb_s)):
            idx = d * nc + (j if d == 0 else nc - 1 - j)
            st = st_s[...]
            sts_s[idx] = st.astype(BF16)
            st_s[...] = st * dec_s[pl.ds(pl.multiple_of(idx * 8, 8), 1), :] + u_s[idx]
        return carry

    lax.fori_loop(0, nc, scan, 0)

    def inter(cc, carry):
        jobs = [(c, d, pl.ds(pl.multiple_of(c * CHUNK, CHUNK), CHUNK))
                for c in (2 * cc, 2 * cc + 1) for d in range(2)]
        res = [_dot_nt(qe_s[d, r, :], sts_s[d * nc + c]) for c, d, r in jobs]
        for (c, d, r), o in zip(jobs, res):
            o_s[d, r, :] += o
        return carry

    lax.fori_loop(0, nc // 2, inter, 0)

    if not has_state:
        for st_s, s_ref in ((stf_s, sf_ref), (stb_s, sb_ref)):
            s = st_s[...].T
            for h in range(GLA_HEADS):
                s_ref[h] = s[h * GLA_DK:(h + 1) * GLA_DK, h * GLA_DV:(h + 1) * GLA_DV]

    ones_blk = block_mask.astype(F32).astype(BF16)
    rt = 256
    for r0 in range(0, n, rt):
        o = o_s[0, r0:r0 + rt, :] + o_s[1, r0:r0 + rt, :]
        hi, lo = _split2(o * o)
        ms = (_dot(hi, ones_blk) + _dot(lo, ones_blk)) * (1.0 / GLA_DV)
        g = qkvg_ref[r0:r0 + rt, 3 * GLA_W:4 * GLA_W]
        o_ref[r0:r0 + rt, :] = o * lax.rsqrt(ms + NORM_EPS) * nw_ref[...] * (g * _sigmoid(g))


def _gla(p, wgf, wgb, bgf, bgb, nw, n, nb, row_blk0, s0f=None, s0b=None):
    has_state = s0f is not None
    full = lambda shape: pl.BlockSpec(shape, lambda b: (0,) * len(shape))
    in_specs = [
        pl.BlockSpec((n, 4 * GLA_W), lambda b: (row_blk0 + b, 0)),
        pl.BlockSpec((n, 128), lambda b: (row_blk0 + b, MISC_BLK)),
        full((128, GLA_W)), full((128, GLA_W)), full((1, GLA_W)), full((1, GLA_W)), full((1, GLA_W)),
    ]
    args = [p, p, wgf, wgb, bgf, bgb, nw]
    out_specs = [pl.BlockSpec((n, GLA_W), lambda b: (b, 0))]
    out_shape = [jax.ShapeDtypeStruct((nb * n, GLA_W), F32)]
    if has_state:
        layer = s0f[1]
        st_spec = pl.BlockSpec((None, None, GLA_HEADS, GLA_DK, GLA_DV), lambda b: (b, layer, 0, 0, 0))
        in_specs += [st_spec, st_spec]
        args += [s0f[0], s0b[0]]
    else:
        st_spec = pl.BlockSpec((None, GLA_HEADS, GLA_DK, GLA_DV), lambda b: (b, 0, 0, 0))
        out_specs += [st_spec, st_spec]
        out_shape += [jax.ShapeDtypeStruct((nb, GLA_HEADS, GLA_DK, GLA_DV), F32)] * 2
    return pl.pallas_call(
        functools.partial(_gla_kernel, n=n, has_state=has_state),
        grid=(nb,),
        in_specs=in_specs,
        out_specs=out_specs,
        out_shape=out_shape,
        scratch_shapes=[
            pltpu.VMEM((n, GLA_W), F32), pltpu.VMEM((n, GLA_W), F32),
            pltpu.VMEM((2, n, GLA_W), F32),
            pltpu.VMEM((2, n, GLA_W), BF16),
            pltpu.VMEM((2 * n // CHUNK, GLA_W, GLA_W), F32),
            pltpu.VMEM((2 * n // CHUNK * 8, GLA_W), F32),
            pltpu.VMEM((2 * n // CHUNK, GLA_W, GLA_W), BF16),
            pltpu.VMEM((GLA_W, GLA_W), F32), pltpu.VMEM((GLA_W, GLA_W), F32),
        ],
        compiler_params=_cparams(("parallel",)),
        name="gla_state" if has_state else "gla_ctx",
    )(*args)


CONV_PAD = 16
CONV_ROWS = 128
CONV_SHIFT_ROWS = 24


def _conv_kernel(u_ref, w_ref, b_ref, lnw_ref, lnb_ref, o_ref, zp_s, zsh_s, *, n):
    u = u_ref[...]
    zp_s[0:CONV_PAD, :] = jnp.zeros((CONV_PAD, CONV_W), F32)
    zp_s[CONV_PAD + n:2 * CONV_PAD + n, :] = jnp.zeros((CONV_PAD, CONV_W), F32)
    zp_s[CONV_PAD:CONV_PAD + n, :] = u[:, 0:CONV_W] * _sigmoid(u[:, CONV_W:2 * CONV_W])
    m = n + CONV_SHIFT_ROWS
    for s in range(1, 8):
        zsh_s[s - 1] = zp_s[s:s + m, :]
    off = CONV_PAD - CONV_K // 2

    def tap(r0, j):
        s = (off + j) % 8
        base = r0 + off + j - s
        return zp_s[base:base + CONV_ROWS, :] if s == 0 else zsh_s[s - 1, base:base + CONV_ROWS, :]

    for r0 in range(0, n, CONV_ROWS):
        acc = tap(r0, 0) * w_ref[0:1, :]
        for j in range(1, CONV_K):
            acc = acc + tap(r0, j) * w_ref[j:j + 1, :]
        acc = acc + b_ref[...]
        mu = jnp.mean(acc, axis=-1, keepdims=True)
        xc = acc - mu
        var = jnp.mean(xc * xc, axis=-1, keepdims=True)
        y = xc * lax.rsqrt(var + NORM_EPS) * lnw_ref[...] + lnb_ref[...]
        o_ref[r0:r0 + CONV_ROWS, :] = y * _sigmoid(y)


def _conv(p, cw, cb, lnw, lnb, n, nb, row_blk0):
    full = lambda shape: pl.BlockSpec(shape, lambda b: (0,) * len(shape))
    return pl.pallas_call(
        functools.partial(_conv_kernel, n=n),
        grid=(nb,),
        in_specs=[
            pl.BlockSpec((n, 2 * CONV_W), lambda b: (row_blk0 + b, 2)),
            full((32, CONV_W)), full((1, CONV_W)), full((1, CONV_W)), full((1, CONV_W)),
        ],
        out_specs=pl.BlockSpec((n, CONV_W), lambda b: (b, 0)),
        out_shape=jax.ShapeDtypeStruct((nb * n, CONV_W), F32),
        scratch_shapes=[pltpu.VMEM((n + 2 * CONV_PAD, CONV_W), F32),
                        pltpu.VMEM((7, n + CONV_SHIFT_ROWS, CONV_W), F32)],
        compiler_params=_cparams(("parallel",)),
        name="conv%d" % n,
    )(p, cw, cb, lnw, lnb)


HEAD_BLK = 128


def _kv_expand(ckvn, t, wk_ref, wp_ref, wv_ref, k_ref, v_ref):
    cb = ckvn.astype(BF16)
    th, tl = _split2(t)
    k_ref[...] = (_dot(cb, wk_ref[...]) + _dot(th, wp_ref[...]) + _dot(tl, wp_ref[...])).astype(BF16)
    v_ref[...] = _dot(cb, wv_ref[...]).astype(BF16)


def _mla_prep_kernel(cq_ref, cm_ref, tqc_ref, tqs_ref, tk_ref, qnw_ref, kvw_ref, wa_ref, wb_ref,
                     wk_ref, wp_ref, wv_ref, q_ref, k_ref, v_ref, ckv_ref, kr_ref):
    qn = _rms(cq_ref[...], qnw_ref[...]).astype(BF16)
    tqc = jnp.concatenate([tqc_ref[...]] * MLA_HEADS, axis=1)
    tqs = jnp.concatenate([tqs_ref[...]] * MLA_HEADS, axis=1)
    scale = (MLA_NOPE + MLA_ROPE) ** -0.5
    q_ref[...] = ((_dot(qn, wa_ref[...]) * tqc + _dot(qn, wb_ref[...]) * tqs) * scale).astype(BF16)
    cm = cm_ref[...]
    ckvn = _rms(cm[:, 0:MLA_KV_RANK], kvw_ref[...])
    ckv_ref[...] = ckvn
    t = cm[:, MLA_KV_RANK:2 * MLA_KV_RANK] * tk_ref[...]
    kr_ref[...] = t
    _kv_expand(ckvn, t, wk_ref, wp_ref, wv_ref, k_ref, v_ref)


def _rope_tile(i):
    nctx = N_CTX // TM
    return jnp.where(i < nctx, 0, 1 + (i - nctx) % (DEC_SEQ // TM))


def _mla_prep(p, tabs, qnw, kvw, wa, wb, wk, wp, wv):
    tqc, tqs, tk = tabs
    full = lambda shape: pl.BlockSpec(shape, lambda i: (0,) * len(shape))
    tab_spec = pl.BlockSpec((TM, HEAD_BLK), lambda i: (_rope_tile(i), 0))
    hw = MLA_HEADS * HEAD_BLK
    return pl.pallas_call(
        _mla_prep_kernel,
        grid=(N_TOK // TM,),
        in_specs=[
            pl.BlockSpec((TM, MLA_Q_RANK), lambda i: (i, 6)),
            pl.BlockSpec((TM, 2 * MLA_KV_RANK), lambda i: (i, 7)),
            tab_spec, tab_spec, tab_spec,
            full((1, MLA_Q_RANK)), full((1, MLA_KV_RANK)),
            full((MLA_Q_RANK, hw)), full((MLA_Q_RANK, hw)),
            full((MLA_KV_RANK, hw)), full((128, hw)), full((MLA_KV_RANK, MLA_HEADS * MLA_V)),
        ],
        out_specs=[
            pl.BlockSpec((TM, hw), lambda i: (i, 0)),
            pl.BlockSpec((TM, hw), lambda i: (i, 0)),
            pl.BlockSpec((TM, MLA_HEADS * MLA_V), lambda i: (i, 0)),
            pl.BlockSpec((TM, MLA_KV_RANK), lambda i: (i, 0)),
            pl.BlockSpec((TM, 128), lambda i: (i, 0)),
        ],
        out_shape=[
            jax.ShapeDtypeStruct((N_TOK, hw), BF16),
            jax.ShapeDtypeStruct((N_TOK, hw), BF16),
            jax.ShapeDtypeStruct((N_TOK, MLA_HEADS * MLA_V), BF16),
            jax.ShapeDtypeStruct((N_TOK, MLA_KV_RANK), F32),
            jax.ShapeDtypeStruct((N_TOK, 128), F32),
        ],
        compiler_params=_cparams(("parallel",)),
        name="mla_prep",
    )(p, p, tqc, tqs, tk, qnw, kvw, wa, wb, wk, wp, wv)


def _cache_kv_kernel(ckv_ref, kr_ref, wk_ref, wp_ref, wv_ref, k_ref, v_ref):
    _kv_expand(ckv_ref[...], kr_ref[...], wk_ref, wp_ref, wv_ref, k_ref, v_ref)


def _cache_kv(ckv, kr, wk, wp, wv):
    n = DEC_BATCH * PAST_LEN
    full = lambda shape: pl.BlockSpec(shape, lambda i: (0,) * len(shape))
    hw = MLA_HEADS * HEAD_BLK
    return pl.pallas_call(
        _cache_kv_kernel,
        grid=(n // PAST_LEN,),
        in_specs=[
            pl.BlockSpec((PAST_LEN, MLA_KV_RANK), lambda i: (i, 0)),
            pl.BlockSpec((PAST_LEN, 128), lambda i: (i, 0)),
            full((MLA_KV_RANK, hw)), full((128, hw)), full((MLA_KV_RANK, MLA_HEADS * MLA_V)),
        ],
        out_specs=[
            pl.BlockSpec((PAST_LEN, hw), lambda i: (i, 0)),
            pl.BlockSpec((PAST_LEN, MLA_HEADS * MLA_V), lambda i: (i, 0)),
        ],
        out_shape=[
            jax.ShapeDtypeStruct((n, hw), BF16),
            jax.ShapeDtypeStruct((n, MLA_HEADS * MLA_V), BF16),
        ],
        compiler_params=_cparams(("parallel",)),
        name="cache_kv",
    )(ckv, kr, wk, wp, wv)


def _attn_kernel(*refs, has_cache):
    if has_cache:
        q_ref, k_ref, v_ref, kc_ref, vc_ref, o_ref = refs
    else:
        q_ref, k_ref, v_ref, o_ref = refs
    nh = q_ref.shape[1] // HEAD_BLK
    cols = [slice(h * HEAD_BLK, (h + 1) * HEAD_BLK) for h in range(nh)]
    vcols = [slice((h // 2) * 2 * MLA_V, (h // 2 + 1) * 2 * MLA_V) for h in range(nh)]
    s = [_dot_nt(q_ref[:, c], k_ref[:, c]) for c in cols]
    m = [jnp.max(x, axis=-1, keepdims=True) for x in s]
    if has_cache:
        sc = [_dot_nt(q_ref[:, c], kc_ref[:, c]) for c in cols]
        m = [jnp.maximum(a, jnp.max(x, axis=-1, keepdims=True)) for a, x in zip(m, sc)]
        pc = [jnp.exp(x - a) for x, a in zip(sc, m)]
    p = [jnp.exp(x - a) for x, a in zip(s, m)]
    den = [jnp.sum(x, axis=-1, keepdims=True) for x in p]
    o = [_dot(x.astype(BF16), v_ref[:, c]) for x, c in zip(p, vcols)]
    if has_cache:
        den = [a + jnp.sum(x, axis=-1, keepdims=True) for a, x in zip(den, pc)]
        o = [a + _dot(x.astype(BF16), vc_ref[:, c]) for a, x, c in zip(o, pc, vcols)]
    outs = [a / b for a, b in zip(o, den)]
    lane = lax.broadcasted_iota(jnp.int32, (1, 2 * MLA_V), 1)
    for j in range(nh // 2):
        o_ref[:, j * 2 * MLA_V:(j + 1) * 2 * MLA_V] = jnp.where(lane < MLA_V, outs[2 * j], outs[2 * j + 1])


def _attention(q, k, v, n, nb, row0, tq, npair, kc=None, vc=None):
    has_cache = kc is not None
    nq = n // tq
    rb0 = row0 // tq
    kb0 = row0 // n
    in_specs = [
        pl.BlockSpec((tq, npair * 2 * HEAD_BLK), lambda b, h, i: (rb0 + b * nq + i, h)),
        pl.BlockSpec((n, npair * 2 * HEAD_BLK), lambda b, h, i: (kb0 + b, h)),
        pl.BlockSpec((n, npair * 2 * MLA_V), lambda b, h, i: (kb0 + b, h)),
    ]
    args = [q, k, v]
    if has_cache:
        in_specs += [
            pl.BlockSpec((PAST_LEN, npair * 2 * HEAD_BLK), lambda b, h, i: (b, h)),
            pl.BlockSpec((PAST_LEN, npair * 2 * MLA_V), lambda b, h, i: (b, h)),
        ]
        args += [kc, vc]
    return pl.pallas_call(
        functools.partial(_attn_kernel, has_cache=has_cache),
        grid=(nb, MLA_HEADS // (2 * npair), nq),
        in_specs=in_specs,
        out_specs=pl.BlockSpec((tq, npair * 2 * MLA_V), lambda b, h, i: (b * nq + i, h)),
        out_shape=jax.ShapeDtypeStruct((nb * n, MLA_HEADS * MLA_V), F32),
        compiler_params=_cparams(("parallel", "parallel", "parallel")),
        name="attn_cache" if has_cache else "attn_ctx",
    )(*args)


def _outproj_kernel(xa_ref, xb_ref, oga_ref, ogb_ref, oca_ref, ocb_ref, oma_ref, omb_ref, mod_ref, nw_ref,
                    wo_ref, wr_ref, br_ref, xo_ref, h_ref, route_ref, cnt_ref, carry_s, wrh_s, wrl_s):
    @pl.when(pl.program_id(0) == 0)
    def _():
        carry_s[...] = jnp.zeros(carry_s.shape, F32)
        wrh, wrl = _split2(wr_ref[...])
        wrh_s[...] = wrh
        wrl_s[...] = wrl

    mod = mod_ref[...]

    def residual(x_ref, og_ref, oc_ref, om_ref):
        attn = (_dot(og_ref[...].astype(BF16), wo_ref[0:GLA_W, :])
                + _dot(oc_ref[...].astype(BF16), wo_ref[GLA_W:GLA_W + CONV_W, :])
                + _dot(om_ref[...].astype(BF16), wo_ref[GLA_W + CONV_W:D_MODEL, :]))
        xo_ref[...] = x_ref[...] + mod[:, 2 * D_MODEL:3 * D_MODEL] * attn

    is_ctx = pl.program_id(0) < N_CTX // TM
    pl.when(is_ctx)(lambda: residual(xa_ref, oga_ref, oca_ref, oma_ref))
    pl.when(jnp.logical_not(is_ctx))(lambda: residual(xb_ref, ogb_ref, ocb_ref, omb_ref))
    xn = xo_ref[...]
    h = _rms(xn, nw_ref[...]) * (1.0 + mod[:, 4 * D_MODEL:5 * D_MODEL]) + mod[:, 3 * D_MODEL:4 * D_MODEL]
    _rows_to_tiles(h_ref, h)
    hh, hl = _split2(h)
    logits = _dot(hh, wrh_s[...]) + _dot(hl, wrh_s[...]) + _dot(hh, wrl_s[...]) + br_ref[...]
    lane = lax.broadcasted_iota(jnp.int32, logits.shape, 1).astype(F32)
    big = 1e9
    gl = jnp.where((lane >= N_EXPERTS) & (lane < N_EXPERTS + N_GROUPS), logits, NEG)
    gmax = jnp.max(gl, axis=-1, keepdims=True)
    gw = 1.0 / jnp.sum(jnp.exp(gl - gmax), axis=-1, keepdims=True)
    gi = jnp.min(jnp.where(gl == gmax, lane, big), axis=-1, keepdims=True) - N_EXPERTS
    lo = gi * EXPERTS_PER_GROUP
    el = jnp.where((lane >= lo) & (lane < lo + EXPERTS_PER_GROUP), logits, NEG)
    m1 = jnp.max(el, axis=-1, keepdims=True)
    i1 = jnp.min(jnp.where(el == m1, lane, big), axis=-1, keepdims=True)
    el2 = jnp.where(lane == i1, NEG, el)
    m2 = jnp.max(el2, axis=-1, keepdims=True)
    i2 = jnp.min(jnp.where(el2 == m2, lane, big), axis=-1, keepdims=True)
    e2 = jnp.exp(m2 - m1)
    p1 = 1.0 / (1.0 + e2)
    onehot = jnp.where((lane == i1) | (lane == i2), 1.0, 0.0)
    tr = lax.broadcasted_iota(jnp.int32, (TM, TM), 0)
    tc = lax.broadcasted_iota(jnp.int32, (TM, TM), 1)
    before = (tc < tr).astype(F32).astype(BF16)
    seen = _dot(before, onehot.astype(BF16)) + carry_s[0:1, :]
    rank1 = jnp.sum(jnp.where(lane == i1, seen, 0.0), axis=-1, keepdims=True)
    rank2 = jnp.sum(jnp.where(lane == i2, seen, 0.0), axis=-1, keepdims=True)
    carry_s[...] = carry_s[...] + jnp.sum(onehot, axis=0, keepdims=True)
    cnt_ref[...] = carry_s[...]
    cols = (i1, i2, gw * p1, gw * (e2 * p1), rank1, rank2)
    route = jnp.zeros(logits.shape, F32)
    for j, col in enumerate(cols):
        route = jnp.where(lane == j, col, route)
    route_ref[...] = route


def _outproj(x, og, oc, om, mods4, l, norm_w, wo, wr, br):
    full = lambda shape: pl.BlockSpec(shape, lambda i: (0,) * len(shape))
    row = lambda w: pl.BlockSpec((TM, w), lambda i: (i, 0))
    return pl.pallas_call(
        _outproj_kernel,
        grid=(N_TOK // TM,),
        in_specs=(_pair_specs(TM, D_MODEL, isinstance(x, tuple)) + _pair_specs(TM, GLA_W, True)
                  + _pair_specs(TM, CONV_W, True) + _pair_specs(TM, MLA_HEADS * MLA_V, True) + [
            pl.BlockSpec((None, None, 1, 6 * D_MODEL), lambda i: (l, _mod_index(i, TM), 0, 0)),
            full((1, D_MODEL)), full((D_MODEL, D_MODEL)), full((D_MODEL, 128)), full((1, 128)),
        ]),
        out_specs=[row(D_MODEL), pl.BlockSpec(_tiled(TM), lambda i: (i, 0)), row(128), full((8, 128))],
        out_shape=[
            jax.ShapeDtypeStruct((N_TOK, D_MODEL), F32),
            jax.ShapeDtypeStruct(_tiled(N_TOK), F32),
            jax.ShapeDtypeStruct((N_TOK, 128), F32),
            jax.ShapeDtypeStruct((8, 128), F32),
        ],
        scratch_shapes=[pltpu.VMEM((8, 128), F32), pltpu.VMEM((D_MODEL, 128), BF16),
                        pltpu.VMEM((D_MODEL, 128), BF16)],
        compiler_params=_cparams(("arbitrary",)),
        name="outproj_router",
    )(*_pair(x), *og, *oc, *om, mods4, norm_w, wo, wr, br)


N_ASSIGN = 2 * N_TOK
TE = 256
N_ETILE = N_ASSIGN // TE
N_WORK = N_ETILE + N_EXPERTS


ROW_SUB = 8


def _tiled(n):
    return (n * ROW_SUB, D_MODEL // ROW_SUB)


def _rows_to_tiles(ref, x, r0=0):
    n = x.shape[0]
    for s in range(ROW_SUB):
        ref[pl.ds(r0 * ROW_SUB + s, n, stride=ROW_SUB), :] = x[:, s * 128:(s + 1) * 128]


def _tiles_to_rows(ref, r0=0, n=None):
    n = ref.shape[0] // ROW_SUB if n is None else n
    return jnp.concatenate([ref[pl.ds(r0 * ROW_SUB + s, n, stride=ROW_SUB), :] for s in range(ROW_SUB)], axis=1)


def _row_tile(ref, t):
    return ref.at[pl.ds(pl.multiple_of(t * ROW_SUB, ROW_SUB), ROW_SUB)]


def _lane_col(x, lane, j):
    return jnp.sum(jnp.where(lane == j, x, 0.0), axis=-1, keepdims=True)


def _plan_kernel(cnt_s, cnt_ref, route_ref, pos_ref, wt_ref, we_ref, wlo_ref, whi_ref, wf_ref, wr_ref, wn_ref,
                 off_s):
    cnt = cnt_ref[...]
    chi = jnp.floor(cnt * (1.0 / 128.0))
    clo = cnt - chi * 128.0
    r = lax.broadcasted_iota(jnp.int32, (128, 128), 0)
    c = lax.broadcasted_iota(jnp.int32, (128, 128), 1)
    below = (r < c).astype(F32).astype(BF16)
    off = (128.0 * _dot(chi.astype(BF16), below) + _dot(clo.astype(BF16), below))[0:1, :]
    route = route_ref[...]
    lane = lax.broadcasted_iota(jnp.int32, route.shape, 1).astype(F32)
    pos1 = _lane_col(off, lane, _lane_col(route, lane, 0.0)) + _lane_col(route, lane, 4.0)
    pos2 = _lane_col(off, lane, _lane_col(route, lane, 1.0)) + _lane_col(route, lane, 5.0)
    pos_ref[...] = jnp.where(lane == 0.0, pos1, jnp.where(lane == 1.0, pos2, 0.0))

    @pl.when(pl.program_id(0) == 0)
    def _():
        def offs(e, acc):
            off_s[e] = acc
            return acc + cnt_s[e]

        off_s[N_EXPERTS] = lax.fori_loop(0, N_EXPERTS, offs, jnp.int32(0))

        def item(w, st):
            j, e, first = st
            active = j < N_ETILE
            jj = jnp.minimum(j, N_ETILE - 1)
            ee = jnp.minimum(e, N_EXPERTS - 1)
            tlo = jj * TE
            thi = tlo + TE
            end = off_s[ee + 1]
            wt_ref[w] = jj
            we_ref[w] = ee
            wlo_ref[w] = jnp.where(active, jnp.maximum(off_s[ee], tlo) - tlo, 0)
            whi_ref[w] = jnp.where(active, jnp.minimum(end, thi) - tlo, 0)
            wf_ref[w] = jnp.where(active, first, 0)
            adv_j = (active & (end >= thi)).astype(jnp.int32)
            adv_e = (active & (end <= thi)).astype(jnp.int32)
            return j + adv_j, e + adv_e, adv_j

        lax.fori_loop(0, N_WORK, item, (jnp.int32(0), jnp.int32(0), jnp.int32(1)))

        def run(w, r):
            r = r + (we_ref[w] != we_ref[jnp.maximum(w - 1, 0)]).astype(jnp.int32)
            wr_ref[w] = r
            return r

        lax.fori_loop(0, N_WORK, run, jnp.int32(0))

        def following(j, nxt):
            w = N_WORK - 1 - j
            wn_ref[w] = nxt
            return jnp.where(we_ref[jnp.maximum(w - 1, 0)] != we_ref[w], we_ref[w], nxt)

        lax.fori_loop(0, N_WORK, following, jnp.int32(-1))


def _plan(cnt_i, cnt, route):
    smem = pl.BlockSpec(memory_space=pltpu.SMEM)
    work = jax.ShapeDtypeStruct((N_WORK,), jnp.int32)
    return pl.pallas_call(
        _plan_kernel,
        grid=(1,),
        in_specs=[smem, pl.BlockSpec((8, 128), lambda i: (0, 0)), pl.BlockSpec((N_TOK, 128), lambda i: (0, 0))],
        out_specs=[pl.BlockSpec((N_TOK, 128), lambda i: (0, 0))] + [smem] * 7,
        out_shape=[jax.ShapeDtypeStruct((N_TOK, 128), F32)] + [work] * 7,
        scratch_shapes=[pltpu.SMEM((N_EXPERTS + 1,), jnp.int32)],
        compiler_params=_cparams(("arbitrary",)),
        name="moe_plan",
    )(cnt_i, cnt, route)


def _dispatch_kernel(pos_s, h_ref, xs_hbm, sem):
    base = pl.program_id(0) * TM

    def copy(i, k):
        return pltpu.make_async_copy(_row_tile(h_ref, i), _row_tile(xs_hbm, pos_s[2 * (base + i) + k]), sem)

    def start(i, carry):
        copy(i, 0).start(priority=0)
        copy(i, 1).start(priority=1)
        return carry

    def wait(i, carry):
        copy(i, 0).wait()
        copy(i, 1).wait()
        return carry

    lax.fori_loop(0, TM, start, 0, unroll=8)
    lax.fori_loop(0, TM, wait, 0, unroll=8)


def _dispatch(pos, h):
    any_spec = pl.BlockSpec(memory_space=pl.ANY)
    return pl.pallas_call(
        _dispatch_kernel,
        grid_spec=pltpu.PrefetchScalarGridSpec(
            num_scalar_prefetch=1, grid=(N_TOK // TM,),
            in_specs=[pl.BlockSpec(_tiled(TM), lambda i, p: (i, 0))], out_specs=any_spec,
            scratch_shapes=[pltpu.SemaphoreType.DMA(())]),
        out_shape=jax.ShapeDtypeStruct(_tiled(N_ASSIGN), F32),
        compiler_params=_cparams(("arbitrary",)),
        name="moe_dispatch",
    )(pos, h)


def _experts_kernel(wt_s, we_s, wlo_s, whi_s, wf_s, wr_s, wn_s, xs_ref, w1_hbm, w3_hbm, w2_hbm, o_ref,
                    w1_s, w3_s, w2_s, f1_s, f3_s, f2_s, sem, *, layer):
    w = pl.program_id(0)
    first = wf_s[w] == 1
    nonempty = whi_s[w] > wlo_s[w]

    def fetch(e, slot):
        return [pltpu.make_async_copy(src.at[layer, e], dst.at[slot], sem.at[slot])
                for src, dst in ((w1_hbm, f1_s), (w3_hbm, f3_s), (w2_hbm, f2_s))]

    @pl.when(w == 0)
    def _():
        for c in fetch(we_s[0], wr_s[0] % 2):
            c.start()

    @pl.when((w == 0) | (we_s[w] != we_s[jnp.maximum(w - 1, 0)]))
    def _():
        slot = wr_s[w] % 2
        for c in fetch(we_s[w], slot):
            c.wait()
        w1_s[...] = f1_s[slot].astype(BF16)
        w3_s[...] = f3_s[slot].astype(BF16)
        w2_s[...] = f2_s[slot].astype(BF16)

        @pl.when(wn_s[w] >= 0)
        def _():
            for c in fetch(wn_s[w], 1 - slot):
                c.start()

    @pl.when(first & jnp.logical_not(nonempty))
    def _():
        o_ref[...] = jnp.zeros(o_ref.shape, F32)

    nsplit = 2
    hrows = TE // nsplit

    def expert_rows(merge):
        x = [_tiles_to_rows(xs_ref, j * hrows, hrows).astype(BF16) for j in range(nsplit)]
        a = [_dot(v, w1_s[...]) for v in x]
        b = [_dot(v, w3_s[...]) for v in x]
        hid = [(u * _sigmoid(u) * v).astype(BF16) for u, v in zip(a, b)]
        y = [_dot(v, w2_s[...]) for v in hid]
        for j in range(nsplit):
            row = lax.broadcasted_iota(jnp.int32, (hrows, 1), 0) + j * hrows
            mine = (row >= wlo_s[w]) & (row < whi_s[w])
            other = _tiles_to_rows(o_ref, j * hrows, hrows) if merge else 0.0
            _rows_to_tiles(o_ref, jnp.where(mine, y[j], other), j * hrows)

    @pl.when(nonempty & first)
    def _():
        expert_rows(False)

    @pl.when(nonempty & jnp.logical_not(first))
    def _():
        expert_rows(True)


def _experts(work, xs, l, w1, w3, w2):
    tile = pl.BlockSpec(_tiled(TE), lambda w, wt, *_: (wt[w], 0))
    any_spec = pl.BlockSpec(memory_space=pl.ANY)
    return pl.pallas_call(
        functools.partial(_experts_kernel, layer=l),
        grid_spec=pltpu.PrefetchScalarGridSpec(
            num_scalar_prefetch=7, grid=(N_WORK,),
            in_specs=[tile, any_spec, any_spec, any_spec],
            out_specs=tile,
            scratch_shapes=[pltpu.VMEM((D_MODEL, EXPERT_FF), BF16), pltpu.VMEM((D_MODEL, EXPERT_FF), BF16),
                            pltpu.VMEM((EXPERT_FF, D_MODEL), BF16),
                            pltpu.VMEM((2, D_MODEL, EXPERT_FF), F32), pltpu.VMEM((2, D_MODEL, EXPERT_FF), F32),
                            pltpu.VMEM((2, EXPERT_FF, D_MODEL), F32), pltpu.SemaphoreType.DMA((2,))]),
        out_shape=jax.ShapeDtypeStruct(_tiled(N_ASSIGN), F32),
        compiler_params=_cparams(("arbitrary",)),
        name="moe_experts",
    )(*work, xs, w1, w3, w2)


def _combine_kernel(pos_s, x_ref, route_ref, mod_ref, fw_ref, ys_hbm, *refs, final):
    if final:
        oa_ref, ob_ref, y_s, sem = refs
    else:
        o_ref, y_s, sem = refs
    step = pl.program_id(0)
    nsteps = pl.num_programs(0)

    def copy(t, i, k):
        slot = t % 2
        src = _row_tile(ys_hbm, pos_s[2 * (t * TM + i) + k])
        return pltpu.make_async_copy(src, _row_tile(y_s.at[slot, k], i), sem.at[slot])

    def gather(t):
        def start(i, carry):
            copy(t, i, 0).start(priority=0)
            copy(t, i, 1).start(priority=1)
            return carry

        lax.fori_loop(0, TM, start, 0, unroll=8)

    def wait(i, carry):
        copy(step, i, 0).wait()
        copy(step, i, 1).wait()
        return carry

    pl.when(step == 0)(lambda: gather(step))
    pl.when(step + 1 < nsteps)(lambda: gather(step + 1))
    lax.fori_loop(0, TM, wait, 0, unroll=8)
    route = route_ref[...]
    lane = lax.broadcasted_iota(jnp.int32, route.shape, 1)
    slot = step % 2
    moe = (_lane_col(route, lane, 2) * _tiles_to_rows(y_s.at[slot, 0])
           + _lane_col(route, lane, 3) * _tiles_to_rows(y_s.at[slot, 1]))
    x = x_ref[...] + mod_ref[:, 5 * D_MODEL:6 * D_MODEL] * moe
    if final:
        y = _rms(x, fw_ref[...])
        is_ctx = pl.program_id(0) < N_CTX // TM

        @pl.when(is_ctx)
        def _():
            oa_ref[...] = y

        @pl.when(jnp.logical_not(is_ctx))
        def _():
            ob_ref[...] = y
    else:
        o_ref[...] = x


def _combine(pos, x, route, mods4, l, fw, ys, final):
    row = lambda w: pl.BlockSpec((TM, w), lambda i, p: (i, 0))
    if final:
        out_specs = _pair_specs(TM, D_MODEL, True)
        out_shape = [jax.ShapeDtypeStruct((N_CTX, D_MODEL), F32), jax.ShapeDtypeStruct((N_SMP, D_MODEL), F32)]
    else:
        out_specs = row(D_MODEL)
        out_shape = jax.ShapeDtypeStruct((N_TOK, D_MODEL), F32)
    return pl.pallas_call(
        functools.partial(_combine_kernel, final=final),
        grid_spec=pltpu.PrefetchScalarGridSpec(
            num_scalar_prefetch=1, grid=(N_TOK // TM,),
            in_specs=[row(D_MODEL), row(128),
                      pl.BlockSpec((None, None, 1, 6 * D_MODEL), lambda i, p: (l, _mod_index(i, TM), 0, 0)),
                      pl.BlockSpec((1, D_MODEL), lambda i, p: (0, 0)),
                      pl.BlockSpec(memory_space=pl.ANY)],
            out_specs=out_specs,
            scratch_shapes=[pltpu.VMEM((2, 2) + _tiled(TM), F32), pltpu.SemaphoreType.DMA((2,))]),
        out_shape=out_shape,
        compiler_params=_cparams(("arbitrary",)),
        name="moe_combine_final" if final else "moe_combine",
    )(pos, x, route, mods4, fw, ys)


def _rope_swap(w):
    nf = MLA_ROPE // 4
    g = w.reshape(w.shape[:-1] + (2, 2, nf))
    return jnp.stack([-g[..., 1, :], g[..., 0, :]], axis=-2).reshape(w.shape)


def _rope_tables():
    nf = MLA_ROPE // 4
    pos = np.arange(DEC_SEQ)
    row = (pos // GRID_W).astype(np.float32)
    col = (pos % GRID_W).astype(np.float32)
    inv = (np.float32(ROPE_THETA) ** (-np.arange(nf, dtype=np.float32) / np.float32(nf))).astype(np.float32)
    ar = (row[:, None] * inv).astype(np.float32)
    ac = (col[:, None] * inv).astype(np.float32)
    cos = np.concatenate([np.cos(ar), np.cos(ar), np.cos(ac), np.cos(ac)], axis=1)
    sin = np.concatenate([np.sin(ar), np.sin(ar), np.sin(ac), np.sin(ac)], axis=1)
    cos = np.concatenate([np.ones((TM, MLA_ROPE)), cos], axis=0).astype(np.float32)
    sin = np.concatenate([np.zeros((TM, MLA_ROPE)), sin], axis=0).astype(np.float32)
    n = cos.shape[0]
    z32 = np.zeros((n, 32), np.float32)
    tqc = np.concatenate([np.ones((n, MLA_NOPE), np.float32), cos, z32], axis=1)
    tqs = np.concatenate([np.zeros((n, MLA_NOPE), np.float32), sin, z32], axis=1)
    tk = np.concatenate([z32, cos, sin, z32], axis=1)
    return jnp.asarray(tqc), jnp.asarray(tqs), jnp.asarray(tk)


def _prep_layer(l, w_in, gla_wg_f, gla_wg_b, mla_wuq, mla_wukv, w_out, moe_wg, moe_bg, moe_we, moe_be):
    offs = np.cumsum([0, 256, 256, 256, 256, 16, 16, 512, 256, 128, 32])
    seg = lambda i: w_in[l][:, offs[i]:offs[i + 1]]
    kr = seg(9)
    w_in_r = jnp.concatenate(
        [seg(0), seg(1), seg(2), seg(3), seg(6), seg(7), seg(8), seg(4), seg(5), kr, _rope_swap(kr),
         jnp.zeros((D_MODEL, 32), F32)], axis=1).astype(BF16)
    wgf = jnp.zeros((128, GLA_W), F32).at[0:16].set(gla_wg_f[l])
    wgb = jnp.zeros((128, GLA_W), F32).at[16:32].set(gla_wg_b[l])
    wq = mla_wuq[l].reshape(MLA_Q_RANK, MLA_HEADS, MLA_NOPE + MLA_ROPE)
    nope, rope = wq[..., :MLA_NOPE], wq[..., MLA_NOPE:]
    z = lambda n: jnp.zeros((MLA_Q_RANK, MLA_HEADS, n), F32)
    hw = MLA_HEADS * HEAD_BLK
    wa = jnp.concatenate([nope, rope, z(32)], axis=-1).reshape(MLA_Q_RANK, hw).astype(BF16)
    wb = jnp.concatenate([z(MLA_NOPE), _rope_swap(rope), z(32)], axis=-1).reshape(MLA_Q_RANK, hw).astype(BF16)
    wkv = mla_wukv[l].reshape(MLA_KV_RANK, MLA_HEADS, MLA_NOPE + MLA_V)
    wk = jnp.concatenate([wkv[..., :MLA_NOPE], jnp.zeros((MLA_KV_RANK, MLA_HEADS, 64), F32)],
                         axis=-1).reshape(MLA_KV_RANK, hw).astype(BF16)
    wv = wkv[..., MLA_NOPE:].reshape(MLA_KV_RANK, MLA_HEADS * MLA_V).astype(BF16)
    wp_np = np.zeros((128, MLA_HEADS, HEAD_BLK), np.float32)
    for j in range(MLA_ROPE):
        wp_np[32 + j, :, MLA_NOPE + j] = 1.0
        wp_np[64 + j, :, MLA_NOPE + j] = 1.0
    wp = jnp.asarray(wp_np.reshape(128, hw)).astype(BF16)
    wo = w_out[l].astype(BF16)
    wr = jnp.concatenate([moe_we[l], moe_wg[l], jnp.zeros((D_MODEL, 128 - N_EXPERTS - N_GROUPS), F32)], axis=1)
    br = jnp.concatenate([moe_be[l], moe_bg[l], jnp.zeros((128 - N_EXPERTS - N_GROUPS,), F32)]).reshape(1, 128)
    return dict(w_in_r=w_in_r, wgf=wgf, wgb=wgb, wa=wa, wb=wb, wk=wk, wv=wv, wp=wp, wo=wo, wr=wr, br=br)


def kernel(x_prompt, x_sample, cache_ckv, cache_krope, state_gla_fwd, state_gla_bwd, c, c_ctx, ada_w, ada_b,
           norm1_w, norm2_w, w_in, gla_wg_f, gla_bg_f, gla_wg_b, gla_bg_b, gla_norm_w, conv_w, conv_b,
           conv_ln_w, conv_ln_b, mla_qnorm_w, mla_wuq, mla_kvnorm_w, mla_wukv, w_out, moe_wg, moe_bg, moe_we,
           moe_be, moe_w1, moe_w3, moe_w2, final_norm_w):
    x = (x_prompt.reshape(N_CTX, D_MODEL), x_sample.reshape(N_SMP, D_MODEL))
    cond8 = jnp.concatenate([c_ctx[None, :], c, jnp.zeros((N_MOD - 1 - DEC_BATCH, D_MODEL), F32)], axis=0)
    mods4 = _ada_table(cond8, ada_w, ada_b).reshape(DEPTH, N_MOD, 1, 6 * D_MODEL)
    tabs = _rope_tables()
    row1 = lambda a: a.reshape(1, -1)

    ckv_new, kr_new, sf_new, sb_new = [], [], [], []
    for l in range(DEPTH):
        w = _prep_layer(l, w_in, gla_wg_f, gla_wg_b, mla_wuq, mla_wukv, w_out, moe_wg, moe_bg, moe_we, moe_be)
        p = _inproj(x, mods4, l, row1(norm1_w[l]), w["w_in_r"])

        gla_args = (p, w["wgf"], w["wgb"], row1(gla_bg_f[l]), row1(gla_bg_b[l]), row1(gla_norm_w[l]))
        og_c, sf, sb = _gla(*gla_args, SEQ, BATCH, 0)
        og_s = _gla(*gla_args, DEC_SEQ, DEC_BATCH, N_CTX // DEC_SEQ, (state_gla_fwd, l), (state_gla_bwd, l))[0]
        og = (og_c, og_s)

        cw = jnp.concatenate([conv_w[l], jnp.zeros((1, CONV_W), F32)], axis=0)
        conv_args = (p, cw, row1(conv_b[l]), row1(conv_ln_w[l]), row1(conv_ln_b[l]))
        oc = (_conv(*conv_args, SEQ, BATCH, 0), _conv(*conv_args, DEC_SEQ, DEC_BATCH, N_CTX // DEC_SEQ))

        q, k, v, ckvn, kr = _mla_prep(p, tabs, row1(mla_qnorm_w[l]), row1(mla_kvnorm_w[l]),
                                      w["wa"], w["wb"], w["wk"], w["wp"], w["wv"])
        cc = cache_ckv[:, l].reshape(DEC_BATCH * PAST_LEN, MLA_KV_RANK)
        ckr = cache_krope[:, l].reshape(DEC_BATCH * PAST_LEN, MLA_ROPE)
        ckr = jnp.pad(ckr, ((0, 0), (32, 64)))
        kc, vc = _cache_kv(cc, ckr, w["wk"], w["wp"], w["wv"])
        om = (_attention(q, k, v, SEQ, BATCH, 0, SEQ, MLA_HEADS // 2),
              _attention(q, k, v, DEC_SEQ, DEC_BATCH, N_CTX, 512, 1, kc, vc))

        xn, h2, route, cnt = _outproj(x, og, oc, om, mods4, l, row1(norm2_w[l]), w["wo"], w["wr"], w["br"])
        posf, *work = _plan(cnt[0, :N_EXPERTS].astype(jnp.int32), cnt, route)
        pos = posf[:, 0:2].astype(jnp.int32).reshape(N_ASSIGN)
        ys = _experts(work, _dispatch(pos, h2), l, moe_w1, moe_w3, moe_w2)
        x = _combine(pos, xn, route, mods4, l, row1(final_norm_w), ys, l == DEPTH - 1)

        ckv_new.append(ckvn[:N_CTX].reshape(BATCH, SEQ, MLA_KV_RANK))
        kr_new.append(kr[:N_CTX, 32:32 + MLA_ROPE].reshape(BATCH, SEQ, MLA_ROPE))
        sf_new.append(sf)
        sb_new.append(sb)

    y_ctx, y_smp = x
    return (y_ctx.reshape(BATCH, SEQ, D_MODEL), y_smp.reshape(DEC_BATCH, DEC_SEQ, D_MODEL),
            jnp.stack(ckv_new, axis=1), jnp.stack(kr_new, axis=1),
            jnp.stack(sf_new, axis=1), jnp.stack(sb_new, axis=1))
```

```python
import functools

import numpy as np
import jax
import jax.numpy as jnp
from jax import lax
from jax.experimental import pallas as pl
from jax.experimental.pallas import tpu as pltpu

F32 = jnp.float32
BF16 = jnp.bfloat16

D_MODEL = 1024
BATCH = 16
SEQ = 256
DEPTH = 2
DEC_BATCH = 4
DEC_SEQ = 1024
PAST_LEN = 512
GRID_W = 64
NORM_EPS = 1e-6
GLA_HEADS = 4
GLA_DK = 64
GLA_DV = 64
GLA_W = 256
GLA_GATE_RANK = 16
GLA_TAU = 16.0
CONV_W = 256
CONV_K = 31
MLA_HEADS = 8
MLA_NOPE = 64
MLA_ROPE = 32
MLA_V = 64
MLA_Q_RANK = 256
MLA_KV_RANK = 128
ROPE_THETA = 10000.0
N_GROUPS = 4
EXPERTS_PER_GROUP = 8
N_EXPERTS = 32
EXPERT_FF = 256

N_CTX = BATCH * SEQ
N_SMP = DEC_BATCH * DEC_SEQ
N_TOK = N_CTX + N_SMP
N_MOD = 8
TM = 512
P_COLS = 2048
MISC_BLK = 15
CHUNK = 64
SUB = 16
NEG = -1e30
VMEM_LIMIT = 56 * 1024 * 1024


def _cparams(sem):
    return pltpu.CompilerParams(dimension_semantics=sem, vmem_limit_bytes=VMEM_LIMIT)


def _dot(a, b):
    return jnp.dot(a, b, preferred_element_type=F32)


def _dot_nt(a, b):
    return lax.dot_general(a, b, (((1,), (1,)), ((), ())), preferred_element_type=F32)


def _split2(x):
    hi = x.astype(BF16)
    lo = (x - hi.astype(F32)).astype(BF16)
    return hi, lo


def _dot3(a, w):
    ah, al = _split2(a)
    wh, wl = _split2(w)
    return _dot(ah, wh) + _dot(al, wh) + _dot(ah, wl)


def _sigmoid(x):
    return 1.0 / (1.0 + jnp.exp(-x))


def _rms(x, w):
    ms = jnp.mean(x * x, axis=-1, keepdims=True)
    return x * lax.rsqrt(ms + NORM_EPS) * w


def _mod_index(i, tile):
    nctx = N_CTX // tile
    per = DEC_SEQ // tile
    return jnp.where(i < nctx, 0, 1 + (i - nctx) // per)


def _ada_kernel(c_ref, w_ref, b_ref, o_ref):
    c = c_ref[...]
    o_ref[...] = _dot3(c * _sigmoid(c), w_ref[...]) + b_ref[...]


def _ada_table(cond8, ada_w, ada_b):
    tn = 1024
    n6 = 6 * D_MODEL
    return pl.pallas_call(
        _ada_kernel,
        grid=(DEPTH, n6 // tn),
        in_specs=[
            pl.BlockSpec((N_MOD, D_MODEL), lambda l, j: (0, 0)),
            pl.BlockSpec((None, D_MODEL, tn), lambda l, j: (l, 0, j)),
            pl.BlockSpec((None, 1, tn), lambda l, j: (l, 0, j)),
        ],
        out_specs=pl.BlockSpec((None, N_MOD, tn), lambda l, j: (l, 0, j)),
        out_shape=jax.ShapeDtypeStruct((DEPTH, N_MOD, n6), F32),
        compiler_params=_cparams(("parallel", "parallel")),
        name="ada_table",
    )(cond8, ada_w, ada_b.reshape(DEPTH, 1, n6))


def _pair_specs(tile, width, split):
    nctx = N_CTX // tile
    offb = 0 if split else nctx
    return [pl.BlockSpec((tile, width), lambda i, *_: (jnp.minimum(i, nctx - 1), 0)),
            pl.BlockSpec((tile, width), lambda i, *_: (jnp.maximum(i - nctx, 0) + offb, 0))]


def _pair(x):
    return x if isinstance(x, tuple) else (x, x)


TM_IN = 512


def _inproj_kernel(xa_ref, xb_ref, mod_ref, nw_ref, w_ref, o_ref):
    mod = mod_ref[...]

    def project(x_ref):
        h = _rms(x_ref[...], nw_ref[...]) * (1.0 + mod[:, D_MODEL:2 * D_MODEL]) + mod[:, 0:D_MODEL]
        o_ref[...] = _dot(h.astype(BF16), w_ref[...])

    is_ctx = pl.program_id(0) < N_CTX // TM_IN
    pl.when(is_ctx)(lambda: project(xa_ref))
    pl.when(jnp.logical_not(is_ctx))(lambda: project(xb_ref))


def _inproj(x, mods4, l, norm_w, w_in_r):
    return pl.pallas_call(
        _inproj_kernel,
        grid=(N_TOK // TM_IN,),
        in_specs=_pair_specs(TM_IN, D_MODEL, isinstance(x, tuple)) + [
            pl.BlockSpec((None, None, 1, 6 * D_MODEL), lambda i: (l, _mod_index(i, TM_IN), 0, 0)),
            pl.BlockSpec((1, D_MODEL), lambda i: (0, 0)),
            pl.BlockSpec((D_MODEL, P_COLS), lambda i: (0, 0)),
        ],
        out_specs=pl.BlockSpec((TM_IN, P_COLS), lambda i: (i, 0)),
        out_shape=jax.ShapeDtypeStruct((N_TOK, P_COLS), F32),
        compiler_params=_cparams(("parallel",)),
        name="inproj",
    )(*_pair(x), mods4, norm_w, w_in_r)


def _log_sigmoid(x):
    return jnp.minimum(x, 0.0) - jnp.log(1.0 + jnp.exp(-jnp.abs(x)))


def _split3(x):
    hi = x.astype(BF16)
    r = x - hi.astype(F32)
    mid = r.astype(BF16)
    lo = (r - mid.astype(F32)).astype(BF16)
    return hi, mid, lo


def _gla_consts(reverse):
    t = lax.broadcasted_iota(jnp.int32, (CHUNK, CHUNK), 0)
    s = lax.broadcasted_iota(jnp.int32, (CHUNK, CHUNK), 1)
    mid = jnp.bitwise_and(t, -SUB) + SUB // 2
    if reverse:
        cum = s >= t
        ref = s >= mid
    else:
        cum = s <= t
        ref = s <= mid
    cum_ref = jnp.concatenate([cum, ref], axis=0).astype(F32).astype(BF16)
    rt = jnp.bitwise_and(lax.broadcasted_iota(jnp.int32, (GLA_HEADS * SUB, CHUNK), 0), SUB - 1)
    cs = lax.broadcasted_iota(jnp.int32, (GLA_HEADS * SUB, CHUNK), 1)
    rowi = lax.broadcasted_iota(jnp.int32, (CHUNK, 1), 0)
    causal, valid = [], []
    for i in range(CHUNK // SUB):
        tq = rt + i * SUB
        causal.append((cs >= tq) if reverse else (cs <= tq))
        valid.append((rowi >= i * SUB) if reverse else (rowi < (i + 1) * SUB))
    return cum_ref, causal, valid


def _gla_local(jobs, nc, o_s, qe_s, u_s, dec_s, head_masks, block_mask):
    nsub = CHUNK // SUB
    rows = [pl.ds(pl.multiple_of(c * CHUNK, CHUNK), CHUNK) for c, *_ in jobs]
    bs = []
    for (c, d, qs, k, vb, vt, la_ref, consts), r in zip(jobs, rows):
        h0, h1, h2 = _split3(la_ref[r, :])
        br = _dot(consts[0], h0) + _dot(consts[0], h1) + _dot(consts[0], h2)
        bs.append((br[0:CHUNK], br[CHUNK:2 * CHUNK]))
    ops = []
    for (c, d, qs, k, vb, vt, la_ref, consts), (b, rr) in zip(jobs, bs):
        qhat = qs * jnp.exp(b - rr)
        for i in range(nsub):
            r_i = rr[i * SUB:i * SUB + 1]
            kt = (k * jnp.exp(jnp.where(consts[2][i], r_i - b, NEG))).astype(BF16)
            qi = qhat[i * SUB:(i + 1) * SUB]
            ops.append((jnp.concatenate([jnp.where(m, qi, 0.0) for m in head_masks], axis=0).astype(BF16), kt))
    scores = [_dot_nt(qbig, kt) for qbig, kt in ops]
    outs = []
    for j, (c, d, qs, k, vb, vt, la_ref, consts) in enumerate(jobs):
        for i in range(nsub):
            a = jnp.where(consts[1][i], scores[j * nsub + i], 0.0)
            outs.append(_dot(a.astype(BF16), vb))
    for j, ((c, d, qs, k, vb, vt, la_ref, consts), (b, rr), r) in enumerate(zip(jobs, bs, rows)):
        parts = []
        for i in range(nsub):
            ov = outs[j * nsub + i]
            oi = jnp.where(head_masks[0], ov[0:SUB], 0.0)
            for h in range(1, GLA_HEADS):
                oi = oi + jnp.where(head_masks[h], ov[h * SUB:(h + 1) * SUB], 0.0)
            parts.append(oi)
        o_s[d, r, :] = jnp.concatenate(parts, axis=0)
        qe_s[d, r, :] = (qs * jnp.exp(b)).astype(BF16)
        b_last = b[0:1] if d == 1 else b[CHUNK - 1:CHUNK]
        kl = (k * jnp.exp(b_last - b)).astype(BF16)
        idx = d * nc + c
        u_s[idx] = jnp.where(block_mask, _dot(vt, kl), 0.0)
        dec_s[pl.ds(pl.multiple_of(idx * 8, 8), 8), :] = jnp.broadcast_to(jnp.exp(b_last), (8, GLA_W))


GLA_SAFE_EXPONENT = 60.0


def _gla_exact_intra(d, nc, qkvg_ref, la_ref, cum, o_s, b_s, acc_s, ones_blk):
    reverse = d == 1
    tcol = lax.broadcasted_iota(jnp.int32, (CHUNK, 1), 0)

    def chunk(c, carry):
        r0 = pl.multiple_of(c * CHUNK, CHUNK)
        rows = pl.ds(r0, CHUNK)
        h0, h1, h2 = _split3(la_ref[rows, :])
        b_s[...] = _dot(cum, h0) + _dot(cum, h1) + _dot(cum, h2)
        acc_s[...] = jnp.zeros((CHUNK, GLA_W), F32)
        qs = qkvg_ref[rows, 0:GLA_W] * (GLA_DK ** -0.5)

        def key(s, carry2):
            bs = b_s[pl.ds(s, 1), :]
            ks = qkvg_ref[pl.ds(r0 + s, 1), GLA_W:2 * GLA_W]
            vs = qkvg_ref[pl.ds(r0 + s, 1), 2 * GLA_W:3 * GLA_W]
            live = (tcol <= s) if reverse else (tcol >= s)
            hi, lo = _split2(qs * ks * jnp.exp(jnp.where(live, b_s[...] - bs, NEG)))
            acc_s[...] += (_dot(hi, ones_blk) + _dot(lo, ones_blk)) * vs
            return carry2

        lax.fori_loop(0, CHUNK, key, 0)
        o_s[d, rows, :] = acc_s[...]
        return carry

    lax.fori_loop(0, nc, chunk, 0)


def _gla_kernel(*refs, n, has_state):
    if has_state:
        (qkvg_ref, misc_ref, wgf_ref, wgb_ref, bgf_ref, bgb_ref, nw_ref, s0f_ref, s0b_ref,
         o_ref, laf_s, lab_s, o_s, qe_s, u_s, dec_s, sts_s, stf_s, stb_s, b_s, acc_s) = refs
    else:
        (qkvg_ref, misc_ref, wgf_ref, wgb_ref, bgf_ref, bgb_ref, nw_ref,
         o_ref, sf_ref, sb_ref, laf_s, lab_s, o_s, qe_s, u_s, dec_s, sts_s, stf_s, stb_s, b_s, acc_s) = refs
    misc = misc_ref[...]
    laf_s[...] = _log_sigmoid(_dot3(misc, wgf_ref[...]) + bgf_ref[...]) * (1.0 / GLA_TAU)
    lab_s[...] = _log_sigmoid(_dot3(misc, wgb_ref[...]) + bgb_ref[...]) * (1.0 / GLA_TAU)
    stf_s[...] = jnp.zeros((GLA_W, GLA_W), F32)
    stb_s[...] = jnp.zeros((GLA_W, GLA_W), F32)
    if has_state:
        for st_s, s0_ref in ((stf_s, s0f_ref), (stb_s, s0b_ref)):
            for h in range(GLA_HEADS):
                st_s[h * GLA_DK:(h + 1) * GLA_DK, h * GLA_DV:(h + 1) * GLA_DV] = s0_ref[h]
            st_s[...] = st_s[...].T

    lane = lax.broadcasted_iota(jnp.int32, (1, GLA_W), 1)
    head_masks = [jnp.right_shift(lane, 6) == h for h in range(GLA_HEADS)]
    bi = jnp.right_shift(lax.broadcasted_iota(jnp.int32, (GLA_W, GLA_W), 0), 6)
    bj = jnp.right_shift(lax.broadcasted_iota(jnp.int32, (GLA_W, GLA_W), 1), 6)
    block_mask = bi == bj
    consts_f = _gla_consts(False)
    consts_b = _gla_consts(True)
    nc = n // CHUNK

    def local(cc, carry):
        jobs = []
        for c in (2 * cc, 2 * cc + 1):
            rows = pl.ds(pl.multiple_of(c * CHUNK, CHUNK), CHUNK)
            qs = qkvg_ref[rows, 0:GLA_W] * (GLA_DK ** -0.5)
            k = qkvg_ref[rows, GLA_W:2 * GLA_W]
            v = qkvg_ref[rows, 2 * GLA_W:3 * GLA_W]
            vb = v.astype(BF16)
            vt = v.T.astype(BF16)
            jobs += [(c, 0, qs, k, vb, vt, laf_s, consts_f), (c, 1, qs, k, vb, vt, lab_s, consts_b)]
        _gla_local(jobs, nc, o_s, qe_s, u_s, dec_s, head_masks, block_mask)
        return carry

    lax.fori_loop(0, nc // 2, local, 0)

    ones_blk = block_mask.astype(F32).astype(BF16)
    for d, (la_s, consts) in enumerate(((laf_s, consts_f), (lab_s, consts_b))):
        steepest = jnp.min(la_s[...])
        pl.when(steepest * (SUB // 2) < -GLA_SAFE_EXPONENT)(
            functools.partial(_gla_exact_intra, d, nc, qkvg_ref, la_s, consts[0][0:CHUNK], o_s, b_s, acc_s, ones_blk))

    def scan(j, carry):
        for d, st_s in enumerate((stf_s, stb_s)):
            idx = d * nc + (j if d == 0 else nc - 1 - j)
            st = st_s[...]
            sts_s[idx] = st.astype(BF16)
            st_s[...] = st * dec_s[pl.ds(pl.multiple_of(idx * 8, 8), 1), :] + u_s[idx]
        return carry

    lax.fori_loop(0, nc, scan, 0)

    def inter(cc, carry):
        jobs = [(c, d, pl.ds(pl.multiple_of(c * CHUNK, CHUNK), CHUNK))
                for c in (2 * cc, 2 * cc + 1) for d in range(2)]
        res = [_dot_nt(qe_s[d, r, :], sts_s[d * nc + c]) for c, d, r in jobs]
        for (c, d, r), o in zip(jobs, res):
            o_s[d, r, :] += o
        return carry

    lax.fori_loop(0, nc // 2, inter, 0)

    if not has_state:
        for st_s, s_ref in ((stf_s, sf_ref), (stb_s, sb_ref)):
            s = st_s[...].T
            for h in range(GLA_HEADS):
                s_ref[h] = s[h * GLA_DK:(h + 1) * GLA_DK, h * GLA_DV:(h + 1) * GLA_DV]

    rt = 256
    for r0 in range(0, n, rt):
        o = o_s[0, r0:r0 + rt, :] + o_s[1, r0:r0 + rt, :]
        hi, lo = _split2(o * o)
        ms = (_dot(hi, ones_blk) + _dot(lo, ones_blk)) * (1.0 / GLA_DV)
        g = qkvg_ref[r0:r0 + rt, 3 * GLA_W:4 * GLA_W]
        o_ref[r0:r0 + rt, :] = o * lax.rsqrt(ms + NORM_EPS) * nw_ref[...] * (g * _sigmoid(g))


def _gla(p, wgf, wgb, bgf, bgb, nw, n, nb, row_blk0, s0f=None, s0b=None):
    has_state = s0f is not None
    full = lambda shape: pl.BlockSpec(shape, lambda b: (0,) * len(shape))
    in_specs = [
        pl.BlockSpec((n, 4 * GLA_W), lambda b: (row_blk0 + b, 0)),
        pl.BlockSpec((n, 128), lambda b: (row_blk0 + b, MISC_BLK)),
        full((128, GLA_W)), full((128, GLA_W)), full((1, GLA_W)), full((1, GLA_W)), full((1, GLA_W)),
    ]
    args = [p, p, wgf, wgb, bgf, bgb, nw]
    out_specs = [pl.BlockSpec((n, GLA_W), lambda b: (b, 0))]
    out_shape = [jax.ShapeDtypeStruct((nb * n, GLA_W), F32)]
    if has_state:
        layer = s0f[1]
        st_spec = pl.BlockSpec((None, None, GLA_HEADS, GLA_DK, GLA_DV), lambda b: (b, layer, 0, 0, 0))
        in_specs += [st_spec, st_spec]
        args += [s0f[0], s0b[0]]
    else:
        st_spec = pl.BlockSpec((None, GLA_HEADS, GLA_DK, GLA_DV), lambda b: (b, 0, 0, 0))
        out_specs += [st_spec, st_spec]
        out_shape += [jax.ShapeDtypeStruct((nb, GLA_HEADS, GLA_DK, GLA_DV), F32)] * 2
    return pl.pallas_call(
        functools.partial(_gla_kernel, n=n, has_state=has_state),
        grid=(nb,),
        in_specs=in_specs,
        out_specs=out_specs,
        out_shape=out_shape,
        scratch_shapes=[
            pltpu.VMEM((n, GLA_W), F32), pltpu.VMEM((n, GLA_W), F32),
            pltpu.VMEM((2, n, GLA_W), F32),
            pltpu.VMEM((2, n, GLA_W), BF16),
            pltpu.VMEM((2 * n // CHUNK, GLA_W, GLA_W), F32),
            pltpu.VMEM((2 * n // CHUNK * 8, GLA_W), F32),
            pltpu.VMEM((2 * n // CHUNK, GLA_W, GLA_W), BF16),
            pltpu.VMEM((GLA_W, GLA_W), F32), pltpu.VMEM((GLA_W, GLA_W), F32),
            pltpu.VMEM((CHUNK, GLA_W), F32), pltpu.VMEM((CHUNK, GLA_W), F32),
        ],
        compiler_params=_cparams(("parallel",)),
        name="gla_state" if has_state else "gla_ctx",
    )(*args)


CONV_PAD = 16
CONV_ROWS = 128
CONV_SHIFT_ROWS = 24


def _conv_kernel(u_ref, w_ref, b_ref, lnw_ref, lnb_ref, o_ref, zp_s, zsh_s, *, n):
    u = u_ref[...]
    zp_s[0:CONV_PAD, :] = jnp.zeros((CONV_PAD, CONV_W), F32)
    zp_s[CONV_PAD + n:2 * CONV_PAD + n, :] = jnp.zeros((CONV_PAD, CONV_W), F32)
    zp_s[CONV_PAD:CONV_PAD + n, :] = u[:, 0:CONV_W] * _sigmoid(u[:, CONV_W:2 * CONV_W])
    m = n + CONV_SHIFT_ROWS
    for s in range(1, 8):
        zsh_s[s - 1] = zp_s[s:s + m, :]
    off = CONV_PAD - CONV_K // 2

    def tap(r0, j):
        s = (off + j) % 8
        base = r0 + off + j - s
        return zp_s[base:base + CONV_ROWS, :] if s == 0 else zsh_s[s - 1, base:base + CONV_ROWS, :]

    for r0 in range(0, n, CONV_ROWS):
        acc = tap(r0, 0) * w_ref[0:1, :]
        for j in range(1, CONV_K):
            acc = acc + tap(r0, j) * w_ref[j:j + 1, :]
        acc = acc + b_ref[...]
        mu = jnp.mean(acc, axis=-1, keepdims=True)
        xc = acc - mu
        var = jnp.mean(xc * xc, axis=-1, keepdims=True)
        y = xc * lax.rsqrt(var + NORM_EPS) * lnw_ref[...] + lnb_ref[...]
        o_ref[r0:r0 + CONV_ROWS, :] = y * _sigmoid(y)


def _conv(p, cw, cb, lnw, lnb, n, nb, row_blk0):
    full = lambda shape: pl.BlockSpec(shape, lambda b: (0,) * len(shape))
    return pl.pallas_call(
        functools.partial(_conv_kernel, n=n),
        grid=(nb,),
        in_specs=[
            pl.BlockSpec((n, 2 * CONV_W), lambda b: (row_blk0 + b, 2)),
            full((32, CONV_W)), full((1, CONV_W)), full((1, CONV_W)), full((1, CONV_W)),
        ],
        out_specs=pl.BlockSpec((n, CONV_W), lambda b: (b, 0)),
        out_shape=jax.ShapeDtypeStruct((nb * n, CONV_W), F32),
        scratch_shapes=[pltpu.VMEM((n + 2 * CONV_PAD, CONV_W), F32),
                        pltpu.VMEM((7, n + CONV_SHIFT_ROWS, CONV_W), F32)],
        compiler_params=_cparams(("parallel",)),
        name="conv%d" % n,
    )(p, cw, cb, lnw, lnb)


HEAD_BLK = 128


def _kv_expand(ckvn, t, wk_ref, wp_ref, wv_ref, k_ref, v_ref):
    cb = ckvn.astype(BF16)
    th, tl = _split2(t)
    k_ref[...] = (_dot(cb, wk_ref[...]) + _dot(th, wp_ref[...]) + _dot(tl, wp_ref[...])).astype(BF16)
    v_ref[...] = _dot(cb, wv_ref[...]).astype(BF16)


def _mla_prep_kernel(cq_ref, cm_ref, tqc_ref, tqs_ref, tk_ref, qnw_ref, kvw_ref, wa_ref, wb_ref,
                     wk_ref, wp_ref, wv_ref, q_ref, k_ref, v_ref, ckv_ref, kr_ref):
    qn = _rms(cq_ref[...], qnw_ref[...]).astype(BF16)
    tqc = jnp.concatenate([tqc_ref[...]] * MLA_HEADS, axis=1)
    tqs = jnp.concatenate([tqs_ref[...]] * MLA_HEADS, axis=1)
    scale = (MLA_NOPE + MLA_ROPE) ** -0.5
    q_ref[...] = ((_dot(qn, wa_ref[...]) * tqc + _dot(qn, wb_ref[...]) * tqs) * scale).astype(BF16)
    cm = cm_ref[...]
    ckvn = _rms(cm[:, 0:MLA_KV_RANK], kvw_ref[...])
    ckv_ref[...] = ckvn
    t = cm[:, MLA_KV_RANK:2 * MLA_KV_RANK] * tk_ref[...]
    kr_ref[...] = t
    _kv_expand(ckvn, t, wk_ref, wp_ref, wv_ref, k_ref, v_ref)


def _rope_tile(i):
    nctx = N_CTX // TM
    return jnp.where(i < nctx, 0, 1 + (i - nctx) % (DEC_SEQ // TM))


def _mla_prep(p, tabs, qnw, kvw, wa, wb, wk, wp, wv):
    tqc, tqs, tk = tabs
    full = lambda shape: pl.BlockSpec(shape, lambda i: (0,) * len(shape))
    tab_spec = pl.BlockSpec((TM, HEAD_BLK), lambda i: (_rope_tile(i), 0))
    hw = MLA_HEADS * HEAD_BLK
    return pl.pallas_call(
        _mla_prep_kernel,
        grid=(N_TOK // TM,),
        in_specs=[
            pl.BlockSpec((TM, MLA_Q_RANK), lambda i: (i, 6)),
            pl.BlockSpec((TM, 2 * MLA_KV_RANK), lambda i: (i, 7)),
            tab_spec, tab_spec, tab_spec,
            full((1, MLA_Q_RANK)), full((1, MLA_KV_RANK)),
            full((MLA_Q_RANK, hw)), full((MLA_Q_RANK, hw)),
            full((MLA_KV_RANK, hw)), full((128, hw)), full((MLA_KV_RANK, MLA_HEADS * MLA_V)),
        ],
        out_specs=[
            pl.BlockSpec((TM, hw), lambda i: (i, 0)),
            pl.BlockSpec((TM, hw), lambda i: (i, 0)),
            pl.BlockSpec((TM, MLA_HEADS * MLA_V), lambda i: (i, 0)),
            pl.BlockSpec((TM, MLA_KV_RANK), lambda i: (i, 0)),
            pl.BlockSpec((TM, 128), lambda i: (i, 0)),
        ],
        out_shape=[
            jax.ShapeDtypeStruct((N_TOK, hw), BF16),
            jax.ShapeDtypeStruct((N_TOK, hw), BF16),
            jax.ShapeDtypeStruct((N_TOK, MLA_HEADS * MLA_V), BF16),
            jax.ShapeDtypeStruct((N_TOK, MLA_KV_RANK), F32),
            jax.ShapeDtypeStruct((N_TOK, 128), F32),
        ],
        compiler_params=_cparams(("parallel",)),
        name="mla_prep",
    )(p, p, tqc, tqs, tk, qnw, kvw, wa, wb, wk, wp, wv)


def _cache_kv_kernel(ckv_ref, kr_ref, wk_ref, wp_ref, wv_ref, k_ref, v_ref):
    _kv_expand(ckv_ref[...], kr_ref[...], wk_ref, wp_ref, wv_ref, k_ref, v_ref)


def _cache_kv(ckv, kr, wk, wp, wv):
    n = DEC_BATCH * PAST_LEN
    full = lambda shape: pl.BlockSpec(shape, lambda i: (0,) * len(shape))
    hw = MLA_HEADS * HEAD_BLK
    return pl.pallas_call(
        _cache_kv_kernel,
        grid=(n // PAST_LEN,),
        in_specs=[
            pl.BlockSpec((PAST_LEN, MLA_KV_RANK), lambda i: (i, 0)),
            pl.BlockSpec((PAST_LEN, 128), lambda i: (i, 0)),
            full((MLA_KV_RANK, hw)), full((128, hw)), full((MLA_KV_RANK, MLA_HEADS * MLA_V)),
        ],
        out_specs=[
            pl.BlockSpec((PAST_LEN, hw), lambda i: (i, 0)),
            pl.BlockSpec((PAST_LEN, MLA_HEADS * MLA_V), lambda i: (i, 0)),
        ],
        out_shape=[
            jax.ShapeDtypeStruct((n, hw), BF16),
            jax.ShapeDtypeStruct((n, MLA_HEADS * MLA_V), BF16),
        ],
        compiler_params=_cparams(("parallel",)),
        name="cache_kv",
    )(ckv, kr, wk, wp, wv)


def _attn_kernel(*refs, has_cache):
    if has_cache:
        q_ref, k_ref, v_ref, kc_ref, vc_ref, o_ref = refs
    else:
        q_ref, k_ref, v_ref, o_ref = refs
    nh = q_ref.shape[1] // HEAD_BLK
    cols = [slice(h * HEAD_BLK, (h + 1) * HEAD_BLK) for h in range(nh)]
    vcols = [slice((h // 2) * 2 * MLA_V, (h // 2 + 1) * 2 * MLA_V) for h in range(nh)]
    s = [_dot_nt(q_ref[:, c], k_ref[:, c]) for c in cols]
    m = [jnp.max(x, axis=-1, keepdims=True) for x in s]
    if has_cache:
        sc = [_dot_nt(q_ref[:, c], kc_ref[:, c]) for c in cols]
        m = [jnp.maximum(a, jnp.max(x, axis=-1, keepdims=True)) for a, x in zip(m, sc)]
        pc = [jnp.exp(x - a) for x, a in zip(sc, m)]
    p = [jnp.exp(x - a) for x, a in zip(s, m)]
    den = [jnp.sum(x, axis=-1, keepdims=True) for x in p]
    o = [_dot(x.astype(BF16), v_ref[:, c]) for x, c in zip(p, vcols)]
    if has_cache:
        den = [a + jnp.sum(x, axis=-1, keepdims=True) for a, x in zip(den, pc)]
        o = [a + _dot(x.astype(BF16), vc_ref[:, c]) for a, x, c in zip(o, pc, vcols)]
    outs = [a / b for a, b in zip(o, den)]
    lane = lax.broadcasted_iota(jnp.int32, (1, 2 * MLA_V), 1)
    for j in range(nh // 2):
        o_ref[:, j * 2 * MLA_V:(j + 1) * 2 * MLA_V] = jnp.where(lane < MLA_V, outs[2 * j], outs[2 * j + 1])


def _attention(q, k, v, n, nb, row0, tq, npair, kc=None, vc=None):
    has_cache = kc is not None
    nq = n // tq
    rb0 = row0 // tq
    kb0 = row0 // n
    in_specs = [
        pl.BlockSpec((tq, npair * 2 * HEAD_BLK), lambda b, h, i: (rb0 + b * nq + i, h)),
        pl.BlockSpec((n, npair * 2 * HEAD_BLK), lambda b, h, i: (kb0 + b, h)),
        pl.BlockSpec((n, npair * 2 * MLA_V), lambda b, h, i: (kb0 + b, h)),
    ]
    args = [q, k, v]
    if has_cache:
        in_specs += [
            pl.BlockSpec((PAST_LEN, npair * 2 * HEAD_BLK), lambda b, h, i: (b, h)),
            pl.BlockSpec((PAST_LEN, npair * 2 * MLA_V), lambda b, h, i: (b, h)),
        ]
        args += [kc, vc]
    return pl.pallas_call(
        functools.partial(_attn_kernel, has_cache=has_cache),
        grid=(nb, MLA_HEADS // (2 * npair), nq),
        in_specs=in_specs,
        out_specs=pl.BlockSpec((tq, npair * 2 * MLA_V), lambda b, h, i: (b * nq + i, h)),
        out_shape=jax.ShapeDtypeStruct((nb * n, MLA_HEADS * MLA_V), F32),
        compiler_params=_cparams(("parallel", "parallel", "parallel")),
        name="attn_cache" if has_cache else "attn_ctx",
    )(*args)


def _outproj_kernel(xa_ref, xb_ref, oga_ref, ogb_ref, oca_ref, ocb_ref, oma_ref, omb_ref, mod_ref, nw_ref,
                    wo_ref, wr_ref, br_ref, xo_ref, h_ref, route_ref, cnt_ref, carry_s, wrh_s, wrl_s):
    @pl.when(pl.program_id(0) == 0)
    def _():
        carry_s[...] = jnp.zeros(carry_s.shape, F32)
        wrh, wrl = _split2(wr_ref[...])
        wrh_s[...] = wrh
        wrl_s[...] = wrl

    mod = mod_ref[...]

    def residual(x_ref, og_ref, oc_ref, om_ref):
        attn = (_dot(og_ref[...].astype(BF16), wo_ref[0:GLA_W, :])
                + _dot(oc_ref[...].astype(BF16), wo_ref[GLA_W:GLA_W + CONV_W, :])
                + _dot(om_ref[...].astype(BF16), wo_ref[GLA_W + CONV_W:D_MODEL, :]))
        xo_ref[...] = x_ref[...] + mod[:, 2 * D_MODEL:3 * D_MODEL] * attn

    is_ctx = pl.program_id(0) < N_CTX // TM
    pl.when(is_ctx)(lambda: residual(xa_ref, oga_ref, oca_ref, oma_ref))
    pl.when(jnp.logical_not(is_ctx))(lambda: residual(xb_ref, ogb_ref, ocb_ref, omb_ref))
    xn = xo_ref[...]
    h = _rms(xn, nw_ref[...]) * (1.0 + mod[:, 4 * D_MODEL:5 * D_MODEL]) + mod[:, 3 * D_MODEL:4 * D_MODEL]
    _rows_to_tiles(h_ref, h)
    hh, hl = _split2(h)
    logits = _dot(hh, wrh_s[...]) + _dot(hl, wrh_s[...]) + _dot(hh, wrl_s[...]) + br_ref[...]
    lane = lax.broadcasted_iota(jnp.int32, logits.shape, 1).astype(F32)
    big = 1e9
    gl = jnp.where((lane >= N_EXPERTS) & (lane < N_EXPERTS + N_GROUPS), logits, NEG)
    gmax = jnp.max(gl, axis=-1, keepdims=True)
    gw = 1.0 / jnp.sum(jnp.exp(gl - gmax), axis=-1, keepdims=True)
    gi = jnp.min(jnp.where(gl == gmax, lane, big), axis=-1, keepdims=True) - N_EXPERTS
    lo = gi * EXPERTS_PER_GROUP
    el = jnp.where((lane >= lo) & (lane < lo + EXPERTS_PER_GROUP), logits, NEG)
    m1 = jnp.max(el, axis=-1, keepdims=True)
    i1 = jnp.min(jnp.where(el == m1, lane, big), axis=-1, keepdims=True)
    el2 = jnp.where(lane == i1, NEG, el)
    m2 = jnp.max(el2, axis=-1, keepdims=True)
    i2 = jnp.min(jnp.where(el2 == m2, lane, big), axis=-1, keepdims=True)
    e2 = jnp.exp(m2 - m1)
    p1 = 1.0 / (1.0 + e2)
    onehot = jnp.where((lane == i1) | (lane == i2), 1.0, 0.0)
    tr = lax.broadcasted_iota(jnp.int32, (TM, TM), 0)
    tc = lax.broadcasted_iota(jnp.int32, (TM, TM), 1)
    before = (tc < tr).astype(F32).astype(BF16)
    seen = _dot(before, onehot.astype(BF16)) + carry_s[0:1, :]
    rank1 = jnp.sum(jnp.where(lane == i1, seen, 0.0), axis=-1, keepdims=True)
    rank2 = jnp.sum(jnp.where(lane == i2, seen, 0.0), axis=-1, keepdims=True)
    carry_s[...] = carry_s[...] + jnp.sum(onehot, axis=0, keepdims=True)
    cnt_ref[...] = carry_s[...]
    cols = (i1, i2, gw * p1, gw * (e2 * p1), rank1, rank2)
    route = jnp.zeros(logits.shape, F32)
    for j, col in enumerate(cols):
        route = jnp.where(lane == j, col, route)
    route_ref[...] = route


def _outproj(x, og, oc, om, mods4, l, norm_w, wo, wr, br):
    full = lambda shape: pl.BlockSpec(shape, lambda i: (0,) * len(shape))
    row = lambda w: pl.BlockSpec((TM, w), lambda i: (i, 0))
    return pl.pallas_call(
        _outproj_kernel,
        grid=(N_TOK // TM,),
        in_specs=(_pair_specs(TM, D_MODEL, isinstance(x, tuple)) + _pair_specs(TM, GLA_W, True)
                  + _pair_specs(TM, CONV_W, True) + _pair_specs(TM, MLA_HEADS * MLA_V, True) + [
            pl.BlockSpec((None, None, 1, 6 * D_MODEL), lambda i: (l, _mod_index(i, TM), 0, 0)),
            full((1, D_MODEL)), full((D_MODEL, D_MODEL)), full((D_MODEL, 128)), full((1, 128)),
        ]),
        out_specs=[row(D_MODEL), pl.BlockSpec(_tiled(TM), lambda i: (i, 0)), row(128), full((8, 128))],
        out_shape=[
            jax.ShapeDtypeStruct((N_TOK, D_MODEL), F32),
            jax.ShapeDtypeStruct(_tiled(N_TOK), F32),
            jax.ShapeDtypeStruct((N_TOK, 128), F32),
            jax.ShapeDtypeStruct((8, 128), F32),
        ],
        scratch_shapes=[pltpu.VMEM((8, 128), F32), pltpu.VMEM((D_MODEL, 128), BF16),
                        pltpu.VMEM((D_MODEL, 128), BF16)],
        compiler_params=_cparams(("arbitrary",)),
        name="outproj_router",
    )(*_pair(x), *og, *oc, *om, mods4, norm_w, wo, wr, br)


N_ASSIGN = 2 * N_TOK
TE = 256
N_ETILE = N_ASSIGN // TE
N_WORK = N_ETILE + N_EXPERTS


ROW_SUB = 8


def _tiled(n):
    return (n * ROW_SUB, D_MODEL // ROW_SUB)


def _rows_to_tiles(ref, x, r0=0):
    n = x.shape[0]
    for s in range(ROW_SUB):
        ref[pl.ds(r0 * ROW_SUB + s, n, stride=ROW_SUB), :] = x[:, s * 128:(s + 1) * 128]


def _tiles_to_rows(ref, r0=0, n=None):
    n = ref.shape[0] // ROW_SUB if n is None else n
    return jnp.concatenate([ref[pl.ds(r0 * ROW_SUB + s, n, stride=ROW_SUB), :] for s in range(ROW_SUB)], axis=1)


def _row_tile(ref, t):
    start = t * ROW_SUB if isinstance(t, int) else pl.multiple_of(t * ROW_SUB, ROW_SUB)
    return ref.at[pl.ds(start, ROW_SUB)]


def _lane_col(x, lane, j):
    return jnp.sum(jnp.where(lane == j, x, 0.0), axis=-1, keepdims=True)


def _plan_kernel(cnt_s, cnt_ref, route_ref, pos_ref, wt_ref, we_ref, wlo_ref, whi_ref, wf_ref, wr_ref, wn_ref,
                 off_s):
    cnt = cnt_ref[...]
    chi = jnp.floor(cnt * (1.0 / 128.0))
    clo = cnt - chi * 128.0
    r = lax.broadcasted_iota(jnp.int32, (128, 128), 0)
    c = lax.broadcasted_iota(jnp.int32, (128, 128), 1)
    below = (r < c).astype(F32).astype(BF16)
    off = (128.0 * _dot(chi.astype(BF16), below) + _dot(clo.astype(BF16), below))[0:1, :]
    route = route_ref[...]
    lane = lax.broadcasted_iota(jnp.int32, route.shape, 1).astype(F32)
    pos1 = _lane_col(off, lane, _lane_col(route, lane, 0.0)) + _lane_col(route, lane, 4.0)
    pos2 = _lane_col(off, lane, _lane_col(route, lane, 1.0)) + _lane_col(route, lane, 5.0)
    pos_ref[...] = jnp.where(lane == 0.0, pos1, jnp.where(lane == 1.0, pos2, 0.0))

    @pl.when(pl.program_id(0) == 0)
    def _():
        def offs(e, acc):
            off_s[e] = acc
            return acc + cnt_s[e]

        off_s[N_EXPERTS] = lax.fori_loop(0, N_EXPERTS, offs, jnp.int32(0))

        def item(w, st):
            j, e, first = st
            active = j < N_ETILE
            jj = jnp.minimum(j, N_ETILE - 1)
            ee = jnp.minimum(e, N_EXPERTS - 1)
            tlo = jj * TE
            thi = tlo + TE
            end = off_s[ee + 1]
            wt_ref[w] = jj
            we_ref[w] = ee
            wlo_ref[w] = jnp.where(active, jnp.maximum(off_s[ee], tlo) - tlo, 0)
            whi_ref[w] = jnp.where(active, jnp.minimum(end, thi) - tlo, 0)
            wf_ref[w] = jnp.where(active, first, 0)
            adv_j = (active & (end >= thi)).astype(jnp.int32)
            adv_e = (active & (end <= thi)).astype(jnp.int32)
            return j + adv_j, e + adv_e, adv_j

        lax.fori_loop(0, N_WORK, item, (jnp.int32(0), jnp.int32(0), jnp.int32(1)))

        def run(w, r):
            r = r + (we_ref[w] != we_ref[jnp.maximum(w - 1, 0)]).astype(jnp.int32)
            wr_ref[w] = r
            return r

        lax.fori_loop(0, N_WORK, run, jnp.int32(0))

        def following(j, nxt):
            w = N_WORK - 1 - j
            wn_ref[w] = nxt
            return jnp.where(we_ref[jnp.maximum(w - 1, 0)] != we_ref[w], we_ref[w], nxt)

        lax.fori_loop(0, N_WORK, following, jnp.int32(-1))


def _plan(cnt_i, cnt, route):
    smem = pl.BlockSpec(memory_space=pltpu.SMEM)
    work = jax.ShapeDtypeStruct((N_WORK,), jnp.int32)
    return pl.pallas_call(
        _plan_kernel,
        grid=(1,),
        in_specs=[smem, pl.BlockSpec((8, 128), lambda i: (0, 0)), pl.BlockSpec((N_TOK, 128), lambda i: (0, 0))],
        out_specs=[pl.BlockSpec((N_TOK, 128), lambda i: (0, 0))] + [smem] * 7,
        out_shape=[jax.ShapeDtypeStruct((N_TOK, 128), F32)] + [work] * 7,
        scratch_shapes=[pltpu.SMEM((N_EXPERTS + 1,), jnp.int32)],
        compiler_params=_cparams(("arbitrary",)),
        name="moe_plan",
    )(cnt_i, cnt, route)


def _dispatch_kernel(pos_s, h_ref, xs_hbm, sem):
    base = pl.program_id(0) * TM

    def copy(i, k):
        return pltpu.make_async_copy(_row_tile(h_ref, i), _row_tile(xs_hbm, pos_s[2 * (base + i) + k]), sem)

    def wait(i, carry):
        copy(i, 0).wait()
        copy(i, 1).wait()
        return carry

    for i in range(TM):
        copy(i, 0).start(priority=0)
        copy(i, 1).start(priority=1)
    lax.fori_loop(0, TM, wait, 0, unroll=8)


def _dispatch(pos, h):
    any_spec = pl.BlockSpec(memory_space=pl.ANY)
    return pl.pallas_call(
        _dispatch_kernel,
        grid_spec=pltpu.PrefetchScalarGridSpec(
            num_scalar_prefetch=1, grid=(N_TOK // TM,),
            in_specs=[pl.BlockSpec(_tiled(TM), lambda i, p: (i, 0))], out_specs=any_spec,
            scratch_shapes=[pltpu.SemaphoreType.DMA(())]),
        out_shape=jax.ShapeDtypeStruct(_tiled(N_ASSIGN), F32),
        compiler_params=_cparams(("arbitrary",)),
        name="moe_dispatch",
    )(pos, h)


def _experts_kernel(wt_s, we_s, wlo_s, whi_s, wf_s, wr_s, wn_s, xs_ref, w1_hbm, w3_hbm, w2_hbm, o_ref,
                    w1_s, w3_s, w2_s, f1_s, f3_s, f2_s, sem, *, layer):
    w = pl.program_id(0)
    first = wf_s[w] == 1
    nonempty = whi_s[w] > wlo_s[w]

    def fetch(e, slot):
        return [pltpu.make_async_copy(src.at[layer, e], dst.at[slot], sem.at[slot])
                for src, dst in ((w1_hbm, f1_s), (w3_hbm, f3_s), (w2_hbm, f2_s))]

    @pl.when(w == 0)
    def _():
        for c in fetch(we_s[0], wr_s[0] % 2):
            c.start()

    @pl.when((w == 0) | (we_s[w] != we_s[jnp.maximum(w - 1, 0)]))
    def _():
        slot = wr_s[w] % 2
        for c in fetch(we_s[w], slot):
            c.wait()
        w1_s[...] = f1_s[slot].astype(BF16)
        w3_s[...] = f3_s[slot].astype(BF16)
        w2_s[...] = f2_s[slot].astype(BF16)

        @pl.when(wn_s[w] >= 0)
        def _():
            for c in fetch(wn_s[w], 1 - slot):
                c.start()

    @pl.when(first & jnp.logical_not(nonempty))
    def _():
        o_ref[...] = jnp.zeros(o_ref.shape, F32)

    nsplit = 2
    hrows = TE // nsplit

    def expert_rows(merge):
        x = [_tiles_to_rows(xs_ref, j * hrows, hrows).astype(BF16) for j in range(nsplit)]
        a = [_dot(v, w1_s[...]) for v in x]
        b = [_dot(v, w3_s[...]) for v in x]
        hid = [(u * _sigmoid(u) * v).astype(BF16) for u, v in zip(a, b)]
        y = [_dot(v, w2_s[...]) for v in hid]
        for j in range(nsplit):
            row = lax.broadcasted_iota(jnp.int32, (hrows, 1), 0) + j * hrows
            mine = (row >= wlo_s[w]) & (row < whi_s[w])
            other = _tiles_to_rows(o_ref, j * hrows, hrows) if merge else 0.0
            _rows_to_tiles(o_ref, jnp.where(mine, y[j], other), j * hrows)

    @pl.when(nonempty & first)
    def _():
        expert_rows(False)

    @pl.when(nonempty & jnp.logical_not(first))
    def _():
        expert_rows(True)


def _experts(work, xs, l, w1, w3, w2):
    tile = pl.BlockSpec(_tiled(TE), lambda w, wt, *_: (wt[w], 0))
    any_spec = pl.BlockSpec(memory_space=pl.ANY)
    return pl.pallas_call(
        functools.partial(_experts_kernel, layer=l),
        grid_spec=pltpu.PrefetchScalarGridSpec(
            num_scalar_prefetch=7, grid=(N_WORK,),
            in_specs=[tile, any_spec, any_spec, any_spec],
            out_specs=tile,
            scratch_shapes=[pltpu.VMEM((D_MODEL, EXPERT_FF), BF16), pltpu.VMEM((D_MODEL, EXPERT_FF), BF16),
                            pltpu.VMEM((EXPERT_FF, D_MODEL), BF16),
                            pltpu.VMEM((2, D_MODEL, EXPERT_FF), F32), pltpu.VMEM((2, D_MODEL, EXPERT_FF), F32),
                            pltpu.VMEM((2, EXPERT_FF, D_MODEL), F32), pltpu.SemaphoreType.DMA((2,))]),
        out_shape=jax.ShapeDtypeStruct(_tiled(N_ASSIGN), F32),
        compiler_params=_cparams(("arbitrary",)),
        name="moe_experts",
    )(*work, xs, w1, w3, w2)


def _combine_kernel(pos_s, x_ref, route_ref, mod_ref, fw_ref, ys_hbm, *refs, final):
    if final:
        oa_ref, ob_ref, y_s, sem = refs
    else:
        o_ref, y_s, sem = refs
    step = pl.program_id(0)
    nsteps = pl.num_programs(0)

    def copy(t, i, k):
        slot = t % 2
        src = _row_tile(ys_hbm, pos_s[2 * (t * TM + i) + k])
        return pltpu.make_async_copy(src, _row_tile(y_s.at[slot, k], i), sem.at[slot])

    def gather(t):
        for i in range(TM):
            copy(t, i, 0).start(priority=0)
            copy(t, i, 1).start(priority=1)

    def wait(i, carry):
        copy(step, i, 0).wait()
        copy(step, i, 1).wait()
        return carry

    pl.when(step == 0)(lambda: gather(step))
    pl.when(step + 1 < nsteps)(lambda: gather(step + 1))
    lax.fori_loop(0, TM, wait, 0, unroll=8)
    route = route_ref[...]
    lane = lax.broadcasted_iota(jnp.int32, route.shape, 1)
    slot = step % 2
    moe = (_lane_col(route, lane, 2) * _tiles_to_rows(y_s.at[slot, 0])
           + _lane_col(route, lane, 3) * _tiles_to_rows(y_s.at[slot, 1]))
    x = x_ref[...] + mod_ref[:, 5 * D_MODEL:6 * D_MODEL] * moe
    if final:
        y = _rms(x, fw_ref[...])
        is_ctx = pl.program_id(0) < N_CTX // TM

        @pl.when(is_ctx)
        def _():
            oa_ref[...] = y

        @pl.when(jnp.logical_not(is_ctx))
        def _():
            ob_ref[...] = y
    else:
        o_ref[...] = x


def _combine(pos, x, route, mods4, l, fw, ys, final):
    row = lambda w: pl.BlockSpec((TM, w), lambda i, p: (i, 0))
    if final:
        out_specs = _pair_specs(TM, D_MODEL, True)
        out_shape = [jax.ShapeDtypeStruct((N_CTX, D_MODEL), F32), jax.ShapeDtypeStruct((N_SMP, D_MODEL), F32)]
    else:
        out_specs = row(D_MODEL)
        out_shape = jax.ShapeDtypeStruct((N_TOK, D_MODEL), F32)
    return pl.pallas_call(
        functools.partial(_combine_kernel, final=final),
        grid_spec=pltpu.PrefetchScalarGridSpec(
            num_scalar_prefetch=1, grid=(N_TOK // TM,),
            in_specs=[row(D_MODEL), row(128),
                      pl.BlockSpec((None, None, 1, 6 * D_MODEL), lambda i, p: (l, _mod_index(i, TM), 0, 0)),
                      pl.BlockSpec((1, D_MODEL), lambda i, p: (0, 0)),
                      pl.BlockSpec(memory_space=pl.ANY)],
            out_specs=out_specs,
            scratch_shapes=[pltpu.VMEM((2, 2) + _tiled(TM), F32), pltpu.SemaphoreType.DMA((2,))]),
        out_shape=out_shape,
        compiler_params=_cparams(("arbitrary",)),
        name="moe_combine_final" if final else "moe_combine",
    )(pos, x, route, mods4, fw, ys)


def _rope_swap(w):
    nf = MLA_ROPE // 4
    g = w.reshape(w.shape[:-1] + (2, 2, nf))
    return jnp.stack([-g[..., 1, :], g[..., 0, :]], axis=-2).reshape(w.shape)


def _rope_tables():
    nf = MLA_ROPE // 4
    pos = np.arange(DEC_SEQ)
    row = (pos // GRID_W).astype(np.float32)
    col = (pos % GRID_W).astype(np.float32)
    inv = (np.float32(ROPE_THETA) ** (-np.arange(nf, dtype=np.float32) / np.float32(nf))).astype(np.float32)
    ar = (row[:, None] * inv).astype(np.float32)
    ac = (col[:, None] * inv).astype(np.float32)
    cos = np.concatenate([np.cos(ar), np.cos(ar), np.cos(ac), np.cos(ac)], axis=1)
    sin = np.concatenate([np.sin(ar), np.sin(ar), np.sin(ac), np.sin(ac)], axis=1)
    cos = np.concatenate([np.ones((TM, MLA_ROPE)), cos], axis=0).astype(np.float32)
    sin = np.concatenate([np.zeros((TM, MLA_ROPE)), sin], axis=0).astype(np.float32)
    n = cos.shape[0]
    z32 = np.zeros((n, 32), np.float32)
    tqc = np.concatenate([np.ones((n, MLA_NOPE), np.float32), cos, z32], axis=1)
    tqs = np.concatenate([np.zeros((n, MLA_NOPE), np.float32), sin, z32], axis=1)
    tk = np.concatenate([z32, cos, sin, z32], axis=1)
    return jnp.asarray(tqc), jnp.asarray(tqs), jnp.asarray(tk)


def _prep_layer(l, w_in, gla_wg_f, gla_wg_b, mla_wuq, mla_wukv, w_out, moe_wg, moe_bg, moe_we, moe_be):
    offs = np.cumsum([0, 256, 256, 256, 256, 16, 16, 512, 256, 128, 32])
    seg = lambda i: w_in[l][:, offs[i]:offs[i + 1]]
    kr = seg(9)
    w_in_r = jnp.concatenate(
        [seg(0), seg(1), seg(2), seg(3), seg(6), seg(7), seg(8), seg(4), seg(5), kr, _rope_swap(kr),
         jnp.zeros((D_MODEL, 32), F32)], axis=1).astype(BF16)
    wgf = jnp.zeros((128, GLA_W), F32).at[0:16].set(gla_wg_f[l])
    wgb = jnp.zeros((128, GLA_W), F32).at[16:32].set(gla_wg_b[l])
    wq = mla_wuq[l].reshape(MLA_Q_RANK, MLA_HEADS, MLA_NOPE + MLA_ROPE)
    nope, rope = wq[..., :MLA_NOPE], wq[..., MLA_NOPE:]
    z = lambda n: jnp.zeros((MLA_Q_RANK, MLA_HEADS, n), F32)
    hw = MLA_HEADS * HEAD_BLK
    wa = jnp.concatenate([nope, rope, z(32)], axis=-1).reshape(MLA_Q_RANK, hw).astype(BF16)
    wb = jnp.concatenate([z(MLA_NOPE), _rope_swap(rope), z(32)], axis=-1).reshape(MLA_Q_RANK, hw).astype(BF16)
    wkv = mla_wukv[l].reshape(MLA_KV_RANK, MLA_HEADS, MLA_NOPE + MLA_V)
    wk = jnp.concatenate([wkv[..., :MLA_NOPE], jnp.zeros((MLA_KV_RANK, MLA_HEADS, 64), F32)],
                         axis=-1).reshape(MLA_KV_RANK, hw).astype(BF16)
    wv = wkv[..., MLA_NOPE:].reshape(MLA_KV_RANK, MLA_HEADS * MLA_V).astype(BF16)
    wp_np = np.zeros((128, MLA_HEADS, HEAD_BLK), np.float32)
    for j in range(MLA_ROPE):
        wp_np[32 + j, :, MLA_NOPE + j] = 1.0
        wp_np[64 + j, :, MLA_NOPE + j] = 1.0
    wp = jnp.asarray(wp_np.reshape(128, hw)).astype(BF16)
    wo = w_out[l].astype(BF16)
    wr = jnp.concatenate([moe_we[l], moe_wg[l], jnp.zeros((D_MODEL, 128 - N_EXPERTS - N_GROUPS), F32)], axis=1)
    br = jnp.concatenate([moe_be[l], moe_bg[l], jnp.zeros((128 - N_EXPERTS - N_GROUPS,), F32)]).reshape(1, 128)
    return dict(w_in_r=w_in_r, wgf=wgf, wgb=wgb, wa=wa, wb=wb, wk=wk, wv=wv, wp=wp, wo=wo, wr=wr, br=br)


def kernel(x_prompt, x_sample, cache_ckv, cache_krope, state_gla_fwd, state_gla_bwd, c, c_ctx, ada_w, ada_b,
           norm1_w, norm2_w, w_in, gla_wg_f, gla_bg_f, gla_wg_b, gla_bg_b, gla_norm_w, conv_w, conv_b,
           conv_ln_w, conv_ln_b, mla_qnorm_w, mla_wuq, mla_kvnorm_w, mla_wukv, w_out, moe_wg, moe_bg, moe_we,
           moe_be, moe_w1, moe_w3, moe_w2, final_norm_w):
    x = (x_prompt.reshape(N_CTX, D_MODEL), x_sample.reshape(N_SMP, D_MODEL))
    cond8 = jnp.concatenate([c_ctx[None, :], c, jnp.zeros((N_MOD - 1 - DEC_BATCH, D_MODEL), F32)], axis=0)
    mods4 = _ada_table(cond8, ada_w, ada_b).reshape(DEPTH, N_MOD, 1, 6 * D_MODEL)
    tabs = _rope_tables()
    row1 = lambda a: a.reshape(1, -1)

    ckv_new, kr_new, sf_new, sb_new = [], [], [], []
    for l in range(DEPTH):
        w = _prep_layer(l, w_in, gla_wg_f, gla_wg_b, mla_wuq, mla_wukv, w_out, moe_wg, moe_bg, moe_we, moe_be)
        p = _inproj(x, mods4, l, row1(norm1_w[l]), w["w_in_r"])

        gla_args = (p, w["wgf"], w["wgb"], row1(gla_bg_f[l]), row1(gla_bg_b[l]), row1(gla_norm_w[l]))
        og_c, sf, sb = _gla(*gla_args, SEQ, BATCH, 0)
        og_s = _gla(*gla_args, DEC_SEQ, DEC_BATCH, N_CTX // DEC_SEQ, (state_gla_fwd, l), (state_gla_bwd, l))[0]
        og = (og_c, og_s)

        cw = jnp.concatenate([conv_w[l], jnp.zeros((1, CONV_W), F32)], axis=0)
        conv_args = (p, cw, row1(conv_b[l]), row1(conv_ln_w[l]), row1(conv_ln_b[l]))
        oc = (_conv(*conv_args, SEQ, BATCH, 0), _conv(*conv_args, DEC_SEQ, DEC_BATCH, N_CTX // DEC_SEQ))

        q, k, v, ckvn, kr = _mla_prep(p, tabs, row1(mla_qnorm_w[l]), row1(mla_kvnorm_w[l]),
                                      w["wa"], w["wb"], w["wk"], w["wp"], w["wv"])
        cc = cache_ckv[:, l].reshape(DEC_BATCH * PAST_LEN, MLA_KV_RANK)
        ckr = cache_krope[:, l].reshape(DEC_BATCH * PAST_LEN, MLA_ROPE)
        ckr = jnp.pad(ckr, ((0, 0), (32, 64)))
        kc, vc = _cache_kv(cc, ckr, w["wk"], w["wp"], w["wv"])
        om = (_attention(q, k, v, SEQ, BATCH, 0, SEQ, MLA_HEADS // 2),
              _attention(q, k, v, DEC_SEQ, DEC_BATCH, N_CTX, 512, 1, kc, vc))

        xn, h2, route, cnt = _outproj(x, og, oc, om, mods4, l, row1(norm2_w[l]), w["wo"], w["wr"], w["br"])
        posf, *work = _plan(cnt[0, :N_EXPERTS].astype(jnp.int32), cnt, route)
        pos = posf[:, 0:2].astype(jnp.int32).reshape(N_ASSIGN)
        ys = _experts(work, _dispatch(pos, h2), l, moe_w1, moe_w3, moe_w2)
        x = _combine(pos, xn, route, mods4, l, row1(final_norm_w), ys, l == DEPTH - 1)

        ckv_new.append(ckvn[:N_CTX].reshape(BATCH, SEQ, MLA_KV_RANK))
        kr_new.append(kr[:N_CTX, 32:32 + MLA_ROPE].reshape(BATCH, SEQ, MLA_ROPE))
        sf_new.append(sf)
        sb_new.append(sb)

    y_ctx, y_smp = x
    return (y_ctx.reshape(BATCH, SEQ, D_MODEL), y_smp.reshape(DEC_BATCH, DEC_SEQ, D_MODEL),
            jnp.stack(ckv_new, axis=1), jnp.stack(kr_new, axis=1),
            jnp.stack(sf_new, axis=1), jnp.stack(sb_new, axis=1))
```

```python
import functools

import numpy as np
import jax
import jax.numpy as jnp
from jax import lax
from jax.experimental import pallas as pl
from jax.experimental.pallas import tpu as pltpu

F32 = jnp.float32
BF16 = jnp.bfloat16

D_MODEL = 1024
BATCH = 16
SEQ = 256
DEPTH = 2
DEC_BATCH = 4
DEC_SEQ = 1024
PAST_LEN = 512
GRID_W = 64
NORM_EPS = 1e-6
GLA_HEADS = 4
GLA_DK = 64
GLA_DV = 64
GLA_W = 256
GLA_GATE_RANK = 16
GLA_TAU = 16.0
CONV_W = 256
CONV_K = 31
MLA_HEADS = 8
MLA_NOPE = 64
MLA_ROPE = 32
MLA_V = 64
MLA_Q_RANK = 256
MLA_KV_RANK = 128
ROPE_THETA = 10000.0
N_GROUPS = 4
EXPERTS_PER_GROUP = 8
N_EXPERTS = 32
EXPERT_FF = 256

N_CTX = BATCH * SEQ
N_SMP = DEC_BATCH * DEC_SEQ
N_TOK = N_CTX + N_SMP
N_MOD = 8
TM = 512
P_COLS = 2048
MISC_BLK = 15
CHUNK = 64
SUB = 16
NEG = -1e30
VMEM_LIMIT = 56 * 1024 * 1024


def _cparams(sem):
    return pltpu.CompilerParams(dimension_semantics=sem, vmem_limit_bytes=VMEM_LIMIT)


def _dot(a, b):
    return jnp.dot(a, b, preferred_element_type=F32)


def _dot_nt(a, b):
    return lax.dot_general(a, b, (((1,), (1,)), ((), ())), preferred_element_type=F32)


def _split2(x):
    hi = x.astype(BF16)
    lo = (x - hi.astype(F32)).astype(BF16)
    return hi, lo


def _dot3(a, w):
    ah, al = _split2(a)
    wh, wl = _split2(w)
    return _dot(ah, wh) + _dot(al, wh) + _dot(ah, wl)


def _sigmoid(x):
    return 1.0 / (1.0 + jnp.exp(-x))


def _rms(x, w):
    ms = jnp.mean(x * x, axis=-1, keepdims=True)
    return x * lax.rsqrt(ms + NORM_EPS) * w


def _mod_index(i, tile):
    nctx = N_CTX // tile
    per = DEC_SEQ // tile
    return jnp.where(i < nctx, 0, 1 + (i - nctx) // per)


def _ada_kernel(c_ref, w_ref, b_ref, o_ref):
    c = c_ref[...]
    o_ref[...] = _dot3(c * _sigmoid(c), w_ref[...]) + b_ref[...]


def _ada_table(cond8, ada_w, ada_b):
    tn = 1024
    n6 = 6 * D_MODEL
    return pl.pallas_call(
        _ada_kernel,
        grid=(DEPTH, n6 // tn),
        in_specs=[
            pl.BlockSpec((N_MOD, D_MODEL), lambda l, j: (0, 0)),
            pl.BlockSpec((None, D_MODEL, tn), lambda l, j: (l, 0, j)),
            pl.BlockSpec((None, 1, tn), lambda l, j: (l, 0, j)),
        ],
        out_specs=pl.BlockSpec((None, N_MOD, tn), lambda l, j: (l, 0, j)),
        out_shape=jax.ShapeDtypeStruct((DEPTH, N_MOD, n6), F32),
        compiler_params=_cparams(("parallel", "parallel")),
        name="ada_table",
    )(cond8, ada_w, ada_b.reshape(DEPTH, 1, n6))


def _pair_specs(tile, width, split):
    nctx = N_CTX // tile
    offb = 0 if split else nctx
    return [pl.BlockSpec((tile, width), lambda i, *_: (jnp.minimum(i, nctx - 1), 0)),
            pl.BlockSpec((tile, width), lambda i, *_: (jnp.maximum(i - nctx, 0) + offb, 0))]


def _pair(x):
    return x if isinstance(x, tuple) else (x, x)


TM_IN = 512


def _inproj_kernel(xa_ref, xb_ref, mod_ref, nw_ref, w_ref, o_ref):
    mod = mod_ref[...]

    def project(x_ref):
        h = _rms(x_ref[...], nw_ref[...]) * (1.0 + mod[:, D_MODEL:2 * D_MODEL]) + mod[:, 0:D_MODEL]
        o_ref[...] = _dot(h.astype(BF16), w_ref[...])

    is_ctx = pl.program_id(0) < N_CTX // TM_IN
    pl.when(is_ctx)(lambda: project(xa_ref))
    pl.when(jnp.logical_not(is_ctx))(lambda: project(xb_ref))


def _inproj(x, mods4, l, norm_w, w_in_r):
    return pl.pallas_call(
        _inproj_kernel,
        grid=(N_TOK // TM_IN,),
        in_specs=_pair_specs(TM_IN, D_MODEL, isinstance(x, tuple)) + [
            pl.BlockSpec((None, None, 1, 6 * D_MODEL), lambda i: (l, _mod_index(i, TM_IN), 0, 0)),
            pl.BlockSpec((1, D_MODEL), lambda i: (0, 0)),
            pl.BlockSpec((D_MODEL, P_COLS), lambda i: (0, 0)),
        ],
        out_specs=pl.BlockSpec((TM_IN, P_COLS), lambda i: (i, 0)),
        out_shape=jax.ShapeDtypeStruct((N_TOK, P_COLS), F32),
        compiler_params=_cparams(("parallel",)),
        name="inproj",
    )(*_pair(x), mods4, norm_w, w_in_r)


def _log_sigmoid(x):
    return jnp.minimum(x, 0.0) - jnp.log(1.0 + jnp.exp(-jnp.abs(x)))


def _split3(x):
    hi = x.astype(BF16)
    r = x - hi.astype(F32)
    mid = r.astype(BF16)
    lo = (r - mid.astype(F32)).astype(BF16)
    return hi, mid, lo


def _gla_consts(reverse):
    t = lax.broadcasted_iota(jnp.int32, (CHUNK, CHUNK), 0)
    s = lax.broadcasted_iota(jnp.int32, (CHUNK, CHUNK), 1)
    mid = jnp.bitwise_and(t, -SUB) + SUB // 2
    if reverse:
        cum = s >= t
        ref = s >= mid
    else:
        cum = s <= t
        ref = s <= mid
    cum_ref = jnp.concatenate([cum, ref], axis=0).astype(F32).astype(BF16)
    rt = jnp.bitwise_and(lax.broadcasted_iota(jnp.int32, (GLA_HEADS * SUB, CHUNK), 0), SUB - 1)
    cs = lax.broadcasted_iota(jnp.int32, (GLA_HEADS * SUB, CHUNK), 1)
    rowi = lax.broadcasted_iota(jnp.int32, (CHUNK, 1), 0)
    causal, valid = [], []
    for i in range(CHUNK // SUB):
        tq = rt + i * SUB
        causal.append((cs >= tq) if reverse else (cs <= tq))
        valid.append((rowi >= i * SUB) if reverse else (rowi < (i + 1) * SUB))
    return cum_ref, causal, valid


def _gla_local(jobs, nc, o_s, qe_s, u_s, dec_s, head_masks, block_mask):
    nsub = CHUNK // SUB
    rows = [pl.ds(pl.multiple_of(c * CHUNK, CHUNK), CHUNK) for c, *_ in jobs]
    bs = []
    for (c, d, qs, k, vb, vt, la_ref, consts), r in zip(jobs, rows):
        h0, h1, h2 = _split3(la_ref[r, :])
        br = _dot(consts[0], h0) + _dot(consts[0], h1) + _dot(consts[0], h2)
        bs.append((br[0:CHUNK], br[CHUNK:2 * CHUNK]))
    ops = []
    for (c, d, qs, k, vb, vt, la_ref, consts), (b, rr) in zip(jobs, bs):
        qhat = qs * jnp.exp(b - rr)
        for i in range(nsub):
            r_i = rr[i * SUB:i * SUB + 1]
            kt = (k * jnp.exp(jnp.where(consts[2][i], r_i - b, NEG))).astype(BF16)
            qi = qhat[i * SUB:(i + 1) * SUB]
            ops.append((jnp.concatenate([jnp.where(m, qi, 0.0) for m in head_masks], axis=0).astype(BF16), kt))
    scores = [_dot_nt(qbig, kt) for qbig, kt in ops]
    outs = []
    for j, (c, d, qs, k, vb, vt, la_ref, consts) in enumerate(jobs):
        for i in range(nsub):
            a = jnp.where(consts[1][i], scores[j * nsub + i], 0.0)
            outs.append(_dot(a.astype(BF16), vb))
    for j, ((c, d, qs, k, vb, vt, la_ref, consts), (b, rr), r) in enumerate(zip(jobs, bs, rows)):
        parts = []
        for i in range(nsub):
            ov = outs[j * nsub + i]
            oi = jnp.where(head_masks[0], ov[0:SUB], 0.0)
            for h in range(1, GLA_HEADS):
                oi = oi + jnp.where(head_masks[h], ov[h * SUB:(h + 1) * SUB], 0.0)
            parts.append(oi)
        o_s[d, r, :] = jnp.concatenate(parts, axis=0)
        qe_s[d, r, :] = (qs * jnp.exp(b)).astype(BF16)
        b_last = b[0:1] if d == 1 else b[CHUNK - 1:CHUNK]
        kl = (k * jnp.exp(b_last - b)).astype(BF16)
        idx = d * nc + c
        u_s[idx] = jnp.where(block_mask, _dot(vt, kl), 0.0)
        dec_s[pl.ds(pl.multiple_of(idx * 8, 8), 8), :] = jnp.broadcast_to(jnp.exp(b_last), (8, GLA_W))


GLA_SAFE_EXPONENT = 60.0


def _gla_exact_intra(d, nc, qkvg_ref, la_ref, cum, o_s, b_s, acc_s, ones_blk):
    reverse = d == 1
    tcol = lax.broadcasted_iota(jnp.int32, (CHUNK, 1), 0)

    def chunk(c, carry):
        r0 = pl.multiple_of(c * CHUNK, CHUNK)
        rows = pl.ds(r0, CHUNK)
        h0, h1, h2 = _split3(la_ref[rows, :])
        b_s[...] = _dot(cum, h0) + _dot(cum, h1) + _dot(cum, h2)
        acc_s[...] = jnp.zeros((CHUNK, GLA_W), F32)
        qs = qkvg_ref[rows, 0:GLA_W] * (GLA_DK ** -0.5)

        def key(s, carry2):
            bs = b_s[pl.ds(s, 1), :]
            ks = qkvg_ref[pl.ds(r0 + s, 1), GLA_W:2 * GLA_W]
            vs = qkvg_ref[pl.ds(r0 + s, 1), 2 * GLA_W:3 * GLA_W]
            live = (tcol <= s) if reverse else (tcol >= s)
            hi, lo = _split2(qs * ks * jnp.exp(jnp.where(live, b_s[...] - bs, NEG)))
            acc_s[...] += (_dot(hi, ones_blk) + _dot(lo, ones_blk)) * vs
            return carry2

        lax.fori_loop(0, CHUNK, key, 0)
        o_s[d, rows, :] = acc_s[...]
        return carry

    lax.fori_loop(0, nc, chunk, 0)


def _gla_kernel(*refs, n, has_state):
    if has_state:
        (qkvg_ref, misc_ref, wgf_ref, wgb_ref, bgf_ref, bgb_ref, nw_ref, s0f_ref, s0b_ref,
         o_ref, laf_s, lab_s, o_s, qe_s, u_s, dec_s, sts_s, stf_s, stb_s, b_s, acc_s) = refs
    else:
        (qkvg_ref, misc_ref, wgf_ref, wgb_ref, bgf_ref, bgb_ref, nw_ref,
         o_ref, sf_ref, sb_ref, laf_s, lab_s, o_s, qe_s, u_s, dec_s, sts_s, stf_s, stb_s, b_s, acc_s) = refs
    misc = misc_ref[...]
    laf_s[...] = _log_sigmoid(_dot3(misc, wgf_ref[...]) + bgf_ref[...]) * (1.0 / GLA_TAU)
    lab_s[...] = _log_sigmoid(_dot3(misc, wgb_ref[...]) + bgb_ref[...]) * (1.0 / GLA_TAU)
    stf_s[...] = jnp.zeros((GLA_W, GLA_W), F32)
    stb_s[...] = jnp.zeros((GLA_W, GLA_W), F32)
    if has_state:
        for st_s, s0_ref in ((stf_s, s0f_ref), (stb_s, s0b_ref)):
            for h in range(GLA_HEADS):
                st_s[h * GLA_DK:(h + 1) * GLA_DK, h * GLA_DV:(h + 1) * GLA_DV] = s0_ref[h]
            st_s[...] = st_s[...].T

    lane = lax.broadcasted_iota(jnp.int32, (1, GLA_W), 1)
    head_masks = [jnp.right_shift(lane, 6) == h for h in range(GLA_HEADS)]
    bi = jnp.right_shift(lax.broadcasted_iota(jnp.int32, (GLA_W, GLA_W), 0), 6)
    bj = jnp.right_shift(lax.broadcasted_iota(jnp.int32, (GLA_W, GLA_W), 1), 6)
    block_mask = bi == bj
    consts_f = _gla_consts(False)
    consts_b = _gla_consts(True)
    nc = n // CHUNK

    def local(cc, carry):
        jobs = []
        for c in (2 * cc, 2 * cc + 1):
            rows = pl.ds(pl.multiple_of(c * CHUNK, CHUNK), CHUNK)
            qs = qkvg_ref[rows, 0:GLA_W] * (GLA_DK ** -0.5)
            k = qkvg_ref[rows, GLA_W:2 * GLA_W]
            v = qkvg_ref[rows, 2 * GLA_W:3 * GLA_W]
            vb = v.astype(BF16)
            vt = v.T.astype(BF16)
            jobs += [(c, 0, qs, k, vb, vt, laf_s, consts_f), (c, 1, qs, k, vb, vt, lab_s, consts_b)]
        _gla_local(jobs, nc, o_s, qe_s, u_s, dec_s, head_masks, block_mask)
        return carry

    lax.fori_loop(0, nc // 2, local, 0)

    ones_blk = block_mask.astype(F32).astype(BF16)
    @pl.when(jnp.min(jnp.minimum(laf_s[...], lab_s[...])) * (SUB // 2) < -GLA_SAFE_EXPONENT)
    def _():
        for d, (la_s, consts) in enumerate(((laf_s, consts_f), (lab_s, consts_b))):
            _gla_exact_intra(d, nc, qkvg_ref, la_s, consts[0][0:CHUNK], o_s, b_s, acc_s, ones_blk)

    def scan(j, carry):
        for d, st_s in enumerate((stf_s, stb_s)):
            idx = d * nc + (j if d == 0 else nc - 1 - j)
            st = st_s[...]
            sts_s[idx] = st.astype(BF16)
            st_s[...] = st * dec_s[pl.ds(pl.multiple_of(idx * 8, 8), 1), :] + u_s[idx]
        return carry

    lax.fori_loop(0, nc, scan, 0)

    def inter(cc, carry):
        jobs = [(c, d, pl.ds(pl.multiple_of(c * CHUNK, CHUNK), CHUNK))
                for c in (2 * cc, 2 * cc + 1) for d in range(2)]
        res = [_dot_nt(qe_s[d, r, :], sts_s[d * nc + c]) for c, d, r in jobs]
        for (c, d, r), o in zip(jobs, res):
            o_s[d, r, :] += o
        return carry

    lax.fori_loop(0, nc // 2, inter, 0)

    if not has_state:
        for st_s, s_ref in ((stf_s, sf_ref), (stb_s, sb_ref)):
            s = st_s[...].T
            for h in range(GLA_HEADS):
                s_ref[h] = s[h * GLA_DK:(h + 1) * GLA_DK, h * GLA_DV:(h + 1) * GLA_DV]

    rt = 256
    for r0 in range(0, n, rt):
        o = o_s[0, r0:r0 + rt, :] + o_s[1, r0:r0 + rt, :]
        hi, lo = _split2(o * o)
        ms = (_dot(hi, ones_blk) + _dot(lo, ones_blk)) * (1.0 / GLA_DV)
        g = qkvg_ref[r0:r0 + rt, 3 * GLA_W:4 * GLA_W]
        o_ref[r0:r0 + rt, :] = o * lax.rsqrt(ms + NORM_EPS) * nw_ref[...] * (g * _sigmoid(g))


def _gla(p, wgf, wgb, bgf, bgb, nw, n, nb, row_blk0, s0f=None, s0b=None):
    has_state = s0f is not None
    full = lambda shape: pl.BlockSpec(shape, lambda b: (0,) * len(shape))
    in_specs = [
        pl.BlockSpec((n, 4 * GLA_W), lambda b: (row_blk0 + b, 0)),
        pl.BlockSpec((n, 128), lambda b: (row_blk0 + b, MISC_BLK)),
        full((128, GLA_W)), full((128, GLA_W)), full((1, GLA_W)), full((1, GLA_W)), full((1, GLA_W)),
    ]
    args = [p, p, wgf, wgb, bgf, bgb, nw]
    out_specs = [pl.BlockSpec((n, GLA_W), lambda b: (b, 0))]
    out_shape = [jax.ShapeDtypeStruct((nb * n, GLA_W), F32)]
    if has_state:
        layer = s0f[1]
        st_spec = pl.BlockSpec((None, None, GLA_HEADS, GLA_DK, GLA_DV), lambda b: (b, layer, 0, 0, 0))
        in_specs += [st_spec, st_spec]
        args += [s0f[0], s0b[0]]
    else:
        st_spec = pl.BlockSpec((None, GLA_HEADS, GLA_DK, GLA_DV), lambda b: (b, 0, 0, 0))
        out_specs += [st_spec, st_spec]
        out_shape += [jax.ShapeDtypeStruct((nb, GLA_HEADS, GLA_DK, GLA_DV), F32)] * 2
    return pl.pallas_call(
        functools.partial(_gla_kernel, n=n, has_state=has_state),
        grid=(nb,),
        in_specs=in_specs,
        out_specs=out_specs,
        out_shape=out_shape,
        scratch_shapes=[
            pltpu.VMEM((n, GLA_W), F32), pltpu.VMEM((n, GLA_W), F32),
            pltpu.VMEM((2, n, GLA_W), F32),
            pltpu.VMEM((2, n, GLA_W), BF16),
            pltpu.VMEM((2 * n // CHUNK, GLA_W, GLA_W), F32),
            pltpu.VMEM((2 * n // CHUNK * 8, GLA_W), F32),
            pltpu.VMEM((2 * n // CHUNK, GLA_W, GLA_W), BF16),
            pltpu.VMEM((GLA_W, GLA_W), F32), pltpu.VMEM((GLA_W, GLA_W), F32),
            pltpu.VMEM((CHUNK, GLA_W), F32), pltpu.VMEM((CHUNK, GLA_W), F32),
        ],
        compiler_params=_cparams(("parallel",)),
        name="gla_state" if has_state else "gla_ctx",
    )(*args)


CONV_PAD = 16
CONV_ROWS = 128
CONV_SHIFT_ROWS = 24


def _conv_kernel(u_ref, w_ref, b_ref, lnw_ref, lnb_ref, o_ref, zp_s, zsh_s, *, n):
    u = u_ref[...]
    zp_s[0:CONV_PAD, :] = jnp.zeros((CONV_PAD, CONV_W), F32)
    zp_s[CONV_PAD + n:2 * CONV_PAD + n, :] = jnp.zeros((CONV_PAD, CONV_W), F32)
    zp_s[CONV_PAD:CONV_PAD + n, :] = u[:, 0:CONV_W] * _sigmoid(u[:, CONV_W:2 * CONV_W])
    m = n + CONV_SHIFT_ROWS
    for s in range(1, 8):
        zsh_s[s - 1] = zp_s[s:s + m, :]
    off = CONV_PAD - CONV_K // 2

    def tap(r0, j):
        s = (off + j) % 8
        base = r0 + off + j - s
        return zp_s[base:base + CONV_ROWS, :] if s == 0 else zsh_s[s - 1, base:base + CONV_ROWS, :]

    for r0 in range(0, n, CONV_ROWS):
        acc = tap(r0, 0) * w_ref[0:1, :]
        for j in range(1, CONV_K):
            acc = acc + tap(r0, j) * w_ref[j:j + 1, :]
        acc = acc + b_ref[...]
        mu = jnp.mean(acc, axis=-1, keepdims=True)
        xc = acc - mu
        var = jnp.mean(xc * xc, axis=-1, keepdims=True)
        y = xc * lax.rsqrt(var + NORM_EPS) * lnw_ref[...] + lnb_ref[...]
        o_ref[r0:r0 + CONV_ROWS, :] = y * _sigmoid(y)


def _conv(p, cw, cb, lnw, lnb, n, nb, row_blk0):
    full = lambda shape: pl.BlockSpec(shape, lambda b: (0,) * len(shape))
    return pl.pallas_call(
        functools.partial(_conv_kernel, n=n),
        grid=(nb,),
        in_specs=[
            pl.BlockSpec((n, 2 * CONV_W), lambda b: (row_blk0 + b, 2)),
            full((32, CONV_W)), full((1, CONV_W)), full((1, CONV_W)), full((1, CONV_W)),
        ],
        out_specs=pl.BlockSpec((n, CONV_W), lambda b: (b, 0)),
        out_shape=jax.ShapeDtypeStruct((nb * n, CONV_W), F32),
        scratch_shapes=[pltpu.VMEM((n + 2 * CONV_PAD, CONV_W), F32),
                        pltpu.VMEM((7, n + CONV_SHIFT_ROWS, CONV_W), F32)],
        compiler_params=_cparams(("parallel",)),
        name="conv%d" % n,
    )(p, cw, cb, lnw, lnb)


HEAD_BLK = 128


def _kv_expand(ckvn, t, wk_ref, wp_ref, wv_ref, k_ref, v_ref):
    cb = ckvn.astype(BF16)
    th, tl = _split2(t)
    k_ref[...] = (_dot(cb, wk_ref[...]) + _dot(th, wp_ref[...]) + _dot(tl, wp_ref[...])).astype(BF16)
    v_ref[...] = _dot(cb, wv_ref[...]).astype(BF16)


def _mla_prep_kernel(cq_ref, cm_ref, tqc_ref, tqs_ref, tk_ref, qnw_ref, kvw_ref, wa_ref, wb_ref,
                     wk_ref, wp_ref, wv_ref, q_ref, k_ref, v_ref, ckv_ref, kr_ref):
    qn = _rms(cq_ref[...], qnw_ref[...]).astype(BF16)
    tqc = jnp.concatenate([tqc_ref[...]] * MLA_HEADS, axis=1)
    tqs = jnp.concatenate([tqs_ref[...]] * MLA_HEADS, axis=1)
    scale = (MLA_NOPE + MLA_ROPE) ** -0.5
    q_ref[...] = ((_dot(qn, wa_ref[...]) * tqc + _dot(qn, wb_ref[...]) * tqs) * scale).astype(BF16)
    cm = cm_ref[...]
    ckvn = _rms(cm[:, 0:MLA_KV_RANK], kvw_ref[...])
    ckv_ref[...] = ckvn
    t = cm[:, MLA_KV_RANK:2 * MLA_KV_RANK] * tk_ref[...]
    kr_ref[...] = t
    _kv_expand(ckvn, t, wk_ref, wp_ref, wv_ref, k_ref, v_ref)


def _rope_tile(i):
    nctx = N_CTX // TM
    return jnp.where(i < nctx, 0, 1 + (i - nctx) % (DEC_SEQ // TM))


def _mla_prep(p, tabs, qnw, kvw, wa, wb, wk, wp, wv):
    tqc, tqs, tk = tabs
    full = lambda shape: pl.BlockSpec(shape, lambda i: (0,) * len(shape))
    tab_spec = pl.BlockSpec((TM, HEAD_BLK), lambda i: (_rope_tile(i), 0))
    hw = MLA_HEADS * HEAD_BLK
    return pl.pallas_call(
        _mla_prep_kernel,
        grid=(N_TOK // TM,),
        in_specs=[
            pl.BlockSpec((TM, MLA_Q_RANK), lambda i: (i, 6)),
            pl.BlockSpec((TM, 2 * MLA_KV_RANK), lambda i: (i, 7)),
            tab_spec, tab_spec, tab_spec,
            full((1, MLA_Q_RANK)), full((1, MLA_KV_RANK)),
            full((MLA_Q_RANK, hw)), full((MLA_Q_RANK, hw)),
            full((MLA_KV_RANK, hw)), full((128, hw)), full((MLA_KV_RANK, MLA_HEADS * MLA_V)),
        ],
        out_specs=[
            pl.BlockSpec((TM, hw), lambda i: (i, 0)),
            pl.BlockSpec((TM, hw), lambda i: (i, 0)),
            pl.BlockSpec((TM, MLA_HEADS * MLA_V), lambda i: (i, 0)),
            pl.BlockSpec((TM, MLA_KV_RANK), lambda i: (i, 0)),
            pl.BlockSpec((TM, 128), lambda i: (i, 0)),
        ],
        out_shape=[
            jax.ShapeDtypeStruct((N_TOK, hw), BF16),
            jax.ShapeDtypeStruct((N_TOK, hw), BF16),
            jax.ShapeDtypeStruct((N_TOK, MLA_HEADS * MLA_V), BF16),
            jax.ShapeDtypeStruct((N_TOK, MLA_KV_RANK), F32),
            jax.ShapeDtypeStruct((N_TOK, 128), F32),
        ],
        compiler_params=_cparams(("parallel",)),
        name="mla_prep",
    )(p, p, tqc, tqs, tk, qnw, kvw, wa, wb, wk, wp, wv)


def _cache_kv_kernel(ckv_ref, kr_ref, wk_ref, wp_ref, wv_ref, k_ref, v_ref):
    _kv_expand(ckv_ref[...], kr_ref[...], wk_ref, wp_ref, wv_ref, k_ref, v_ref)


def _cache_kv(ckv, kr, wk, wp, wv):
    n = DEC_BATCH * PAST_LEN
    full = lambda shape: pl.BlockSpec(shape, lambda i: (0,) * len(shape))
    hw = MLA_HEADS * HEAD_BLK
    return pl.pallas_call(
        _cache_kv_kernel,
        grid=(n // PAST_LEN,),
        in_specs=[
            pl.BlockSpec((PAST_LEN, MLA_KV_RANK), lambda i: (i, 0)),
            pl.BlockSpec((PAST_LEN, 128), lambda i: (i, 0)),
            full((MLA_KV_RANK, hw)), full((128, hw)), full((MLA_KV_RANK, MLA_HEADS * MLA_V)),
        ],
        out_specs=[
            pl.BlockSpec((PAST_LEN, hw), lambda i: (i, 0)),
            pl.BlockSpec((PAST_LEN, MLA_HEADS * MLA_V), lambda i: (i, 0)),
        ],
        out_shape=[
            jax.ShapeDtypeStruct((n, hw), BF16),
            jax.ShapeDtypeStruct((n, MLA_HEADS * MLA_V), BF16),
        ],
        compiler_params=_cparams(("parallel",)),
        name="cache_kv",
    )(ckv, kr, wk, wp, wv)


def _attn_kernel(*refs, has_cache):
    if has_cache:
        q_ref, k_ref, v_ref, kc_ref, vc_ref, o_ref = refs
    else:
        q_ref, k_ref, v_ref, o_ref = refs
    nh = q_ref.shape[1] // HEAD_BLK
    cols = [slice(h * HEAD_BLK, (h + 1) * HEAD_BLK) for h in range(nh)]
    vcols = [slice((h // 2) * 2 * MLA_V, (h // 2 + 1) * 2 * MLA_V) for h in range(nh)]
    s = [_dot_nt(q_ref[:, c], k_ref[:, c]) for c in cols]
    m = [jnp.max(x, axis=-1, keepdims=True) for x in s]
    if has_cache:
        sc = [_dot_nt(q_ref[:, c], kc_ref[:, c]) for c in cols]
        m = [jnp.maximum(a, jnp.max(x, axis=-1, keepdims=True)) for a, x in zip(m, sc)]
        pc = [jnp.exp(x - a) for x, a in zip(sc, m)]
    p = [jnp.exp(x - a) for x, a in zip(s, m)]
    den = [jnp.sum(x, axis=-1, keepdims=True) for x in p]
    o = [_dot(x.astype(BF16), v_ref[:, c]) for x, c in zip(p, vcols)]
    if has_cache:
        den = [a + jnp.sum(x, axis=-1, keepdims=True) for a, x in zip(den, pc)]
        o = [a + _dot(x.astype(BF16), vc_ref[:, c]) for a, x, c in zip(o, pc, vcols)]
    outs = [a / b for a, b in zip(o, den)]
    lane = lax.broadcasted_iota(jnp.int32, (1, 2 * MLA_V), 1)
    for j in range(nh // 2):
        o_ref[:, j * 2 * MLA_V:(j + 1) * 2 * MLA_V] = jnp.where(lane < MLA_V, outs[2 * j], outs[2 * j + 1])


def _attention(q, k, v, n, nb, row0, tq, npair, kc=None, vc=None):
    has_cache = kc is not None
    nq = n // tq
    rb0 = row0 // tq
    kb0 = row0 // n
    in_specs = [
        pl.BlockSpec((tq, npair * 2 * HEAD_BLK), lambda b, h, i: (rb0 + b * nq + i, h)),
        pl.BlockSpec((n, npair * 2 * HEAD_BLK), lambda b, h, i: (kb0 + b, h)),
        pl.BlockSpec((n, npair * 2 * MLA_V), lambda b, h, i: (kb0 + b, h)),
    ]
    args = [q, k, v]
    if has_cache:
        in_specs += [
            pl.BlockSpec((PAST_LEN, npair * 2 * HEAD_BLK), lambda b, h, i: (b, h)),
            pl.BlockSpec((PAST_LEN, npair * 2 * MLA_V), lambda b, h, i: (b, h)),
        ]
        args += [kc, vc]
    return pl.pallas_call(
        functools.partial(_attn_kernel, has_cache=has_cache),
        grid=(nb, MLA_HEADS // (2 * npair), nq),
        in_specs=in_specs,
        out_specs=pl.BlockSpec((tq, npair * 2 * MLA_V), lambda b, h, i: (b * nq + i, h)),
        out_shape=jax.ShapeDtypeStruct((nb * n, MLA_HEADS * MLA_V), F32),
        compiler_params=_cparams(("parallel", "parallel", "parallel")),
        name="attn_cache" if has_cache else "attn_ctx",
    )(*args)


def _outproj_kernel(xa_ref, xb_ref, oga_ref, ogb_ref, oca_ref, ocb_ref, oma_ref, omb_ref, mod_ref, nw_ref,
                    wo_ref, wr_ref, br_ref, xo_ref, h_ref, route_ref, cnt_ref, carry_s, wrh_s, wrl_s):
    @pl.when(pl.program_id(0) == 0)
    def _():
        carry_s[...] = jnp.zeros(carry_s.shape, F32)
        wrh, wrl = _split2(wr_ref[...])
        wrh_s[...] = wrh
        wrl_s[...] = wrl

    mod = mod_ref[...]

    def residual(x_ref, og_ref, oc_ref, om_ref):
        attn = (_dot(og_ref[...].astype(BF16), wo_ref[0:GLA_W, :])
                + _dot(oc_ref[...].astype(BF16), wo_ref[GLA_W:GLA_W + CONV_W, :])
                + _dot(om_ref[...].astype(BF16), wo_ref[GLA_W + CONV_W:D_MODEL, :]))
        xo_ref[...] = x_ref[...] + mod[:, 2 * D_MODEL:3 * D_MODEL] * attn

    is_ctx = pl.program_id(0) < N_CTX // TM
    pl.when(is_ctx)(lambda: residual(xa_ref, oga_ref, oca_ref, oma_ref))
    pl.when(jnp.logical_not(is_ctx))(lambda: residual(xb_ref, ogb_ref, ocb_ref, omb_ref))
    xn = xo_ref[...]
    h = _rms(xn, nw_ref[...]) * (1.0 + mod[:, 4 * D_MODEL:5 * D_MODEL]) + mod[:, 3 * D_MODEL:4 * D_MODEL]
    _rows_to_tiles(h_ref, h)
    hh, hl = _split2(h)
    logits = _dot(hh, wrh_s[...]) + _dot(hl, wrh_s[...]) + _dot(hh, wrl_s[...]) + br_ref[...]
    lane = lax.broadcasted_iota(jnp.int32, logits.shape, 1).astype(F32)
    big = 1e9
    gl = jnp.where((lane >= N_EXPERTS) & (lane < N_EXPERTS + N_GROUPS), logits, NEG)
    gmax = jnp.max(gl, axis=-1, keepdims=True)
    gw = 1.0 / jnp.sum(jnp.exp(gl - gmax), axis=-1, keepdims=True)
    gi = jnp.min(jnp.where(gl == gmax, lane, big), axis=-1, keepdims=True) - N_EXPERTS
    lo = gi * EXPERTS_PER_GROUP
    el = jnp.where((lane >= lo) & (lane < lo + EXPERTS_PER_GROUP), logits, NEG)
    m1 = jnp.max(el, axis=-1, keepdims=True)
    i1 = jnp.min(jnp.where(el == m1, lane, big), axis=-1, keepdims=True)
    el2 = jnp.where(lane == i1, NEG, el)
    m2 = jnp.max(el2, axis=-1, keepdims=True)
    i2 = jnp.min(jnp.where(el2 == m2, lane, big), axis=-1, keepdims=True)
    e2 = jnp.exp(m2 - m1)
    p1 = 1.0 / (1.0 + e2)
    onehot = jnp.where((lane == i1) | (lane == i2), 1.0, 0.0)
    tr = lax.broadcasted_iota(jnp.int32, (TM, TM), 0)
    tc = lax.broadcasted_iota(jnp.int32, (TM, TM), 1)
    before = (tc < tr).astype(F32).astype(BF16)
    seen = _dot(before, onehot.astype(BF16)) + carry_s[0:1, :]
    rank1 = jnp.sum(jnp.where(lane == i1, seen, 0.0), axis=-1, keepdims=True)
    rank2 = jnp.sum(jnp.where(lane == i2, seen, 0.0), axis=-1, keepdims=True)
    carry_s[...] = carry_s[...] + jnp.sum(onehot, axis=0, keepdims=True)
    cnt_ref[...] = carry_s[...]
    cols = (i1, i2, gw * p1, gw * (e2 * p1), rank1, rank2)
    route = jnp.zeros(logits.shape, F32)
    for j, col in enumerate(cols):
        route = jnp.where(lane == j, col, route)
    route_ref[...] = route


def _outproj(x, og, oc, om, mods4, l, norm_w, wo, wr, br):
    full = lambda shape: pl.BlockSpec(shape, lambda i: (0,) * len(shape))
    row = lambda w: pl.BlockSpec((TM, w), lambda i: (i, 0))
    return pl.pallas_call(
        _outproj_kernel,
        grid=(N_TOK // TM,),
        in_specs=(_pair_specs(TM, D_MODEL, isinstance(x, tuple)) + _pair_specs(TM, GLA_W, True)
                  + _pair_specs(TM, CONV_W, True) + _pair_specs(TM, MLA_HEADS * MLA_V, True) + [
            pl.BlockSpec((None, None, 1, 6 * D_MODEL), lambda i: (l, _mod_index(i, TM), 0, 0)),
            full((1, D_MODEL)), full((D_MODEL, D_MODEL)), full((D_MODEL, 128)), full((1, 128)),
        ]),
        out_specs=[row(D_MODEL), pl.BlockSpec(_tiled(TM), lambda i: (i, 0)), row(128), full((8, 128))],
        out_shape=[
            jax.ShapeDtypeStruct((N_TOK, D_MODEL), F32),
            jax.ShapeDtypeStruct(_tiled(N_TOK), F32),
            jax.ShapeDtypeStruct((N_TOK, 128), F32),
            jax.ShapeDtypeStruct((8, 128), F32),
        ],
        scratch_shapes=[pltpu.VMEM((8, 128), F32), pltpu.VMEM((D_MODEL, 128), BF16),
                        pltpu.VMEM((D_MODEL, 128), BF16)],
        compiler_params=_cparams(("arbitrary",)),
        name="outproj_router",
    )(*_pair(x), *og, *oc, *om, mods4, norm_w, wo, wr, br)


N_ASSIGN = 2 * N_TOK
TE = 256
N_ETILE = N_ASSIGN // TE
N_WORK = N_ETILE + N_EXPERTS


ROW_SUB = 8


def _tiled(n):
    return (n * ROW_SUB, D_MODEL // ROW_SUB)


def _rows_to_tiles(ref, x, r0=0):
    n = x.shape[0]
    for s in range(ROW_SUB):
        ref[pl.ds(r0 * ROW_SUB + s, n, stride=ROW_SUB), :] = x[:, s * 128:(s + 1) * 128]


def _tiles_to_rows(ref, r0=0, n=None):
    n = ref.shape[0] // ROW_SUB if n is None else n
    return jnp.concatenate([ref[pl.ds(r0 * ROW_SUB + s, n, stride=ROW_SUB), :] for s in range(ROW_SUB)], axis=1)


def _row_tile(ref, t):
    start = t * ROW_SUB if isinstance(t, int) else pl.multiple_of(t * ROW_SUB, ROW_SUB)
    return ref.at[pl.ds(start, ROW_SUB)]


def _lane_col(x, lane, j):
    return jnp.sum(jnp.where(lane == j, x, 0.0), axis=-1, keepdims=True)


def _plan_kernel(cnt_s, cnt_ref, route_ref, pos_ref, wt_ref, we_ref, wlo_ref, whi_ref, wf_ref, wr_ref, wn_ref,
                 off_s):
    cnt = cnt_ref[...]
    chi = jnp.floor(cnt * (1.0 / 128.0))
    clo = cnt - chi * 128.0
    r = lax.broadcasted_iota(jnp.int32, (128, 128), 0)
    c = lax.broadcasted_iota(jnp.int32, (128, 128), 1)
    below = (r < c).astype(F32).astype(BF16)
    off = (128.0 * _dot(chi.astype(BF16), below) + _dot(clo.astype(BF16), below))[0:1, :]
    route = route_ref[...]
    lane = lax.broadcasted_iota(jnp.int32, route.shape, 1).astype(F32)
    pos1 = _lane_col(off, lane, _lane_col(route, lane, 0.0)) + _lane_col(route, lane, 4.0)
    pos2 = _lane_col(off, lane, _lane_col(route, lane, 1.0)) + _lane_col(route, lane, 5.0)
    pos_ref[...] = jnp.where(lane == 0.0, pos1, jnp.where(lane == 1.0, pos2, 0.0))

    @pl.when(pl.program_id(0) == 0)
    def _():
        def offs(e, acc):
            off_s[e] = acc
            return acc + cnt_s[e]

        off_s[N_EXPERTS] = lax.fori_loop(0, N_EXPERTS, offs, jnp.int32(0))

        def item(w, st):
            j, e, first = st
            active = j < N_ETILE
            jj = jnp.minimum(j, N_ETILE - 1)
            ee = jnp.minimum(e, N_EXPERTS - 1)
            tlo = jj * TE
            thi = tlo + TE
            end = off_s[ee + 1]
            wt_ref[w] = jj
            we_ref[w] = ee
            wlo_ref[w] = jnp.where(active, jnp.maximum(off_s[ee], tlo) - tlo, 0)
            whi_ref[w] = jnp.where(active, jnp.minimum(end, thi) - tlo, 0)
            wf_ref[w] = jnp.where(active, first, 0)
            adv_j = (active & (end >= thi)).astype(jnp.int32)
            adv_e = (active & (end <= thi)).astype(jnp.int32)
            return j + adv_j, e + adv_e, adv_j

        lax.fori_loop(0, N_WORK, item, (jnp.int32(0), jnp.int32(0), jnp.int32(1)))

        def run(w, r):
            r = r + (we_ref[w] != we_ref[jnp.maximum(w - 1, 0)]).astype(jnp.int32)
            wr_ref[w] = r
            return r

        lax.fori_loop(0, N_WORK, run, jnp.int32(0))

        def following(j, nxt):
            w = N_WORK - 1 - j
            wn_ref[w] = nxt
            return jnp.where(we_ref[jnp.maximum(w - 1, 0)] != we_ref[w], we_ref[w], nxt)

        lax.fori_loop(0, N_WORK, following, jnp.int32(-1))


def _plan(cnt_i, cnt, route):
    smem = pl.BlockSpec(memory_space=pltpu.SMEM)
    work = jax.ShapeDtypeStruct((N_WORK,), jnp.int32)
    return pl.pallas_call(
        _plan_kernel,
        grid=(1,),
        in_specs=[smem, pl.BlockSpec((8, 128), lambda i: (0, 0)), pl.BlockSpec((N_TOK, 128), lambda i: (0, 0))],
        out_specs=[pl.BlockSpec((N_TOK, 128), lambda i: (0, 0))] + [smem] * 7,
        out_shape=[jax.ShapeDtypeStruct((N_TOK, 128), F32)] + [work] * 7,
        scratch_shapes=[pltpu.SMEM((N_EXPERTS + 1,), jnp.int32)],
        compiler_params=_cparams(("arbitrary",)),
        name="moe_plan",
    )(cnt_i, cnt, route)


def _dispatch_kernel(pos_s, h_ref, xs_hbm, sem):
    base = pl.program_id(0) * TM

    def copy(i, k):
        return pltpu.make_async_copy(_row_tile(h_ref, i), _row_tile(xs_hbm, pos_s[2 * (base + i) + k]), sem)

    def wait(i, carry):
        copy(i, 0).wait()
        copy(i, 1).wait()
        return carry

    for i in range(TM):
        copy(i, 0).start(priority=0)
        copy(i, 1).start(priority=1)
    lax.fori_loop(0, TM, wait, 0, unroll=8)


def _dispatch(pos, h):
    any_spec = pl.BlockSpec(memory_space=pl.ANY)
    return pl.pallas_call(
        _dispatch_kernel,
        grid_spec=pltpu.PrefetchScalarGridSpec(
            num_scalar_prefetch=1, grid=(N_TOK // TM,),
            in_specs=[pl.BlockSpec(_tiled(TM), lambda i, p: (i, 0))], out_specs=any_spec,
            scratch_shapes=[pltpu.SemaphoreType.DMA(())]),
        out_shape=jax.ShapeDtypeStruct(_tiled(N_ASSIGN), F32),
        compiler_params=_cparams(("arbitrary",)),
        name="moe_dispatch",
    )(pos, h)


def _experts_kernel(wt_s, we_s, wlo_s, whi_s, wf_s, wr_s, wn_s, xs_ref, w1_hbm, w3_hbm, w2_hbm, o_ref,
                    w1_s, w3_s, w2_s, f1_s, f3_s, f2_s, sem, *, layer):
    w = pl.program_id(0)
    first = wf_s[w] == 1
    nonempty = whi_s[w] > wlo_s[w]

    def fetch(e, slot):
        return [pltpu.make_async_copy(src.at[layer, e], dst.at[slot], sem.at[slot])
                for src, dst in ((w1_hbm, f1_s), (w3_hbm, f3_s), (w2_hbm, f2_s))]

    @pl.when(w == 0)
    def _():
        for c in fetch(we_s[0], wr_s[0] % 2):
            c.start()

    @pl.when((w == 0) | (we_s[w] != we_s[jnp.maximum(w - 1, 0)]))
    def _():
        slot = wr_s[w] % 2
        for c in fetch(we_s[w], slot):
            c.wait()
        w1_s[...] = f1_s[slot].astype(BF16)
        w3_s[...] = f3_s[slot].astype(BF16)
        w2_s[...] = f2_s[slot].astype(BF16)

        @pl.when(wn_s[w] >= 0)
        def _():
            for c in fetch(wn_s[w], 1 - slot):
                c.start()

    @pl.when(first & jnp.logical_not(nonempty))
    def _():
        o_ref[...] = jnp.zeros(o_ref.shape, F32)

    nsplit = 2
    hrows = TE // nsplit

    def expert_rows(merge):
        x = [_tiles_to_rows(xs_ref, j * hrows, hrows).astype(BF16) for j in range(nsplit)]
        a = [_dot(v, w1_s[...]) for v in x]
        b = [_dot(v, w3_s[...]) for v in x]
        hid = [(u * _sigmoid(u) * v).astype(BF16) for u, v in zip(a, b)]
        y = [_dot(v, w2_s[...]) for v in hid]
        for j in range(nsplit):
            row = lax.broadcasted_iota(jnp.int32, (hrows, 1), 0) + j * hrows
            mine = (row >= wlo_s[w]) & (row < whi_s[w])
            other = _tiles_to_rows(o_ref, j * hrows, hrows) if merge else 0.0
            _rows_to_tiles(o_ref, jnp.where(mine, y[j], other), j * hrows)

    @pl.when(nonempty & first)
    def _():
        expert_rows(False)

    @pl.when(nonempty & jnp.logical_not(first))
    def _():
        expert_rows(True)


def _experts(work, xs, l, w1, w3, w2):
    tile = pl.BlockSpec(_tiled(TE), lambda w, wt, *_: (wt[w], 0))
    any_spec = pl.BlockSpec(memory_space=pl.ANY)
    return pl.pallas_call(
        functools.partial(_experts_kernel, layer=l),
        grid_spec=pltpu.PrefetchScalarGridSpec(
            num_scalar_prefetch=7, grid=(N_WORK,),
            in_specs=[tile, any_spec, any_spec, any_spec],
            out_specs=tile,
            scratch_shapes=[pltpu.VMEM((D_MODEL, EXPERT_FF), BF16), pltpu.VMEM((D_MODEL, EXPERT_FF), BF16),
                            pltpu.VMEM((EXPERT_FF, D_MODEL), BF16),
                            pltpu.VMEM((2, D_MODEL, EXPERT_FF), F32), pltpu.VMEM((2, D_MODEL, EXPERT_FF), F32),
                            pltpu.VMEM((2, EXPERT_FF, D_MODEL), F32), pltpu.SemaphoreType.DMA((2,))]),
        out_shape=jax.ShapeDtypeStruct(_tiled(N_ASSIGN), F32),
        compiler_params=_cparams(("arbitrary",)),
        name="moe_experts",
    )(*work, xs, w1, w3, w2)


def _moe_residual(pos_s, x_ref, route_ref, mod_ref, ys_hbm, y_s, sem):
    step = pl.program_id(0)
    nsteps = pl.num_programs(0)

    def copy(t, i, k):
        slot = t % 2
        src = _row_tile(ys_hbm, pos_s[2 * (t * TM + i) + k])
        return pltpu.make_async_copy(src, _row_tile(y_s.at[slot, k], i), sem.at[slot])

    def gather(t):
        for i in range(TM):
            copy(t, i, 0).start(priority=0)
            copy(t, i, 1).start(priority=1)

    def wait(i, carry):
        copy(step, i, 0).wait()
        copy(step, i, 1).wait()
        return carry

    pl.when(step == 0)(lambda: gather(step))
    pl.when(step + 1 < nsteps)(lambda: gather(step + 1))
    lax.fori_loop(0, TM, wait, 0, unroll=8)
    route = route_ref[...]
    lane = lax.broadcasted_iota(jnp.int32, route.shape, 1)
    slot = step % 2
    moe = (_lane_col(route, lane, 2) * _tiles_to_rows(y_s.at[slot, 0])
           + _lane_col(route, lane, 3) * _tiles_to_rows(y_s.at[slot, 1]))
    return x_ref[...] + mod_ref[:, 5 * D_MODEL:6 * D_MODEL] * moe


def _combine_inproj_kernel(pos_s, x_ref, route_ref, modp_ref, ys_hbm, mod_ref, nw_ref, w_ref,
                           xo_ref, p_ref, y_s, sem):
    x = _moe_residual(pos_s, x_ref, route_ref, modp_ref, ys_hbm, y_s, sem)
    xo_ref[...] = x
    mod = mod_ref[...]
    h = _rms(x, nw_ref[...]) * (1.0 + mod[:, D_MODEL:2 * D_MODEL]) + mod[:, 0:D_MODEL]
    p_ref[...] = _dot(h.astype(BF16), w_ref[...])


def _combine_inproj(pos, x, route, mods4, l, ys, norm_w, w_in_r):
    row = lambda w: pl.BlockSpec((TM, w), lambda i, p: (i, 0))
    mod = lambda layer: pl.BlockSpec((None, None, 1, 6 * D_MODEL), lambda i, p: (layer, _mod_index(i, TM), 0, 0))
    return pl.pallas_call(
        _combine_inproj_kernel,
        grid_spec=pltpu.PrefetchScalarGridSpec(
            num_scalar_prefetch=1, grid=(N_TOK // TM,),
            in_specs=[row(D_MODEL), row(128), mod(l - 1), pl.BlockSpec(memory_space=pl.ANY), mod(l),
                      pl.BlockSpec((1, D_MODEL), lambda i, p: (0, 0)),
                      pl.BlockSpec((D_MODEL, P_COLS), lambda i, p: (0, 0))],
            out_specs=[row(D_MODEL), row(P_COLS)],
            scratch_shapes=[pltpu.VMEM((2, 2) + _tiled(TM), F32), pltpu.SemaphoreType.DMA((2,))]),
        out_shape=[jax.ShapeDtypeStruct((N_TOK, D_MODEL), F32), jax.ShapeDtypeStruct((N_TOK, P_COLS), F32)],
        compiler_params=_cparams(("arbitrary",)),
        name="moe_combine_inproj",
    )(pos, x, route, mods4, ys, mods4, norm_w, w_in_r)


def _combine_kernel(pos_s, x_ref, route_ref, mod_ref, fw_ref, ys_hbm, *refs, final):
    if final:
        oa_ref, ob_ref, y_s, sem = refs
    else:
        o_ref, y_s, sem = refs
    x = _moe_residual(pos_s, x_ref, route_ref, mod_ref, ys_hbm, y_s, sem)
    if final:
        y = _rms(x, fw_ref[...])
        is_ctx = pl.program_id(0) < N_CTX // TM

        @pl.when(is_ctx)
        def _():
            oa_ref[...] = y

        @pl.when(jnp.logical_not(is_ctx))
        def _():
            ob_ref[...] = y
    else:
        o_ref[...] = x


def _combine(pos, x, route, mods4, l, fw, ys, final):
    row = lambda w: pl.BlockSpec((TM, w), lambda i, p: (i, 0))
    if final:
        out_specs = _pair_specs(TM, D_MODEL, True)
        out_shape = [jax.ShapeDtypeStruct((N_CTX, D_MODEL), F32), jax.ShapeDtypeStruct((N_SMP, D_MODEL), F32)]
    else:
        out_specs = row(D_MODEL)
        out_shape = jax.ShapeDtypeStruct((N_TOK, D_MODEL), F32)
    return pl.pallas_call(
        functools.partial(_combine_kernel, final=final),
        grid_spec=pltpu.PrefetchScalarGridSpec(
            num_scalar_prefetch=1, grid=(N_TOK // TM,),
            in_specs=[row(D_MODEL), row(128),
                      pl.BlockSpec((None, None, 1, 6 * D_MODEL), lambda i, p: (l, _mod_index(i, TM), 0, 0)),
                      pl.BlockSpec((1, D_MODEL), lambda i, p: (0, 0)),
                      pl.BlockSpec(memory_space=pl.ANY)],
            out_specs=out_specs,
            scratch_shapes=[pltpu.VMEM((2, 2) + _tiled(TM), F32), pltpu.SemaphoreType.DMA((2,))]),
        out_shape=out_shape,
        compiler_params=_cparams(("arbitrary",)),
        name="moe_combine_final" if final else "moe_combine",
    )(pos, x, route, mods4, fw, ys)


def _rope_swap(w):
    nf = MLA_ROPE // 4
    g = w.reshape(w.shape[:-1] + (2, 2, nf))
    return jnp.stack([-g[..., 1, :], g[..., 0, :]], axis=-2).reshape(w.shape)


def _rope_tables():
    nf = MLA_ROPE // 4
    pos = np.arange(DEC_SEQ)
    row = (pos // GRID_W).astype(np.float32)
    col = (pos % GRID_W).astype(np.float32)
    inv = (np.float32(ROPE_THETA) ** (-np.arange(nf, dtype=np.float32) / np.float32(nf))).astype(np.float32)
    ar = (row[:, None] * inv).astype(np.float32)
    ac = (col[:, None] * inv).astype(np.float32)
    cos = np.concatenate([np.cos(ar), np.cos(ar), np.cos(ac), np.cos(ac)], axis=1)
    sin = np.concatenate([np.sin(ar), np.sin(ar), np.sin(ac), np.sin(ac)], axis=1)
    cos = np.concatenate([np.ones((TM, MLA_ROPE)), cos], axis=0).astype(np.float32)
    sin = np.concatenate([np.zeros((TM, MLA_ROPE)), sin], axis=0).astype(np.float32)
    n = cos.shape[0]
    z32 = np.zeros((n, 32), np.float32)
    tqc = np.concatenate([np.ones((n, MLA_NOPE), np.float32), cos, z32], axis=1)
    tqs = np.concatenate([np.zeros((n, MLA_NOPE), np.float32), sin, z32], axis=1)
    tk = np.concatenate([z32, cos, sin, z32], axis=1)
    return jnp.asarray(tqc), jnp.asarray(tqs), jnp.asarray(tk)


def _prep_layer(l, w_in, gla_wg_f, gla_wg_b, mla_wuq, mla_wukv, w_out, moe_wg, moe_bg, moe_we, moe_be):
    offs = np.cumsum([0, 256, 256, 256, 256, 16, 16, 512, 256, 128, 32])
    seg = lambda i: w_in[l][:, offs[i]:offs[i + 1]]
    kr = seg(9)
    w_in_r = jnp.concatenate(
        [seg(0), seg(1), seg(2), seg(3), seg(6), seg(7), seg(8), seg(4), seg(5), kr, _rope_swap(kr),
         jnp.zeros((D_MODEL, 32), F32)], axis=1).astype(BF16)
    wgf = jnp.zeros((128, GLA_W), F32).at[0:16].set(gla_wg_f[l])
    wgb = jnp.zeros((128, GLA_W), F32).at[16:32].set(gla_wg_b[l])
    wq = mla_wuq[l].reshape(MLA_Q_RANK, MLA_HEADS, MLA_NOPE + MLA_ROPE)
    nope, rope = wq[..., :MLA_NOPE], wq[..., MLA_NOPE:]
    z = lambda n: jnp.zeros((MLA_Q_RANK, MLA_HEADS, n), F32)
    hw = MLA_HEADS * HEAD_BLK
    wa = jnp.concatenate([nope, rope, z(32)], axis=-1).reshape(MLA_Q_RANK, hw).astype(BF16)
    wb = jnp.concatenate([z(MLA_NOPE), _rope_swap(rope), z(32)], axis=-1).reshape(MLA_Q_RANK, hw).astype(BF16)
    wkv = mla_wukv[l].reshape(MLA_KV_RANK, MLA_HEADS, MLA_NOPE + MLA_V)
    wk = jnp.concatenate([wkv[..., :MLA_NOPE], jnp.zeros((MLA_KV_RANK, MLA_HEADS, 64), F32)],
                         axis=-1).reshape(MLA_KV_RANK, hw).astype(BF16)
    wv = wkv[..., MLA_NOPE:].reshape(MLA_KV_RANK, MLA_HEADS * MLA_V).astype(BF16)
    wp_np = np.zeros((128, MLA_HEADS, HEAD_BLK), np.float32)
    for j in range(MLA_ROPE):
        wp_np[32 + j, :, MLA_NOPE + j] = 1.0
        wp_np[64 + j, :, MLA_NOPE + j] = 1.0
    wp = jnp.asarray(wp_np.reshape(128, hw)).astype(BF16)
    wo = w_out[l].astype(BF16)
    wr = jnp.concatenate([moe_we[l], moe_wg[l], jnp.zeros((D_MODEL, 128 - N_EXPERTS - N_GROUPS), F32)], axis=1)
    br = jnp.concatenate([moe_be[l], moe_bg[l], jnp.zeros((128 - N_EXPERTS - N_GROUPS,), F32)]).reshape(1, 128)
    return dict(w_in_r=w_in_r, wgf=wgf, wgb=wgb, wa=wa, wb=wb, wk=wk, wv=wv, wp=wp, wo=wo, wr=wr, br=br)


def kernel(x_prompt, x_sample, cache_ckv, cache_krope, state_gla_fwd, state_gla_bwd, c, c_ctx, ada_w, ada_b,
           norm1_w, norm2_w, w_in, gla_wg_f, gla_bg_f, gla_wg_b, gla_bg_b, gla_norm_w, conv_w, conv_b,
           conv_ln_w, conv_ln_b, mla_qnorm_w, mla_wuq, mla_kvnorm_w, mla_wukv, w_out, moe_wg, moe_bg, moe_we,
           moe_be, moe_w1, moe_w3, moe_w2, final_norm_w):
    x = (x_prompt.reshape(N_CTX, D_MODEL), x_sample.reshape(N_SMP, D_MODEL))
    cond8 = jnp.concatenate([c_ctx[None, :], c, jnp.zeros((N_MOD - 1 - DEC_BATCH, D_MODEL), F32)], axis=0)
    mods4 = _ada_table(cond8, ada_w, ada_b).reshape(DEPTH, N_MOD, 1, 6 * D_MODEL)
    tabs = _rope_tables()
    row1 = lambda a: a.reshape(1, -1)

    ckv_new, kr_new, sf_new, sb_new = [], [], [], []
    for l in range(DEPTH):
        w = _prep_layer(l, w_in, gla_wg_f, gla_wg_b, mla_wuq, mla_wukv, w_out, moe_wg, moe_bg, moe_we, moe_be)
        if l == 0:
            p = _inproj(x, mods4, l, row1(norm1_w[l]), w["w_in_r"])
        else:
            x, p = _combine_inproj(pos, xn, route, mods4, l, ys, row1(norm1_w[l]), w["w_in_r"])

        gla_args = (p, w["wgf"], w["wgb"], row1(gla_bg_f[l]), row1(gla_bg_b[l]), row1(gla_norm_w[l]))
        og_c, sf, sb = _gla(*gla_args, SEQ, BATCH, 0)
        og_s = _gla(*gla_args, DEC_SEQ, DEC_BATCH, N_CTX // DEC_SEQ, (state_gla_fwd, l), (state_gla_bwd, l))[0]
        og = (og_c, og_s)

        cw = jnp.concatenate([conv_w[l], jnp.zeros((1, CONV_W), F32)], axis=0)
        conv_args = (p, cw, row1(conv_b[l]), row1(conv_ln_w[l]), row1(conv_ln_b[l]))
        oc = (_conv(*conv_args, SEQ, BATCH, 0), _conv(*conv_args, DEC_SEQ, DEC_BATCH, N_CTX // DEC_SEQ))

        q, k, v, ckvn, kr = _mla_prep(p, tabs, row1(mla_qnorm_w[l]), row1(mla_kvnorm_w[l]),
                                      w["wa"], w["wb"], w["wk"], w["wp"], w["wv"])
        cc = cache_ckv[:, l].reshape(DEC_BATCH * PAST_LEN, MLA_KV_RANK)
        ckr = cache_krope[:, l].reshape(DEC_BATCH * PAST_LEN, MLA_ROPE)
        ckr = jnp.pad(ckr, ((0, 0), (32, 64)))
        kc, vc = _cache_kv(cc, ckr, w["wk"], w["wp"], w["wv"])
        om = (_attention(q, k, v, SEQ, BATCH, 0, SEQ, MLA_HEADS // 2),
              _attention(q, k, v, DEC_SEQ, DEC_BATCH, N_CTX, 512, 1, kc, vc))

        xn, h2, route, cnt = _outproj(x, og, oc, om, mods4, l, row1(norm2_w[l]), w["wo"], w["wr"], w["br"])
        posf, *work = _plan(cnt[0, :N_EXPERTS].astype(jnp.int32), cnt, route)
        pos = posf[:, 0:2].astype(jnp.int32).reshape(N_ASSIGN)
        ys = _experts(work, _dispatch(pos, h2), l, moe_w1, moe_w3, moe_w2)
        if l == DEPTH - 1:
            x = _combine(pos, xn, route, mods4, l, row1(final_norm_w), ys, True)

        ckv_new.append(ckvn[:N_CTX].reshape(BATCH, SEQ, MLA_KV_RANK))
        kr_new.append(kr[:N_CTX, 32:32 + MLA_ROPE].reshape(BATCH, SEQ, MLA_ROPE))
        sf_new.append(sf)
        sb_new.append(sb)

    y_ctx, y_smp = x
    return (y_ctx.reshape(BATCH, SEQ, D_MODEL), y_smp.reshape(DEC_BATCH, DEC_SEQ, D_MODEL),
            jnp.stack(ckv_new, axis=1), jnp.stack(kr_new, axis=1),
            jnp.stack(sf_new, axis=1), jnp.stack(sb_new, axis=1))
```

```python
import functools

import numpy as np
import jax
import jax.numpy as jnp
from jax import lax
from jax.experimental import pallas as pl
from jax.experimental.pallas import tpu as pltpu

F32 = jnp.float32
BF16 = jnp.bfloat16

D_MODEL = 1024
BATCH = 16
SEQ = 256
DEPTH = 2
DEC_BATCH = 4
DEC_SEQ = 1024
PAST_LEN = 512
GRID_W = 64
NORM_EPS = 1e-6
GLA_HEADS = 4
GLA_DK = 64
GLA_DV = 64
GLA_W = 256
GLA_GATE_RANK = 16
GLA_TAU = 16.0
CONV_W = 256
CONV_K = 31
MLA_HEADS = 8
MLA_NOPE = 64
MLA_ROPE = 32
MLA_V = 64
MLA_Q_RANK = 256
MLA_KV_RANK = 128
ROPE_THETA = 10000.0
N_GROUPS = 4
EXPERTS_PER_GROUP = 8
N_EXPERTS = 32
EXPERT_FF = 256

N_CTX = BATCH * SEQ
N_SMP = DEC_BATCH * DEC_SEQ
N_TOK = N_CTX + N_SMP
N_MOD = 8
TM = 512
P_COLS = 2048
MISC_BLK = 15
CHUNK = 64
SUB = 16
NEG = -1e30
VMEM_LIMIT = 56 * 1024 * 1024


def _cparams(sem):
    return pltpu.CompilerParams(dimension_semantics=sem, vmem_limit_bytes=VMEM_LIMIT)


def _dot(a, b):
    return jnp.dot(a, b, preferred_element_type=F32)


def _dot_nt(a, b):
    return lax.dot_general(a, b, (((1,), (1,)), ((), ())), preferred_element_type=F32)


def _split2(x):
    hi = x.astype(BF16)
    lo = (x - hi.astype(F32)).astype(BF16)
    return hi, lo


def _dot3(a, w):
    ah, al = _split2(a)
    wh, wl = _split2(w)
    return _dot(ah, wh) + _dot(al, wh) + _dot(ah, wl)


def _sigmoid(x):
    return 1.0 / (1.0 + jnp.exp(-x))


def _rms(x, w):
    ms = jnp.mean(x * x, axis=-1, keepdims=True)
    return x * lax.rsqrt(ms + NORM_EPS) * w


def _mod_index(i, tile):
    nctx = N_CTX // tile
    per = DEC_SEQ // tile
    return jnp.where(i < nctx, 0, 1 + (i - nctx) // per)


def _ada_kernel(c_ref, w_ref, b_ref, o_ref):
    c = c_ref[...]
    o_ref[...] = _dot3(c * _sigmoid(c), w_ref[...]) + b_ref[...]


def _ada_table(cond8, ada_w, ada_b):
    tn = 1024
    n6 = 6 * D_MODEL
    return pl.pallas_call(
        _ada_kernel,
        grid=(DEPTH, n6 // tn),
        in_specs=[
            pl.BlockSpec((N_MOD, D_MODEL), lambda l, j: (0, 0)),
            pl.BlockSpec((None, D_MODEL, tn), lambda l, j: (l, 0, j)),
            pl.BlockSpec((None, 1, tn), lambda l, j: (l, 0, j)),
        ],
        out_specs=pl.BlockSpec((None, N_MOD, tn), lambda l, j: (l, 0, j)),
        out_shape=jax.ShapeDtypeStruct((DEPTH, N_MOD, n6), F32),
        compiler_params=_cparams(("parallel", "parallel")),
        name="ada_table",
    )(cond8, ada_w, ada_b.reshape(DEPTH, 1, n6))


def _pair_specs(tile, width, split):
    nctx = N_CTX // tile
    offb = 0 if split else nctx
    return [pl.BlockSpec((tile, width), lambda i, *_: (jnp.minimum(i, nctx - 1), 0)),
            pl.BlockSpec((tile, width), lambda i, *_: (jnp.maximum(i - nctx, 0) + offb, 0))]


def _pair(x):
    return x if isinstance(x, tuple) else (x, x)


TM_IN = 512


def _inproj_kernel(xa_ref, xb_ref, mod_ref, nw_ref, w_ref, o_ref):
    mod = mod_ref[...]

    def project(x_ref):
        h = _rms(x_ref[...], nw_ref[...]) * (1.0 + mod[:, D_MODEL:2 * D_MODEL]) + mod[:, 0:D_MODEL]
        o_ref[...] = _dot(h.astype(BF16), w_ref[...])

    is_ctx = pl.program_id(0) < N_CTX // TM_IN
    pl.when(is_ctx)(lambda: project(xa_ref))
    pl.when(jnp.logical_not(is_ctx))(lambda: project(xb_ref))


def _inproj(x, mods4, l, norm_w, w_in_r):
    return pl.pallas_call(
        _inproj_kernel,
        grid=(N_TOK // TM_IN,),
        in_specs=_pair_specs(TM_IN, D_MODEL, isinstance(x, tuple)) + [
            pl.BlockSpec((None, None, 1, 6 * D_MODEL), lambda i: (l, _mod_index(i, TM_IN), 0, 0)),
            pl.BlockSpec((1, D_MODEL), lambda i: (0, 0)),
            pl.BlockSpec((None, D_MODEL, P_COLS), lambda i: (l, 0, 0)),
        ],
        out_specs=pl.BlockSpec((TM_IN, P_COLS), lambda i: (i, 0)),
        out_shape=jax.ShapeDtypeStruct((N_TOK, P_COLS), F32),
        compiler_params=_cparams(("parallel",)),
        name="inproj",
    )(*_pair(x), mods4, norm_w, w_in_r)


def _log_sigmoid(x):
    return jnp.minimum(x, 0.0) - jnp.log(1.0 + jnp.exp(-jnp.abs(x)))


def _split3(x):
    hi = x.astype(BF16)
    r = x - hi.astype(F32)
    mid = r.astype(BF16)
    lo = (r - mid.astype(F32)).astype(BF16)
    return hi, mid, lo


def _gla_consts(reverse):
    t = lax.broadcasted_iota(jnp.int32, (CHUNK, CHUNK), 0)
    s = lax.broadcasted_iota(jnp.int32, (CHUNK, CHUNK), 1)
    mid = jnp.bitwise_and(t, -SUB) + SUB // 2
    if reverse:
        cum = s >= t
        ref = s >= mid
    else:
        cum = s <= t
        ref = s <= mid
    cum_ref = jnp.concatenate([cum, ref], axis=0).astype(F32).astype(BF16)
    rt = jnp.bitwise_and(lax.broadcasted_iota(jnp.int32, (GLA_HEADS * SUB, CHUNK), 0), SUB - 1)
    cs = lax.broadcasted_iota(jnp.int32, (GLA_HEADS * SUB, CHUNK), 1)
    rowi = lax.broadcasted_iota(jnp.int32, (CHUNK, 1), 0)
    causal, valid = [], []
    for i in range(CHUNK // SUB):
        tq = rt + i * SUB
        causal.append((cs >= tq) if reverse else (cs <= tq))
        valid.append((rowi >= i * SUB) if reverse else (rowi < (i + 1) * SUB))
    return cum_ref, causal, valid


def _gla_local(jobs, nc, o_s, qe_s, u_s, dec_s, head_masks, block_mask):
    nsub = CHUNK // SUB
    rows = [pl.ds(pl.multiple_of(c * CHUNK, CHUNK), CHUNK) for c, *_ in jobs]
    bs = []
    for (c, d, qs, k, vb, vt, la_ref, consts), r in zip(jobs, rows):
        h0, h1, h2 = _split3(la_ref[r, :])
        br = _dot(consts[0], h0) + _dot(consts[0], h1) + _dot(consts[0], h2)
        bs.append((br[0:CHUNK], br[CHUNK:2 * CHUNK]))
    ops = []
    for (c, d, qs, k, vb, vt, la_ref, consts), (b, rr) in zip(jobs, bs):
        qhat = qs * jnp.exp(b - rr)
        for i in range(nsub):
            r_i = rr[i * SUB:i * SUB + 1]
            kt = (k * jnp.exp(jnp.where(consts[2][i], r_i - b, NEG))).astype(BF16)
            qi = qhat[i * SUB:(i + 1) * SUB]
            ops.append((jnp.concatenate([jnp.where(m, qi, 0.0) for m in head_masks], axis=0).astype(BF16), kt))
    scores = [_dot_nt(qbig, kt) for qbig, kt in ops]
    outs = []
    for j, (c, d, qs, k, vb, vt, la_ref, consts) in enumerate(jobs):
        for i in range(nsub):
            a = jnp.where(consts[1][i], scores[j * nsub + i], 0.0)
            outs.append(_dot(a.astype(BF16), vb))
    for j, ((c, d, qs, k, vb, vt, la_ref, consts), (b, rr), r) in enumerate(zip(jobs, bs, rows)):
        parts = []
        for i in range(nsub):
            ov = outs[j * nsub + i]
            oi = jnp.where(head_masks[0], ov[0:SUB], 0.0)
            for h in range(1, GLA_HEADS):
                oi = oi + jnp.where(head_masks[h], ov[h * SUB:(h + 1) * SUB], 0.0)
            parts.append(oi)
        o_s[d, r, :] = jnp.concatenate(parts, axis=0)
        qe_s[d, r, :] = (qs * jnp.exp(b)).astype(BF16)
        b_last = b[0:1] if d == 1 else b[CHUNK - 1:CHUNK]
        kl = (k * jnp.exp(b_last - b)).astype(BF16)
        idx = d * nc + c
        u_s[idx] = jnp.where(block_mask, _dot(vt, kl), 0.0)
        dec_s[pl.ds(pl.multiple_of(idx * 8, 8), 8), :] = jnp.broadcast_to(jnp.exp(b_last), (8, GLA_W))


GLA_SAFE_EXPONENT = 60.0


def _gla_exact_intra(d, nc, qkvg_ref, la_ref, cum, o_s, b_s, acc_s, ones_blk):
    reverse = d == 1
    tcol = lax.broadcasted_iota(jnp.int32, (CHUNK, 1), 0)

    def chunk(c, carry):
        r0 = pl.multiple_of(c * CHUNK, CHUNK)
        rows = pl.ds(r0, CHUNK)
        h0, h1, h2 = _split3(la_ref[rows, :])
        b_s[...] = _dot(cum, h0) + _dot(cum, h1) + _dot(cum, h2)
        acc_s[...] = jnp.zeros((CHUNK, GLA_W), F32)
        qs = qkvg_ref[rows, 0:GLA_W] * (GLA_DK ** -0.5)

        def key(s, carry2):
            bs = b_s[pl.ds(s, 1), :]
            ks = qkvg_ref[pl.ds(r0 + s, 1), GLA_W:2 * GLA_W]
            vs = qkvg_ref[pl.ds(r0 + s, 1), 2 * GLA_W:3 * GLA_W]
            live = (tcol <= s) if reverse else (tcol >= s)
            hi, lo = _split2(qs * ks * jnp.exp(jnp.where(live, b_s[...] - bs, NEG)))
            acc_s[...] += (_dot(hi, ones_blk) + _dot(lo, ones_blk)) * vs
            return carry2

        lax.fori_loop(0, CHUNK, key, 0)
        o_s[d, rows, :] = acc_s[...]
        return carry

    lax.fori_loop(0, nc, chunk, 0)


def _gla_kernel(*refs, n, has_state):
    if has_state:
        (qkvg_ref, misc_ref, wgf_ref, wgb_ref, bgf_ref, bgb_ref, nw_ref, s0f_ref, s0b_ref,
         o_ref, laf_s, lab_s, o_s, qe_s, u_s, dec_s, sts_s, stf_s, stb_s, b_s, acc_s) = refs
    else:
        (qkvg_ref, misc_ref, wgf_ref, wgb_ref, bgf_ref, bgb_ref, nw_ref,
         o_ref, sf_ref, sb_ref, laf_s, lab_s, o_s, qe_s, u_s, dec_s, sts_s, stf_s, stb_s, b_s, acc_s) = refs
    misc = misc_ref[...]
    laf_s[...] = _log_sigmoid(_dot3(misc, wgf_ref[...]) + bgf_ref[...]) * (1.0 / GLA_TAU)
    lab_s[...] = _log_sigmoid(_dot3(misc, wgb_ref[...]) + bgb_ref[...]) * (1.0 / GLA_TAU)
    stf_s[...] = jnp.zeros((GLA_W, GLA_W), F32)
    stb_s[...] = jnp.zeros((GLA_W, GLA_W), F32)
    if has_state:
        for st_s, s0_ref in ((stf_s, s0f_ref), (stb_s, s0b_ref)):
            for h in range(GLA_HEADS):
                st_s[h * GLA_DK:(h + 1) * GLA_DK, h * GLA_DV:(h + 1) * GLA_DV] = s0_ref[h]
            st_s[...] = st_s[...].T

    lane = lax.broadcasted_iota(jnp.int32, (1, GLA_W), 1)
    head_masks = [jnp.right_shift(lane, 6) == h for h in range(GLA_HEADS)]
    bi = jnp.right_shift(lax.broadcasted_iota(jnp.int32, (GLA_W, GLA_W), 0), 6)
    bj = jnp.right_shift(lax.broadcasted_iota(jnp.int32, (GLA_W, GLA_W), 1), 6)
    block_mask = bi == bj
    consts_f = _gla_consts(False)
    consts_b = _gla_consts(True)
    nc = n // CHUNK

    def local(cc, carry):
        jobs = []
        for c in (2 * cc, 2 * cc + 1):
            rows = pl.ds(pl.multiple_of(c * CHUNK, CHUNK), CHUNK)
            qs = qkvg_ref[rows, 0:GLA_W] * (GLA_DK ** -0.5)
            k = qkvg_ref[rows, GLA_W:2 * GLA_W]
            v = qkvg_ref[rows, 2 * GLA_W:3 * GLA_W]
            vb = v.astype(BF16)
            vt = v.T.astype(BF16)
            jobs += [(c, 0, qs, k, vb, vt, laf_s, consts_f), (c, 1, qs, k, vb, vt, lab_s, consts_b)]
        _gla_local(jobs, nc, o_s, qe_s, u_s, dec_s, head_masks, block_mask)
        return carry

    lax.fori_loop(0, nc // 2, local, 0)

    ones_blk = block_mask.astype(F32).astype(BF16)
    @pl.when(jnp.min(jnp.minimum(laf_s[...], lab_s[...])) * (SUB // 2) < -GLA_SAFE_EXPONENT)
    def _():
        for d, (la_s, consts) in enumerate(((laf_s, consts_f), (lab_s, consts_b))):
            _gla_exact_intra(d, nc, qkvg_ref, la_s, consts[0][0:CHUNK], o_s, b_s, acc_s, ones_blk)

    def scan(j, carry):
        for d, st_s in enumerate((stf_s, stb_s)):
            idx = d * nc + (j if d == 0 else nc - 1 - j)
            st = st_s[...]
            sts_s[idx] = st.astype(BF16)
            st_s[...] = st * dec_s[pl.ds(pl.multiple_of(idx * 8, 8), 1), :] + u_s[idx]
        return carry

    lax.fori_loop(0, nc, scan, 0)

    def inter(cc, carry):
        jobs = [(c, d, pl.ds(pl.multiple_of(c * CHUNK, CHUNK), CHUNK))
                for c in (2 * cc, 2 * cc + 1) for d in range(2)]
        res = [_dot_nt(qe_s[d, r, :], sts_s[d * nc + c]) for c, d, r in jobs]
        for (c, d, r), o in zip(jobs, res):
            o_s[d, r, :] += o
        return carry

    lax.fori_loop(0, nc // 2, inter, 0)

    if not has_state:
        for st_s, s_ref in ((stf_s, sf_ref), (stb_s, sb_ref)):
            s = st_s[...].T
            for h in range(GLA_HEADS):
                s_ref[h] = s[h * GLA_DK:(h + 1) * GLA_DK, h * GLA_DV:(h + 1) * GLA_DV]

    rt = 256
    for r0 in range(0, n, rt):
        o = o_s[0, r0:r0 + rt, :] + o_s[1, r0:r0 + rt, :]
        hi, lo = _split2(o * o)
        ms = (_dot(hi, ones_blk) + _dot(lo, ones_blk)) * (1.0 / GLA_DV)
        g = qkvg_ref[r0:r0 + rt, 3 * GLA_W:4 * GLA_W]
        o_ref[r0:r0 + rt, :] = o * lax.rsqrt(ms + NORM_EPS) * nw_ref[...] * (g * _sigmoid(g))


def _gla(p, wgf, wgb, bgf, bgb, nw, n, nb, row_blk0, s0f=None, s0b=None):
    has_state = s0f is not None
    full = lambda shape: pl.BlockSpec(shape, lambda b: (0,) * len(shape))
    in_specs = [
        pl.BlockSpec((n, 4 * GLA_W), lambda b: (row_blk0 + b, 0)),
        pl.BlockSpec((n, 128), lambda b: (row_blk0 + b, MISC_BLK)),
        full((128, GLA_W)), full((128, GLA_W)), full((1, GLA_W)), full((1, GLA_W)), full((1, GLA_W)),
    ]
    args = [p, p, wgf, wgb, bgf, bgb, nw]
    out_specs = [pl.BlockSpec((n, GLA_W), lambda b: (b, 0))]
    out_shape = [jax.ShapeDtypeStruct((nb * n, GLA_W), F32)]
    if has_state:
        layer = s0f[1]
        st_spec = pl.BlockSpec((None, None, GLA_HEADS, GLA_DK, GLA_DV), lambda b: (b, layer, 0, 0, 0))
        in_specs += [st_spec, st_spec]
        args += [s0f[0], s0b[0]]
    else:
        st_spec = pl.BlockSpec((None, GLA_HEADS, GLA_DK, GLA_DV), lambda b: (b, 0, 0, 0))
        out_specs += [st_spec, st_spec]
        out_shape += [jax.ShapeDtypeStruct((nb, GLA_HEADS, GLA_DK, GLA_DV), F32)] * 2
    return pl.pallas_call(
        functools.partial(_gla_kernel, n=n, has_state=has_state),
        grid=(nb,),
        in_specs=in_specs,
        out_specs=out_specs,
        out_shape=out_shape,
        scratch_shapes=[
            pltpu.VMEM((n, GLA_W), F32), pltpu.VMEM((n, GLA_W), F32),
            pltpu.VMEM((2, n, GLA_W), F32),
            pltpu.VMEM((2, n, GLA_W), BF16),
            pltpu.VMEM((2 * n // CHUNK, GLA_W, GLA_W), F32),
            pltpu.VMEM((2 * n // CHUNK * 8, GLA_W), F32),
            pltpu.VMEM((2 * n // CHUNK, GLA_W, GLA_W), BF16),
            pltpu.VMEM((GLA_W, GLA_W), F32), pltpu.VMEM((GLA_W, GLA_W), F32),
            pltpu.VMEM((CHUNK, GLA_W), F32), pltpu.VMEM((CHUNK, GLA_W), F32),
        ],
        compiler_params=_cparams(("parallel",)),
        name="gla_state" if has_state else "gla_ctx",
    )(*args)


CONV_PAD = 16
CONV_ROWS = 128
CONV_SHIFT_ROWS = 24


def _conv_kernel(u_ref, w_ref, b_ref, lnw_ref, lnb_ref, o_ref, zp_s, zsh_s, *, n):
    u = u_ref[...]
    zp_s[0:CONV_PAD, :] = jnp.zeros((CONV_PAD, CONV_W), F32)
    zp_s[CONV_PAD + n:2 * CONV_PAD + n, :] = jnp.zeros((CONV_PAD, CONV_W), F32)
    zp_s[CONV_PAD:CONV_PAD + n, :] = u[:, 0:CONV_W] * _sigmoid(u[:, CONV_W:2 * CONV_W])
    m = n + CONV_SHIFT_ROWS
    for s in range(1, 8):
        zsh_s[s - 1] = zp_s[s:s + m, :]
    off = CONV_PAD - CONV_K // 2

    def tap(r0, j):
        s = (off + j) % 8
        base = r0 + off + j - s
        return zp_s[base:base + CONV_ROWS, :] if s == 0 else zsh_s[s - 1, base:base + CONV_ROWS, :]

    for r0 in range(0, n, CONV_ROWS):
        acc = tap(r0, 0) * w_ref[0:1, :]
        for j in range(1, CONV_K):
            acc = acc + tap(r0, j) * w_ref[j:j + 1, :]
        acc = acc + b_ref[...]
        mu = jnp.mean(acc, axis=-1, keepdims=True)
        xc = acc - mu
        var = jnp.mean(xc * xc, axis=-1, keepdims=True)
        y = xc * lax.rsqrt(var + NORM_EPS) * lnw_ref[...] + lnb_ref[...]
        o_ref[r0:r0 + CONV_ROWS, :] = y * _sigmoid(y)


def _conv(p, cw, cb, lnw, lnb, n, nb, row_blk0):
    full = lambda shape: pl.BlockSpec(shape, lambda b: (0,) * len(shape))
    return pl.pallas_call(
        functools.partial(_conv_kernel, n=n),
        grid=(nb,),
        in_specs=[
            pl.BlockSpec((n, 2 * CONV_W), lambda b: (row_blk0 + b, 2)),
            full((32, CONV_W)), full((1, CONV_W)), full((1, CONV_W)), full((1, CONV_W)),
        ],
        out_specs=pl.BlockSpec((n, CONV_W), lambda b: (b, 0)),
        out_shape=jax.ShapeDtypeStruct((nb * n, CONV_W), F32),
        scratch_shapes=[pltpu.VMEM((n + 2 * CONV_PAD, CONV_W), F32),
                        pltpu.VMEM((7, n + CONV_SHIFT_ROWS, CONV_W), F32)],
        compiler_params=_cparams(("parallel",)),
        name="conv%d" % n,
    )(p, cw, cb, lnw, lnb)


HEAD_BLK = 128


def _kv_expand(ckvn, t, wk_ref, wp_ref, wv_ref, k_ref, v_ref):
    cb = ckvn.astype(BF16)
    th, tl = _split2(t)
    k_ref[...] = (_dot(cb, wk_ref[...]) + _dot(th, wp_ref[...]) + _dot(tl, wp_ref[...])).astype(BF16)
    v_ref[...] = _dot(cb, wv_ref[...]).astype(BF16)


def _mla_prep_kernel(cq_ref, cm_ref, tqc_ref, tqs_ref, tk_ref, qnw_ref, kvw_ref, wa_ref, wb_ref,
                     wk_ref, wp_ref, wv_ref, q_ref, k_ref, v_ref, ckv_ref, kr_ref):
    qn = _rms(cq_ref[...], qnw_ref[...]).astype(BF16)
    tqc = jnp.concatenate([tqc_ref[...]] * MLA_HEADS, axis=1)
    tqs = jnp.concatenate([tqs_ref[...]] * MLA_HEADS, axis=1)
    scale = (MLA_NOPE + MLA_ROPE) ** -0.5
    q_ref[...] = ((_dot(qn, wa_ref[...]) * tqc + _dot(qn, wb_ref[...]) * tqs) * scale).astype(BF16)
    cm = cm_ref[...]
    ckvn = _rms(cm[:, 0:MLA_KV_RANK], kvw_ref[...])
    ckv_ref[...] = ckvn
    t = cm[:, MLA_KV_RANK:2 * MLA_KV_RANK] * tk_ref[...]
    kr_ref[...] = t
    _kv_expand(ckvn, t, wk_ref, wp_ref, wv_ref, k_ref, v_ref)


def _rope_tile(i):
    nctx = N_CTX // TM
    return jnp.where(i < nctx, 0, 1 + (i - nctx) % (DEC_SEQ // TM))


def _mla_prep(p, tabs, qnw, kvw, wa, wb, wk, wp, wv):
    tqc, tqs, tk = tabs
    full = lambda shape: pl.BlockSpec(shape, lambda i: (0,) * len(shape))
    tab_spec = pl.BlockSpec((TM, HEAD_BLK), lambda i: (_rope_tile(i), 0))
    hw = MLA_HEADS * HEAD_BLK
    return pl.pallas_call(
        _mla_prep_kernel,
        grid=(N_TOK // TM,),
        in_specs=[
            pl.BlockSpec((TM, MLA_Q_RANK), lambda i: (i, 6)),
            pl.BlockSpec((TM, 2 * MLA_KV_RANK), lambda i: (i, 7)),
            tab_spec, tab_spec, tab_spec,
            full((1, MLA_Q_RANK)), full((1, MLA_KV_RANK)),
            full((MLA_Q_RANK, hw)), full((MLA_Q_RANK, hw)),
            full((MLA_KV_RANK, hw)), full((128, hw)), full((MLA_KV_RANK, MLA_HEADS * MLA_V)),
        ],
        out_specs=[
            pl.BlockSpec((TM, hw), lambda i: (i, 0)),
            pl.BlockSpec((TM, hw), lambda i: (i, 0)),
            pl.BlockSpec((TM, MLA_HEADS * MLA_V), lambda i: (i, 0)),
            pl.BlockSpec((TM, MLA_KV_RANK), lambda i: (i, 0)),
            pl.BlockSpec((TM, 128), lambda i: (i, 0)),
        ],
        out_shape=[
            jax.ShapeDtypeStruct((N_TOK, hw), BF16),
            jax.ShapeDtypeStruct((N_TOK, hw), BF16),
            jax.ShapeDtypeStruct((N_TOK, MLA_HEADS * MLA_V), BF16),
            jax.ShapeDtypeStruct((N_TOK, MLA_KV_RANK), F32),
            jax.ShapeDtypeStruct((N_TOK, 128), F32),
        ],
        compiler_params=_cparams(("parallel",)),
        name="mla_prep",
    )(p, p, tqc, tqs, tk, qnw, kvw, wa, wb, wk, wp, wv)


def _cache_kv_kernel(ckv_ref, kr_ref, wk_ref, wp_ref, wv_ref, k_ref, v_ref):
    _kv_expand(ckv_ref[...], kr_ref[...], wk_ref, wp_ref, wv_ref, k_ref, v_ref)


def _cache_kv(ckv, kr, wk, wp, wv):
    n = DEC_BATCH * PAST_LEN
    full = lambda shape: pl.BlockSpec(shape, lambda i: (0,) * len(shape))
    hw = MLA_HEADS * HEAD_BLK
    return pl.pallas_call(
        _cache_kv_kernel,
        grid=(n // PAST_LEN,),
        in_specs=[
            pl.BlockSpec((PAST_LEN, MLA_KV_RANK), lambda i: (i, 0)),
            pl.BlockSpec((PAST_LEN, 128), lambda i: (i, 0)),
            full((MLA_KV_RANK, hw)), full((128, hw)), full((MLA_KV_RANK, MLA_HEADS * MLA_V)),
        ],
        out_specs=[
            pl.BlockSpec((PAST_LEN, hw), lambda i: (i, 0)),
            pl.BlockSpec((PAST_LEN, MLA_HEADS * MLA_V), lambda i: (i, 0)),
        ],
        out_shape=[
            jax.ShapeDtypeStruct((n, hw), BF16),
            jax.ShapeDtypeStruct((n, MLA_HEADS * MLA_V), BF16),
        ],
        compiler_params=_cparams(("parallel",)),
        name="cache_kv",
    )(ckv, kr, wk, wp, wv)


def _attn_kernel(*refs, has_cache):
    if has_cache:
        q_ref, k_ref, v_ref, kc_ref, vc_ref, o_ref = refs
    else:
        q_ref, k_ref, v_ref, o_ref = refs
    nh = q_ref.shape[1] // HEAD_BLK
    cols = [slice(h * HEAD_BLK, (h + 1) * HEAD_BLK) for h in range(nh)]
    vcols = [slice((h // 2) * 2 * MLA_V, (h // 2 + 1) * 2 * MLA_V) for h in range(nh)]
    s = [_dot_nt(q_ref[:, c], k_ref[:, c]) for c in cols]
    m = [jnp.max(x, axis=-1, keepdims=True) for x in s]
    if has_cache:
        sc = [_dot_nt(q_ref[:, c], kc_ref[:, c]) for c in cols]
        m = [jnp.maximum(a, jnp.max(x, axis=-1, keepdims=True)) for a, x in zip(m, sc)]
        pc = [jnp.exp(x - a) for x, a in zip(sc, m)]
    p = [jnp.exp(x - a) for x, a in zip(s, m)]
    den = [jnp.sum(x, axis=-1, keepdims=True) for x in p]
    o = [_dot(x.astype(BF16), v_ref[:, c]) for x, c in zip(p, vcols)]
    if has_cache:
        den = [a + jnp.sum(x, axis=-1, keepdims=True) for a, x in zip(den, pc)]
        o = [a + _dot(x.astype(BF16), vc_ref[:, c]) for a, x, c in zip(o, pc, vcols)]
    outs = [a / b for a, b in zip(o, den)]
    lane = lax.broadcasted_iota(jnp.int32, (1, 2 * MLA_V), 1)
    for j in range(nh // 2):
        o_ref[:, j * 2 * MLA_V:(j + 1) * 2 * MLA_V] = jnp.where(lane < MLA_V, outs[2 * j], outs[2 * j + 1])


def _attention(q, k, v, n, nb, row0, tq, npair, kc=None, vc=None):
    has_cache = kc is not None
    nq = n // tq
    rb0 = row0 // tq
    kb0 = row0 // n
    in_specs = [
        pl.BlockSpec((tq, npair * 2 * HEAD_BLK), lambda b, h, i: (rb0 + b * nq + i, h)),
        pl.BlockSpec((n, npair * 2 * HEAD_BLK), lambda b, h, i: (kb0 + b, h)),
        pl.BlockSpec((n, npair * 2 * MLA_V), lambda b, h, i: (kb0 + b, h)),
    ]
    args = [q, k, v]
    if has_cache:
        in_specs += [
            pl.BlockSpec((PAST_LEN, npair * 2 * HEAD_BLK), lambda b, h, i: (b, h)),
            pl.BlockSpec((PAST_LEN, npair * 2 * MLA_V), lambda b, h, i: (b, h)),
        ]
        args += [kc, vc]
    return pl.pallas_call(
        functools.partial(_attn_kernel, has_cache=has_cache),
        grid=(nb, MLA_HEADS // (2 * npair), nq),
        in_specs=in_specs,
        out_specs=pl.BlockSpec((tq, npair * 2 * MLA_V), lambda b, h, i: (b * nq + i, h)),
        out_shape=jax.ShapeDtypeStruct((nb * n, MLA_HEADS * MLA_V), F32),
        compiler_params=_cparams(("parallel", "parallel", "parallel")),
        name="attn_cache" if has_cache else "attn_ctx",
    )(*args)


def _outproj_kernel(xa_ref, xb_ref, oga_ref, ogb_ref, oca_ref, ocb_ref, oma_ref, omb_ref, mod_ref, nw_ref,
                    wo_ref, wr_ref, br_ref, xo_ref, h_ref, route_ref, cnt_ref, carry_s, wrh_s, wrl_s):
    @pl.when(pl.program_id(0) == 0)
    def _():
        carry_s[...] = jnp.zeros(carry_s.shape, F32)
        wrh, wrl = _split2(wr_ref[...])
        wrh_s[...] = wrh
        wrl_s[...] = wrl

    mod = mod_ref[...]

    def residual(x_ref, og_ref, oc_ref, om_ref):
        attn = (_dot(og_ref[...].astype(BF16), wo_ref[0:GLA_W, :])
                + _dot(oc_ref[...].astype(BF16), wo_ref[GLA_W:GLA_W + CONV_W, :])
                + _dot(om_ref[...].astype(BF16), wo_ref[GLA_W + CONV_W:D_MODEL, :]))
        xo_ref[...] = x_ref[...] + mod[:, 2 * D_MODEL:3 * D_MODEL] * attn

    is_ctx = pl.program_id(0) < N_CTX // TM
    pl.when(is_ctx)(lambda: residual(xa_ref, oga_ref, oca_ref, oma_ref))
    pl.when(jnp.logical_not(is_ctx))(lambda: residual(xb_ref, ogb_ref, ocb_ref, omb_ref))
    xn = xo_ref[...]
    h = _rms(xn, nw_ref[...]) * (1.0 + mod[:, 4 * D_MODEL:5 * D_MODEL]) + mod[:, 3 * D_MODEL:4 * D_MODEL]
    _rows_to_tiles(h_ref, h)
    hh, hl = _split2(h)
    logits = _dot(hh, wrh_s[...]) + _dot(hl, wrh_s[...]) + _dot(hh, wrl_s[...]) + br_ref[...]
    lane = lax.broadcasted_iota(jnp.int32, logits.shape, 1).astype(F32)
    big = 1e9
    gl = jnp.where((lane >= N_EXPERTS) & (lane < N_EXPERTS + N_GROUPS), logits, NEG)
    gmax = jnp.max(gl, axis=-1, keepdims=True)
    gw = 1.0 / jnp.sum(jnp.exp(gl - gmax), axis=-1, keepdims=True)
    gi = jnp.min(jnp.where(gl == gmax, lane, big), axis=-1, keepdims=True) - N_EXPERTS
    lo = gi * EXPERTS_PER_GROUP
    el = jnp.where((lane >= lo) & (lane < lo + EXPERTS_PER_GROUP), logits, NEG)
    m1 = jnp.max(el, axis=-1, keepdims=True)
    i1 = jnp.min(jnp.where(el == m1, lane, big), axis=-1, keepdims=True)
    el2 = jnp.where(lane == i1, NEG, el)
    m2 = jnp.max(el2, axis=-1, keepdims=True)
    i2 = jnp.min(jnp.where(el2 == m2, lane, big), axis=-1, keepdims=True)
    e2 = jnp.exp(m2 - m1)
    p1 = 1.0 / (1.0 + e2)
    onehot = jnp.where((lane == i1) | (lane == i2), 1.0, 0.0)
    tr = lax.broadcasted_iota(jnp.int32, (TM, TM), 0)
    tc = lax.broadcasted_iota(jnp.int32, (TM, TM), 1)
    before = (tc < tr).astype(F32).astype(BF16)
    seen = _dot(before, onehot.astype(BF16)) + carry_s[0:1, :]
    rank1 = jnp.sum(jnp.where(lane == i1, seen, 0.0), axis=-1, keepdims=True)
    rank2 = jnp.sum(jnp.where(lane == i2, seen, 0.0), axis=-1, keepdims=True)
    carry_s[...] = carry_s[...] + jnp.sum(onehot, axis=0, keepdims=True)
    cnt_ref[...] = carry_s[...]
    cols = (i1, i2, gw * p1, gw * (e2 * p1), rank1, rank2)
    route = jnp.zeros(logits.shape, F32)
    for j, col in enumerate(cols):
        route = jnp.where(lane == j, col, route)
    route_ref[...] = route


def _outproj(x, og, oc, om, mods4, l, norm_w, wo, wr, br):
    full = lambda shape: pl.BlockSpec(shape, lambda i: (0,) * len(shape))
    row = lambda w: pl.BlockSpec((TM, w), lambda i: (i, 0))
    return pl.pallas_call(
        _outproj_kernel,
        grid=(N_TOK // TM,),
        in_specs=(_pair_specs(TM, D_MODEL, isinstance(x, tuple)) + _pair_specs(TM, GLA_W, True)
                  + _pair_specs(TM, CONV_W, True) + _pair_specs(TM, MLA_HEADS * MLA_V, True) + [
            pl.BlockSpec((None, None, 1, 6 * D_MODEL), lambda i: (l, _mod_index(i, TM), 0, 0)),
            full((1, D_MODEL)), full((D_MODEL, D_MODEL)), full((D_MODEL, 128)), full((1, 128)),
        ]),
        out_specs=[row(D_MODEL), pl.BlockSpec(_tiled(TM), lambda i: (i, 0)), row(128), full((8, 128))],
        out_shape=[
            jax.ShapeDtypeStruct((N_TOK, D_MODEL), F32),
            jax.ShapeDtypeStruct(_tiled(N_TOK), F32),
            jax.ShapeDtypeStruct((N_TOK, 128), F32),
            jax.ShapeDtypeStruct((8, 128), F32),
        ],
        scratch_shapes=[pltpu.VMEM((8, 128), F32), pltpu.VMEM((D_MODEL, 128), BF16),
                        pltpu.VMEM((D_MODEL, 128), BF16)],
        compiler_params=_cparams(("arbitrary",)),
        name="outproj_router",
    )(*_pair(x), *og, *oc, *om, mods4, norm_w, wo, wr, br)


N_ASSIGN = 2 * N_TOK
TE = 256
N_ETILE = N_ASSIGN // TE
N_WORK = N_ETILE + N_EXPERTS


ROW_SUB = 8


def _tiled(n):
    return (n * ROW_SUB, D_MODEL // ROW_SUB)


def _rows_to_tiles(ref, x, r0=0):
    n = x.shape[0]
    for s in range(ROW_SUB):
        ref[pl.ds(r0 * ROW_SUB + s, n, stride=ROW_SUB), :] = x[:, s * 128:(s + 1) * 128]


def _tiles_to_rows(ref, r0=0, n=None):
    n = ref.shape[0] // ROW_SUB if n is None else n
    return jnp.concatenate([ref[pl.ds(r0 * ROW_SUB + s, n, stride=ROW_SUB), :] for s in range(ROW_SUB)], axis=1)


def _row_tile(ref, t):
    start = t * ROW_SUB if isinstance(t, int) else pl.multiple_of(t * ROW_SUB, ROW_SUB)
    return ref.at[pl.ds(start, ROW_SUB)]


def _lane_col(x, lane, j):
    return jnp.sum(jnp.where(lane == j, x, 0.0), axis=-1, keepdims=True)


def _plan_kernel(cnt_s, cnt_ref, route_ref, pos_ref, wt_ref, we_ref, wlo_ref, whi_ref, wf_ref, wr_ref, wn_ref,
                 off_s):
    cnt = cnt_ref[...]
    chi = jnp.floor(cnt * (1.0 / 128.0))
    clo = cnt - chi * 128.0
    r = lax.broadcasted_iota(jnp.int32, (128, 128), 0)
    c = lax.broadcasted_iota(jnp.int32, (128, 128), 1)
    below = (r < c).astype(F32).astype(BF16)
    off = (128.0 * _dot(chi.astype(BF16), below) + _dot(clo.astype(BF16), below))[0:1, :]
    route = route_ref[...]
    lane = lax.broadcasted_iota(jnp.int32, route.shape, 1).astype(F32)
    pos1 = _lane_col(off, lane, _lane_col(route, lane, 0.0)) + _lane_col(route, lane, 4.0)
    pos2 = _lane_col(off, lane, _lane_col(route, lane, 1.0)) + _lane_col(route, lane, 5.0)
    pos_ref[...] = jnp.where(lane == 0.0, pos1, jnp.where(lane == 1.0, pos2, 0.0))

    @pl.when(pl.program_id(0) == 0)
    def _():
        def offs(e, acc):
            off_s[e] = acc
            return acc + cnt_s[e]

        off_s[N_EXPERTS] = lax.fori_loop(0, N_EXPERTS, offs, jnp.int32(0))

        def item(w, st):
            j, e, first = st
            active = j < N_ETILE
            jj = jnp.minimum(j, N_ETILE - 1)
            ee = jnp.minimum(e, N_EXPERTS - 1)
            tlo = jj * TE
            thi = tlo + TE
            end = off_s[ee + 1]
            wt_ref[w] = jj
            we_ref[w] = ee
            wlo_ref[w] = jnp.where(active, jnp.maximum(off_s[ee], tlo) - tlo, 0)
            whi_ref[w] = jnp.where(active, jnp.minimum(end, thi) - tlo, 0)
            wf_ref[w] = jnp.where(active, first, 0)
            adv_j = (active & (end >= thi)).astype(jnp.int32)
            adv_e = (active & (end <= thi)).astype(jnp.int32)
            return j + adv_j, e + adv_e, adv_j

        lax.fori_loop(0, N_WORK, item, (jnp.int32(0), jnp.int32(0), jnp.int32(1)))

        def run(w, r):
            r = r + (we_ref[w] != we_ref[jnp.maximum(w - 1, 0)]).astype(jnp.int32)
            wr_ref[w] = r
            return r

        lax.fori_loop(0, N_WORK, run, jnp.int32(0))

        def following(j, nxt):
            w = N_WORK - 1 - j
            wn_ref[w] = nxt
            return jnp.where(we_ref[jnp.maximum(w - 1, 0)] != we_ref[w], we_ref[w], nxt)

        lax.fori_loop(0, N_WORK, following, jnp.int32(-1))


def _plan(cnt_i, cnt, route):
    smem = pl.BlockSpec(memory_space=pltpu.SMEM)
    work = jax.ShapeDtypeStruct((N_WORK,), jnp.int32)
    return pl.pallas_call(
        _plan_kernel,
        grid=(1,),
        in_specs=[smem, pl.BlockSpec((8, 128), lambda i: (0, 0)), pl.BlockSpec((N_TOK, 128), lambda i: (0, 0))],
        out_specs=[pl.BlockSpec((N_TOK, 128), lambda i: (0, 0))] + [smem] * 7,
        out_shape=[jax.ShapeDtypeStruct((N_TOK, 128), F32)] + [work] * 7,
        scratch_shapes=[pltpu.SMEM((N_EXPERTS + 1,), jnp.int32)],
        compiler_params=_cparams(("arbitrary",)),
        name="moe_plan",
    )(cnt_i, cnt, route)


def _dispatch_kernel(pos_s, h_ref, xs_hbm, sem):
    base = pl.program_id(0) * TM

    def copy(i, k):
        return pltpu.make_async_copy(_row_tile(h_ref, i), _row_tile(xs_hbm, pos_s[2 * (base + i) + k]), sem)

    def wait(i, carry):
        copy(i, 0).wait()
        copy(i, 1).wait()
        return carry

    for i in range(TM):
        copy(i, 0).start(priority=0)
        copy(i, 1).start(priority=1)
    lax.fori_loop(0, TM, wait, 0, unroll=8)


def _dispatch(pos, h):
    any_spec = pl.BlockSpec(memory_space=pl.ANY)
    return pl.pallas_call(
        _dispatch_kernel,
        grid_spec=pltpu.PrefetchScalarGridSpec(
            num_scalar_prefetch=1, grid=(N_TOK // TM,),
            in_specs=[pl.BlockSpec(_tiled(TM), lambda i, p: (i, 0))], out_specs=any_spec,
            scratch_shapes=[pltpu.SemaphoreType.DMA(())]),
        out_shape=jax.ShapeDtypeStruct(_tiled(N_ASSIGN), F32),
        compiler_params=_cparams(("arbitrary",)),
        name="moe_dispatch",
    )(pos, h)


def _experts_kernel(wt_s, we_s, wlo_s, whi_s, wf_s, wr_s, wn_s, xs_ref, w1_hbm, w3_hbm, w2_hbm, o_ref,
                    w1_s, w3_s, w2_s, f1_s, f3_s, f2_s, sem, *, layer):
    w = pl.program_id(0)
    first = wf_s[w] == 1
    nonempty = whi_s[w] > wlo_s[w]

    def fetch(e, slot):
        return [pltpu.make_async_copy(src.at[layer, e], dst.at[slot], sem.at[slot])
                for src, dst in ((w1_hbm, f1_s), (w3_hbm, f3_s), (w2_hbm, f2_s))]

    @pl.when(w == 0)
    def _():
        for c in fetch(we_s[0], wr_s[0] % 2):
            c.start()

    @pl.when((w == 0) | (we_s[w] != we_s[jnp.maximum(w - 1, 0)]))
    def _():
        slot = wr_s[w] % 2
        for c in fetch(we_s[w], slot):
            c.wait()
        w1_s[...] = f1_s[slot].astype(BF16)
        w3_s[...] = f3_s[slot].astype(BF16)
        w2_s[...] = f2_s[slot].astype(BF16)

        @pl.when(wn_s[w] >= 0)
        def _():
            for c in fetch(wn_s[w], 1 - slot):
                c.start()

    @pl.when(first & jnp.logical_not(nonempty))
    def _():
        o_ref[...] = jnp.zeros(o_ref.shape, F32)

    nsplit = 2
    hrows = TE // nsplit

    def expert_rows(groups, merge):
        x = [_tiles_to_rows(xs_ref, j * hrows, hrows).astype(BF16) for j in groups]
        a = [_dot(v, w1_s[...]) for v in x]
        b = [_dot(v, w3_s[...]) for v in x]
        hid = [(u * _sigmoid(u) * v).astype(BF16) for u, v in zip(a, b)]
        y = [_dot(v, w2_s[...]) for v in hid]
        for j, yj in zip(groups, y):
            row = lax.broadcasted_iota(jnp.int32, (hrows, 1), 0) + j * hrows
            mine = (row >= wlo_s[w]) & (row < whi_s[w])
            other = _tiles_to_rows(o_ref, j * hrows, hrows) if merge else 0.0
            _rows_to_tiles(o_ref, jnp.where(mine, yj, other), j * hrows)
        if not merge:
            for j in set(range(nsplit)) - set(groups):
                _rows_to_tiles(o_ref, jnp.zeros((hrows, D_MODEL), F32), j * hrows)

    need = [(wlo_s[w] < (j + 1) * hrows) & (whi_s[w] > j * hrows) for j in range(nsplit)]
    cases = (((0, 1), need[0] & need[1]), ((0,), need[0] & jnp.logical_not(need[1])),
             ((1,), need[1] & jnp.logical_not(need[0])))
    for groups, cond in cases:
        pl.when(nonempty & cond & first)(functools.partial(expert_rows, groups, False))
        pl.when(nonempty & cond & jnp.logical_not(first))(functools.partial(expert_rows, groups, True))


def _experts(work, xs, l, w1, w3, w2):
    tile = pl.BlockSpec(_tiled(TE), lambda w, wt, *_: (wt[w], 0))
    any_spec = pl.BlockSpec(memory_space=pl.ANY)
    return pl.pallas_call(
        functools.partial(_experts_kernel, layer=l),
        grid_spec=pltpu.PrefetchScalarGridSpec(
            num_scalar_prefetch=7, grid=(N_WORK,),
            in_specs=[tile, any_spec, any_spec, any_spec],
            out_specs=tile,
            scratch_shapes=[pltpu.VMEM((D_MODEL, EXPERT_FF), BF16), pltpu.VMEM((D_MODEL, EXPERT_FF), BF16),
                            pltpu.VMEM((EXPERT_FF, D_MODEL), BF16),
                            pltpu.VMEM((2, D_MODEL, EXPERT_FF), F32), pltpu.VMEM((2, D_MODEL, EXPERT_FF), F32),
                            pltpu.VMEM((2, EXPERT_FF, D_MODEL), F32), pltpu.SemaphoreType.DMA((2,))]),
        out_shape=jax.ShapeDtypeStruct(_tiled(N_ASSIGN), F32),
        compiler_params=_cparams(("arbitrary",)),
        name="moe_experts",
    )(*work, xs, w1, w3, w2)


def _moe_residual(pos_s, x_ref, route_ref, mod_ref, ys_hbm, y_s, sem):
    step = pl.program_id(0)
    nsteps = pl.num_programs(0)

    def copy(t, i, k):
        slot = t % 2
        src = _row_tile(ys_hbm, pos_s[2 * (t * TM + i) + k])
        return pltpu.make_async_copy(src, _row_tile(y_s.at[slot, k], i), sem.at[slot])

    def gather(t):
        for i in range(TM):
            copy(t, i, 0).start(priority=0)
            copy(t, i, 1).start(priority=1)

    def wait(i, carry):
        copy(step, i, 0).wait()
        copy(step, i, 1).wait()
        return carry

    pl.when(step == 0)(lambda: gather(step))
    pl.when(step + 1 < nsteps)(lambda: gather(step + 1))
    lax.fori_loop(0, TM, wait, 0, unroll=8)
    route = route_ref[...]
    lane = lax.broadcasted_iota(jnp.int32, route.shape, 1)
    slot = step % 2
    moe = (_lane_col(route, lane, 2) * _tiles_to_rows(y_s.at[slot, 0])
           + _lane_col(route, lane, 3) * _tiles_to_rows(y_s.at[slot, 1]))
    return x_ref[...] + mod_ref[:, 5 * D_MODEL:6 * D_MODEL] * moe


def _combine_inproj_kernel(pos_s, x_ref, route_ref, modp_ref, ys_hbm, mod_ref, nw_ref, w_ref,
                           xo_ref, p_ref, y_s, sem):
    x = _moe_residual(pos_s, x_ref, route_ref, modp_ref, ys_hbm, y_s, sem)
    xo_ref[...] = x
    mod = mod_ref[...]
    h = _rms(x, nw_ref[...]) * (1.0 + mod[:, D_MODEL:2 * D_MODEL]) + mod[:, 0:D_MODEL]
    p_ref[...] = _dot(h.astype(BF16), w_ref[...])


def _combine_inproj(pos, x, route, mods4, l, ys, norm_w, w_in_r):
    row = lambda w: pl.BlockSpec((TM, w), lambda i, p: (i, 0))
    mod = lambda layer: pl.BlockSpec((None, None, 1, 6 * D_MODEL), lambda i, p: (layer, _mod_index(i, TM), 0, 0))
    return pl.pallas_call(
        _combine_inproj_kernel,
        grid_spec=pltpu.PrefetchScalarGridSpec(
            num_scalar_prefetch=1, grid=(N_TOK // TM,),
            in_specs=[row(D_MODEL), row(128), mod(l - 1), pl.BlockSpec(memory_space=pl.ANY), mod(l),
                      pl.BlockSpec((1, D_MODEL), lambda i, p: (0, 0)),
                      pl.BlockSpec((None, D_MODEL, P_COLS), lambda i, p: (l, 0, 0))],
            out_specs=[row(D_MODEL), row(P_COLS)],
            scratch_shapes=[pltpu.VMEM((2, 2) + _tiled(TM), F32), pltpu.SemaphoreType.DMA((2,))]),
        out_shape=[jax.ShapeDtypeStruct((N_TOK, D_MODEL), F32), jax.ShapeDtypeStruct((N_TOK, P_COLS), F32)],
        compiler_params=_cparams(("arbitrary",)),
        name="moe_combine_inproj",
    )(pos, x, route, mods4, ys, mods4, norm_w, w_in_r)


def _combine_kernel(pos_s, x_ref, route_ref, mod_ref, fw_ref, ys_hbm, *refs, final):
    if final:
        oa_ref, ob_ref, y_s, sem = refs
    else:
        o_ref, y_s, sem = refs
    x = _moe_residual(pos_s, x_ref, route_ref, mod_ref, ys_hbm, y_s, sem)
    if final:
        y = _rms(x, fw_ref[...])
        is_ctx = pl.program_id(0) < N_CTX // TM

        @pl.when(is_ctx)
        def _():
            oa_ref[...] = y

        @pl.when(jnp.logical_not(is_ctx))
        def _():
            ob_ref[...] = y
    else:
        o_ref[...] = x


def _combine(pos, x, route, mods4, l, fw, ys, final):
    row = lambda w: pl.BlockSpec((TM, w), lambda i, p: (i, 0))
    if final:
        out_specs = _pair_specs(TM, D_MODEL, True)
        out_shape = [jax.ShapeDtypeStruct((N_CTX, D_MODEL), F32), jax.ShapeDtypeStruct((N_SMP, D_MODEL), F32)]
    else:
        out_specs = row(D_MODEL)
        out_shape = jax.ShapeDtypeStruct((N_TOK, D_MODEL), F32)
    return pl.pallas_call(
        functools.partial(_combine_kernel, final=final),
        grid_spec=pltpu.PrefetchScalarGridSpec(
            num_scalar_prefetch=1, grid=(N_TOK // TM,),
            in_specs=[row(D_MODEL), row(128),
                      pl.BlockSpec((None, None, 1, 6 * D_MODEL), lambda i, p: (l, _mod_index(i, TM), 0, 0)),
                      pl.BlockSpec((1, D_MODEL), lambda i, p: (0, 0)),
                      pl.BlockSpec(memory_space=pl.ANY)],
            out_specs=out_specs,
            scratch_shapes=[pltpu.VMEM((2, 2) + _tiled(TM), F32), pltpu.SemaphoreType.DMA((2,))]),
        out_shape=out_shape,
        compiler_params=_cparams(("arbitrary",)),
        name="moe_combine_final" if final else "moe_combine",
    )(pos, x, route, mods4, fw, ys)


def _rope_swap(w):
    nf = MLA_ROPE // 4
    g = w.reshape(w.shape[:-1] + (2, 2, nf))
    return jnp.stack([-g[..., 1, :], g[..., 0, :]], axis=-2).reshape(w.shape)


def _rope_tables():
    nf = MLA_ROPE // 4
    pos = np.arange(DEC_SEQ)
    row = (pos // GRID_W).astype(np.float32)
    col = (pos % GRID_W).astype(np.float32)
    inv = (np.float32(ROPE_THETA) ** (-np.arange(nf, dtype=np.float32) / np.float32(nf))).astype(np.float32)
    ar = (row[:, None] * inv).astype(np.float32)
    ac = (col[:, None] * inv).astype(np.float32)
    cos = np.concatenate([np.cos(ar), np.cos(ar), np.cos(ac), np.cos(ac)], axis=1)
    sin = np.concatenate([np.sin(ar), np.sin(ar), np.sin(ac), np.sin(ac)], axis=1)
    cos = np.concatenate([np.ones((TM, MLA_ROPE)), cos], axis=0).astype(np.float32)
    sin = np.concatenate([np.zeros((TM, MLA_ROPE)), sin], axis=0).astype(np.float32)
    n = cos.shape[0]
    z32 = np.zeros((n, 32), np.float32)
    tqc = np.concatenate([np.ones((n, MLA_NOPE), np.float32), cos, z32], axis=1)
    tqs = np.concatenate([np.zeros((n, MLA_NOPE), np.float32), sin, z32], axis=1)
    tk = np.concatenate([z32, cos, sin, z32], axis=1)
    return jnp.asarray(tqc), jnp.asarray(tqs), jnp.asarray(tk)


W_IN_COLS = 1984
W_IN_ROWS = 256


def _w_in_relayout_kernel(w_ref, sh_ref, s1_ref, s2_ref, o_ref):
    gate_cols = 4 * GLA_W
    o_ref[:, 0:gate_cols] = w_ref[:, 0:gate_cols].astype(BF16)
    for j in range(gate_cols // 128, MISC_BLK):
        hi = min(128 * j + 256, W_IN_COLS)
        src = w_ref[:, 128 * j:hi].astype(BF16)
        o_ref[:, 128 * j:128 * (j + 1)] = _dot(src, sh_ref[0:hi - 128 * j, :]).astype(BF16)
    head = w_ref[:, gate_cols:gate_cols + 128].astype(BF16)
    tail = w_ref[:, 128 * MISC_BLK:W_IN_COLS].astype(BF16)
    o_ref[:, 128 * MISC_BLK:P_COLS] = (_dot(head, s1_ref[...]) + _dot(tail, s2_ref[...])).astype(BF16)


def _w_in_relayout(w_in):
    sh = np.zeros((256, 128), np.float32)
    sh[32 + np.arange(128), np.arange(128)] = 1.0
    s1 = np.zeros((128, 128), np.float32)
    s1[np.arange(32), np.arange(32)] = 1.0
    s2 = np.zeros((W_IN_COLS - 128 * MISC_BLK, 128), np.float32)
    nf = MLA_ROPE // 4
    for j in range(MLA_ROPE):
        s2[32 + j, 32 + j] = 1.0
        half = (j // nf) % 2
        src = j + nf if half == 0 else j - nf
        s2[32 + src, 64 + j] = -1.0 if half == 0 else 1.0
    consts = [jnp.asarray(a).astype(BF16) for a in (sh, s1, s2)]
    full = lambda a: pl.BlockSpec(a.shape, lambda l, i: (0, 0))
    return pl.pallas_call(
        _w_in_relayout_kernel,
        grid=(DEPTH, D_MODEL // W_IN_ROWS),
        in_specs=[pl.BlockSpec((None, W_IN_ROWS, W_IN_COLS), lambda l, i: (l, i, 0))] + [full(a) for a in consts],
        out_specs=pl.BlockSpec((None, W_IN_ROWS, P_COLS), lambda l, i: (l, i, 0)),
        out_shape=jax.ShapeDtypeStruct((DEPTH, D_MODEL, P_COLS), BF16),
        compiler_params=_cparams(("parallel", "parallel")),
        name="w_in_relayout",
    )(w_in, *consts)


def _prep_layer(l, gla_wg_f, gla_wg_b, mla_wuq, mla_wukv, w_out, moe_wg, moe_bg, moe_we, moe_be):
    wgf = jnp.zeros((128, GLA_W), F32).at[0:16].set(gla_wg_f[l])
    wgb = jnp.zeros((128, GLA_W), F32).at[16:32].set(gla_wg_b[l])
    wq = mla_wuq[l].reshape(MLA_Q_RANK, MLA_HEADS, MLA_NOPE + MLA_ROPE)
    nope, rope = wq[..., :MLA_NOPE], wq[..., MLA_NOPE:]
    z = lambda n: jnp.zeros((MLA_Q_RANK, MLA_HEADS, n), F32)
    hw = MLA_HEADS * HEAD_BLK
    wa = jnp.concatenate([nope, rope, z(32)], axis=-1).reshape(MLA_Q_RANK, hw).astype(BF16)
    wb = jnp.concatenate([z(MLA_NOPE), _rope_swap(rope), z(32)], axis=-1).reshape(MLA_Q_RANK, hw).astype(BF16)
    wkv = mla_wukv[l].reshape(MLA_KV_RANK, MLA_HEADS, MLA_NOPE + MLA_V)
    wk = jnp.concatenate([wkv[..., :MLA_NOPE], jnp.zeros((MLA_KV_RANK, MLA_HEADS, 64), F32)],
                         axis=-1).reshape(MLA_KV_RANK, hw).astype(BF16)
    wv = wkv[..., MLA_NOPE:].reshape(MLA_KV_RANK, MLA_HEADS * MLA_V).astype(BF16)
    wp_np = np.zeros((128, MLA_HEADS, HEAD_BLK), np.float32)
    for j in range(MLA_ROPE):
        wp_np[32 + j, :, MLA_NOPE + j] = 1.0
        wp_np[64 + j, :, MLA_NOPE + j] = 1.0
    wp = jnp.asarray(wp_np.reshape(128, hw)).astype(BF16)
    wo = w_out[l].astype(BF16)
    wr = jnp.concatenate([moe_we[l], moe_wg[l], jnp.zeros((D_MODEL, 128 - N_EXPERTS - N_GROUPS), F32)], axis=1)
    br = jnp.concatenate([moe_be[l], moe_bg[l], jnp.zeros((128 - N_EXPERTS - N_GROUPS,), F32)]).reshape(1, 128)
    return dict(wgf=wgf, wgb=wgb, wa=wa, wb=wb, wk=wk, wv=wv, wp=wp, wo=wo, wr=wr, br=br)


def kernel(x_prompt, x_sample, cache_ckv, cache_krope, state_gla_fwd, state_gla_bwd, c, c_ctx, ada_w, ada_b,
           norm1_w, norm2_w, w_in, gla_wg_f, gla_bg_f, gla_wg_b, gla_bg_b, gla_norm_w, conv_w, conv_b,
           conv_ln_w, conv_ln_b, mla_qnorm_w, mla_wuq, mla_kvnorm_w, mla_wukv, w_out, moe_wg, moe_bg, moe_we,
           moe_be, moe_w1, moe_w3, moe_w2, final_norm_w):
    x = (x_prompt.reshape(N_CTX, D_MODEL), x_sample.reshape(N_SMP, D_MODEL))
    cond8 = jnp.concatenate([c_ctx[None, :], c, jnp.zeros((N_MOD - 1 - DEC_BATCH, D_MODEL), F32)], axis=0)
    mods4 = _ada_table(cond8, ada_w, ada_b).reshape(DEPTH, N_MOD, 1, 6 * D_MODEL)
    tabs = _rope_tables()
    w_in_r = _w_in_relayout(w_in)
    row1 = lambda a: a.reshape(1, -1)

    ckv_new, kr_new, sf_new, sb_new = [], [], [], []
    for l in range(DEPTH):
        w = _prep_layer(l, gla_wg_f, gla_wg_b, mla_wuq, mla_wukv, w_out, moe_wg, moe_bg, moe_we, moe_be)
        if l == 0:
            p = _inproj(x, mods4, l, row1(norm1_w[l]), w_in_r)
        else:
            x, p = _combine_inproj(pos, xn, route, mods4, l, ys, row1(norm1_w[l]), w_in_r)

        gla_args = (p, w["wgf"], w["wgb"], row1(gla_bg_f[l]), row1(gla_bg_b[l]), row1(gla_norm_w[l]))
        og_c, sf, sb = _gla(*gla_args, SEQ, BATCH, 0)
        og_s = _gla(*gla_args, DEC_SEQ, DEC_BATCH, N_CTX // DEC_SEQ, (state_gla_fwd, l), (state_gla_bwd, l))[0]
        og = (og_c, og_s)

        cw = jnp.concatenate([conv_w[l], jnp.zeros((1, CONV_W), F32)], axis=0)
        conv_args = (p, cw, row1(conv_b[l]), row1(conv_ln_w[l]), row1(conv_ln_b[l]))
        oc = (_conv(*conv_args, SEQ, BATCH, 0), _conv(*conv_args, DEC_SEQ, DEC_BATCH, N_CTX // DEC_SEQ))

        q, k, v, ckvn, kr = _mla_prep(p, tabs, row1(mla_qnorm_w[l]), row1(mla_kvnorm_w[l]),
                                      w["wa"], w["wb"], w["wk"], w["wp"], w["wv"])
        cc = cache_ckv[:, l].reshape(DEC_BATCH * PAST_LEN, MLA_KV_RANK)
        ckr = cache_krope[:, l].reshape(DEC_BATCH * PAST_LEN, MLA_ROPE)
        ckr = jnp.pad(ckr, ((0, 0), (32, 64)))
        kc, vc = _cache_kv(cc, ckr, w["wk"], w["wp"], w["wv"])
        om = (_attention(q, k, v, SEQ, BATCH, 0, SEQ, MLA_HEADS // 2),
              _attention(q, k, v, DEC_SEQ, DEC_BATCH, N_CTX, 512, 1, kc, vc))

        xn, h2, route, cnt = _outproj(x, og, oc, om, mods4, l, row1(norm2_w[l]), w["wo"], w["wr"], w["br"])
        posf, *work = _plan(cnt[0, :N_EXPERTS].astype(jnp.int32), cnt, route)
        pos = posf[:, 0:2].astype(jnp.int32).reshape(N_ASSIGN)
        ys = _experts(work, _dispatch(pos, h2), l, moe_w1, moe_w3, moe_w2)
        if l == DEPTH - 1:
            x = _combine(pos, xn, route, mods4, l, row1(final_norm_w), ys, True)

        ckv_new.append(ckvn[:N_CTX].reshape(BATCH, SEQ, MLA_KV_RANK))
        kr_new.append(kr[:N_CTX, 32:32 + MLA_ROPE].reshape(BATCH, SEQ, MLA_ROPE))
        sf_new.append(sf)
        sb_new.append(sb)

    y_ctx, y_smp = x
    return (y_ctx.reshape(BATCH, SEQ, D_MODEL), y_smp.reshape(DEC_BATCH, DEC_SEQ, D_MODEL),
            jnp.stack(ckv_new, axis=1), jnp.stack(kr_new, axis=1),
            jnp.stack(sf_new, axis=1), jnp.stack(sb_new, axis=1))
```

```python
import functools

import numpy as np
import jax
import jax.numpy as jnp
from jax import lax
from jax.experimental import pallas as pl
from jax.experimental.pallas import tpu as pltpu

F32 = jnp.float32
BF16 = jnp.bfloat16

D_MODEL = 1024
BATCH = 16
SEQ = 256
DEPTH = 2
DEC_BATCH = 4
DEC_SEQ = 1024
PAST_LEN = 512
GRID_W = 64
NORM_EPS = 1e-6
GLA_HEADS = 4
GLA_DK = 64
GLA_DV = 64
GLA_W = 256
GLA_GATE_RANK = 16
GLA_TAU = 16.0
CONV_W = 256
CONV_K = 31
MLA_HEADS = 8
MLA_NOPE = 64
MLA_ROPE = 32
MLA_V = 64
MLA_Q_RANK = 256
MLA_KV_RANK = 128
ROPE_THETA = 10000.0
N_GROUPS = 4
EXPERTS_PER_GROUP = 8
N_EXPERTS = 32
EXPERT_FF = 256

N_CTX = BATCH * SEQ
N_SMP = DEC_BATCH * DEC_SEQ
N_TOK = N_CTX + N_SMP
N_MOD = 8
TM = 512
P_COLS = 2048
MISC_BLK = 15
CHUNK = 64
SUB = 16
NEG = -1e30
VMEM_LIMIT = 56 * 1024 * 1024


def _cparams(sem):
    return pltpu.CompilerParams(dimension_semantics=sem, vmem_limit_bytes=VMEM_LIMIT)


def _dot(a, b):
    return jnp.dot(a, b, preferred_element_type=F32)


def _dot_nt(a, b):
    return lax.dot_general(a, b, (((1,), (1,)), ((), ())), preferred_element_type=F32)


def _split2(x):
    hi = x.astype(BF16)
    lo = (x - hi.astype(F32)).astype(BF16)
    return hi, lo


def _dot3(a, w):
    ah, al = _split2(a)
    wh, wl = _split2(w)
    return _dot(ah, wh) + _dot(al, wh) + _dot(ah, wl)


def _sigmoid(x):
    return 1.0 / (1.0 + jnp.exp(-x))


def _rms(x, w):
    ms = jnp.mean(x * x, axis=-1, keepdims=True)
    return x * lax.rsqrt(ms + NORM_EPS) * w


def _mod_index(i, tile):
    nctx = N_CTX // tile
    per = DEC_SEQ // tile
    return jnp.where(i < nctx, 0, 1 + (i - nctx) // per)


def _ada_kernel(c_ref, w_ref, b_ref, o_ref):
    c = c_ref[...]
    o_ref[...] = _dot3(c * _sigmoid(c), w_ref[...]) + b_ref[...]


def _ada_table(cond8, ada_w, ada_b):
    tn = 1024
    n6 = 6 * D_MODEL
    return pl.pallas_call(
        _ada_kernel,
        grid=(DEPTH, n6 // tn),
        in_specs=[
            pl.BlockSpec((N_MOD, D_MODEL), lambda l, j: (0, 0)),
            pl.BlockSpec((None, D_MODEL, tn), lambda l, j: (l, 0, j)),
            pl.BlockSpec((None, 1, tn), lambda l, j: (l, 0, j)),
        ],
        out_specs=pl.BlockSpec((None, N_MOD, tn), lambda l, j: (l, 0, j)),
        out_shape=jax.ShapeDtypeStruct((DEPTH, N_MOD, n6), F32),
        compiler_params=_cparams(("parallel", "parallel")),
        name="ada_table",
    )(cond8, ada_w, ada_b.reshape(DEPTH, 1, n6))


def _pair_specs(tile, width, split):
    nctx = N_CTX // tile
    offb = 0 if split else nctx
    return [pl.BlockSpec((tile, width), lambda i, *_: (jnp.minimum(i, nctx - 1), 0)),
            pl.BlockSpec((tile, width), lambda i, *_: (jnp.maximum(i - nctx, 0) + offb, 0))]


def _pair(x):
    return x if isinstance(x, tuple) else (x, x)


TM_IN = 512


def _inproj_kernel(xa_ref, xb_ref, mod_ref, nw_ref, w_ref, o_ref):
    mod = mod_ref[...]

    def project(x_ref):
        h = _rms(x_ref[...], nw_ref[...]) * (1.0 + mod[:, D_MODEL:2 * D_MODEL]) + mod[:, 0:D_MODEL]
        o_ref[...] = _dot(h.astype(BF16), w_ref[...])

    is_ctx = pl.program_id(0) < N_CTX // TM_IN
    pl.when(is_ctx)(lambda: project(xa_ref))
    pl.when(jnp.logical_not(is_ctx))(lambda: project(xb_ref))


def _inproj(x, mods4, l, norm_w, w_in_r):
    return pl.pallas_call(
        _inproj_kernel,
        grid=(N_TOK // TM_IN,),
        in_specs=_pair_specs(TM_IN, D_MODEL, isinstance(x, tuple)) + [
            pl.BlockSpec((None, None, 1, 6 * D_MODEL), lambda i: (l, _mod_index(i, TM_IN), 0, 0)),
            pl.BlockSpec((1, D_MODEL), lambda i: (0, 0)),
            pl.BlockSpec((None, D_MODEL, P_COLS), lambda i: (l, 0, 0)),
        ],
        out_specs=pl.BlockSpec((TM_IN, P_COLS), lambda i: (i, 0)),
        out_shape=jax.ShapeDtypeStruct((N_TOK, P_COLS), F32),
        compiler_params=_cparams(("parallel",)),
        name="inproj",
    )(*_pair(x), mods4, norm_w, w_in_r)


def _log_sigmoid(x):
    return jnp.minimum(x, 0.0) - jnp.log(1.0 + jnp.exp(-jnp.abs(x)))


def _split3(x):
    hi = x.astype(BF16)
    r = x - hi.astype(F32)
    mid = r.astype(BF16)
    lo = (r - mid.astype(F32)).astype(BF16)
    return hi, mid, lo


def _gla_consts(reverse):
    t = lax.broadcasted_iota(jnp.int32, (CHUNK, CHUNK), 0)
    s = lax.broadcasted_iota(jnp.int32, (CHUNK, CHUNK), 1)
    mid = jnp.bitwise_and(t, -SUB) + SUB // 2
    if reverse:
        cum = s >= t
        ref = s >= mid
    else:
        cum = s <= t
        ref = s <= mid
    cum_ref = jnp.concatenate([cum, ref], axis=0).astype(F32).astype(BF16)
    rt = jnp.bitwise_and(lax.broadcasted_iota(jnp.int32, (GLA_HEADS * SUB, CHUNK), 0), SUB - 1)
    cs = lax.broadcasted_iota(jnp.int32, (GLA_HEADS * SUB, CHUNK), 1)
    rowi = lax.broadcasted_iota(jnp.int32, (CHUNK, 1), 0)
    causal, valid = [], []
    for i in range(CHUNK // SUB):
        tq = rt + i * SUB
        causal.append((cs >= tq) if reverse else (cs <= tq))
        valid.append((rowi >= i * SUB) if reverse else (rowi < (i + 1) * SUB))
    return cum_ref, causal, valid


def _gla_local(jobs, nc, o_s, qe_s, u_s, dec_s, head_masks, block_mask):
    nsub = CHUNK // SUB
    rows = [pl.ds(pl.multiple_of(c * CHUNK, CHUNK), CHUNK) for c, *_ in jobs]
    bs = []
    for (c, d, qs, k, vb, vt, la_ref, consts), r in zip(jobs, rows):
        h0, h1, h2 = _split3(la_ref[r, :])
        br = _dot(consts[0], h0) + _dot(consts[0], h1) + _dot(consts[0], h2)
        bs.append((br[0:CHUNK], br[CHUNK:2 * CHUNK]))
    ops = []
    for (c, d, qs, k, vb, vt, la_ref, consts), (b, rr) in zip(jobs, bs):
        qhat = qs * jnp.exp(b - rr)
        for i in range(nsub):
            r_i = rr[i * SUB:i * SUB + 1]
            kt = (k * jnp.exp(jnp.where(consts[2][i], r_i - b, NEG))).astype(BF16)
            qi = qhat[i * SUB:(i + 1) * SUB]
            ops.append((jnp.concatenate([jnp.where(m, qi, 0.0) for m in head_masks], axis=0).astype(BF16), kt))
    scores = [_dot_nt(qbig, kt) for qbig, kt in ops]
    outs = []
    for j, (c, d, qs, k, vb, vt, la_ref, consts) in enumerate(jobs):
        for i in range(nsub):
            a = jnp.where(consts[1][i], scores[j * nsub + i], 0.0)
            outs.append(_dot(a.astype(BF16), vb))
    for j, ((c, d, qs, k, vb, vt, la_ref, consts), (b, rr), r) in enumerate(zip(jobs, bs, rows)):
        parts = []
        for i in range(nsub):
            ov = outs[j * nsub + i]
            oi = jnp.where(head_masks[0], ov[0:SUB], 0.0)
            for h in range(1, GLA_HEADS):
                oi = oi + jnp.where(head_masks[h], ov[h * SUB:(h + 1) * SUB], 0.0)
            parts.append(oi)
        o_s[d, r, :] = jnp.concatenate(parts, axis=0)
        qe_s[d, r, :] = (qs * jnp.exp(b)).astype(BF16)
        b_last = b[0:1] if d == 1 else b[CHUNK - 1:CHUNK]
        kl = (k * jnp.exp(b_last - b)).astype(BF16)
        idx = d * nc + c
        u_s[idx] = jnp.where(block_mask, _dot(vt, kl), 0.0)
        dec_s[pl.ds(pl.multiple_of(idx * 8, 8), 8), :] = jnp.broadcast_to(jnp.exp(b_last), (8, GLA_W))


GLA_SAFE_EXPONENT = 60.0


def _gla_exact_intra(d, nc, qkvg_ref, la_ref, cum, o_s, b_s, acc_s, ones_blk):
    reverse = d == 1
    tcol = lax.broadcasted_iota(jnp.int32, (CHUNK, 1), 0)

    def chunk(c, carry):
        r0 = pl.multiple_of(c * CHUNK, CHUNK)
        rows = pl.ds(r0, CHUNK)
        h0, h1, h2 = _split3(la_ref[rows, :])
        b_s[...] = _dot(cum, h0) + _dot(cum, h1) + _dot(cum, h2)
        acc_s[...] = jnp.zeros((CHUNK, GLA_W), F32)
        qs = qkvg_ref[rows, 0:GLA_W] * (GLA_DK ** -0.5)

        def key(s, carry2):
            bs = b_s[pl.ds(s, 1), :]
            ks = qkvg_ref[pl.ds(r0 + s, 1), GLA_W:2 * GLA_W]
            vs = qkvg_ref[pl.ds(r0 + s, 1), 2 * GLA_W:3 * GLA_W]
            live = (tcol <= s) if reverse else (tcol >= s)
            hi, lo = _split2(qs * ks * jnp.exp(jnp.where(live, b_s[...] - bs, NEG)))
            acc_s[...] += (_dot(hi, ones_blk) + _dot(lo, ones_blk)) * vs
            return carry2

        lax.fori_loop(0, CHUNK, key, 0)
        o_s[d, rows, :] = acc_s[...]
        return carry

    lax.fori_loop(0, nc, chunk, 0)


def _gla_kernel(*refs, n, has_state):
    if has_state:
        (qkvg_ref, misc_ref, wgf_ref, wgb_ref, bgf_ref, bgb_ref, nw_ref, s0f_ref, s0b_ref,
         o_ref, laf_s, lab_s, o_s, qe_s, u_s, dec_s, sts_s, stf_s, stb_s, b_s, acc_s) = refs
    else:
        (qkvg_ref, misc_ref, wgf_ref, wgb_ref, bgf_ref, bgb_ref, nw_ref,
         o_ref, sf_ref, sb_ref, laf_s, lab_s, o_s, qe_s, u_s, dec_s, sts_s, stf_s, stb_s, b_s, acc_s) = refs
    mh, ml = _split2(misc_ref[...])
    gates = []
    for w_ref in (wgf_ref, wgb_ref):
        wh, wl = _split2(w_ref[...])
        gates.append((_dot(mh, wh), _dot(ml, wh), _dot(mh, wl)))
    laf_s[...] = _log_sigmoid(gates[0][0] + gates[0][1] + gates[0][2] + bgf_ref[...]) * (1.0 / GLA_TAU)
    lab_s[...] = _log_sigmoid(gates[1][0] + gates[1][1] + gates[1][2] + bgb_ref[...]) * (1.0 / GLA_TAU)
    stf_s[...] = jnp.zeros((GLA_W, GLA_W), F32)
    stb_s[...] = jnp.zeros((GLA_W, GLA_W), F32)
    if has_state:
        for st_s, s0_ref in ((stf_s, s0f_ref), (stb_s, s0b_ref)):
            for h in range(GLA_HEADS):
                st_s[h * GLA_DK:(h + 1) * GLA_DK, h * GLA_DV:(h + 1) * GLA_DV] = s0_ref[h]
            st_s[...] = st_s[...].T

    lane = lax.broadcasted_iota(jnp.int32, (1, GLA_W), 1)
    head_masks = [jnp.right_shift(lane, 6) == h for h in range(GLA_HEADS)]
    bi = jnp.right_shift(lax.broadcasted_iota(jnp.int32, (GLA_W, GLA_W), 0), 6)
    bj = jnp.right_shift(lax.broadcasted_iota(jnp.int32, (GLA_W, GLA_W), 1), 6)
    block_mask = bi == bj
    consts_f = _gla_consts(False)
    consts_b = _gla_consts(True)
    nc = n // CHUNK

    def local(cc, carry):
        jobs = []
        for c in (2 * cc, 2 * cc + 1):
            rows = pl.ds(pl.multiple_of(c * CHUNK, CHUNK), CHUNK)
            qs = qkvg_ref[rows, 0:GLA_W] * (GLA_DK ** -0.5)
            k = qkvg_ref[rows, GLA_W:2 * GLA_W]
            v = qkvg_ref[rows, 2 * GLA_W:3 * GLA_W]
            vb = v.astype(BF16)
            vt = v.T.astype(BF16)
            jobs += [(c, 0, qs, k, vb, vt, laf_s, consts_f), (c, 1, qs, k, vb, vt, lab_s, consts_b)]
        _gla_local(jobs, nc, o_s, qe_s, u_s, dec_s, head_masks, block_mask)
        return carry

    lax.fori_loop(0, nc // 2, local, 0)

    ones_blk = block_mask.astype(F32).astype(BF16)
    @pl.when(jnp.min(jnp.minimum(laf_s[...], lab_s[...])) * (SUB // 2) < -GLA_SAFE_EXPONENT)
    def _():
        for d, (la_s, consts) in enumerate(((laf_s, consts_f), (lab_s, consts_b))):
            _gla_exact_intra(d, nc, qkvg_ref, la_s, consts[0][0:CHUNK], o_s, b_s, acc_s, ones_blk)

    def scan(j, carry):
        for d, st_s in enumerate((stf_s, stb_s)):
            idx = d * nc + (j if d == 0 else nc - 1 - j)
            st = st_s[...]
            sts_s[idx] = st.astype(BF16)
            st_s[...] = st * dec_s[pl.ds(pl.multiple_of(idx * 8, 8), 1), :] + u_s[idx]
        return carry

    lax.fori_loop(0, nc, scan, 0)

    def inter(cc, carry):
        jobs = [(c, d, pl.ds(pl.multiple_of(c * CHUNK, CHUNK), CHUNK))
                for c in (2 * cc, 2 * cc + 1) for d in range(2)]
        res = [_dot_nt(qe_s[d, r, :], sts_s[d * nc + c]) for c, d, r in jobs]
        for (c, d, r), o in zip(jobs, res):
            o_s[d, r, :] += o
        return carry

    lax.fori_loop(0, nc // 2, inter, 0)

    if not has_state:
        for st_s, s_ref in ((stf_s, sf_ref), (stb_s, sb_ref)):
            s = st_s[...].T
            for h in range(GLA_HEADS):
                s_ref[h] = s[h * GLA_DK:(h + 1) * GLA_DK, h * GLA_DV:(h + 1) * GLA_DV]

    rt = 256
    for r0 in range(0, n, rt):
        o = o_s[0, r0:r0 + rt, :] + o_s[1, r0:r0 + rt, :]
        hi, lo = _split2(o * o)
        ms = (_dot(hi, ones_blk) + _dot(lo, ones_blk)) * (1.0 / GLA_DV)
        g = qkvg_ref[r0:r0 + rt, 3 * GLA_W:4 * GLA_W]
        o_ref[r0:r0 + rt, :] = o * lax.rsqrt(ms + NORM_EPS) * nw_ref[...] * (g * _sigmoid(g))


def _gla(p, wgf, wgb, bgf, bgb, nw, n, nb, row_blk0, s0f=None, s0b=None):
    has_state = s0f is not None
    full = lambda shape: pl.BlockSpec(shape, lambda b: (0,) * len(shape))
    in_specs = [
        pl.BlockSpec((n, 4 * GLA_W), lambda b: (row_blk0 + b, 0)),
        pl.BlockSpec((n, 128), lambda b: (row_blk0 + b, MISC_BLK)),
        full((128, GLA_W)), full((128, GLA_W)), full((1, GLA_W)), full((1, GLA_W)), full((1, GLA_W)),
    ]
    args = [p, p, wgf, wgb, bgf, bgb, nw]
    out_specs = [pl.BlockSpec((n, GLA_W), lambda b: (b, 0))]
    out_shape = [jax.ShapeDtypeStruct((nb * n, GLA_W), F32)]
    if has_state:
        layer = s0f[1]
        st_spec = pl.BlockSpec((None, None, GLA_HEADS, GLA_DK, GLA_DV), lambda b: (b, layer, 0, 0, 0))
        in_specs += [st_spec, st_spec]
        args += [s0f[0], s0b[0]]
    else:
        st_spec = pl.BlockSpec((None, GLA_HEADS, GLA_DK, GLA_DV), lambda b: (b, 0, 0, 0))
        out_specs += [st_spec, st_spec]
        out_shape += [jax.ShapeDtypeStruct((nb, GLA_HEADS, GLA_DK, GLA_DV), F32)] * 2
    return pl.pallas_call(
        functools.partial(_gla_kernel, n=n, has_state=has_state),
        grid=(nb,),
        in_specs=in_specs,
        out_specs=out_specs,
        out_shape=out_shape,
        scratch_shapes=[
            pltpu.VMEM((n, GLA_W), F32), pltpu.VMEM((n, GLA_W), F32),
            pltpu.VMEM((2, n, GLA_W), F32),
            pltpu.VMEM((2, n, GLA_W), BF16),
            pltpu.VMEM((2 * n // CHUNK, GLA_W, GLA_W), F32),
            pltpu.VMEM((2 * n // CHUNK * 8, GLA_W), F32),
            pltpu.VMEM((2 * n // CHUNK, GLA_W, GLA_W), BF16),
            pltpu.VMEM((GLA_W, GLA_W), F32), pltpu.VMEM((GLA_W, GLA_W), F32),
            pltpu.VMEM((CHUNK, GLA_W), F32), pltpu.VMEM((CHUNK, GLA_W), F32),
        ],
        compiler_params=_cparams(("parallel",)),
        name="gla_state" if has_state else "gla_ctx",
    )(*args)


CONV_PAD = 16
CONV_ROWS = 128
CONV_SHIFT_ROWS = 24


def _conv_kernel(u_ref, w_ref, b_ref, lnw_ref, lnb_ref, o_ref, zp_s, zsh_s, *, n):
    u = u_ref[...]
    zp_s[0:CONV_PAD, :] = jnp.zeros((CONV_PAD, CONV_W), F32)
    zp_s[CONV_PAD + n:2 * CONV_PAD + n, :] = jnp.zeros((CONV_PAD, CONV_W), F32)
    zp_s[CONV_PAD:CONV_PAD + n, :] = u[:, 0:CONV_W] * _sigmoid(u[:, CONV_W:2 * CONV_W])
    m = n + CONV_SHIFT_ROWS
    for s in range(1, 8):
        zsh_s[s - 1] = zp_s[s:s + m, :]
    off = CONV_PAD - CONV_K // 2

    def tap(r0, j):
        s = (off + j) % 8
        base = r0 + off + j - s
        return zp_s[base:base + CONV_ROWS, :] if s == 0 else zsh_s[s - 1, base:base + CONV_ROWS, :]

    for r0 in range(0, n, CONV_ROWS):
        acc = tap(r0, 0) * w_ref[0:1, :]
        for j in range(1, CONV_K):
            acc = acc + tap(r0, j) * w_ref[j:j + 1, :]
        acc = acc + b_ref[...]
        mu = jnp.mean(acc, axis=-1, keepdims=True)
        xc = acc - mu
        var = jnp.mean(xc * xc, axis=-1, keepdims=True)
        y = xc * lax.rsqrt(var + NORM_EPS) * lnw_ref[...] + lnb_ref[...]
        o_ref[r0:r0 + CONV_ROWS, :] = y * _sigmoid(y)


def _conv(p, cw, cb, lnw, lnb, n, nb, row_blk0):
    full = lambda shape: pl.BlockSpec(shape, lambda b: (0,) * len(shape))
    return pl.pallas_call(
        functools.partial(_conv_kernel, n=n),
        grid=(nb,),
        in_specs=[
            pl.BlockSpec((n, 2 * CONV_W), lambda b: (row_blk0 + b, 2)),
            full((32, CONV_W)), full((1, CONV_W)), full((1, CONV_W)), full((1, CONV_W)),
        ],
        out_specs=pl.BlockSpec((n, CONV_W), lambda b: (b, 0)),
        out_shape=jax.ShapeDtypeStruct((nb * n, CONV_W), F32),
        scratch_shapes=[pltpu.VMEM((n + 2 * CONV_PAD, CONV_W), F32),
                        pltpu.VMEM((7, n + CONV_SHIFT_ROWS, CONV_W), F32)],
        compiler_params=_cparams(("parallel",)),
        name="conv%d" % n,
    )(p, cw, cb, lnw, lnb)


HEAD_BLK = 128


def _kv_expand(ckvn, t, wk_ref, wp_ref, wv_ref, k_ref, v_ref):
    cb = ckvn.astype(BF16)
    th, tl = _split2(t)
    k_ref[...] = (_dot(cb, wk_ref[...]) + _dot(th, wp_ref[...]) + _dot(tl, wp_ref[...])).astype(BF16)
    v_ref[...] = _dot(cb, wv_ref[...]).astype(BF16)


def _mla_prep_kernel(cq_ref, cm_ref, tqc_ref, tqs_ref, tk_ref, qnw_ref, kvw_ref, wa_ref, wb_ref,
                     wk_ref, wp_ref, wv_ref, q_ref, k_ref, v_ref, ckv_ref, kr_ref):
    qn = _rms(cq_ref[...], qnw_ref[...]).astype(BF16)
    cm = cm_ref[...]
    ckvn = _rms(cm[:, 0:MLA_KV_RANK], kvw_ref[...])
    t = cm[:, MLA_KV_RANK:2 * MLA_KV_RANK] * tk_ref[...]
    cb = ckvn.astype(BF16)
    th, tl = _split2(t)
    qa = _dot(qn, wa_ref[...])
    qb = _dot(qn, wb_ref[...])
    kk = _dot(cb, wk_ref[...]) + _dot(th, wp_ref[...]) + _dot(tl, wp_ref[...])
    vv = _dot(cb, wv_ref[...])
    tqc = jnp.concatenate([tqc_ref[...]] * MLA_HEADS, axis=1)
    tqs = jnp.concatenate([tqs_ref[...]] * MLA_HEADS, axis=1)
    scale = (MLA_NOPE + MLA_ROPE) ** -0.5
    q_ref[...] = ((qa * tqc + qb * tqs) * scale).astype(BF16)
    k_ref[...] = kk.astype(BF16)
    v_ref[...] = vv.astype(BF16)
    ckv_ref[...] = ckvn
    kr_ref[...] = t


def _rope_tile(i):
    nctx = N_CTX // TM
    return jnp.where(i < nctx, 0, 1 + (i - nctx) % (DEC_SEQ // TM))


def _mla_prep(p, tabs, qnw, kvw, wa, wb, wk, wp, wv):
    tqc, tqs, tk = tabs
    full = lambda shape: pl.BlockSpec(shape, lambda i: (0,) * len(shape))
    tab_spec = pl.BlockSpec((TM, HEAD_BLK), lambda i: (_rope_tile(i), 0))
    hw = MLA_HEADS * HEAD_BLK
    return pl.pallas_call(
        _mla_prep_kernel,
        grid=(N_TOK // TM,),
        in_specs=[
            pl.BlockSpec((TM, MLA_Q_RANK), lambda i: (i, 6)),
            pl.BlockSpec((TM, 2 * MLA_KV_RANK), lambda i: (i, 7)),
            tab_spec, tab_spec, tab_spec,
            full((1, MLA_Q_RANK)), full((1, MLA_KV_RANK)),
            full((MLA_Q_RANK, hw)), full((MLA_Q_RANK, hw)),
            full((MLA_KV_RANK, hw)), full((128, hw)), full((MLA_KV_RANK, MLA_HEADS * MLA_V)),
        ],
        out_specs=[
            pl.BlockSpec((TM, hw), lambda i: (i, 0)),
            pl.BlockSpec((TM, hw), lambda i: (i, 0)),
            pl.BlockSpec((TM, MLA_HEADS * MLA_V), lambda i: (i, 0)),
            pl.BlockSpec((TM, MLA_KV_RANK), lambda i: (i, 0)),
            pl.BlockSpec((TM, 128), lambda i: (i, 0)),
        ],
        out_shape=[
            jax.ShapeDtypeStruct((N_TOK, hw), BF16),
            jax.ShapeDtypeStruct((N_TOK, hw), BF16),
            jax.ShapeDtypeStruct((N_TOK, MLA_HEADS * MLA_V), BF16),
            jax.ShapeDtypeStruct((N_TOK, MLA_KV_RANK), F32),
            jax.ShapeDtypeStruct((N_TOK, 128), F32),
        ],
        compiler_params=_cparams(("parallel",)),
        name="mla_prep",
    )(p, p, tqc, tqs, tk, qnw, kvw, wa, wb, wk, wp, wv)


def _cache_kv_kernel(ckv_ref, kr_ref, wk_ref, wp_ref, wv_ref, k_ref, v_ref):
    _kv_expand(ckv_ref[...], kr_ref[...], wk_ref, wp_ref, wv_ref, k_ref, v_ref)


def _cache_kv(ckv, kr, wk, wp, wv):
    n = DEC_BATCH * PAST_LEN
    full = lambda shape: pl.BlockSpec(shape, lambda i: (0,) * len(shape))
    hw = MLA_HEADS * HEAD_BLK
    return pl.pallas_call(
        _cache_kv_kernel,
        grid=(n // PAST_LEN,),
        in_specs=[
            pl.BlockSpec((PAST_LEN, MLA_KV_RANK), lambda i: (i, 0)),
            pl.BlockSpec((PAST_LEN, 128), lambda i: (i, 0)),
            full((MLA_KV_RANK, hw)), full((128, hw)), full((MLA_KV_RANK, MLA_HEADS * MLA_V)),
        ],
        out_specs=[
            pl.BlockSpec((PAST_LEN, hw), lambda i: (i, 0)),
            pl.BlockSpec((PAST_LEN, MLA_HEADS * MLA_V), lambda i: (i, 0)),
        ],
        out_shape=[
            jax.ShapeDtypeStruct((n, hw), BF16),
            jax.ShapeDtypeStruct((n, MLA_HEADS * MLA_V), BF16),
        ],
        compiler_params=_cparams(("parallel",)),
        name="cache_kv",
    )(ckv, kr, wk, wp, wv)


def _attn_kernel(*refs, has_cache):
    if has_cache:
        q_ref, k_ref, v_ref, kc_ref, vc_ref, o_ref = refs
    else:
        q_ref, k_ref, v_ref, o_ref = refs
    nh = q_ref.shape[1] // HEAD_BLK
    cols = [slice(h * HEAD_BLK, (h + 1) * HEAD_BLK) for h in range(nh)]
    vcols = [slice((h // 2) * 2 * MLA_V, (h // 2 + 1) * 2 * MLA_V) for h in range(nh)]
    s = [_dot_nt(q_ref[:, c], k_ref[:, c]) for c in cols]
    m = [jnp.max(x, axis=-1, keepdims=True) for x in s]
    if has_cache:
        sc = [_dot_nt(q_ref[:, c], kc_ref[:, c]) for c in cols]
        m = [jnp.maximum(a, jnp.max(x, axis=-1, keepdims=True)) for a, x in zip(m, sc)]
        pc = [jnp.exp(x - a) for x, a in zip(sc, m)]
    p = [jnp.exp(x - a) for x, a in zip(s, m)]
    den = [jnp.sum(x, axis=-1, keepdims=True) for x in p]
    o = [_dot(x.astype(BF16), v_ref[:, c]) for x, c in zip(p, vcols)]
    if has_cache:
        den = [a + jnp.sum(x, axis=-1, keepdims=True) for a, x in zip(den, pc)]
        o = [a + _dot(x.astype(BF16), vc_ref[:, c]) for a, x, c in zip(o, pc, vcols)]
    outs = [a / b for a, b in zip(o, den)]
    lane = lax.broadcasted_iota(jnp.int32, (1, 2 * MLA_V), 1)
    for j in range(nh // 2):
        o_ref[:, j * 2 * MLA_V:(j + 1) * 2 * MLA_V] = jnp.where(lane < MLA_V, outs[2 * j], outs[2 * j + 1])


def _attention(q, k, v, n, nb, row0, tq, npair, kc=None, vc=None):
    has_cache = kc is not None
    nq = n // tq
    rb0 = row0 // tq
    kb0 = row0 // n
    in_specs = [
        pl.BlockSpec((tq, npair * 2 * HEAD_BLK), lambda b, h, i: (rb0 + b * nq + i, h)),
        pl.BlockSpec((n, npair * 2 * HEAD_BLK), lambda b, h, i: (kb0 + b, h)),
        pl.BlockSpec((n, npair * 2 * MLA_V), lambda b, h, i: (kb0 + b, h)),
    ]
    args = [q, k, v]
    if has_cache:
        in_specs += [
            pl.BlockSpec((PAST_LEN, npair * 2 * HEAD_BLK), lambda b, h, i: (b, h)),
            pl.BlockSpec((PAST_LEN, npair * 2 * MLA_V), lambda b, h, i: (b, h)),
        ]
        args += [kc, vc]
    return pl.pallas_call(
        functools.partial(_attn_kernel, has_cache=has_cache),
        grid=(nb, MLA_HEADS // (2 * npair), nq),
        in_specs=in_specs,
        out_specs=pl.BlockSpec((tq, npair * 2 * MLA_V), lambda b, h, i: (b * nq + i, h)),
        out_shape=jax.ShapeDtypeStruct((nb * n, MLA_HEADS * MLA_V), F32),
        compiler_params=_cparams(("parallel", "parallel", "parallel")),
        name="attn_cache" if has_cache else "attn_ctx",
    )(*args)


def _outproj_kernel(xa_ref, xb_ref, oga_ref, ogb_ref, oca_ref, ocb_ref, oma_ref, omb_ref, mod_ref, nw_ref,
                    wo_ref, wr_ref, br_ref, xo_ref, h_ref, route_ref, cnt_ref, carry_s, wrh_s, wrl_s):
    @pl.when(pl.program_id(0) == 0)
    def _():
        carry_s[...] = jnp.zeros(carry_s.shape, F32)
        wrh, wrl = _split2(wr_ref[...])
        wrh_s[...] = wrh
        wrl_s[...] = wrl

    mod = mod_ref[...]
    groups = [slice(g * (TM // 2), (g + 1) * (TM // 2)) for g in range(2)]

    def residual(x_ref, og_ref, oc_ref, om_ref):
        a1 = [_dot(og_ref[r, :].astype(BF16), wo_ref[0:GLA_W, :]) for r in groups]
        a2 = [_dot(oc_ref[r, :].astype(BF16), wo_ref[GLA_W:GLA_W + CONV_W, :]) for r in groups]
        a3 = [_dot(om_ref[r, :].astype(BF16), wo_ref[GLA_W + CONV_W:D_MODEL, :]) for r in groups]
        for r, u, v, t in zip(groups, a1, a2, a3):
            xo_ref[r, :] = x_ref[r, :] + mod[:, 2 * D_MODEL:3 * D_MODEL] * (u + v + t)

    is_ctx = pl.program_id(0) < N_CTX // TM
    pl.when(is_ctx)(lambda: residual(xa_ref, oga_ref, oca_ref, oma_ref))
    pl.when(jnp.logical_not(is_ctx))(lambda: residual(xb_ref, ogb_ref, ocb_ref, omb_ref))
    hs = [_rms(xo_ref[r, :], nw_ref[...]) * (1.0 + mod[:, 4 * D_MODEL:5 * D_MODEL]) + mod[:, 3 * D_MODEL:4 * D_MODEL]
          for r in groups]
    for g, h in enumerate(hs):
        _rows_to_tiles(h_ref, h, g * (TM // 2))
    parts = [_split2(h) for h in hs]
    l1 = [_dot(hh, wrh_s[...]) for hh, hl in parts]
    l2 = [_dot(hl, wrh_s[...]) for hh, hl in parts]
    l3 = [_dot(hh, wrl_s[...]) for hh, hl in parts]
    logits = jnp.concatenate([u + v + t for u, v, t in zip(l1, l2, l3)], axis=0) + br_ref[...]
    lane = lax.broadcasted_iota(jnp.int32, logits.shape, 1).astype(F32)
    big = 1e9
    gl = jnp.where((lane >= N_EXPERTS) & (lane < N_EXPERTS + N_GROUPS), logits, NEG)
    gmax = jnp.max(gl, axis=-1, keepdims=True)
    gw = 1.0 / jnp.sum(jnp.exp(gl - gmax), axis=-1, keepdims=True)
    gi = jnp.min(jnp.where(gl == gmax, lane, big), axis=-1, keepdims=True) - N_EXPERTS
    lo = gi * EXPERTS_PER_GROUP
    el = jnp.where((lane >= lo) & (lane < lo + EXPERTS_PER_GROUP), logits, NEG)
    m1 = jnp.max(el, axis=-1, keepdims=True)
    i1 = jnp.min(jnp.where(el == m1, lane, big), axis=-1, keepdims=True)
    el2 = jnp.where(lane == i1, NEG, el)
    m2 = jnp.max(el2, axis=-1, keepdims=True)
    i2 = jnp.min(jnp.where(el2 == m2, lane, big), axis=-1, keepdims=True)
    e2 = jnp.exp(m2 - m1)
    p1 = 1.0 / (1.0 + e2)
    onehot = jnp.where((lane == i1) | (lane == i2), 1.0, 0.0)
    tr = lax.broadcasted_iota(jnp.int32, (TM, TM), 0)
    tc = lax.broadcasted_iota(jnp.int32, (TM, TM), 1)
    before = (tc < tr).astype(F32).astype(BF16)
    seen = _dot(before, onehot.astype(BF16)) + carry_s[0:1, :]
    rank1 = jnp.sum(jnp.where(lane == i1, seen, 0.0), axis=-1, keepdims=True)
    rank2 = jnp.sum(jnp.where(lane == i2, seen, 0.0), axis=-1, keepdims=True)
    carry_s[...] = carry_s[...] + jnp.sum(onehot, axis=0, keepdims=True)
    cnt_ref[...] = carry_s[...]
    cols = (i1, i2, gw * p1, gw * (e2 * p1), rank1, rank2)
    route = jnp.zeros(logits.shape, F32)
    for j, col in enumerate(cols):
        route = jnp.where(lane == j, col, route)
    route_ref[...] = route


def _outproj(x, og, oc, om, mods4, l, norm_w, wo, wr, br):
    full = lambda shape: pl.BlockSpec(shape, lambda i: (0,) * len(shape))
    row = lambda w: pl.BlockSpec((TM, w), lambda i: (i, 0))
    return pl.pallas_call(
        _outproj_kernel,
        grid=(N_TOK // TM,),
        in_specs=(_pair_specs(TM, D_MODEL, isinstance(x, tuple)) + _pair_specs(TM, GLA_W, True)
                  + _pair_specs(TM, CONV_W, True) + _pair_specs(TM, MLA_HEADS * MLA_V, True) + [
            pl.BlockSpec((None, None, 1, 6 * D_MODEL), lambda i: (l, _mod_index(i, TM), 0, 0)),
            full((1, D_MODEL)), full((D_MODEL, D_MODEL)), full((D_MODEL, 128)), full((1, 128)),
        ]),
        out_specs=[row(D_MODEL), pl.BlockSpec(_tiled(TM), lambda i: (i, 0)), row(128), full((8, 128))],
        out_shape=[
            jax.ShapeDtypeStruct((N_TOK, D_MODEL), F32),
            jax.ShapeDtypeStruct(_tiled(N_TOK), F32),
            jax.ShapeDtypeStruct((N_TOK, 128), F32),
            jax.ShapeDtypeStruct((8, 128), F32),
        ],
        scratch_shapes=[pltpu.VMEM((8, 128), F32), pltpu.VMEM((D_MODEL, 128), BF16),
                        pltpu.VMEM((D_MODEL, 128), BF16)],
        compiler_params=_cparams(("arbitrary",)),
        name="outproj_router",
    )(*_pair(x), *og, *oc, *om, mods4, norm_w, wo, wr, br)


N_ASSIGN = 2 * N_TOK
TE = 256
N_ETILE = N_ASSIGN // TE
N_WORK = N_ETILE + N_EXPERTS


ROW_SUB = 8


def _tiled(n):
    return (n * ROW_SUB, D_MODEL // ROW_SUB)


def _rows_to_tiles(ref, x, r0=0):
    n = x.shape[0]
    for s in range(ROW_SUB):
        ref[pl.ds(r0 * ROW_SUB + s, n, stride=ROW_SUB), :] = x[:, s * 128:(s + 1) * 128]


def _tiles_to_rows(ref, r0=0, n=None):
    n = ref.shape[0] // ROW_SUB if n is None else n
    return jnp.concatenate([ref[pl.ds(r0 * ROW_SUB + s, n, stride=ROW_SUB), :] for s in range(ROW_SUB)], axis=1)


def _row_tile(ref, t):
    start = t * ROW_SUB if isinstance(t, int) else pl.multiple_of(t * ROW_SUB, ROW_SUB)
    return ref.at[pl.ds(start, ROW_SUB)]


def _lane_col(x, lane, j):
    return jnp.sum(jnp.where(lane == j, x, 0.0), axis=-1, keepdims=True)


def _plan_kernel(cnt_s, cnt_ref, route_ref, pos_ref, wt_ref, we_ref, wlo_ref, whi_ref, wf_ref, wr_ref, wn_ref,
                 off_s):
    cnt = cnt_ref[...]
    chi = jnp.floor(cnt * (1.0 / 128.0))
    clo = cnt - chi * 128.0
    r = lax.broadcasted_iota(jnp.int32, (128, 128), 0)
    c = lax.broadcasted_iota(jnp.int32, (128, 128), 1)
    below = (r < c).astype(F32).astype(BF16)
    off = (128.0 * _dot(chi.astype(BF16), below) + _dot(clo.astype(BF16), below))[0:1, :]
    route = route_ref[...]
    lane = lax.broadcasted_iota(jnp.int32, route.shape, 1).astype(F32)
    pos1 = _lane_col(off, lane, _lane_col(route, lane, 0.0)) + _lane_col(route, lane, 4.0)
    pos2 = _lane_col(off, lane, _lane_col(route, lane, 1.0)) + _lane_col(route, lane, 5.0)
    pos_ref[...] = jnp.where(lane == 0.0, pos1, jnp.where(lane == 1.0, pos2, 0.0))

    @pl.when(pl.program_id(0) == 0)
    def _():
        def offs(e, acc):
            off_s[e] = acc
            return acc + cnt_s[e]

        off_s[N_EXPERTS] = lax.fori_loop(0, N_EXPERTS, offs, jnp.int32(0))

        def item(w, st):
            j, e, first = st
            active = j < N_ETILE
            jj = jnp.minimum(j, N_ETILE - 1)
            ee = jnp.minimum(e, N_EXPERTS - 1)
            tlo = jj * TE
            thi = tlo + TE
            end = off_s[ee + 1]
            wt_ref[w] = jj
            we_ref[w] = ee
            wlo_ref[w] = jnp.where(active, jnp.maximum(off_s[ee], tlo) - tlo, 0)
            whi_ref[w] = jnp.where(active, jnp.minimum(end, thi) - tlo, 0)
            wf_ref[w] = jnp.where(active, first, 0)
            adv_j = (active & (end >= thi)).astype(jnp.int32)
            adv_e = (active & (end <= thi)).astype(jnp.int32)
            return j + adv_j, e + adv_e, adv_j

        lax.fori_loop(0, N_WORK, item, (jnp.int32(0), jnp.int32(0), jnp.int32(1)))

        def run(w, r):
            r = r + (we_ref[w] != we_ref[jnp.maximum(w - 1, 0)]).astype(jnp.int32)
            wr_ref[w] = r
            return r

        lax.fori_loop(0, N_WORK, run, jnp.int32(0))

        def following(j, nxt):
            w = N_WORK - 1 - j
            wn_ref[w] = nxt
            return jnp.where(we_ref[jnp.maximum(w - 1, 0)] != we_ref[w], we_ref[w], nxt)

        lax.fori_loop(0, N_WORK, following, jnp.int32(-1))


def _plan(cnt_i, cnt, route):
    smem = pl.BlockSpec(memory_space=pltpu.SMEM)
    work = jax.ShapeDtypeStruct((N_WORK,), jnp.int32)
    return pl.pallas_call(
        _plan_kernel,
        grid=(1,),
        in_specs=[smem, pl.BlockSpec((8, 128), lambda i: (0, 0)), pl.BlockSpec((N_TOK, 128), lambda i: (0, 0))],
        out_specs=[pl.BlockSpec((N_TOK, 128), lambda i: (0, 0))] + [smem] * 7,
        out_shape=[jax.ShapeDtypeStruct((N_TOK, 128), F32)] + [work] * 7,
        scratch_shapes=[pltpu.SMEM((N_EXPERTS + 1,), jnp.int32)],
        compiler_params=_cparams(("arbitrary",)),
        name="moe_plan",
    )(cnt_i, cnt, route)


def _dispatch_kernel(pos_s, h_ref, xs_hbm, sem):
    base = pl.program_id(0) * TM

    def copy(i, k):
        return pltpu.make_async_copy(_row_tile(h_ref, i), _row_tile(xs_hbm, pos_s[2 * (base + i) + k]), sem)

    def wait(i, carry):
        copy(i, 0).wait()
        copy(i, 1).wait()
        return carry

    for i in range(TM):
        copy(i, 0).start(priority=0)
        copy(i, 1).start(priority=1)
    lax.fori_loop(0, TM, wait, 0, unroll=8)


def _dispatch(pos, h):
    any_spec = pl.BlockSpec(memory_space=pl.ANY)
    return pl.pallas_call(
        _dispatch_kernel,
        grid_spec=pltpu.PrefetchScalarGridSpec(
            num_scalar_prefetch=1, grid=(N_TOK // TM,),
            in_specs=[pl.BlockSpec(_tiled(TM), lambda i, p: (i, 0))], out_specs=any_spec,
            scratch_shapes=[pltpu.SemaphoreType.DMA(())]),
        out_shape=jax.ShapeDtypeStruct(_tiled(N_ASSIGN), F32),
        compiler_params=_cparams(("arbitrary",)),
        name="moe_dispatch",
    )(pos, h)


def _experts_kernel(wt_s, we_s, wlo_s, whi_s, wf_s, wr_s, wn_s, xs_ref, w1_hbm, w3_hbm, w2_hbm, o_ref,
                    w1_s, w3_s, w2_s, f1_s, f3_s, f2_s, sem, *, layer):
    w = pl.program_id(0)
    first = wf_s[w] == 1
    nonempty = whi_s[w] > wlo_s[w]

    def fetch(e, slot):
        return [pltpu.make_async_copy(src.at[layer, e], dst.at[slot], sem.at[slot])
                for src, dst in ((w1_hbm, f1_s), (w3_hbm, f3_s), (w2_hbm, f2_s))]

    @pl.when(w == 0)
    def _():
        for c in fetch(we_s[0], wr_s[0] % 2):
            c.start()

    @pl.when((w == 0) | (we_s[w] != we_s[jnp.maximum(w - 1, 0)]))
    def _():
        slot = wr_s[w] % 2
        for c in fetch(we_s[w], slot):
            c.wait()
        w1_s[...] = f1_s[slot].astype(BF16)
        w3_s[...] = f3_s[slot].astype(BF16)
        w2_s[...] = f2_s[slot].astype(BF16)

        @pl.when(wn_s[w] >= 0)
        def _():
            for c in fetch(wn_s[w], 1 - slot):
                c.start()

    @pl.when(first & jnp.logical_not(nonempty))
    def _():
        o_ref[...] = jnp.zeros(o_ref.shape, F32)

    nsplit = 2
    hrows = TE // nsplit

    def expert_rows(groups, merge):
        x = [_tiles_to_rows(xs_ref, j * hrows, hrows).astype(BF16) for j in groups]
        a = [_dot(v, w1_s[...]) for v in x]
        b = [_dot(v, w3_s[...]) for v in x]
        hid = [(u * _sigmoid(u) * v).astype(BF16) for u, v in zip(a, b)]
        y = [_dot(v, w2_s[...]) for v in hid]
        for j, yj in zip(groups, y):
            row = lax.broadcasted_iota(jnp.int32, (hrows, 1), 0) + j * hrows
            mine = (row >= wlo_s[w]) & (row < whi_s[w])
            other = _tiles_to_rows(o_ref, j * hrows, hrows) if merge else 0.0
            _rows_to_tiles(o_ref, jnp.where(mine, yj, other), j * hrows)
        if not merge:
            for j in set(range(nsplit)) - set(groups):
                _rows_to_tiles(o_ref, jnp.zeros((hrows, D_MODEL), F32), j * hrows)

    need = [(wlo_s[w] < (j + 1) * hrows) & (whi_s[w] > j * hrows) for j in range(nsplit)]
    cases = (((0, 1), need[0] & need[1]), ((0,), need[0] & jnp.logical_not(need[1])),
             ((1,), need[1] & jnp.logical_not(need[0])))
    for groups, cond in cases:
        pl.when(nonempty & cond & first)(functools.partial(expert_rows, groups, False))
        pl.when(nonempty & cond & jnp.logical_not(first))(functools.partial(expert_rows, groups, True))


def _experts(work, xs, l, w1, w3, w2):
    tile = pl.BlockSpec(_tiled(TE), lambda w, wt, *_: (wt[w], 0))
    any_spec = pl.BlockSpec(memory_space=pl.ANY)
    return pl.pallas_call(
        functools.partial(_experts_kernel, layer=l),
        grid_spec=pltpu.PrefetchScalarGridSpec(
            num_scalar_prefetch=7, grid=(N_WORK,),
            in_specs=[tile, any_spec, any_spec, any_spec],
            out_specs=tile,
            scratch_shapes=[pltpu.VMEM((D_MODEL, EXPERT_FF), BF16), pltpu.VMEM((D_MODEL, EXPERT_FF), BF16),
                            pltpu.VMEM((EXPERT_FF, D_MODEL), BF16),
                            pltpu.VMEM((2, D_MODEL, EXPERT_FF), F32), pltpu.VMEM((2, D_MODEL, EXPERT_FF), F32),
                            pltpu.VMEM((2, EXPERT_FF, D_MODEL), F32), pltpu.SemaphoreType.DMA((2,))]),
        out_shape=jax.ShapeDtypeStruct(_tiled(N_ASSIGN), F32),
        compiler_params=_cparams(("arbitrary",)),
        name="moe_experts",
    )(*work, xs, w1, w3, w2)


def _moe_residual(pos_s, x_ref, route_ref, mod_ref, ys_hbm, y_s, sem):
    step = pl.program_id(0)
    nsteps = pl.num_programs(0)

    def copy(t, i, k):
        slot = t % 2
        src = _row_tile(ys_hbm, pos_s[2 * (t * TM + i) + k])
        return pltpu.make_async_copy(src, _row_tile(y_s.at[slot, k], i), sem.at[slot])

    def gather(t):
        for i in range(TM):
            copy(t, i, 0).start(priority=0)
            copy(t, i, 1).start(priority=1)

    def wait(i, carry):
        copy(step, i, 0).wait()
        copy(step, i, 1).wait()
        return carry

    pl.when(step == 0)(lambda: gather(step))
    pl.when(step + 1 < nsteps)(lambda: gather(step + 1))
    lax.fori_loop(0, TM, wait, 0, unroll=8)
    route = route_ref[...]
    lane = lax.broadcasted_iota(jnp.int32, route.shape, 1)
    slot = step % 2
    moe = (_lane_col(route, lane, 2) * _tiles_to_rows(y_s.at[slot, 0])
           + _lane_col(route, lane, 3) * _tiles_to_rows(y_s.at[slot, 1]))
    return x_ref[...] + mod_ref[:, 5 * D_MODEL:6 * D_MODEL] * moe


def _combine_inproj_kernel(pos_s, x_ref, route_ref, modp_ref, ys_hbm, mod_ref, nw_ref, w_ref,
                           xo_ref, p_ref, y_s, sem):
    x = _moe_residual(pos_s, x_ref, route_ref, modp_ref, ys_hbm, y_s, sem)
    xo_ref[...] = x
    mod = mod_ref[...]
    h = _rms(x, nw_ref[...]) * (1.0 + mod[:, D_MODEL:2 * D_MODEL]) + mod[:, 0:D_MODEL]
    p_ref[...] = _dot(h.astype(BF16), w_ref[...])


def _combine_inproj(pos, x, route, mods4, l, ys, norm_w, w_in_r):
    row = lambda w: pl.BlockSpec((TM, w), lambda i, p: (i, 0))
    mod = lambda layer: pl.BlockSpec((None, None, 1, 6 * D_MODEL), lambda i, p: (layer, _mod_index(i, TM), 0, 0))
    return pl.pallas_call(
        _combine_inproj_kernel,
        grid_spec=pltpu.PrefetchScalarGridSpec(
            num_scalar_prefetch=1, grid=(N_TOK // TM,),
            in_specs=[row(D_MODEL), row(128), mod(l - 1), pl.BlockSpec(memory_space=pl.ANY), mod(l),
                      pl.BlockSpec((1, D_MODEL), lambda i, p: (0, 0)),
                      pl.BlockSpec((None, D_MODEL, P_COLS), lambda i, p: (l, 0, 0))],
            out_specs=[row(D_MODEL), row(P_COLS)],
            scratch_shapes=[pltpu.VMEM((2, 2) + _tiled(TM), F32), pltpu.SemaphoreType.DMA((2,))]),
        out_shape=[jax.ShapeDtypeStruct((N_TOK, D_MODEL), F32), jax.ShapeDtypeStruct((N_TOK, P_COLS), F32)],
        compiler_params=_cparams(("arbitrary",)),
        name="moe_combine_inproj",
    )(pos, x, route, mods4, ys, mods4, norm_w, w_in_r)


def _combine_kernel(pos_s, x_ref, route_ref, mod_ref, fw_ref, ys_hbm, *refs, final):
    if final:
        oa_ref, ob_ref, y_s, sem = refs
    else:
        o_ref, y_s, sem = refs
    x = _moe_residual(pos_s, x_ref, route_ref, mod_ref, ys_hbm, y_s, sem)
    if final:
        y = _rms(x, fw_ref[...])
        is_ctx = pl.program_id(0) < N_CTX // TM

        @pl.when(is_ctx)
        def _():
            oa_ref[...] = y

        @pl.when(jnp.logical_not(is_ctx))
        def _():
            ob_ref[...] = y
    else:
        o_ref[...] = x


def _combine(pos, x, route, mods4, l, fw, ys, final):
    row = lambda w: pl.BlockSpec((TM, w), lambda i, p: (i, 0))
    if final:
        out_specs = _pair_specs(TM, D_MODEL, True)
        out_shape = [jax.ShapeDtypeStruct((N_CTX, D_MODEL), F32), jax.ShapeDtypeStruct((N_SMP, D_MODEL), F32)]
    else:
        out_specs = row(D_MODEL)
        out_shape = jax.ShapeDtypeStruct((N_TOK, D_MODEL), F32)
    return pl.pallas_call(
        functools.partial(_combine_kernel, final=final),
        grid_spec=pltpu.PrefetchScalarGridSpec(
            num_scalar_prefetch=1, grid=(N_TOK // TM,),
            in_specs=[row(D_MODEL), row(128),
                      pl.BlockSpec((None, None, 1, 6 * D_MODEL), lambda i, p: (l, _mod_index(i, TM), 0, 0)),
                      pl.BlockSpec((1, D_MODEL), lambda i, p: (0, 0)),
                      pl.BlockSpec(memory_space=pl.ANY)],
            out_specs=out_specs,
            scratch_shapes=[pltpu.VMEM((2, 2) + _tiled(TM), F32), pltpu.SemaphoreType.DMA((2,))]),
        out_shape=out_shape,
        compiler_params=_cparams(("arbitrary",)),
        name="moe_combine_final" if final else "moe_combine",
    )(pos, x, route, mods4, fw, ys)


def _rope_swap(w):
    nf = MLA_ROPE // 4
    g = w.reshape(w.shape[:-1] + (2, 2, nf))
    return jnp.stack([-g[..., 1, :], g[..., 0, :]], axis=-2).reshape(w.shape)


def _rope_tables():
    nf = MLA_ROPE // 4
    pos = np.arange(DEC_SEQ)
    row = (pos // GRID_W).astype(np.float32)
    col = (pos % GRID_W).astype(np.float32)
    inv = (np.float32(ROPE_THETA) ** (-np.arange(nf, dtype=np.float32) / np.float32(nf))).astype(np.float32)
    ar = (row[:, None] * inv).astype(np.float32)
    ac = (col[:, None] * inv).astype(np.float32)
    cos = np.concatenate([np.cos(ar), np.cos(ar), np.cos(ac), np.cos(ac)], axis=1)
    sin = np.concatenate([np.sin(ar), np.sin(ar), np.sin(ac), np.sin(ac)], axis=1)
    cos = np.concatenate([np.ones((TM, MLA_ROPE)), cos], axis=0).astype(np.float32)
    sin = np.concatenate([np.zeros((TM, MLA_ROPE)), sin], axis=0).astype(np.float32)
    n = cos.shape[0]
    z32 = np.zeros((n, 32), np.float32)
    tqc = np.concatenate([np.ones((n, MLA_NOPE), np.float32), cos, z32], axis=1)
    tqs = np.concatenate([np.zeros((n, MLA_NOPE), np.float32), sin, z32], axis=1)
    tk = np.concatenate([z32, cos, sin, z32], axis=1)
    return jnp.asarray(tqc), jnp.asarray(tqs), jnp.asarray(tk)


W_IN_COLS = 1984
W_IN_ROWS = 256


def _w_in_relayout_kernel(w_ref, sh_ref, s1_ref, s2_ref, o_ref):
    gate_cols = 4 * GLA_W
    o_ref[:, 0:gate_cols] = w_ref[:, 0:gate_cols].astype(BF16)
    for j in range(gate_cols // 128, MISC_BLK):
        hi = min(128 * j + 256, W_IN_COLS)
        src = w_ref[:, 128 * j:hi].astype(BF16)
        o_ref[:, 128 * j:128 * (j + 1)] = _dot(src, sh_ref[0:hi - 128 * j, :]).astype(BF16)
    head = w_ref[:, gate_cols:gate_cols + 128].astype(BF16)
    tail = w_ref[:, 128 * MISC_BLK:W_IN_COLS].astype(BF16)
    o_ref[:, 128 * MISC_BLK:P_COLS] = (_dot(head, s1_ref[...]) + _dot(tail, s2_ref[...])).astype(BF16)


def _w_in_relayout(w_in):
    sh = np.zeros((256, 128), np.float32)
    sh[32 + np.arange(128), np.arange(128)] = 1.0
    s1 = np.zeros((128, 128), np.float32)
    s1[np.arange(32), np.arange(32)] = 1.0
    s2 = np.zeros((W_IN_COLS - 128 * MISC_BLK, 128), np.float32)
    nf = MLA_ROPE // 4
    for j in range(MLA_ROPE):
        s2[32 + j, 32 + j] = 1.0
        half = (j // nf) % 2
        src = j + nf if half == 0 else j - nf
        s2[32 + src, 64 + j] = -1.0 if half == 0 else 1.0
    consts = [jnp.asarray(a).astype(BF16) for a in (sh, s1, s2)]
    full = lambda a: pl.BlockSpec(a.shape, lambda l, i: (0, 0))
    return pl.pallas_call(
        _w_in_relayout_kernel,
        grid=(DEPTH, D_MODEL // W_IN_ROWS),
        in_specs=[pl.BlockSpec((None, W_IN_ROWS, W_IN_COLS), lambda l, i: (l, i, 0))] + [full(a) for a in consts],
        out_specs=pl.BlockSpec((None, W_IN_ROWS, P_COLS), lambda l, i: (l, i, 0)),
        out_shape=jax.ShapeDtypeStruct((DEPTH, D_MODEL, P_COLS), BF16),
        compiler_params=_cparams(("parallel", "parallel")),
        name="w_in_relayout",
    )(w_in, *consts)


def _prep_layer(l, gla_wg_f, gla_wg_b, mla_wuq, mla_wukv, w_out, moe_wg, moe_bg, moe_we, moe_be):
    wgf = jnp.zeros((128, GLA_W), F32).at[0:16].set(gla_wg_f[l])
    wgb = jnp.zeros((128, GLA_W), F32).at[16:32].set(gla_wg_b[l])
    wq = mla_wuq[l].reshape(MLA_Q_RANK, MLA_HEADS, MLA_NOPE + MLA_ROPE)
    nope, rope = wq[..., :MLA_NOPE], wq[..., MLA_NOPE:]
    z = lambda n: jnp.zeros((MLA_Q_RANK, MLA_HEADS, n), F32)
    hw = MLA_HEADS * HEAD_BLK
    wa = jnp.concatenate([nope, rope, z(32)], axis=-1).reshape(MLA_Q_RANK, hw).astype(BF16)
    wb = jnp.concatenate([z(MLA_NOPE), _rope_swap(rope), z(32)], axis=-1).reshape(MLA_Q_RANK, hw).astype(BF16)
    wkv = mla_wukv[l].reshape(MLA_KV_RANK, MLA_HEADS, MLA_NOPE + MLA_V)
    wk = jnp.concatenate([wkv[..., :MLA_NOPE], jnp.zeros((MLA_KV_RANK, MLA_HEADS, 64), F32)],
                         axis=-1).reshape(MLA_KV_RANK, hw).astype(BF16)
    wv = wkv[..., MLA_NOPE:].reshape(MLA_KV_RANK, MLA_HEADS * MLA_V).astype(BF16)
    wp_np = np.zeros((128, MLA_HEADS, HEAD_BLK), np.float32)
    for j in range(MLA_ROPE):
        wp_np[32 + j, :, MLA_NOPE + j] = 1.0
        wp_np[64 + j, :, MLA_NOPE + j] = 1.0
    wp = jnp.asarray(wp_np.reshape(128, hw)).astype(BF16)
    wo = w_out[l].astype(BF16)
    wr = jnp.concatenate([moe_we[l], moe_wg[l], jnp.zeros((D_MODEL, 128 - N_EXPERTS - N_GROUPS), F32)], axis=1)
    br = jnp.concatenate([moe_be[l], moe_bg[l], jnp.zeros((128 - N_EXPERTS - N_GROUPS,), F32)]).reshape(1, 128)
    return dict(wgf=wgf, wgb=wgb, wa=wa, wb=wb, wk=wk, wv=wv, wp=wp, wo=wo, wr=wr, br=br)


def kernel(x_prompt, x_sample, cache_ckv, cache_krope, state_gla_fwd, state_gla_bwd, c, c_ctx, ada_w, ada_b,
           norm1_w, norm2_w, w_in, gla_wg_f, gla_bg_f, gla_wg_b, gla_bg_b, gla_norm_w, conv_w, conv_b,
           conv_ln_w, conv_ln_b, mla_qnorm_w, mla_wuq, mla_kvnorm_w, mla_wukv, w_out, moe_wg, moe_bg, moe_we,
           moe_be, moe_w1, moe_w3, moe_w2, final_norm_w):
    x = (x_prompt.reshape(N_CTX, D_MODEL), x_sample.reshape(N_SMP, D_MODEL))
    cond8 = jnp.concatenate([c_ctx[None, :], c, jnp.zeros((N_MOD - 1 - DEC_BATCH, D_MODEL), F32)], axis=0)
    mods4 = _ada_table(cond8, ada_w, ada_b).reshape(DEPTH, N_MOD, 1, 6 * D_MODEL)
    tabs = _rope_tables()
    w_in_r = _w_in_relayout(w_in)
    row1 = lambda a: a.reshape(1, -1)

    ckv_new, kr_new, sf_new, sb_new = [], [], [], []
    for l in range(DEPTH):
        w = _prep_layer(l, gla_wg_f, gla_wg_b, mla_wuq, mla_wukv, w_out, moe_wg, moe_bg, moe_we, moe_be)
        if l == 0:
            p = _inproj(x, mods4, l, row1(norm1_w[l]), w_in_r)
        else:
            x, p = _combine_inproj(pos, xn, route, mods4, l, ys, row1(norm1_w[l]), w_in_r)

        gla_args = (p, w["wgf"], w["wgb"], row1(gla_bg_f[l]), row1(gla_bg_b[l]), row1(gla_norm_w[l]))
        og_c, sf, sb = _gla(*gla_args, SEQ, BATCH, 0)
        og_s = _gla(*gla_args, DEC_SEQ, DEC_BATCH, N_CTX // DEC_SEQ, (state_gla_fwd, l), (state_gla_bwd, l))[0]
        og = (og_c, og_s)

        cw = jnp.concatenate([conv_w[l], jnp.zeros((1, CONV_W), F32)], axis=0)
        conv_args = (p, cw, row1(conv_b[l]), row1(conv_ln_w[l]), row1(conv_ln_b[l]))
        oc = (_conv(*conv_args, SEQ, BATCH, 0), _conv(*conv_args, DEC_SEQ, DEC_BATCH, N_CTX // DEC_SEQ))

        q, k, v, ckvn, kr = _mla_prep(p, tabs, row1(mla_qnorm_w[l]), row1(mla_kvnorm_w[l]),
                                      w["wa"], w["wb"], w["wk"], w["wp"], w["wv"])
        cc = cache_ckv[:, l].reshape(DEC_BATCH * PAST_LEN, MLA_KV_RANK)
        ckr = cache_krope[:, l].reshape(DEC_BATCH * PAST_LEN, MLA_ROPE)
        ckr = jnp.pad(ckr, ((0, 0), (32, 64)))
        kc, vc = _cache_kv(cc, ckr, w["wk"], w["wp"], w["wv"])
        om = (_attention(q, k, v, SEQ, BATCH, 0, SEQ, MLA_HEADS // 2),
              _attention(q, k, v, DEC_SEQ, DEC_BATCH, N_CTX, DEC_SEQ, 1, kc, vc))

        xn, h2, route, cnt = _outproj(x, og, oc, om, mods4, l, row1(norm2_w[l]), w["wo"], w["wr"], w["br"])
        posf, *work = _plan(cnt[0, :N_EXPERTS].astype(jnp.int32), cnt, route)
        pos = posf[:, 0:2].astype(jnp.int32).reshape(N_ASSIGN)
        ys = _experts(work, _dispatch(pos, h2), l, moe_w1, moe_w3, moe_w2)
        if l == DEPTH - 1:
            x = _combine(pos, xn, route, mods4, l, row1(final_norm_w), ys, True)

        ckv_new.append(ckvn[:N_CTX].reshape(BATCH, SEQ, MLA_KV_RANK))
        kr_new.append(kr[:N_CTX, 32:32 + MLA_ROPE].reshape(BATCH, SEQ, MLA_ROPE))
        sf_new.append(sf)
        sb_new.append(sb)

    y_ctx, y_smp = x
    return (y_ctx.reshape(BATCH, SEQ, D_MODEL), y_smp.reshape(DEC_BATCH, DEC_SEQ, D_MODEL),
            jnp.stack(ckv_new, axis=1), jnp.stack(kr_new, axis=1),
            jnp.stack(sf_new, axis=1), jnp.stack(sb_new, axis=1))
```

```python
import functools

import numpy as np
import jax
import jax.numpy as jnp
from jax import lax
from jax.experimental import pallas as pl
from jax.experimental.pallas import tpu as pltpu

F32 = jnp.float32
BF16 = jnp.bfloat16

D_MODEL = 1024
BATCH = 16
SEQ = 256
DEPTH = 2
DEC_BATCH = 4
DEC_SEQ = 1024
PAST_LEN = 512
GRID_W = 64
NORM_EPS = 1e-6
GLA_HEADS = 4
GLA_DK = 64
GLA_DV = 64
GLA_W = 256
GLA_GATE_RANK = 16
GLA_TAU = 16.0
CONV_W = 256
CONV_K = 31
MLA_HEADS = 8
MLA_NOPE = 64
MLA_ROPE = 32
MLA_V = 64
MLA_Q_RANK = 256
MLA_KV_RANK = 128
ROPE_THETA = 10000.0
N_GROUPS = 4
EXPERTS_PER_GROUP = 8
N_EXPERTS = 32
EXPERT_FF = 256

N_CTX = BATCH * SEQ
N_SMP = DEC_BATCH * DEC_SEQ
N_TOK = N_CTX + N_SMP
N_MOD = 8
TM = 512
P_COLS = 2048
MISC_BLK = 15
CHUNK = 64
SUB = 16
NEG = -1e30
VMEM_LIMIT = 56 * 1024 * 1024


def _cparams(sem):
    return pltpu.CompilerParams(dimension_semantics=sem, vmem_limit_bytes=VMEM_LIMIT)


def _dot(a, b):
    return jnp.dot(a, b, preferred_element_type=F32)


def _dot_nt(a, b):
    return lax.dot_general(a, b, (((1,), (1,)), ((), ())), preferred_element_type=F32)


def _split2(x):
    hi = x.astype(BF16)
    lo = (x - hi.astype(F32)).astype(BF16)
    return hi, lo


def _dot3(a, w):
    ah, al = _split2(a)
    wh, wl = _split2(w)
    return _dot(ah, wh) + _dot(al, wh) + _dot(ah, wl)


def _sigmoid(x):
    return 1.0 / (1.0 + jnp.exp(-x))


def _rms(x, w):
    ms = jnp.mean(x * x, axis=-1, keepdims=True)
    return x * lax.rsqrt(ms + NORM_EPS) * w


def _mod_index(i, tile):
    nctx = N_CTX // tile
    per = DEC_SEQ // tile
    return jnp.where(i < nctx, 0, 1 + (i - nctx) // per)


def _ada_kernel(c_ref, w_ref, b_ref, o_ref):
    c = c_ref[...]
    o_ref[...] = _dot3(c * _sigmoid(c), w_ref[...]) + b_ref[...]


def _ada_table(cond8, ada_w, ada_b):
    tn = 1024
    n6 = 6 * D_MODEL
    return pl.pallas_call(
        _ada_kernel,
        grid=(DEPTH, n6 // tn),
        in_specs=[
            pl.BlockSpec((N_MOD, D_MODEL), lambda l, j: (0, 0)),
            pl.BlockSpec((None, D_MODEL, tn), lambda l, j: (l, 0, j)),
            pl.BlockSpec((None, 1, tn), lambda l, j: (l, 0, j)),
        ],
        out_specs=pl.BlockSpec((None, N_MOD, tn), lambda l, j: (l, 0, j)),
        out_shape=jax.ShapeDtypeStruct((DEPTH, N_MOD, n6), F32),
        compiler_params=_cparams(("parallel", "parallel")),
        name="ada_table",
    )(cond8, ada_w, ada_b.reshape(DEPTH, 1, n6))


def _pair_specs(tile, width, split):
    nctx = N_CTX // tile
    offb = 0 if split else nctx
    return [pl.BlockSpec((tile, width), lambda i, *_: (jnp.minimum(i, nctx - 1), 0)),
            pl.BlockSpec((tile, width), lambda i, *_: (jnp.maximum(i - nctx, 0) + offb, 0))]


def _pair(x):
    return x if isinstance(x, tuple) else (x, x)


TM_IN = 512


def _inproj_kernel(xa_ref, xb_ref, mod_ref, nw_ref, w_ref, o_ref):
    mod = mod_ref[...]

    def project(x_ref):
        h = _rms(x_ref[...], nw_ref[...]) * (1.0 + mod[:, D_MODEL:2 * D_MODEL]) + mod[:, 0:D_MODEL]
        o_ref[...] = _dot(h.astype(BF16), w_ref[...])

    is_ctx = pl.program_id(0) < N_CTX // TM_IN
    pl.when(is_ctx)(lambda: project(xa_ref))
    pl.when(jnp.logical_not(is_ctx))(lambda: project(xb_ref))


def _inproj(x, mods4, l, norm_w, w_in_r):
    return pl.pallas_call(
        _inproj_kernel,
        grid=(N_TOK // TM_IN,),
        in_specs=_pair_specs(TM_IN, D_MODEL, isinstance(x, tuple)) + [
            pl.BlockSpec((None, None, 1, 6 * D_MODEL), lambda i: (l, _mod_index(i, TM_IN), 0, 0)),
            pl.BlockSpec((1, D_MODEL), lambda i: (0, 0)),
            pl.BlockSpec((None, D_MODEL, P_COLS), lambda i: (l, 0, 0)),
        ],
        out_specs=pl.BlockSpec((TM_IN, P_COLS), lambda i: (i, 0)),
        out_shape=jax.ShapeDtypeStruct((N_TOK, P_COLS), F32),
        compiler_params=_cparams(("parallel",)),
        name="inproj",
    )(*_pair(x), mods4, norm_w, w_in_r)


def _log_sigmoid(x):
    return jnp.minimum(x, 0.0) - jnp.log(1.0 + jnp.exp(-jnp.abs(x)))


def _split3(x):
    hi = x.astype(BF16)
    r = x - hi.astype(F32)
    mid = r.astype(BF16)
    lo = (r - mid.astype(F32)).astype(BF16)
    return hi, mid, lo


def _gla_consts(reverse):
    t = lax.broadcasted_iota(jnp.int32, (CHUNK, CHUNK), 0)
    s = lax.broadcasted_iota(jnp.int32, (CHUNK, CHUNK), 1)
    mid = jnp.bitwise_and(t, -SUB) + SUB // 2
    if reverse:
        cum = s >= t
        ref = s >= mid
    else:
        cum = s <= t
        ref = s <= mid
    cum_ref = jnp.concatenate([cum, ref], axis=0).astype(F32).astype(BF16)
    rt = jnp.bitwise_and(lax.broadcasted_iota(jnp.int32, (GLA_HEADS * SUB, CHUNK), 0), SUB - 1)
    cs = lax.broadcasted_iota(jnp.int32, (GLA_HEADS * SUB, CHUNK), 1)
    rowi = lax.broadcasted_iota(jnp.int32, (CHUNK, 1), 0)
    causal, valid = [], []
    for i in range(CHUNK // SUB):
        tq = rt + i * SUB
        causal.append((cs >= tq) if reverse else (cs <= tq))
        valid.append((rowi >= i * SUB) if reverse else (rowi < (i + 1) * SUB))
    return cum_ref, causal, valid


def _gla_local(jobs, nc, o_s, qe_s, u_s, dec_s, head_masks, block_mask):
    nsub = CHUNK // SUB
    rows = [pl.ds(pl.multiple_of(c * CHUNK, CHUNK), CHUNK) for c, *_ in jobs]
    bs = []
    for (c, d, qs, k, vb, vt, la_ref, consts), r in zip(jobs, rows):
        h0, h1, h2 = _split3(la_ref[r, :])
        br = _dot(consts[0], h0) + _dot(consts[0], h1) + _dot(consts[0], h2)
        bs.append((br[0:CHUNK], br[CHUNK:2 * CHUNK]))
    ops = []
    for (c, d, qs, k, vb, vt, la_ref, consts), (b, rr) in zip(jobs, bs):
        qhat = qs * jnp.exp(b - rr)
        for i in range(nsub):
            r_i = rr[i * SUB:i * SUB + 1]
            kt = (k * jnp.exp(jnp.where(consts[2][i], r_i - b, NEG))).astype(BF16)
            qi = qhat[i * SUB:(i + 1) * SUB]
            ops.append((jnp.concatenate([jnp.where(m, qi, 0.0) for m in head_masks], axis=0).astype(BF16), kt))
    scores = [_dot_nt(qbig, kt) for qbig, kt in ops]
    outs = []
    for j, (c, d, qs, k, vb, vt, la_ref, consts) in enumerate(jobs):
        for i in range(nsub):
            a = jnp.where(consts[1][i], scores[j * nsub + i], 0.0)
            outs.append(_dot(a.astype(BF16), vb))
    for j, ((c, d, qs, k, vb, vt, la_ref, consts), (b, rr), r) in enumerate(zip(jobs, bs, rows)):
        parts = []
        for i in range(nsub):
            ov = outs[j * nsub + i]
            oi = jnp.where(head_masks[0], ov[0:SUB], 0.0)
            for h in range(1, GLA_HEADS):
                oi = oi + jnp.where(head_masks[h], ov[h * SUB:(h + 1) * SUB], 0.0)
            parts.append(oi)
        o_s[d, r, :] = jnp.concatenate(parts, axis=0)
        qe_s[d, r, :] = (qs * jnp.exp(b)).astype(BF16)
        b_last = b[0:1] if d == 1 else b[CHUNK - 1:CHUNK]
        kl = (k * jnp.exp(b_last - b)).astype(BF16)
        idx = d * nc + c
        u_s[idx] = jnp.where(block_mask, _dot(vt, kl), 0.0)
        dec_s[pl.ds(pl.multiple_of(idx * 8, 8), 8), :] = jnp.broadcast_to(jnp.exp(b_last), (8, GLA_W))


GLA_SAFE_EXPONENT = 60.0


def _gla_exact_intra(d, nc, qkvg_ref, la_ref, cum, o_s, b_s, acc_s, ones_blk):
    reverse = d == 1
    tcol = lax.broadcasted_iota(jnp.int32, (CHUNK, 1), 0)

    def chunk(c, carry):
        r0 = pl.multiple_of(c * CHUNK, CHUNK)
        rows = pl.ds(r0, CHUNK)
        h0, h1, h2 = _split3(la_ref[rows, :])
        b_s[...] = _dot(cum, h0) + _dot(cum, h1) + _dot(cum, h2)
        acc_s[...] = jnp.zeros((CHUNK, GLA_W), F32)
        qs = qkvg_ref[rows, 0:GLA_W] * (GLA_DK ** -0.5)

        def key(s, carry2):
            bs = b_s[pl.ds(s, 1), :]
            ks = qkvg_ref[pl.ds(r0 + s, 1), GLA_W:2 * GLA_W]
            vs = qkvg_ref[pl.ds(r0 + s, 1), 2 * GLA_W:3 * GLA_W]
            live = (tcol <= s) if reverse else (tcol >= s)
            hi, lo = _split2(qs * ks * jnp.exp(jnp.where(live, b_s[...] - bs, NEG)))
            acc_s[...] += (_dot(hi, ones_blk) + _dot(lo, ones_blk)) * vs
            return carry2

        lax.fori_loop(0, CHUNK, key, 0)
        o_s[d, rows, :] = acc_s[...]
        return carry

    lax.fori_loop(0, nc, chunk, 0)


def _gla_kernel(*refs, n, has_state):
    if has_state:
        (qkvg_ref, misc_ref, wgf_ref, wgb_ref, bgf_ref, bgb_ref, nw_ref, s0f_ref, s0b_ref,
         o_ref, laf_s, lab_s, o_s, qe_s, u_s, dec_s, sts_s, stf_s, stb_s, b_s, acc_s) = refs
    else:
        (qkvg_ref, misc_ref, wgf_ref, wgb_ref, bgf_ref, bgb_ref, nw_ref,
         o_ref, sf_ref, sb_ref, laf_s, lab_s, o_s, qe_s, u_s, dec_s, sts_s, stf_s, stb_s, b_s, acc_s) = refs
    mh, ml = _split2(misc_ref[...])
    gates = []
    for w_ref in (wgf_ref, wgb_ref):
        wh, wl = _split2(w_ref[...])
        gates.append((_dot(mh, wh), _dot(ml, wh), _dot(mh, wl)))
    laf_s[...] = _log_sigmoid(gates[0][0] + gates[0][1] + gates[0][2] + bgf_ref[...]) * (1.0 / GLA_TAU)
    lab_s[...] = _log_sigmoid(gates[1][0] + gates[1][1] + gates[1][2] + bgb_ref[...]) * (1.0 / GLA_TAU)
    stf_s[...] = jnp.zeros((GLA_W, GLA_W), F32)
    stb_s[...] = jnp.zeros((GLA_W, GLA_W), F32)
    if has_state:
        for st_s, s0_ref in ((stf_s, s0f_ref), (stb_s, s0b_ref)):
            for h in range(GLA_HEADS):
                st_s[h * GLA_DK:(h + 1) * GLA_DK, h * GLA_DV:(h + 1) * GLA_DV] = s0_ref[h]
            st_s[...] = st_s[...].T

    lane = lax.broadcasted_iota(jnp.int32, (1, GLA_W), 1)
    head_masks = [jnp.right_shift(lane, 6) == h for h in range(GLA_HEADS)]
    bi = jnp.right_shift(lax.broadcasted_iota(jnp.int32, (GLA_W, GLA_W), 0), 6)
    bj = jnp.right_shift(lax.broadcasted_iota(jnp.int32, (GLA_W, GLA_W), 1), 6)
    block_mask = bi == bj
    consts_f = _gla_consts(False)
    consts_b = _gla_consts(True)
    nc = n // CHUNK

    def local(cc, carry):
        jobs = []
        for c in (2 * cc, 2 * cc + 1):
            rows = pl.ds(pl.multiple_of(c * CHUNK, CHUNK), CHUNK)
            qs = qkvg_ref[rows, 0:GLA_W] * (GLA_DK ** -0.5)
            k = qkvg_ref[rows, GLA_W:2 * GLA_W]
            v = qkvg_ref[rows, 2 * GLA_W:3 * GLA_W]
            vb = v.astype(BF16)
            vt = v.T.astype(BF16)
            jobs += [(c, 0, qs, k, vb, vt, laf_s, consts_f), (c, 1, qs, k, vb, vt, lab_s, consts_b)]
        _gla_local(jobs, nc, o_s, qe_s, u_s, dec_s, head_masks, block_mask)
        return carry

    lax.fori_loop(0, nc // 2, local, 0)

    ones_blk = block_mask.astype(F32).astype(BF16)
    @pl.when(jnp.min(jnp.minimum(laf_s[...], lab_s[...])) * (SUB // 2) < -GLA_SAFE_EXPONENT)
    def _():
        for d, (la_s, consts) in enumerate(((laf_s, consts_f), (lab_s, consts_b))):
            _gla_exact_intra(d, nc, qkvg_ref, la_s, consts[0][0:CHUNK], o_s, b_s, acc_s, ones_blk)

    def scan(j, carry):
        for d, st_s in enumerate((stf_s, stb_s)):
            idx = d * nc + (j if d == 0 else nc - 1 - j)
            st = st_s[...]
            sts_s[idx] = st.astype(BF16)
            st_s[...] = st * dec_s[pl.ds(pl.multiple_of(idx * 8, 8), 1), :] + u_s[idx]
        return carry

    lax.fori_loop(0, nc, scan, 0)

    def inter(cc, carry):
        jobs = [(c, d, pl.ds(pl.multiple_of(c * CHUNK, CHUNK), CHUNK))
                for c in (2 * cc, 2 * cc + 1) for d in range(2)]
        res = [_dot_nt(qe_s[d, r, :], sts_s[d * nc + c]) for c, d, r in jobs]
        for (c, d, r), o in zip(jobs, res):
            o_s[d, r, :] += o
        return carry

    lax.fori_loop(0, nc // 2, inter, 0)

    if not has_state:
        for st_s, s_ref in ((stf_s, sf_ref), (stb_s, sb_ref)):
            s = st_s[...].T
            for h in range(GLA_HEADS):
                s_ref[h] = s[h * GLA_DK:(h + 1) * GLA_DK, h * GLA_DV:(h + 1) * GLA_DV]

    rt = 256
    for r0 in range(0, n, rt):
        o = o_s[0, r0:r0 + rt, :] + o_s[1, r0:r0 + rt, :]
        hi, lo = _split2(o * o)
        ms = (_dot(hi, ones_blk) + _dot(lo, ones_blk)) * (1.0 / GLA_DV)
        g = qkvg_ref[r0:r0 + rt, 3 * GLA_W:4 * GLA_W]
        o_ref[r0:r0 + rt, :] = o * lax.rsqrt(ms + NORM_EPS) * nw_ref[...] * (g * _sigmoid(g))


def _gla(p, wgf, wgb, bgf, bgb, nw, n, nb, row_blk0, s0f=None, s0b=None):
    has_state = s0f is not None
    full = lambda shape: pl.BlockSpec(shape, lambda b: (0,) * len(shape))
    in_specs = [
        pl.BlockSpec((n, 4 * GLA_W), lambda b: (row_blk0 + b, 0)),
        pl.BlockSpec((n, 128), lambda b: (row_blk0 + b, MISC_BLK)),
        full((128, GLA_W)), full((128, GLA_W)), full((1, GLA_W)), full((1, GLA_W)), full((1, GLA_W)),
    ]
    args = [p, p, wgf, wgb, bgf, bgb, nw]
    out_specs = [pl.BlockSpec((n, GLA_W), lambda b: (b, 0))]
    out_shape = [jax.ShapeDtypeStruct((nb * n, GLA_W), F32)]
    if has_state:
        layer = s0f[1]
        st_spec = pl.BlockSpec((None, None, GLA_HEADS, GLA_DK, GLA_DV), lambda b: (b, layer, 0, 0, 0))
        in_specs += [st_spec, st_spec]
        args += [s0f[0], s0b[0]]
    else:
        st_spec = pl.BlockSpec((None, GLA_HEADS, GLA_DK, GLA_DV), lambda b: (b, 0, 0, 0))
        out_specs += [st_spec, st_spec]
        out_shape += [jax.ShapeDtypeStruct((nb, GLA_HEADS, GLA_DK, GLA_DV), F32)] * 2
    return pl.pallas_call(
        functools.partial(_gla_kernel, n=n, has_state=has_state),
        grid=(nb,),
        in_specs=in_specs,
        out_specs=out_specs,
        out_shape=out_shape,
        scratch_shapes=[
            pltpu.VMEM((n, GLA_W), F32), pltpu.VMEM((n, GLA_W), F32),
            pltpu.VMEM((2, n, GLA_W), F32),
            pltpu.VMEM((2, n, GLA_W), BF16),
            pltpu.VMEM((2 * n // CHUNK, GLA_W, GLA_W), F32),
            pltpu.VMEM((2 * n // CHUNK * 8, GLA_W), F32),
            pltpu.VMEM((2 * n // CHUNK, GLA_W, GLA_W), BF16),
            pltpu.VMEM((GLA_W, GLA_W), F32), pltpu.VMEM((GLA_W, GLA_W), F32),
            pltpu.VMEM((CHUNK, GLA_W), F32), pltpu.VMEM((CHUNK, GLA_W), F32),
        ],
        compiler_params=_cparams(("parallel",)),
        name="gla_state" if has_state else "gla_ctx",
    )(*args)


CONV_PAD = 16
CONV_ROWS = 128
CONV_SHIFT_ROWS = 24


def _conv_kernel(u_ref, w_ref, b_ref, lnw_ref, lnb_ref, o_ref, zp_s, zsh_s, *, n):
    u = u_ref[...]
    zp_s[0:CONV_PAD, :] = jnp.zeros((CONV_PAD, CONV_W), F32)
    zp_s[CONV_PAD + n:2 * CONV_PAD + n, :] = jnp.zeros((CONV_PAD, CONV_W), F32)
    zp_s[CONV_PAD:CONV_PAD + n, :] = u[:, 0:CONV_W] * _sigmoid(u[:, CONV_W:2 * CONV_W])
    m = n + CONV_SHIFT_ROWS
    for s in range(1, 8):
        zsh_s[s - 1] = zp_s[s:s + m, :]
    off = CONV_PAD - CONV_K // 2

    def tap(r0, j):
        s = (off + j) % 8
        base = r0 + off + j - s
        return zp_s[base:base + CONV_ROWS, :] if s == 0 else zsh_s[s - 1, base:base + CONV_ROWS, :]

    for r0 in range(0, n, CONV_ROWS):
        acc = tap(r0, 0) * w_ref[0:1, :]
        for j in range(1, CONV_K):
            acc = acc + tap(r0, j) * w_ref[j:j + 1, :]
        acc = acc + b_ref[...]
        mu = jnp.mean(acc, axis=-1, keepdims=True)
        xc = acc - mu
        var = jnp.mean(xc * xc, axis=-1, keepdims=True)
        y = xc * lax.rsqrt(var + NORM_EPS) * lnw_ref[...] + lnb_ref[...]
        o_ref[r0:r0 + CONV_ROWS, :] = y * _sigmoid(y)


def _conv(p, cw, cb, lnw, lnb, n, nb, row_blk0):
    full = lambda shape: pl.BlockSpec(shape, lambda b: (0,) * len(shape))
    return pl.pallas_call(
        functools.partial(_conv_kernel, n=n),
        grid=(nb,),
        in_specs=[
            pl.BlockSpec((n, 2 * CONV_W), lambda b: (row_blk0 + b, 2)),
            full((32, CONV_W)), full((1, CONV_W)), full((1, CONV_W)), full((1, CONV_W)),
        ],
        out_specs=pl.BlockSpec((n, CONV_W), lambda b: (b, 0)),
        out_shape=jax.ShapeDtypeStruct((nb * n, CONV_W), F32),
        scratch_shapes=[pltpu.VMEM((n + 2 * CONV_PAD, CONV_W), F32),
                        pltpu.VMEM((7, n + CONV_SHIFT_ROWS, CONV_W), F32)],
        compiler_params=_cparams(("parallel",)),
        name="conv%d" % n,
    )(p, cw, cb, lnw, lnb)


HEAD_BLK = 128


def _kv_expand(ckvn, t, wk_ref, wp_ref, wv_ref, k_ref, v_ref):
    cb = ckvn.astype(BF16)
    th, tl = _split2(t)
    k_ref[...] = (_dot(cb, wk_ref[...]) + _dot(th, wp_ref[...]) + _dot(tl, wp_ref[...])).astype(BF16)
    v_ref[...] = _dot(cb, wv_ref[...]).astype(BF16)


def _mla_prep_kernel(cq_ref, cm_ref, tqc_ref, tqs_ref, tk_ref, qnw_ref, kvw_ref, wa_ref, wb_ref,
                     wk_ref, wp_ref, wv_ref, q_ref, k_ref, v_ref, ckv_ref, kr_ref):
    qn = _rms(cq_ref[...], qnw_ref[...]).astype(BF16)
    cm = cm_ref[...]
    ckvn = _rms(cm[:, 0:MLA_KV_RANK], kvw_ref[...])
    t = cm[:, MLA_KV_RANK:2 * MLA_KV_RANK] * tk_ref[...]
    cb = ckvn.astype(BF16)
    th, tl = _split2(t)
    qa = _dot(qn, wa_ref[...])
    qb = _dot(qn, wb_ref[...])
    kk = _dot(cb, wk_ref[...]) + _dot(th, wp_ref[...]) + _dot(tl, wp_ref[...])
    vv = _dot(cb, wv_ref[...])
    tqc = jnp.concatenate([tqc_ref[...]] * MLA_HEADS, axis=1)
    tqs = jnp.concatenate([tqs_ref[...]] * MLA_HEADS, axis=1)
    scale = (MLA_NOPE + MLA_ROPE) ** -0.5 * float(np.log2(np.e))
    q_ref[...] = ((qa * tqc + qb * tqs) * scale).astype(BF16)
    k_ref[...] = kk.astype(BF16)
    v_ref[...] = vv.astype(BF16)
    ckv_ref[...] = ckvn
    kr_ref[...] = t


def _rope_tile(i):
    nctx = N_CTX // TM
    return jnp.where(i < nctx, 0, 1 + (i - nctx) % (DEC_SEQ // TM))


def _mla_prep(p, tabs, qnw, kvw, wa, wb, wk, wp, wv):
    tqc, tqs, tk = tabs
    full = lambda shape: pl.BlockSpec(shape, lambda i: (0,) * len(shape))
    tab_spec = pl.BlockSpec((TM, HEAD_BLK), lambda i: (_rope_tile(i), 0))
    hw = MLA_HEADS * HEAD_BLK
    return pl.pallas_call(
        _mla_prep_kernel,
        grid=(N_TOK // TM,),
        in_specs=[
            pl.BlockSpec((TM, MLA_Q_RANK), lambda i: (i, 6)),
            pl.BlockSpec((TM, 2 * MLA_KV_RANK), lambda i: (i, 7)),
            tab_spec, tab_spec, tab_spec,
            full((1, MLA_Q_RANK)), full((1, MLA_KV_RANK)),
            full((MLA_Q_RANK, hw)), full((MLA_Q_RANK, hw)),
            full((MLA_KV_RANK, hw)), full((128, hw)), full((MLA_KV_RANK, MLA_HEADS * MLA_V)),
        ],
        out_specs=[
            pl.BlockSpec((TM, hw), lambda i: (i, 0)),
            pl.BlockSpec((TM, hw), lambda i: (i, 0)),
            pl.BlockSpec((TM, MLA_HEADS * MLA_V), lambda i: (i, 0)),
            pl.BlockSpec((TM, MLA_KV_RANK), lambda i: (i, 0)),
            pl.BlockSpec((TM, 128), lambda i: (i, 0)),
        ],
        out_shape=[
            jax.ShapeDtypeStruct((N_TOK, hw), BF16),
            jax.ShapeDtypeStruct((N_TOK, hw), BF16),
            jax.ShapeDtypeStruct((N_TOK, MLA_HEADS * MLA_V), BF16),
            jax.ShapeDtypeStruct((N_TOK, MLA_KV_RANK), F32),
            jax.ShapeDtypeStruct((N_TOK, 128), F32),
        ],
        compiler_params=_cparams(("parallel",)),
        name="mla_prep",
    )(p, p, tqc, tqs, tk, qnw, kvw, wa, wb, wk, wp, wv)


def _cache_kv_kernel(ckv_ref, kr_ref, wk_ref, wp_ref, wv_ref, k_ref, v_ref):
    _kv_expand(ckv_ref[...], kr_ref[...], wk_ref, wp_ref, wv_ref, k_ref, v_ref)


def _cache_kv(ckv, kr, wk, wp, wv):
    n = DEC_BATCH * PAST_LEN
    full = lambda shape: pl.BlockSpec(shape, lambda i: (0,) * len(shape))
    hw = MLA_HEADS * HEAD_BLK
    return pl.pallas_call(
        _cache_kv_kernel,
        grid=(n // PAST_LEN,),
        in_specs=[
            pl.BlockSpec((PAST_LEN, MLA_KV_RANK), lambda i: (i, 0)),
            pl.BlockSpec((PAST_LEN, 128), lambda i: (i, 0)),
            full((MLA_KV_RANK, hw)), full((128, hw)), full((MLA_KV_RANK, MLA_HEADS * MLA_V)),
        ],
        out_specs=[
            pl.BlockSpec((PAST_LEN, hw), lambda i: (i, 0)),
            pl.BlockSpec((PAST_LEN, MLA_HEADS * MLA_V), lambda i: (i, 0)),
        ],
        out_shape=[
            jax.ShapeDtypeStruct((n, hw), BF16),
            jax.ShapeDtypeStruct((n, MLA_HEADS * MLA_V), BF16),
        ],
        compiler_params=_cparams(("parallel",)),
        name="cache_kv",
    )(ckv, kr, wk, wp, wv)


def _attn_kernel(*refs, has_cache):
    if has_cache:
        q_ref, k_ref, v_ref, kc_ref, vc_ref, o_ref = refs
    else:
        q_ref, k_ref, v_ref, o_ref = refs
    nh = q_ref.shape[1] // HEAD_BLK
    cols = [slice(h * HEAD_BLK, (h + 1) * HEAD_BLK) for h in range(nh)]
    vcols = [slice((h // 2) * 2 * MLA_V, (h // 2 + 1) * 2 * MLA_V) for h in range(nh)]
    s = [_dot_nt(q_ref[:, c], k_ref[:, c]) for c in cols]
    m = [jnp.max(x, axis=-1, keepdims=True) for x in s]
    if has_cache:
        sc = [_dot_nt(q_ref[:, c], kc_ref[:, c]) for c in cols]
        m = [jnp.maximum(a, jnp.max(x, axis=-1, keepdims=True)) for a, x in zip(m, sc)]
        pc = [jnp.exp2(x - a) for x, a in zip(sc, m)]
    p = [jnp.exp2(x - a) for x, a in zip(s, m)]

    def with_ones(vals):
        return jnp.concatenate([vals, jnp.ones(vals.shape, BF16)], axis=1)

    o = [_dot(x.astype(BF16), with_ones(v_ref[:, c])) for x, c in zip(p, vcols)]
    if has_cache:
        o = [a + _dot(x.astype(BF16), with_ones(vc_ref[:, c])) for a, x, c in zip(o, pc, vcols)]
    outs = [a[:, 0:2 * MLA_V] / a[:, 2 * MLA_V:4 * MLA_V] for a in o]
    lane = lax.broadcasted_iota(jnp.int32, (1, 2 * MLA_V), 1)
    for j in range(nh // 2):
        o_ref[:, j * 2 * MLA_V:(j + 1) * 2 * MLA_V] = jnp.where(lane < MLA_V, outs[2 * j], outs[2 * j + 1])


def _attention(q, k, v, n, nb, row0, tq, npair, kc=None, vc=None):
    has_cache = kc is not None
    nq = n // tq
    rb0 = row0 // tq
    kb0 = row0 // n
    in_specs = [
        pl.BlockSpec((tq, npair * 2 * HEAD_BLK), lambda b, h, i: (rb0 + b * nq + i, h)),
        pl.BlockSpec((n, npair * 2 * HEAD_BLK), lambda b, h, i: (kb0 + b, h)),
        pl.BlockSpec((n, npair * 2 * MLA_V), lambda b, h, i: (kb0 + b, h)),
    ]
    args = [q, k, v]
    if has_cache:
        in_specs += [
            pl.BlockSpec((PAST_LEN, npair * 2 * HEAD_BLK), lambda b, h, i: (b, h)),
            pl.BlockSpec((PAST_LEN, npair * 2 * MLA_V), lambda b, h, i: (b, h)),
        ]
        args += [kc, vc]
    return pl.pallas_call(
        functools.partial(_attn_kernel, has_cache=has_cache),
        grid=(nb, MLA_HEADS // (2 * npair), nq),
        in_specs=in_specs,
        out_specs=pl.BlockSpec((tq, npair * 2 * MLA_V), lambda b, h, i: (b * nq + i, h)),
        out_shape=jax.ShapeDtypeStruct((nb * n, MLA_HEADS * MLA_V), F32),
        compiler_params=_cparams(("parallel", "parallel", "parallel")),
        name="attn_cache" if has_cache else "attn_ctx",
    )(*args)


def _outproj_kernel(xa_ref, xb_ref, oga_ref, ogb_ref, oca_ref, ocb_ref, oma_ref, omb_ref, mod_ref, nw_ref,
                    wo_ref, wr_ref, br_ref, xo_ref, h_ref, route_ref, cnt_ref, carry_s, wrh_s, wrl_s):
    @pl.when(pl.program_id(0) == 0)
    def _():
        carry_s[...] = jnp.zeros(carry_s.shape, F32)
        wrh, wrl = _split2(wr_ref[...])
        wrh_s[...] = wrh
        wrl_s[...] = wrl

    mod = mod_ref[...]
    groups = [slice(g * (TM // 2), (g + 1) * (TM // 2)) for g in range(2)]

    def residual(x_ref, og_ref, oc_ref, om_ref):
        a1 = [_dot(og_ref[r, :].astype(BF16), wo_ref[0:GLA_W, :]) for r in groups]
        a2 = [_dot(oc_ref[r, :].astype(BF16), wo_ref[GLA_W:GLA_W + CONV_W, :]) for r in groups]
        a3 = [_dot(om_ref[r, :].astype(BF16), wo_ref[GLA_W + CONV_W:D_MODEL, :]) for r in groups]
        for r, u, v, t in zip(groups, a1, a2, a3):
            xo_ref[r, :] = x_ref[r, :] + mod[:, 2 * D_MODEL:3 * D_MODEL] * (u + v + t)

    is_ctx = pl.program_id(0) < N_CTX // TM
    pl.when(is_ctx)(lambda: residual(xa_ref, oga_ref, oca_ref, oma_ref))
    pl.when(jnp.logical_not(is_ctx))(lambda: residual(xb_ref, ogb_ref, ocb_ref, omb_ref))
    hs = [_rms(xo_ref[r, :], nw_ref[...]) * (1.0 + mod[:, 4 * D_MODEL:5 * D_MODEL]) + mod[:, 3 * D_MODEL:4 * D_MODEL]
          for r in groups]
    for g, h in enumerate(hs):
        _rows_to_tiles(h_ref, h, g * (TM // 2))
    parts = [_split2(h) for h in hs]
    l1 = [_dot(hh, wrh_s[...]) for hh, hl in parts]
    l2 = [_dot(hl, wrh_s[...]) for hh, hl in parts]
    l3 = [_dot(hh, wrl_s[...]) for hh, hl in parts]
    logits = jnp.concatenate([u + v + t for u, v, t in zip(l1, l2, l3)], axis=0) + br_ref[...]
    lane = lax.broadcasted_iota(jnp.int32, logits.shape, 1).astype(F32)
    big = 1e9
    gl = jnp.where((lane >= N_EXPERTS) & (lane < N_EXPERTS + N_GROUPS), logits, NEG)
    gmax = jnp.max(gl, axis=-1, keepdims=True)
    gw = 1.0 / jnp.sum(jnp.exp(gl - gmax), axis=-1, keepdims=True)
    gi = jnp.min(jnp.where(gl == gmax, lane, big), axis=-1, keepdims=True) - N_EXPERTS
    lo = gi * EXPERTS_PER_GROUP
    el = jnp.where((lane >= lo) & (lane < lo + EXPERTS_PER_GROUP), logits, NEG)
    m1 = jnp.max(el, axis=-1, keepdims=True)
    i1 = jnp.min(jnp.where(el == m1, lane, big), axis=-1, keepdims=True)
    el2 = jnp.where(lane == i1, NEG, el)
    m2 = jnp.max(el2, axis=-1, keepdims=True)
    i2 = jnp.min(jnp.where(el2 == m2, lane, big), axis=-1, keepdims=True)
    e2 = jnp.exp(m2 - m1)
    p1 = 1.0 / (1.0 + e2)
    onehot = jnp.where((lane == i1) | (lane == i2), 1.0, 0.0)
    tr = lax.broadcasted_iota(jnp.int32, (TM, TM), 0)
    tc = lax.broadcasted_iota(jnp.int32, (TM, TM), 1)
    before = (tc < tr).astype(F32).astype(BF16)
    seen = _dot(before, onehot.astype(BF16)) + carry_s[0:1, :]
    rank1 = jnp.sum(jnp.where(lane == i1, seen, 0.0), axis=-1, keepdims=True)
    rank2 = jnp.sum(jnp.where(lane == i2, seen, 0.0), axis=-1, keepdims=True)
    carry_s[...] = carry_s[...] + jnp.sum(onehot, axis=0, keepdims=True)
    cnt_ref[...] = carry_s[...]
    cols = (i1, i2, gw * p1, gw * (e2 * p1), rank1, rank2)
    route = jnp.zeros(logits.shape, F32)
    for j, col in enumerate(cols):
        route = jnp.where(lane == j, col, route)
    route_ref[...] = route


def _outproj(x, og, oc, om, mods4, l, norm_w, wo, wr, br):
    full = lambda shape: pl.BlockSpec(shape, lambda i: (0,) * len(shape))
    row = lambda w: pl.BlockSpec((TM, w), lambda i: (i, 0))
    return pl.pallas_call(
        _outproj_kernel,
        grid=(N_TOK // TM,),
        in_specs=(_pair_specs(TM, D_MODEL, isinstance(x, tuple)) + _pair_specs(TM, GLA_W, True)
                  + _pair_specs(TM, CONV_W, True) + _pair_specs(TM, MLA_HEADS * MLA_V, True) + [
            pl.BlockSpec((None, None, 1, 6 * D_MODEL), lambda i: (l, _mod_index(i, TM), 0, 0)),
            full((1, D_MODEL)), full((D_MODEL, D_MODEL)), full((D_MODEL, 128)), full((1, 128)),
        ]),
        out_specs=[row(D_MODEL), pl.BlockSpec(_tiled(TM), lambda i: (i, 0)), row(128), full((8, 128))],
        out_shape=[
            jax.ShapeDtypeStruct((N_TOK, D_MODEL), F32),
            jax.ShapeDtypeStruct(_tiled(N_TOK), F32),
            jax.ShapeDtypeStruct((N_TOK, 128), F32),
            jax.ShapeDtypeStruct((8, 128), F32),
        ],
        scratch_shapes=[pltpu.VMEM((8, 128), F32), pltpu.VMEM((D_MODEL, 128), BF16),
                        pltpu.VMEM((D_MODEL, 128), BF16)],
        compiler_params=_cparams(("arbitrary",)),
        name="outproj_router",
    )(*_pair(x), *og, *oc, *om, mods4, norm_w, wo, wr, br)


N_ASSIGN = 2 * N_TOK
TE = 256
N_ETILE = N_ASSIGN // TE
N_WORK = N_ETILE + N_EXPERTS


ROW_SUB = 8


def _tiled(n):
    return (n * ROW_SUB, D_MODEL // ROW_SUB)


def _rows_to_tiles(ref, x, r0=0):
    n = x.shape[0]
    for s in range(ROW_SUB):
        ref[pl.ds(r0 * ROW_SUB + s, n, stride=ROW_SUB), :] = x[:, s * 128:(s + 1) * 128]


def _tiles_to_rows(ref, r0=0, n=None):
    n = ref.shape[0] // ROW_SUB if n is None else n
    return jnp.concatenate([ref[pl.ds(r0 * ROW_SUB + s, n, stride=ROW_SUB), :] for s in range(ROW_SUB)], axis=1)


def _row_tile(ref, t):
    start = t * ROW_SUB if isinstance(t, int) else pl.multiple_of(t * ROW_SUB, ROW_SUB)
    return ref.at[pl.ds(start, ROW_SUB)]


def _lane_col(x, lane, j):
    return jnp.sum(jnp.where(lane == j, x, 0.0), axis=-1, keepdims=True)


def _plan_kernel(cnt_s, cnt_ref, route_ref, pos_ref, wt_ref, we_ref, wlo_ref, whi_ref, wf_ref, wr_ref, wn_ref,
                 off_s):
    cnt = cnt_ref[...]
    chi = jnp.floor(cnt * (1.0 / 128.0))
    clo = cnt - chi * 128.0
    r = lax.broadcasted_iota(jnp.int32, (128, 128), 0)
    c = lax.broadcasted_iota(jnp.int32, (128, 128), 1)
    below = (r < c).astype(F32).astype(BF16)
    off = (128.0 * _dot(chi.astype(BF16), below) + _dot(clo.astype(BF16), below))[0:1, :]
    route = route_ref[...]
    lane = lax.broadcasted_iota(jnp.int32, route.shape, 1).astype(F32)
    pos1 = _lane_col(off, lane, _lane_col(route, lane, 0.0)) + _lane_col(route, lane, 4.0)
    pos2 = _lane_col(off, lane, _lane_col(route, lane, 1.0)) + _lane_col(route, lane, 5.0)
    pos_ref[...] = jnp.where(lane == 0.0, pos1, jnp.where(lane == 1.0, pos2, 0.0))

    @pl.when(pl.program_id(0) == 0)
    def _():
        def offs(e, acc):
            off_s[e] = acc
            return acc + cnt_s[e]

        off_s[N_EXPERTS] = lax.fori_loop(0, N_EXPERTS, offs, jnp.int32(0))

        def item(w, st):
            j, e, first = st
            active = j < N_ETILE
            jj = jnp.minimum(j, N_ETILE - 1)
            ee = jnp.minimum(e, N_EXPERTS - 1)
            tlo = jj * TE
            thi = tlo + TE
            end = off_s[ee + 1]
            wt_ref[w] = jj
            we_ref[w] = ee
            wlo_ref[w] = jnp.where(active, jnp.maximum(off_s[ee], tlo) - tlo, 0)
            whi_ref[w] = jnp.where(active, jnp.minimum(end, thi) - tlo, 0)
            wf_ref[w] = jnp.where(active, first, 0)
            adv_j = (active & (end >= thi)).astype(jnp.int32)
            adv_e = (active & (end <= thi)).astype(jnp.int32)
            return j + adv_j, e + adv_e, adv_j

        lax.fori_loop(0, N_WORK, item, (jnp.int32(0), jnp.int32(0), jnp.int32(1)))

        def run(w, r):
            r = r + (we_ref[w] != we_ref[jnp.maximum(w - 1, 0)]).astype(jnp.int32)
            wr_ref[w] = r
            return r

        lax.fori_loop(0, N_WORK, run, jnp.int32(0))

        def following(j, nxt):
            w = N_WORK - 1 - j
            wn_ref[w] = nxt
            return jnp.where(we_ref[jnp.maximum(w - 1, 0)] != we_ref[w], we_ref[w], nxt)

        lax.fori_loop(0, N_WORK, following, jnp.int32(-1))


def _plan(cnt_i, cnt, route):
    smem = pl.BlockSpec(memory_space=pltpu.SMEM)
    work = jax.ShapeDtypeStruct((N_WORK,), jnp.int32)
    return pl.pallas_call(
        _plan_kernel,
        grid=(1,),
        in_specs=[smem, pl.BlockSpec((8, 128), lambda i: (0, 0)), pl.BlockSpec((N_TOK, 128), lambda i: (0, 0))],
        out_specs=[pl.BlockSpec((N_TOK, 128), lambda i: (0, 0))] + [smem] * 7,
        out_shape=[jax.ShapeDtypeStruct((N_TOK, 128), F32)] + [work] * 7,
        scratch_shapes=[pltpu.SMEM((N_EXPERTS + 1,), jnp.int32)],
        compiler_params=_cparams(("arbitrary",)),
        name="moe_plan",
    )(cnt_i, cnt, route)


def _dispatch_kernel(pos_s, h_ref, xs_hbm, sem):
    base = pl.program_id(0) * TM

    def copy(i, k):
        return pltpu.make_async_copy(_row_tile(h_ref, i), _row_tile(xs_hbm, pos_s[2 * (base + i) + k]), sem)

    def wait(i, carry):
        copy(i, 0).wait()
        copy(i, 1).wait()
        return carry

    for i in range(TM):
        copy(i, 0).start(priority=0)
        copy(i, 1).start(priority=1)
    lax.fori_loop(0, TM, wait, 0, unroll=8)


def _dispatch(pos, h):
    any_spec = pl.BlockSpec(memory_space=pl.ANY)
    return pl.pallas_call(
        _dispatch_kernel,
        grid_spec=pltpu.PrefetchScalarGridSpec(
            num_scalar_prefetch=1, grid=(N_TOK // TM,),
            in_specs=[pl.BlockSpec(_tiled(TM), lambda i, p: (i, 0))], out_specs=any_spec,
            scratch_shapes=[pltpu.SemaphoreType.DMA(())]),
        out_shape=jax.ShapeDtypeStruct(_tiled(N_ASSIGN), F32),
        compiler_params=_cparams(("arbitrary",)),
        name="moe_dispatch",
    )(pos, h)


def _experts_kernel(wt_s, we_s, wlo_s, whi_s, wf_s, wr_s, wn_s, xs_ref, w1_hbm, w3_hbm, w2_hbm, o_ref,
                    w1_s, w3_s, w2_s, f1_s, f3_s, f2_s, sem, *, layer):
    w = pl.program_id(0)
    first = wf_s[w] == 1
    nonempty = whi_s[w] > wlo_s[w]

    def fetch(e, slot):
        return [pltpu.make_async_copy(src.at[layer, e], dst.at[slot], sem.at[slot])
                for src, dst in ((w1_hbm, f1_s), (w3_hbm, f3_s), (w2_hbm, f2_s))]

    @pl.when(w == 0)
    def _():
        for c in fetch(we_s[0], wr_s[0] % 2):
            c.start()

    @pl.when((w == 0) | (we_s[w] != we_s[jnp.maximum(w - 1, 0)]))
    def _():
        slot = wr_s[w] % 2
        for c in fetch(we_s[w], slot):
            c.wait()
        w1_s[...] = f1_s[slot].astype(BF16)
        w3_s[...] = f3_s[slot].astype(BF16)
        w2_s[...] = f2_s[slot].astype(BF16)

        @pl.when(wn_s[w] >= 0)
        def _():
            for c in fetch(wn_s[w], 1 - slot):
                c.start()

    @pl.when(first & jnp.logical_not(nonempty))
    def _():
        o_ref[...] = jnp.zeros(o_ref.shape, F32)

    nsplit = 2
    hrows = TE // nsplit

    def expert_rows(groups, merge):
        x = [_tiles_to_rows(xs_ref, j * hrows, hrows).astype(BF16) for j in groups]
        a = [_dot(v, w1_s[...]) for v in x]
        b = [_dot(v, w3_s[...]) for v in x]
        hid = [(u * _sigmoid(u) * v).astype(BF16) for u, v in zip(a, b)]
        y = [_dot(v, w2_s[...]) for v in hid]
        for j, yj in zip(groups, y):
            row = lax.broadcasted_iota(jnp.int32, (hrows, 1), 0) + j * hrows
            mine = (row >= wlo_s[w]) & (row < whi_s[w])
            other = _tiles_to_rows(o_ref, j * hrows, hrows) if merge else 0.0
            _rows_to_tiles(o_ref, jnp.where(mine, yj, other), j * hrows)
        if not merge:
            for j in set(range(nsplit)) - set(groups):
                _rows_to_tiles(o_ref, jnp.zeros((hrows, D_MODEL), F32), j * hrows)

    need = [(wlo_s[w] < (j + 1) * hrows) & (whi_s[w] > j * hrows) for j in range(nsplit)]
    cases = (((0, 1), need[0] & need[1]), ((0,), need[0] & jnp.logical_not(need[1])),
             ((1,), need[1] & jnp.logical_not(need[0])))
    for groups, cond in cases:
        pl.when(nonempty & cond & first)(functools.partial(expert_rows, groups, False))
        pl.when(nonempty & cond & jnp.logical_not(first))(functools.partial(expert_rows, groups, True))


def _experts(work, xs, l, w1, w3, w2):
    tile = pl.BlockSpec(_tiled(TE), lambda w, wt, *_: (wt[w], 0))
    any_spec = pl.BlockSpec(memory_space=pl.ANY)
    return pl.pallas_call(
        functools.partial(_experts_kernel, layer=l),
        grid_spec=pltpu.PrefetchScalarGridSpec(
            num_scalar_prefetch=7, grid=(N_WORK,),
            in_specs=[tile, any_spec, any_spec, any_spec],
            out_specs=tile,
            scratch_shapes=[pltpu.VMEM((D_MODEL, EXPERT_FF), BF16), pltpu.VMEM((D_MODEL, EXPERT_FF), BF16),
                            pltpu.VMEM((EXPERT_FF, D_MODEL), BF16),
                            pltpu.VMEM((2, D_MODEL, EXPERT_FF), F32), pltpu.VMEM((2, D_MODEL, EXPERT_FF), F32),
                            pltpu.VMEM((2, EXPERT_FF, D_MODEL), F32), pltpu.SemaphoreType.DMA((2,))]),
        out_shape=jax.ShapeDtypeStruct(_tiled(N_ASSIGN), F32),
        compiler_params=_cparams(("arbitrary",)),
        name="moe_experts",
    )(*work, xs, w1, w3, w2)


def _moe_residual(pos_s, x_ref, route_ref, mod_ref, ys_hbm, y_s, sem):
    step = pl.program_id(0)
    nsteps = pl.num_programs(0)

    def copy(t, i, k):
        slot = t % 2
        src = _row_tile(ys_hbm, pos_s[2 * (t * TM + i) + k])
        return pltpu.make_async_copy(src, _row_tile(y_s.at[slot, k], i), sem.at[slot])

    def gather(t):
        for i in range(TM):
            copy(t, i, 0).start(priority=0)
            copy(t, i, 1).start(priority=1)

    def wait(i, carry):
        copy(step, i, 0).wait()
        copy(step, i, 1).wait()
        return carry

    pl.when(step == 0)(lambda: gather(step))
    pl.when(step + 1 < nsteps)(lambda: gather(step + 1))
    lax.fori_loop(0, TM, wait, 0, unroll=8)
    route = route_ref[...]
    lane = lax.broadcasted_iota(jnp.int32, route.shape, 1)
    slot = step % 2
    moe = (_lane_col(route, lane, 2) * _tiles_to_rows(y_s.at[slot, 0])
           + _lane_col(route, lane, 3) * _tiles_to_rows(y_s.at[slot, 1]))
    return x_ref[...] + mod_ref[:, 5 * D_MODEL:6 * D_MODEL] * moe


def _combine_inproj_kernel(pos_s, x_ref, route_ref, modp_ref, ys_hbm, mod_ref, nw_ref, w_ref,
                           xo_ref, p_ref, y_s, sem):
    x = _moe_residual(pos_s, x_ref, route_ref, modp_ref, ys_hbm, y_s, sem)
    xo_ref[...] = x
    mod = mod_ref[...]
    h = _rms(x, nw_ref[...]) * (1.0 + mod[:, D_MODEL:2 * D_MODEL]) + mod[:, 0:D_MODEL]
    p_ref[...] = _dot(h.astype(BF16), w_ref[...])


def _combine_inproj(pos, x, route, mods4, l, ys, norm_w, w_in_r):
    row = lambda w: pl.BlockSpec((TM, w), lambda i, p: (i, 0))
    mod = lambda layer: pl.BlockSpec((None, None, 1, 6 * D_MODEL), lambda i, p: (layer, _mod_index(i, TM), 0, 0))
    return pl.pallas_call(
        _combine_inproj_kernel,
        grid_spec=pltpu.PrefetchScalarGridSpec(
            num_scalar_prefetch=1, grid=(N_TOK // TM,),
            in_specs=[row(D_MODEL), row(128), mod(l - 1), pl.BlockSpec(memory_space=pl.ANY), mod(l),
                      pl.BlockSpec((1, D_MODEL), lambda i, p: (0, 0)),
                      pl.BlockSpec((None, D_MODEL, P_COLS), lambda i, p: (l, 0, 0))],
            out_specs=[row(D_MODEL), row(P_COLS)],
            scratch_shapes=[pltpu.VMEM((2, 2) + _tiled(TM), F32), pltpu.SemaphoreType.DMA((2,))]),
        out_shape=[jax.ShapeDtypeStruct((N_TOK, D_MODEL), F32), jax.ShapeDtypeStruct((N_TOK, P_COLS), F32)],
        compiler_params=_cparams(("arbitrary",)),
        name="moe_combine_inproj",
    )(pos, x, route, mods4, ys, mods4, norm_w, w_in_r)


def _combine_kernel(pos_s, x_ref, route_ref, mod_ref, fw_ref, ys_hbm, *refs, final):
    if final:
        oa_ref, ob_ref, y_s, sem = refs
    else:
        o_ref, y_s, sem = refs
    x = _moe_residual(pos_s, x_ref, route_ref, mod_ref, ys_hbm, y_s, sem)
    if final:
        y = _rms(x, fw_ref[...])
        is_ctx = pl.program_id(0) < N_CTX // TM

        @pl.when(is_ctx)
        def _():
            oa_ref[...] = y

        @pl.when(jnp.logical_not(is_ctx))
        def _():
            ob_ref[...] = y
    else:
        o_ref[...] = x


def _combine(pos, x, route, mods4, l, fw, ys, final):
    row = lambda w: pl.BlockSpec((TM, w), lambda i, p: (i, 0))
    if final:
        out_specs = _pair_specs(TM, D_MODEL, True)
        out_shape = [jax.ShapeDtypeStruct((N_CTX, D_MODEL), F32), jax.ShapeDtypeStruct((N_SMP, D_MODEL), F32)]
    else:
        out_specs = row(D_MODEL)
        out_shape = jax.ShapeDtypeStruct((N_TOK, D_MODEL), F32)
    return pl.pallas_call(
        functools.partial(_combine_kernel, final=final),
        grid_spec=pltpu.PrefetchScalarGridSpec(
            num_scalar_prefetch=1, grid=(N_TOK // TM,),
            in_specs=[row(D_MODEL), row(128),
                      pl.BlockSpec((None, None, 1, 6 * D_MODEL), lambda i, p: (l, _mod_index(i, TM), 0, 0)),
                      pl.BlockSpec((1, D_MODEL), lambda i, p: (0, 0)),
                      pl.BlockSpec(memory_space=pl.ANY)],
            out_specs=out_specs,
            scratch_shapes=[pltpu.VMEM((2, 2) + _tiled(TM), F32), pltpu.SemaphoreType.DMA((2,))]),
        out_shape=out_shape,
        compiler_params=_cparams(("arbitrary",)),
        name="moe_combine_final" if final else "moe_combine",
    )(pos, x, route, mods4, fw, ys)


def _rope_swap(w):
    nf = MLA_ROPE // 4
    g = w.reshape(w.shape[:-1] + (2, 2, nf))
    return jnp.stack([-g[..., 1, :], g[..., 0, :]], axis=-2).reshape(w.shape)


def _rope_tables():
    nf = MLA_ROPE // 4
    pos = np.arange(DEC_SEQ)
    row = (pos // GRID_W).astype(np.float32)
    col = (pos % GRID_W).astype(np.float32)
    inv = (np.float32(ROPE_THETA) ** (-np.arange(nf, dtype=np.float32) / np.float32(nf))).astype(np.float32)
    ar = (row[:, None] * inv).astype(np.float32)
    ac = (col[:, None] * inv).astype(np.float32)
    cos = np.concatenate([np.cos(ar), np.cos(ar), np.cos(ac), np.cos(ac)], axis=1)
    sin = np.concatenate([np.sin(ar), np.sin(ar), np.sin(ac), np.sin(ac)], axis=1)
    cos = np.concatenate([np.ones((TM, MLA_ROPE)), cos], axis=0).astype(np.float32)
    sin = np.concatenate([np.zeros((TM, MLA_ROPE)), sin], axis=0).astype(np.float32)
    n = cos.shape[0]
    z32 = np.zeros((n, 32), np.float32)
    tqc = np.concatenate([np.ones((n, MLA_NOPE), np.float32), cos, z32], axis=1)
    tqs = np.concatenate([np.zeros((n, MLA_NOPE), np.float32), sin, z32], axis=1)
    tk = np.concatenate([z32, cos, sin, z32], axis=1)
    return jnp.asarray(tqc), jnp.asarray(tqs), jnp.asarray(tk)


W_IN_COLS = 1984
W_IN_ROWS = 256


def _w_in_relayout_kernel(w_ref, sh_ref, s1_ref, s2_ref, o_ref):
    gate_cols = 4 * GLA_W
    o_ref[:, 0:gate_cols] = w_ref[:, 0:gate_cols].astype(BF16)
    for j in range(gate_cols // 128, MISC_BLK):
        hi = min(128 * j + 256, W_IN_COLS)
        src = w_ref[:, 128 * j:hi].astype(BF16)
        o_ref[:, 128 * j:128 * (j + 1)] = _dot(src, sh_ref[0:hi - 128 * j, :]).astype(BF16)
    head = w_ref[:, gate_cols:gate_cols + 128].astype(BF16)
    tail = w_ref[:, 128 * MISC_BLK:W_IN_COLS].astype(BF16)
    o_ref[:, 128 * MISC_BLK:P_COLS] = (_dot(head, s1_ref[...]) + _dot(tail, s2_ref[...])).astype(BF16)


def _w_in_relayout(w_in):
    sh = np.zeros((256, 128), np.float32)
    sh[32 + np.arange(128), np.arange(128)] = 1.0
    s1 = np.zeros((128, 128), np.float32)
    s1[np.arange(32), np.arange(32)] = 1.0
    s2 = np.zeros((W_IN_COLS - 128 * MISC_BLK, 128), np.float32)
    nf = MLA_ROPE // 4
    for j in range(MLA_ROPE):
        s2[32 + j, 32 + j] = 1.0
        half = (j // nf) % 2
        src = j + nf if half == 0 else j - nf
        s2[32 + src, 64 + j] = -1.0 if half == 0 else 1.0
    consts = [jnp.asarray(a).astype(BF16) for a in (sh, s1, s2)]
    full = lambda a: pl.BlockSpec(a.shape, lambda l, i: (0, 0))
    return pl.pallas_call(
        _w_in_relayout_kernel,
        grid=(DEPTH, D_MODEL // W_IN_ROWS),
        in_specs=[pl.BlockSpec((None, W_IN_ROWS, W_IN_COLS), lambda l, i: (l, i, 0))] + [full(a) for a in consts],
        out_specs=pl.BlockSpec((None, W_IN_ROWS, P_COLS), lambda l, i: (l, i, 0)),
        out_shape=jax.ShapeDtypeStruct((DEPTH, D_MODEL, P_COLS), BF16),
        compiler_params=_cparams(("parallel", "parallel")),
        name="w_in_relayout",
    )(w_in, *consts)


def _prep_layer(l, gla_wg_f, gla_wg_b, mla_wuq, mla_wukv, w_out, moe_wg, moe_bg, moe_we, moe_be):
    wgf = jnp.zeros((128, GLA_W), F32).at[0:16].set(gla_wg_f[l])
    wgb = jnp.zeros((128, GLA_W), F32).at[16:32].set(gla_wg_b[l])
    wq = mla_wuq[l].reshape(MLA_Q_RANK, MLA_HEADS, MLA_NOPE + MLA_ROPE)
    nope, rope = wq[..., :MLA_NOPE], wq[..., MLA_NOPE:]
    z = lambda n: jnp.zeros((MLA_Q_RANK, MLA_HEADS, n), F32)
    hw = MLA_HEADS * HEAD_BLK
    wa = jnp.concatenate([nope, rope, z(32)], axis=-1).reshape(MLA_Q_RANK, hw).astype(BF16)
    wb = jnp.concatenate([z(MLA_NOPE), _rope_swap(rope), z(32)], axis=-1).reshape(MLA_Q_RANK, hw).astype(BF16)
    wkv = mla_wukv[l].reshape(MLA_KV_RANK, MLA_HEADS, MLA_NOPE + MLA_V)
    wk = jnp.concatenate([wkv[..., :MLA_NOPE], jnp.zeros((MLA_KV_RANK, MLA_HEADS, 64), F32)],
                         axis=-1).reshape(MLA_KV_RANK, hw).astype(BF16)
    wv = wkv[..., MLA_NOPE:].reshape(MLA_KV_RANK, MLA_HEADS * MLA_V).astype(BF16)
    wp_np = np.zeros((128, MLA_HEADS, HEAD_BLK), np.float32)
    for j in range(MLA_ROPE):
        wp_np[32 + j, :, MLA_NOPE + j] = 1.0
        wp_np[64 + j, :, MLA_NOPE + j] = 1.0
    wp = jnp.asarray(wp_np.reshape(128, hw)).astype(BF16)
    wo = w_out[l].astype(BF16)
    wr = jnp.concatenate([moe_we[l], moe_wg[l], jnp.zeros((D_MODEL, 128 - N_EXPERTS - N_GROUPS), F32)], axis=1)
    br = jnp.concatenate([moe_be[l], moe_bg[l], jnp.zeros((128 - N_EXPERTS - N_GROUPS,), F32)]).reshape(1, 128)
    return dict(wgf=wgf, wgb=wgb, wa=wa, wb=wb, wk=wk, wv=wv, wp=wp, wo=wo, wr=wr, br=br)


def kernel(x_prompt, x_sample, cache_ckv, cache_krope, state_gla_fwd, state_gla_bwd, c, c_ctx, ada_w, ada_b,
           norm1_w, norm2_w, w_in, gla_wg_f, gla_bg_f, gla_wg_b, gla_bg_b, gla_norm_w, conv_w, conv_b,
           conv_ln_w, conv_ln_b, mla_qnorm_w, mla_wuq, mla_kvnorm_w, mla_wukv, w_out, moe_wg, moe_bg, moe_we,
           moe_be, moe_w1, moe_w3, moe_w2, final_norm_w):
    x = (x_prompt.reshape(N_CTX, D_MODEL), x_sample.reshape(N_SMP, D_MODEL))
    cond8 = jnp.concatenate([c_ctx[None, :], c, jnp.zeros((N_MOD - 1 - DEC_BATCH, D_MODEL), F32)], axis=0)
    mods4 = _ada_table(cond8, ada_w, ada_b).reshape(DEPTH, N_MOD, 1, 6 * D_MODEL)
    tabs = _rope_tables()
    w_in_r = _w_in_relayout(w_in)
    row1 = lambda a: a.reshape(1, -1)

    ckv_new, kr_new, sf_new, sb_new = [], [], [], []
    for l in range(DEPTH):
        w = _prep_layer(l, gla_wg_f, gla_wg_b, mla_wuq, mla_wukv, w_out, moe_wg, moe_bg, moe_we, moe_be)
        if l == 0:
            p = _inproj(x, mods4, l, row1(norm1_w[l]), w_in_r)
        else:
            x, p = _combine_inproj(pos, xn, route, mods4, l, ys, row1(norm1_w[l]), w_in_r)

        gla_args = (p, w["wgf"], w["wgb"], row1(gla_bg_f[l]), row1(gla_bg_b[l]), row1(gla_norm_w[l]))
        og_c, sf, sb = _gla(*gla_args, SEQ, BATCH, 0)
        og_s = _gla(*gla_args, DEC_SEQ, DEC_BATCH, N_CTX // DEC_SEQ, (state_gla_fwd, l), (state_gla_bwd, l))[0]
        og = (og_c, og_s)

        cw = jnp.concatenate([conv_w[l], jnp.zeros((1, CONV_W), F32)], axis=0)
        conv_args = (p, cw, row1(conv_b[l]), row1(conv_ln_w[l]), row1(conv_ln_b[l]))
        oc = (_conv(*conv_args, SEQ, BATCH, 0), _conv(*conv_args, DEC_SEQ, DEC_BATCH, N_CTX // DEC_SEQ))

        q, k, v, ckvn, kr = _mla_prep(p, tabs, row1(mla_qnorm_w[l]), row1(mla_kvnorm_w[l]),
                                      w["wa"], w["wb"], w["wk"], w["wp"], w["wv"])
        cc = cache_ckv[:, l].reshape(DEC_BATCH * PAST_LEN, MLA_KV_RANK)
        ckr = cache_krope[:, l].reshape(DEC_BATCH * PAST_LEN, MLA_ROPE)
        ckr = jnp.pad(ckr, ((0, 0), (32, 64)))
        kc, vc = _cache_kv(cc, ckr, w["wk"], w["wp"], w["wv"])
        om = (_attention(q, k, v, SEQ, BATCH, 0, SEQ, MLA_HEADS // 2),
              _attention(q, k, v, DEC_SEQ, DEC_BATCH, N_CTX, DEC_SEQ, 1, kc, vc))

        xn, h2, route, cnt = _outproj(x, og, oc, om, mods4, l, row1(norm2_w[l]), w["wo"], w["wr"], w["br"])
        posf, *work = _plan(cnt[0, :N_EXPERTS].astype(jnp.int32), cnt, route)
        pos = posf[:, 0:2].astype(jnp.int32).reshape(N_ASSIGN)
        ys = _experts(work, _dispatch(pos, h2), l, moe_w1, moe_w3, moe_w2)
        if l == DEPTH - 1:
            x = _combine(pos, xn, route, mods4, l, row1(final_norm_w), ys, True)

        ckv_new.append(ckvn[:N_CTX].reshape(BATCH, SEQ, MLA_KV_RANK))
        kr_new.append(kr[:N_CTX, 32:32 + MLA_ROPE].reshape(BATCH, SEQ, MLA_ROPE))
        sf_new.append(sf)
        sb_new.append(sb)

    y_ctx, y_smp = x
    return (y_ctx.reshape(BATCH, SEQ, D_MODEL), y_smp.reshape(DEC_BATCH, DEC_SEQ, D_MODEL),
            jnp.stack(ckv_new, axis=1), jnp.stack(kr_new, axis=1),
            jnp.stack(sf_new, axis=1), jnp.stack(sb_new, axis=1))
```

```python
import functools

import numpy as np
import jax
import jax.numpy as jnp
from jax import lax
from jax.experimental import pallas as pl
from jax.experimental.pallas import tpu as pltpu

F32 = jnp.float32
BF16 = jnp.bfloat16

D_MODEL = 1024
BATCH = 16
SEQ = 256
DEPTH = 2
DEC_BATCH = 4
DEC_SEQ = 1024
PAST_LEN = 512
GRID_W = 64
NORM_EPS = 1e-6
GLA_HEADS = 4
GLA_DK = 64
GLA_DV = 64
GLA_W = 256
GLA_GATE_RANK = 16
GLA_TAU = 16.0
CONV_W = 256
CONV_K = 31
MLA_HEADS = 8
MLA_NOPE = 64
MLA_ROPE = 32
MLA_V = 64
MLA_Q_RANK = 256
MLA_KV_RANK = 128
ROPE_THETA = 10000.0
N_GROUPS = 4
EXPERTS_PER_GROUP = 8
N_EXPERTS = 32
EXPERT_FF = 256

N_CTX = BATCH * SEQ
N_SMP = DEC_BATCH * DEC_SEQ
N_TOK = N_CTX + N_SMP
N_MOD = 8
TM = 512
P_COLS = 2048
MISC_BLK = 15
CHUNK = 64
SUB = 16
NEG = -1e30
VMEM_LIMIT = 56 * 1024 * 1024


def _cparams(sem):
    return pltpu.CompilerParams(dimension_semantics=sem, vmem_limit_bytes=VMEM_LIMIT)


def _dot(a, b):
    return jnp.dot(a, b, preferred_element_type=F32)


def _dot_nt(a, b):
    return lax.dot_general(a, b, (((1,), (1,)), ((), ())), preferred_element_type=F32)


def _split2(x):
    hi = x.astype(BF16)
    lo = (x - hi.astype(F32)).astype(BF16)
    return hi, lo


def _dot3(a, w):
    ah, al = _split2(a)
    wh, wl = _split2(w)
    return _dot(ah, wh) + _dot(al, wh) + _dot(ah, wl)


def _sigmoid(x):
    return 1.0 / (1.0 + jnp.exp(-x))


def _rms(x, w):
    ms = jnp.mean(x * x, axis=-1, keepdims=True)
    return x * lax.rsqrt(ms + NORM_EPS) * w


def _mod_index(i, tile):
    nctx = N_CTX // tile
    per = DEC_SEQ // tile
    return jnp.where(i < nctx, 0, 1 + (i - nctx) // per)


def _ada_kernel(c_ref, w_ref, b_ref, o_ref):
    c = c_ref[...]
    o_ref[...] = _dot3(c * _sigmoid(c), w_ref[...]) + b_ref[...]


def _ada_table(cond8, ada_w, ada_b):
    tn = 1024
    n6 = 6 * D_MODEL
    return pl.pallas_call(
        _ada_kernel,
        grid=(DEPTH, n6 // tn),
        in_specs=[
            pl.BlockSpec((N_MOD, D_MODEL), lambda l, j: (0, 0)),
            pl.BlockSpec((None, D_MODEL, tn), lambda l, j: (l, 0, j)),
            pl.BlockSpec((None, 1, tn), lambda l, j: (l, 0, j)),
        ],
        out_specs=pl.BlockSpec((None, N_MOD, tn), lambda l, j: (l, 0, j)),
        out_shape=jax.ShapeDtypeStruct((DEPTH, N_MOD, n6), F32),
        compiler_params=_cparams(("parallel", "parallel")),
        name="ada_table",
    )(cond8, ada_w, ada_b.reshape(DEPTH, 1, n6))


def _pair_specs(tile, width, split):
    nctx = N_CTX // tile
    offb = 0 if split else nctx
    return [pl.BlockSpec((tile, width), lambda i, *_: (jnp.minimum(i, nctx - 1), 0)),
            pl.BlockSpec((tile, width), lambda i, *_: (jnp.maximum(i - nctx, 0) + offb, 0))]


def _pair(x):
    return x if isinstance(x, tuple) else (x, x)


TM_IN = 512


def _inproj_kernel(xa_ref, xb_ref, mod_ref, nw_ref, w_ref, o_ref):
    mod = mod_ref[...]

    def project(x_ref):
        h = _rms(x_ref[...], nw_ref[...]) * (1.0 + mod[:, D_MODEL:2 * D_MODEL]) + mod[:, 0:D_MODEL]
        o_ref[...] = _dot(h.astype(BF16), w_ref[...])

    is_ctx = pl.program_id(0) < N_CTX // TM_IN
    pl.when(is_ctx)(lambda: project(xa_ref))
    pl.when(jnp.logical_not(is_ctx))(lambda: project(xb_ref))


def _inproj(x, mods4, l, norm_w, w_in_r):
    return pl.pallas_call(
        _inproj_kernel,
        grid=(N_TOK // TM_IN,),
        in_specs=_pair_specs(TM_IN, D_MODEL, isinstance(x, tuple)) + [
            pl.BlockSpec((None, None, 1, 6 * D_MODEL), lambda i: (l, _mod_index(i, TM_IN), 0, 0)),
            pl.BlockSpec((1, D_MODEL), lambda i: (0, 0)),
            pl.BlockSpec((None, D_MODEL, P_COLS), lambda i: (l, 0, 0)),
        ],
        out_specs=pl.BlockSpec((TM_IN, P_COLS), lambda i: (i, 0)),
        out_shape=jax.ShapeDtypeStruct((N_TOK, P_COLS), F32),
        compiler_params=_cparams(("parallel",)),
        name="inproj",
    )(*_pair(x), mods4, norm_w, w_in_r)


def _log_sigmoid(x):
    return jnp.minimum(x, 0.0) - jnp.log(1.0 + jnp.exp(-jnp.abs(x)))


def _split3(x):
    hi = x.astype(BF16)
    r = x - hi.astype(F32)
    mid = r.astype(BF16)
    lo = (r - mid.astype(F32)).astype(BF16)
    return hi, mid, lo


def _gla_consts(reverse):
    t = lax.broadcasted_iota(jnp.int32, (CHUNK, CHUNK), 0)
    s = lax.broadcasted_iota(jnp.int32, (CHUNK, CHUNK), 1)
    mid = jnp.bitwise_and(t, -SUB) + SUB // 2
    if reverse:
        cum = s >= t
        ref = s >= mid
    else:
        cum = s <= t
        ref = s <= mid
    cum_ref = jnp.concatenate([cum, ref], axis=0).astype(F32).astype(BF16)
    rt = jnp.bitwise_and(lax.broadcasted_iota(jnp.int32, (GLA_HEADS * SUB, CHUNK), 0), SUB - 1)
    cs = lax.broadcasted_iota(jnp.int32, (GLA_HEADS * SUB, CHUNK), 1)
    rowi = lax.broadcasted_iota(jnp.int32, (CHUNK, 1), 0)
    causal, valid = [], []
    for i in range(CHUNK // SUB):
        tq = rt + i * SUB
        causal.append((cs >= tq) if reverse else (cs <= tq))
        valid.append((rowi >= i * SUB) if reverse else (rowi < (i + 1) * SUB))
    return cum_ref, causal, valid


def _gla_local(jobs, nc, o_s, qe_s, u_s, dec_s, head_masks, block_mask):
    nsub = CHUNK // SUB
    rows = [pl.ds(pl.multiple_of(c * CHUNK, CHUNK), CHUNK) for c, *_ in jobs]
    bs = []
    for (c, d, qs, k, vb, vt, la_ref, consts), r in zip(jobs, rows):
        h0, h1, h2 = _split3(la_ref[r, :])
        br = _dot(consts[0], h0) + _dot(consts[0], h1) + _dot(consts[0], h2)
        bs.append((br[0:CHUNK], br[CHUNK:2 * CHUNK]))
    ops = []
    for (c, d, qs, k, vb, vt, la_ref, consts), (b, rr) in zip(jobs, bs):
        qhat = qs * jnp.exp(b - rr)
        for i in range(nsub):
            r_i = rr[i * SUB:i * SUB + 1]
            kt = (k * jnp.exp(jnp.where(consts[2][i], r_i - b, NEG))).astype(BF16)
            qi = qhat[i * SUB:(i + 1) * SUB]
            ops.append((jnp.concatenate([jnp.where(m, qi, 0.0) for m in head_masks], axis=0).astype(BF16), kt))
    scores = [_dot_nt(qbig, kt) for qbig, kt in ops]
    outs = []
    for j, (c, d, qs, k, vb, vt, la_ref, consts) in enumerate(jobs):
        for i in range(nsub):
            a = jnp.where(consts[1][i], scores[j * nsub + i], 0.0)
            outs.append(_dot(a.astype(BF16), vb))
    for j, ((c, d, qs, k, vb, vt, la_ref, consts), (b, rr), r) in enumerate(zip(jobs, bs, rows)):
        parts = []
        for i in range(nsub):
            ov = outs[j * nsub + i]
            oi = jnp.where(head_masks[0], ov[0:SUB], 0.0)
            for h in range(1, GLA_HEADS):
                oi = oi + jnp.where(head_masks[h], ov[h * SUB:(h + 1) * SUB], 0.0)
            parts.append(oi)
        o_s[d, r, :] = jnp.concatenate(parts, axis=0)
        qe_s[d, r, :] = (qs * jnp.exp(b)).astype(BF16)
        b_last = b[0:1] if d == 1 else b[CHUNK - 1:CHUNK]
        kl = (k * jnp.exp(b_last - b)).astype(BF16)
        idx = d * nc + c
        u_s[idx] = jnp.where(block_mask, _dot(vt, kl), 0.0)
        dec_s[pl.ds(pl.multiple_of(idx * 8, 8), 8), :] = jnp.broadcast_to(jnp.exp(b_last), (8, GLA_W))


GLA_SAFE_EXPONENT = 60.0


def _gla_exact_intra(d, nc, qkvg_ref, la_ref, cum, o_s, b_s, acc_s, ones_blk):
    reverse = d == 1
    tcol = lax.broadcasted_iota(jnp.int32, (CHUNK, 1), 0)

    def chunk(c, carry):
        r0 = pl.multiple_of(c * CHUNK, CHUNK)
        rows = pl.ds(r0, CHUNK)
        h0, h1, h2 = _split3(la_ref[rows, :])
        b_s[...] = _dot(cum, h0) + _dot(cum, h1) + _dot(cum, h2)
        acc_s[...] = jnp.zeros((CHUNK, GLA_W), F32)
        qs = qkvg_ref[rows, 0:GLA_W] * (GLA_DK ** -0.5)

        def key(s, carry2):
            bs = b_s[pl.ds(s, 1), :]
            ks = qkvg_ref[pl.ds(r0 + s, 1), GLA_W:2 * GLA_W]
            vs = qkvg_ref[pl.ds(r0 + s, 1), 2 * GLA_W:3 * GLA_W]
            live = (tcol <= s) if reverse else (tcol >= s)
            hi, lo = _split2(qs * ks * jnp.exp(jnp.where(live, b_s[...] - bs, NEG)))
            acc_s[...] += (_dot(hi, ones_blk) + _dot(lo, ones_blk)) * vs
            return carry2

        lax.fori_loop(0, CHUNK, key, 0)
        o_s[d, rows, :] = acc_s[...]
        return carry

    lax.fori_loop(0, nc, chunk, 0)


def _gla_kernel(*refs, n, has_state):
    if has_state:
        (qkvg_ref, misc_ref, wgf_ref, wgb_ref, bgf_ref, bgb_ref, nw_ref, s0f_ref, s0b_ref,
         o_ref, laf_s, lab_s, o_s, qe_s, u_s, dec_s, sts_s, stf_s, stb_s, b_s, acc_s) = refs
    else:
        (qkvg_ref, misc_ref, wgf_ref, wgb_ref, bgf_ref, bgb_ref, nw_ref,
         o_ref, sf_ref, sb_ref, laf_s, lab_s, o_s, qe_s, u_s, dec_s, sts_s, stf_s, stb_s, b_s, acc_s) = refs
    mh, ml = _split2(misc_ref[...])
    gates = []
    for w_ref in (wgf_ref, wgb_ref):
        wh, wl = _split2(w_ref[...])
        gates.append((_dot(mh, wh), _dot(ml, wh), _dot(mh, wl)))
    laf_s[...] = _log_sigmoid(gates[0][0] + gates[0][1] + gates[0][2] + bgf_ref[...]) * (1.0 / GLA_TAU)
    lab_s[...] = _log_sigmoid(gates[1][0] + gates[1][1] + gates[1][2] + bgb_ref[...]) * (1.0 / GLA_TAU)
    stf_s[...] = jnp.zeros((GLA_W, GLA_W), F32)
    stb_s[...] = jnp.zeros((GLA_W, GLA_W), F32)
    if has_state:
        for st_s, s0_ref in ((stf_s, s0f_ref), (stb_s, s0b_ref)):
            for h in range(GLA_HEADS):
                st_s[h * GLA_DK:(h + 1) * GLA_DK, h * GLA_DV:(h + 1) * GLA_DV] = s0_ref[h]
            st_s[...] = st_s[...].T

    lane = lax.broadcasted_iota(jnp.int32, (1, GLA_W), 1)
    head_masks = [jnp.right_shift(lane, 6) == h for h in range(GLA_HEADS)]
    bi = jnp.right_shift(lax.broadcasted_iota(jnp.int32, (GLA_W, GLA_W), 0), 6)
    bj = jnp.right_shift(lax.broadcasted_iota(jnp.int32, (GLA_W, GLA_W), 1), 6)
    block_mask = bi == bj
    consts_f = _gla_consts(False)
    consts_b = _gla_consts(True)
    nc = n // CHUNK

    def local(cc, carry):
        jobs = []
        for c in (2 * cc, 2 * cc + 1):
            rows = pl.ds(pl.multiple_of(c * CHUNK, CHUNK), CHUNK)
            qs = qkvg_ref[rows, 0:GLA_W] * (GLA_DK ** -0.5)
            k = qkvg_ref[rows, GLA_W:2 * GLA_W]
            v = qkvg_ref[rows, 2 * GLA_W:3 * GLA_W]
            vb = v.astype(BF16)
            vt = v.T.astype(BF16)
            jobs += [(c, 0, qs, k, vb, vt, laf_s, consts_f), (c, 1, qs, k, vb, vt, lab_s, consts_b)]
        _gla_local(jobs, nc, o_s, qe_s, u_s, dec_s, head_masks, block_mask)
        return carry

    lax.fori_loop(0, nc // 2, local, 0)

    ones_blk = block_mask.astype(F32).astype(BF16)
    @pl.when(jnp.min(jnp.minimum(laf_s[...], lab_s[...])) * (SUB // 2) < -GLA_SAFE_EXPONENT)
    def _():
        for d, (la_s, consts) in enumerate(((laf_s, consts_f), (lab_s, consts_b))):
            _gla_exact_intra(d, nc, qkvg_ref, la_s, consts[0][0:CHUNK], o_s, b_s, acc_s, ones_blk)

    def scan(j, carry):
        for d, st_s in enumerate((stf_s, stb_s)):
            idx = d * nc + (j if d == 0 else nc - 1 - j)
            st = st_s[...]
            sts_s[idx] = st.astype(BF16)
            st_s[...] = st * dec_s[pl.ds(pl.multiple_of(idx * 8, 8), 1), :] + u_s[idx]
        return carry

    lax.fori_loop(0, nc, scan, 0)

    def inter(cc, carry):
        jobs = [(c, d, pl.ds(pl.multiple_of(c * CHUNK, CHUNK), CHUNK))
                for c in (2 * cc, 2 * cc + 1) for d in range(2)]
        res = [_dot_nt(qe_s[d, r, :], sts_s[d * nc + c]) for c, d, r in jobs]
        for (c, d, r), o in zip(jobs, res):
            o_s[d, r, :] += o
        return carry

    lax.fori_loop(0, nc // 2, inter, 0)

    if not has_state:
        for st_s, s_ref in ((stf_s, sf_ref), (stb_s, sb_ref)):
            s = st_s[...].T
            for h in range(GLA_HEADS):
                s_ref[h] = s[h * GLA_DK:(h + 1) * GLA_DK, h * GLA_DV:(h + 1) * GLA_DV]

    rt = 256
    for r0 in range(0, n, rt):
        o = o_s[0, r0:r0 + rt, :] + o_s[1, r0:r0 + rt, :]
        hi, lo = _split2(o * o)
        ms = (_dot(hi, ones_blk) + _dot(lo, ones_blk)) * (1.0 / GLA_DV)
        g = qkvg_ref[r0:r0 + rt, 3 * GLA_W:4 * GLA_W]
        o_ref[r0:r0 + rt, :] = (o * lax.rsqrt(ms + NORM_EPS) * nw_ref[...] * (g * _sigmoid(g))).astype(BF16)


def _gla(p, wgf, wgb, bgf, bgb, nw, n, nb, row_blk0, s0f=None, s0b=None):
    has_state = s0f is not None
    full = lambda shape: pl.BlockSpec(shape, lambda b: (0,) * len(shape))
    in_specs = [
        pl.BlockSpec((n, 4 * GLA_W), lambda b: (row_blk0 + b, 0)),
        pl.BlockSpec((n, 128), lambda b: (row_blk0 + b, MISC_BLK)),
        full((128, GLA_W)), full((128, GLA_W)), full((1, GLA_W)), full((1, GLA_W)), full((1, GLA_W)),
    ]
    args = [p, p, wgf, wgb, bgf, bgb, nw]
    out_specs = [pl.BlockSpec((n, GLA_W), lambda b: (b, 0))]
    out_shape = [jax.ShapeDtypeStruct((nb * n, GLA_W), BF16)]
    if has_state:
        layer = s0f[1]
        st_spec = pl.BlockSpec((None, None, GLA_HEADS, GLA_DK, GLA_DV), lambda b: (b, layer, 0, 0, 0))
        in_specs += [st_spec, st_spec]
        args += [s0f[0], s0b[0]]
    else:
        st_spec = pl.BlockSpec((None, GLA_HEADS, GLA_DK, GLA_DV), lambda b: (b, 0, 0, 0))
        out_specs += [st_spec, st_spec]
        out_shape += [jax.ShapeDtypeStruct((nb, GLA_HEADS, GLA_DK, GLA_DV), F32)] * 2
    return pl.pallas_call(
        functools.partial(_gla_kernel, n=n, has_state=has_state),
        grid=(nb,),
        in_specs=in_specs,
        out_specs=out_specs,
        out_shape=out_shape,
        scratch_shapes=[
            pltpu.VMEM((n, GLA_W), F32), pltpu.VMEM((n, GLA_W), F32),
            pltpu.VMEM((2, n, GLA_W), F32),
            pltpu.VMEM((2, n, GLA_W), BF16),
            pltpu.VMEM((2 * n // CHUNK, GLA_W, GLA_W), F32),
            pltpu.VMEM((2 * n // CHUNK * 8, GLA_W), F32),
            pltpu.VMEM((2 * n // CHUNK, GLA_W, GLA_W), BF16),
            pltpu.VMEM((GLA_W, GLA_W), F32), pltpu.VMEM((GLA_W, GLA_W), F32),
            pltpu.VMEM((CHUNK, GLA_W), F32), pltpu.VMEM((CHUNK, GLA_W), F32),
        ],
        compiler_params=_cparams(("parallel",)),
        name="gla_state" if has_state else "gla_ctx",
    )(*args)


CONV_PAD = 16
CONV_ROWS = 128
CONV_SHIFT_ROWS = 24


def _conv_kernel(u_ref, w_ref, b_ref, lnw_ref, lnb_ref, o_ref, zp_s, zsh_s, *, n):
    u = u_ref[...]
    zp_s[0:CONV_PAD, :] = jnp.zeros((CONV_PAD, CONV_W), F32)
    zp_s[CONV_PAD + n:2 * CONV_PAD + n, :] = jnp.zeros((CONV_PAD, CONV_W), F32)
    zp_s[CONV_PAD:CONV_PAD + n, :] = u[:, 0:CONV_W] * _sigmoid(u[:, CONV_W:2 * CONV_W])
    m = n + CONV_SHIFT_ROWS
    for s in range(1, 8):
        zsh_s[s - 1] = zp_s[s:s + m, :]
    off = CONV_PAD - CONV_K // 2

    def tap(r0, j):
        s = (off + j) % 8
        base = r0 + off + j - s
        return zp_s[base:base + CONV_ROWS, :] if s == 0 else zsh_s[s - 1, base:base + CONV_ROWS, :]

    for r0 in range(0, n, CONV_ROWS):
        acc = tap(r0, 0) * w_ref[0:1, :]
        for j in range(1, CONV_K):
            acc = acc + tap(r0, j) * w_ref[j:j + 1, :]
        acc = acc + b_ref[...]
        mu = jnp.mean(acc, axis=-1, keepdims=True)
        xc = acc - mu
        var = jnp.mean(xc * xc, axis=-1, keepdims=True)
        y = xc * lax.rsqrt(var + NORM_EPS) * lnw_ref[...] + lnb_ref[...]
        o_ref[r0:r0 + CONV_ROWS, :] = (y * _sigmoid(y)).astype(BF16)


def _conv(p, cw, cb, lnw, lnb, n, nb, row_blk0):
    full = lambda shape: pl.BlockSpec(shape, lambda b: (0,) * len(shape))
    return pl.pallas_call(
        functools.partial(_conv_kernel, n=n),
        grid=(nb,),
        in_specs=[
            pl.BlockSpec((n, 2 * CONV_W), lambda b: (row_blk0 + b, 2)),
            full((32, CONV_W)), full((1, CONV_W)), full((1, CONV_W)), full((1, CONV_W)),
        ],
        out_specs=pl.BlockSpec((n, CONV_W), lambda b: (b, 0)),
        out_shape=jax.ShapeDtypeStruct((nb * n, CONV_W), BF16),
        scratch_shapes=[pltpu.VMEM((n + 2 * CONV_PAD, CONV_W), F32),
                        pltpu.VMEM((7, n + CONV_SHIFT_ROWS, CONV_W), F32)],
        compiler_params=_cparams(("parallel",)),
        name="conv%d" % n,
    )(p, cw, cb, lnw, lnb)


HEAD_BLK = 128


def _kv_expand(ckvn, t, wk_ref, wp_ref, wv_ref, k_ref, v_ref):
    cb = ckvn.astype(BF16)
    th, tl = _split2(t)
    k_ref[...] = (_dot(cb, wk_ref[...]) + _dot(th, wp_ref[...]) + _dot(tl, wp_ref[...])).astype(BF16)
    v_ref[...] = _dot(cb, wv_ref[...]).astype(BF16)


def _mla_prep_kernel(cq_ref, cm_ref, tqc_ref, tqs_ref, tk_ref, qnw_ref, kvw_ref, wa_ref, wb_ref,
                     wk_ref, wp_ref, wv_ref, q_ref, k_ref, v_ref, ckv_ref, kr_ref):
    qn = _rms(cq_ref[...], qnw_ref[...]).astype(BF16)
    cm = cm_ref[...]
    ckvn = _rms(cm[:, 0:MLA_KV_RANK], kvw_ref[...])
    t = cm[:, MLA_KV_RANK:2 * MLA_KV_RANK] * tk_ref[...]
    cb = ckvn.astype(BF16)
    th, tl = _split2(t)
    qa = _dot(qn, wa_ref[...])
    qb = _dot(qn, wb_ref[...])
    kk = _dot(cb, wk_ref[...]) + _dot(th, wp_ref[...]) + _dot(tl, wp_ref[...])
    vv = _dot(cb, wv_ref[...])
    tqc = jnp.concatenate([tqc_ref[...]] * MLA_HEADS, axis=1)
    tqs = jnp.concatenate([tqs_ref[...]] * MLA_HEADS, axis=1)
    scale = (MLA_NOPE + MLA_ROPE) ** -0.5 * float(np.log2(np.e))
    q_ref[...] = ((qa * tqc + qb * tqs) * scale).astype(BF16)
    k_ref[...] = kk.astype(BF16)
    v_ref[...] = vv.astype(BF16)
    ckv_ref[...] = ckvn
    kr_ref[...] = t


def _rope_tile(i):
    nctx = N_CTX // TM
    return jnp.where(i < nctx, 0, 1 + (i - nctx) % (DEC_SEQ // TM))


def _mla_prep(p, tabs, qnw, kvw, wa, wb, wk, wp, wv):
    tqc, tqs, tk = tabs
    full = lambda shape: pl.BlockSpec(shape, lambda i: (0,) * len(shape))
    tab_spec = pl.BlockSpec((TM, HEAD_BLK), lambda i: (_rope_tile(i), 0))
    hw = MLA_HEADS * HEAD_BLK
    return pl.pallas_call(
        _mla_prep_kernel,
        grid=(N_TOK // TM,),
        in_specs=[
            pl.BlockSpec((TM, MLA_Q_RANK), lambda i: (i, 6)),
            pl.BlockSpec((TM, 2 * MLA_KV_RANK), lambda i: (i, 7)),
            tab_spec, tab_spec, tab_spec,
            full((1, MLA_Q_RANK)), full((1, MLA_KV_RANK)),
            full((MLA_Q_RANK, hw)), full((MLA_Q_RANK, hw)),
            full((MLA_KV_RANK, hw)), full((128, hw)), full((MLA_KV_RANK, MLA_HEADS * MLA_V)),
        ],
        out_specs=[
            pl.BlockSpec((TM, hw), lambda i: (i, 0)),
            pl.BlockSpec((TM, hw), lambda i: (i, 0)),
            pl.BlockSpec((TM, MLA_HEADS * MLA_V), lambda i: (i, 0)),
            pl.BlockSpec((TM, MLA_KV_RANK), lambda i: (i, 0)),
            pl.BlockSpec((TM, 128), lambda i: (i, 0)),
        ],
        out_shape=[
            jax.ShapeDtypeStruct((N_TOK, hw), BF16),
            jax.ShapeDtypeStruct((N_TOK, hw), BF16),
            jax.ShapeDtypeStruct((N_TOK, MLA_HEADS * MLA_V), BF16),
            jax.ShapeDtypeStruct((N_TOK, MLA_KV_RANK), F32),
            jax.ShapeDtypeStruct((N_TOK, 128), F32),
        ],
        compiler_params=_cparams(("parallel",)),
        name="mla_prep",
    )(p, p, tqc, tqs, tk, qnw, kvw, wa, wb, wk, wp, wv)


def _cache_kv_kernel(ckv_ref, kr_ref, wk_ref, wp_ref, wv_ref, k_ref, v_ref):
    _kv_expand(ckv_ref[...], kr_ref[...], wk_ref, wp_ref, wv_ref, k_ref, v_ref)


def _cache_kv(ckv, kr, wk, wp, wv):
    n = DEC_BATCH * PAST_LEN
    full = lambda shape: pl.BlockSpec(shape, lambda i: (0,) * len(shape))
    hw = MLA_HEADS * HEAD_BLK
    return pl.pallas_call(
        _cache_kv_kernel,
        grid=(n // PAST_LEN,),
        in_specs=[
            pl.BlockSpec((PAST_LEN, MLA_KV_RANK), lambda i: (i, 0)),
            pl.BlockSpec((PAST_LEN, 128), lambda i: (i, 0)),
            full((MLA_KV_RANK, hw)), full((128, hw)), full((MLA_KV_RANK, MLA_HEADS * MLA_V)),
        ],
        out_specs=[
            pl.BlockSpec((PAST_LEN, hw), lambda i: (i, 0)),
            pl.BlockSpec((PAST_LEN, MLA_HEADS * MLA_V), lambda i: (i, 0)),
        ],
        out_shape=[
            jax.ShapeDtypeStruct((n, hw), BF16),
            jax.ShapeDtypeStruct((n, MLA_HEADS * MLA_V), BF16),
        ],
        compiler_params=_cparams(("parallel",)),
        name="cache_kv",
    )(ckv, kr, wk, wp, wv)


def _attn_kernel(*refs, has_cache):
    if has_cache:
        q_ref, k_ref, v_ref, kc_ref, vc_ref, o_ref = refs
    else:
        q_ref, k_ref, v_ref, o_ref = refs
    nh = q_ref.shape[1] // HEAD_BLK
    cols = [slice(h * HEAD_BLK, (h + 1) * HEAD_BLK) for h in range(nh)]
    vcols = [slice((h // 2) * 2 * MLA_V, (h // 2 + 1) * 2 * MLA_V) for h in range(nh)]
    s = [_dot_nt(q_ref[:, c], k_ref[:, c]) for c in cols]
    m = [jnp.max(x, axis=-1, keepdims=True) for x in s]
    if has_cache:
        sc = [_dot_nt(q_ref[:, c], kc_ref[:, c]) for c in cols]
        m = [jnp.maximum(a, jnp.max(x, axis=-1, keepdims=True)) for a, x in zip(m, sc)]
        pc = [jnp.exp2(x - a) for x, a in zip(sc, m)]
    p = [jnp.exp2(x - a) for x, a in zip(s, m)]

    def with_ones(vals):
        return jnp.concatenate([vals, jnp.ones(vals.shape, BF16)], axis=1)

    o = [_dot(x.astype(BF16), with_ones(v_ref[:, c])) for x, c in zip(p, vcols)]
    if has_cache:
        o = [a + _dot(x.astype(BF16), with_ones(vc_ref[:, c])) for a, x, c in zip(o, pc, vcols)]
    outs = [a[:, 0:2 * MLA_V] / a[:, 2 * MLA_V:4 * MLA_V] for a in o]
    lane = lax.broadcasted_iota(jnp.int32, (1, 2 * MLA_V), 1)
    for j in range(nh // 2):
        o_ref[:, j * 2 * MLA_V:(j + 1) * 2 * MLA_V] = jnp.where(lane < MLA_V, outs[2 * j], outs[2 * j + 1]).astype(BF16)


def _attention(q, k, v, n, nb, row0, tq, npair, kc=None, vc=None):
    has_cache = kc is not None
    nq = n // tq
    rb0 = row0 // tq
    kb0 = row0 // n
    in_specs = [
        pl.BlockSpec((tq, npair * 2 * HEAD_BLK), lambda b, h, i: (rb0 + b * nq + i, h)),
        pl.BlockSpec((n, npair * 2 * HEAD_BLK), lambda b, h, i: (kb0 + b, h)),
        pl.BlockSpec((n, npair * 2 * MLA_V), lambda b, h, i: (kb0 + b, h)),
    ]
    args = [q, k, v]
    if has_cache:
        in_specs += [
            pl.BlockSpec((PAST_LEN, npair * 2 * HEAD_BLK), lambda b, h, i: (b, h)),
            pl.BlockSpec((PAST_LEN, npair * 2 * MLA_V), lambda b, h, i: (b, h)),
        ]
        args += [kc, vc]
    return pl.pallas_call(
        functools.partial(_attn_kernel, has_cache=has_cache),
        grid=(nb, MLA_HEADS // (2 * npair), nq),
        in_specs=in_specs,
        out_specs=pl.BlockSpec((tq, npair * 2 * MLA_V), lambda b, h, i: (b * nq + i, h)),
        out_shape=jax.ShapeDtypeStruct((nb * n, MLA_HEADS * MLA_V), BF16),
        compiler_params=_cparams(("parallel", "parallel", "parallel")),
        name="attn_cache" if has_cache else "attn_ctx",
    )(*args)


def _outproj_kernel(xa_ref, xb_ref, oga_ref, ogb_ref, oca_ref, ocb_ref, oma_ref, omb_ref, mod_ref, nw_ref,
                    wo_ref, wr_ref, br_ref, xo_ref, h_ref, route_ref, cnt_ref, carry_s, wrh_s, wrl_s):
    @pl.when(pl.program_id(0) == 0)
    def _():
        carry_s[...] = jnp.zeros(carry_s.shape, F32)
        wrh, wrl = _split2(wr_ref[...])
        wrh_s[...] = wrh
        wrl_s[...] = wrl

    mod = mod_ref[...]
    groups = [slice(g * (TM // 2), (g + 1) * (TM // 2)) for g in range(2)]

    def residual(x_ref, og_ref, oc_ref, om_ref):
        a1 = [_dot(og_ref[r, :].astype(BF16), wo_ref[0:GLA_W, :]) for r in groups]
        a2 = [_dot(oc_ref[r, :].astype(BF16), wo_ref[GLA_W:GLA_W + CONV_W, :]) for r in groups]
        a3 = [_dot(om_ref[r, :].astype(BF16), wo_ref[GLA_W + CONV_W:D_MODEL, :]) for r in groups]
        for r, u, v, t in zip(groups, a1, a2, a3):
            xo_ref[r, :] = x_ref[r, :] + mod[:, 2 * D_MODEL:3 * D_MODEL] * (u + v + t)

    is_ctx = pl.program_id(0) < N_CTX // TM
    pl.when(is_ctx)(lambda: residual(xa_ref, oga_ref, oca_ref, oma_ref))
    pl.when(jnp.logical_not(is_ctx))(lambda: residual(xb_ref, ogb_ref, ocb_ref, omb_ref))
    hs = [_rms(xo_ref[r, :], nw_ref[...]) * (1.0 + mod[:, 4 * D_MODEL:5 * D_MODEL]) + mod[:, 3 * D_MODEL:4 * D_MODEL]
          for r in groups]
    for g, h in enumerate(hs):
        _rows_to_tiles(h_ref, h, g * (TM // 2))
    parts = [_split2(h) for h in hs]
    l1 = [_dot(hh, wrh_s[...]) for hh, hl in parts]
    l2 = [_dot(hl, wrh_s[...]) for hh, hl in parts]
    l3 = [_dot(hh, wrl_s[...]) for hh, hl in parts]
    logits = jnp.concatenate([u + v + t for u, v, t in zip(l1, l2, l3)], axis=0) + br_ref[...]
    lane = lax.broadcasted_iota(jnp.int32, logits.shape, 1).astype(F32)
    big = 1e9
    gl = jnp.where((lane >= N_EXPERTS) & (lane < N_EXPERTS + N_GROUPS), logits, NEG)
    gmax = jnp.max(gl, axis=-1, keepdims=True)
    gw = 1.0 / jnp.sum(jnp.exp(gl - gmax), axis=-1, keepdims=True)
    gi = jnp.min(jnp.where(gl == gmax, lane, big), axis=-1, keepdims=True) - N_EXPERTS
    lo = gi * EXPERTS_PER_GROUP
    el = jnp.where((lane >= lo) & (lane < lo + EXPERTS_PER_GROUP), logits, NEG)
    m1 = jnp.max(el, axis=-1, keepdims=True)
    i1 = jnp.min(jnp.where(el == m1, lane, big), axis=-1, keepdims=True)
    el2 = jnp.where(lane == i1, NEG, el)
    m2 = jnp.max(el2, axis=-1, keepdims=True)
    i2 = jnp.min(jnp.where(el2 == m2, lane, big), axis=-1, keepdims=True)
    e2 = jnp.exp(m2 - m1)
    p1 = 1.0 / (1.0 + e2)
    onehot = jnp.where((lane == i1) | (lane == i2), 1.0, 0.0)
    tr = lax.broadcasted_iota(jnp.int32, (TM, TM), 0)
    tc = lax.broadcasted_iota(jnp.int32, (TM, TM), 1)
    before = (tc < tr).astype(F32).astype(BF16)
    seen = _dot(before, onehot.astype(BF16)) + carry_s[0:1, :]
    rank1 = jnp.sum(jnp.where(lane == i1, seen, 0.0), axis=-1, keepdims=True)
    rank2 = jnp.sum(jnp.where(lane == i2, seen, 0.0), axis=-1, keepdims=True)
    carry_s[...] = carry_s[...] + jnp.sum(onehot, axis=0, keepdims=True)
    cnt_ref[...] = carry_s[...]
    cols = (i1, i2, gw * p1, gw * (e2 * p1), rank1, rank2)
    route = jnp.zeros(logits.shape, F32)
    for j, col in enumerate(cols):
        route = jnp.where(lane == j, col, route)
    route_ref[...] = route


def _outproj(x, og, oc, om, mods4, l, norm_w, wo, wr, br):
    full = lambda shape: pl.BlockSpec(shape, lambda i: (0,) * len(shape))
    row = lambda w: pl.BlockSpec((TM, w), lambda i: (i, 0))
    return pl.pallas_call(
        _outproj_kernel,
        grid=(N_TOK // TM,),
        in_specs=(_pair_specs(TM, D_MODEL, isinstance(x, tuple)) + _pair_specs(TM, GLA_W, True)
                  + _pair_specs(TM, CONV_W, True) + _pair_specs(TM, MLA_HEADS * MLA_V, True) + [
            pl.BlockSpec((None, None, 1, 6 * D_MODEL), lambda i: (l, _mod_index(i, TM), 0, 0)),
            full((1, D_MODEL)), full((D_MODEL, D_MODEL)), full((D_MODEL, 128)), full((1, 128)),
        ]),
        out_specs=[row(D_MODEL), pl.BlockSpec(_tiled(TM), lambda i: (i, 0)), row(128), full((8, 128))],
        out_shape=[
            jax.ShapeDtypeStruct((N_TOK, D_MODEL), F32),
            jax.ShapeDtypeStruct(_tiled(N_TOK), F32),
            jax.ShapeDtypeStruct((N_TOK, 128), F32),
            jax.ShapeDtypeStruct((8, 128), F32),
        ],
        scratch_shapes=[pltpu.VMEM((8, 128), F32), pltpu.VMEM((D_MODEL, 128), BF16),
                        pltpu.VMEM((D_MODEL, 128), BF16)],
        compiler_params=_cparams(("arbitrary",)),
        name="outproj_router",
    )(*_pair(x), *og, *oc, *om, mods4, norm_w, wo, wr, br)


N_ASSIGN = 2 * N_TOK
TE = 256
N_ETILE = N_ASSIGN // TE
N_WORK = N_ETILE + N_EXPERTS


ROW_SUB = 8


def _tiled(n):
    return (n * ROW_SUB, D_MODEL // ROW_SUB)


def _rows_to_tiles(ref, x, r0=0):
    n = x.shape[0]
    for s in range(ROW_SUB):
        ref[pl.ds(r0 * ROW_SUB + s, n, stride=ROW_SUB), :] = x[:, s * 128:(s + 1) * 128]


def _tiles_to_rows(ref, r0=0, n=None):
    n = ref.shape[0] // ROW_SUB if n is None else n
    return jnp.concatenate([ref[pl.ds(r0 * ROW_SUB + s, n, stride=ROW_SUB), :] for s in range(ROW_SUB)], axis=1)


def _row_tile(ref, t):
    start = t * ROW_SUB if isinstance(t, int) else pl.multiple_of(t * ROW_SUB, ROW_SUB)
    return ref.at[pl.ds(start, ROW_SUB)]


def _lane_col(x, lane, j):
    return jnp.sum(jnp.where(lane == j, x, 0.0), axis=-1, keepdims=True)


def _plan_kernel(cnt_s, cnt_ref, route_ref, pos_ref, wt_ref, we_ref, wlo_ref, whi_ref, wf_ref, wr_ref, wn_ref,
                 off_s):
    cnt = cnt_ref[...]
    chi = jnp.floor(cnt * (1.0 / 128.0))
    clo = cnt - chi * 128.0
    r = lax.broadcasted_iota(jnp.int32, (128, 128), 0)
    c = lax.broadcasted_iota(jnp.int32, (128, 128), 1)
    below = (r < c).astype(F32).astype(BF16)
    off = (128.0 * _dot(chi.astype(BF16), below) + _dot(clo.astype(BF16), below))[0:1, :]
    route = route_ref[...]
    lane = lax.broadcasted_iota(jnp.int32, route.shape, 1).astype(F32)
    pos1 = _lane_col(off, lane, _lane_col(route, lane, 0.0)) + _lane_col(route, lane, 4.0)
    pos2 = _lane_col(off, lane, _lane_col(route, lane, 1.0)) + _lane_col(route, lane, 5.0)
    pos_ref[...] = jnp.where(lane == 0.0, pos1, jnp.where(lane == 1.0, pos2, 0.0))

    @pl.when(pl.program_id(0) == 0)
    def _():
        def offs(e, acc):
            off_s[e] = acc
            return acc + cnt_s[e]

        off_s[N_EXPERTS] = lax.fori_loop(0, N_EXPERTS, offs, jnp.int32(0))

        def item(w, st):
            j, e, first = st
            active = j < N_ETILE
            jj = jnp.minimum(j, N_ETILE - 1)
            ee = jnp.minimum(e, N_EXPERTS - 1)
            tlo = jj * TE
            thi = tlo + TE
            end = off_s[ee + 1]
            wt_ref[w] = jj
            we_ref[w] = ee
            wlo_ref[w] = jnp.where(active, jnp.maximum(off_s[ee], tlo) - tlo, 0)
            whi_ref[w] = jnp.where(active, jnp.minimum(end, thi) - tlo, 0)
            wf_ref[w] = jnp.where(active, first, 0)
            adv_j = (active & (end >= thi)).astype(jnp.int32)
            adv_e = (active & (end <= thi)).astype(jnp.int32)
            return j + adv_j, e + adv_e, adv_j

        lax.fori_loop(0, N_WORK, item, (jnp.int32(0), jnp.int32(0), jnp.int32(1)))

        def run(w, r):
            r = r + (we_ref[w] != we_ref[jnp.maximum(w - 1, 0)]).astype(jnp.int32)
            wr_ref[w] = r
            return r

        lax.fori_loop(0, N_WORK, run, jnp.int32(0))

        def following(j, nxt):
            w = N_WORK - 1 - j
            wn_ref[w] = nxt
            return jnp.where(we_ref[jnp.maximum(w - 1, 0)] != we_ref[w], we_ref[w], nxt)

        lax.fori_loop(0, N_WORK, following, jnp.int32(-1))


def _plan(cnt_i, cnt, route):
    smem = pl.BlockSpec(memory_space=pltpu.SMEM)
    work = jax.ShapeDtypeStruct((N_WORK,), jnp.int32)
    return pl.pallas_call(
        _plan_kernel,
        grid=(1,),
        in_specs=[smem, pl.BlockSpec((8, 128), lambda i: (0, 0)), pl.BlockSpec((N_TOK, 128), lambda i: (0, 0))],
        out_specs=[pl.BlockSpec((N_TOK, 128), lambda i: (0, 0))] + [smem] * 7,
        out_shape=[jax.ShapeDtypeStruct((N_TOK, 128), F32)] + [work] * 7,
        scratch_shapes=[pltpu.SMEM((N_EXPERTS + 1,), jnp.int32)],
        compiler_params=_cparams(("arbitrary",)),
        name="moe_plan",
    )(cnt_i, cnt, route)


def _dispatch_kernel(pos_s, h_ref, xs_hbm, sem):
    base = pl.program_id(0) * TM

    def copy(i, k):
        return pltpu.make_async_copy(_row_tile(h_ref, i), _row_tile(xs_hbm, pos_s[2 * (base + i) + k]), sem)

    def wait(i, carry):
        copy(i, 0).wait()
        copy(i, 1).wait()
        return carry

    for i in range(TM):
        copy(i, 0).start(priority=0)
        copy(i, 1).start(priority=1)
    lax.fori_loop(0, TM, wait, 0, unroll=8)


def _dispatch(pos, h):
    any_spec = pl.BlockSpec(memory_space=pl.ANY)
    return pl.pallas_call(
        _dispatch_kernel,
        grid_spec=pltpu.PrefetchScalarGridSpec(
            num_scalar_prefetch=1, grid=(N_TOK // TM,),
            in_specs=[pl.BlockSpec(_tiled(TM), lambda i, p: (i, 0))], out_specs=any_spec,
            scratch_shapes=[pltpu.SemaphoreType.DMA(())]),
        out_shape=jax.ShapeDtypeStruct(_tiled(N_ASSIGN), F32),
        compiler_params=_cparams(("arbitrary",)),
        name="moe_dispatch",
    )(pos, h)


def _experts_kernel(wt_s, we_s, wlo_s, whi_s, wf_s, wr_s, wn_s, xs_ref, w1_hbm, w3_hbm, w2_hbm, o_ref,
                    w1_s, w3_s, w2_s, f1_s, f3_s, f2_s, sem, *, layer):
    w = pl.program_id(0)
    first = wf_s[w] == 1
    nonempty = whi_s[w] > wlo_s[w]

    def fetch(e, slot):
        return [pltpu.make_async_copy(src.at[layer, e], dst.at[slot], sem.at[slot])
                for src, dst in ((w1_hbm, f1_s), (w3_hbm, f3_s), (w2_hbm, f2_s))]

    @pl.when(w == 0)
    def _():
        for c in fetch(we_s[0], wr_s[0] % 2):
            c.start()

    @pl.when((w == 0) | (we_s[w] != we_s[jnp.maximum(w - 1, 0)]))
    def _():
        slot = wr_s[w] % 2
        for c in fetch(we_s[w], slot):
            c.wait()
        w1_s[...] = f1_s[slot].astype(BF16)
        w3_s[...] = f3_s[slot].astype(BF16)
        w2_s[...] = f2_s[slot].astype(BF16)

        @pl.when(wn_s[w] >= 0)
        def _():
            for c in fetch(wn_s[w], 1 - slot):
                c.start()

    @pl.when(first & jnp.logical_not(nonempty))
    def _():
        o_ref[...] = jnp.zeros(o_ref.shape, F32)

    nsplit = 2
    hrows = TE // nsplit

    def expert_rows(groups, merge):
        x = [_tiles_to_rows(xs_ref, j * hrows, hrows).astype(BF16) for j in groups]
        a = [_dot(v, w1_s[...]) for v in x]
        b = [_dot(v, w3_s[...]) for v in x]
        hid = [(u * _sigmoid(u) * v).astype(BF16) for u, v in zip(a, b)]
        y = [_dot(v, w2_s[...]) for v in hid]
        for j, yj in zip(groups, y):
            row = lax.broadcasted_iota(jnp.int32, (hrows, 1), 0) + j * hrows
            mine = (row >= wlo_s[w]) & (row < whi_s[w])
            other = _tiles_to_rows(o_ref, j * hrows, hrows) if merge else 0.0
            _rows_to_tiles(o_ref, jnp.where(mine, yj, other), j * hrows)
        if not merge:
            for j in set(range(nsplit)) - set(groups):
                _rows_to_tiles(o_ref, jnp.zeros((hrows, D_MODEL), F32), j * hrows)

    need = [(wlo_s[w] < (j + 1) * hrows) & (whi_s[w] > j * hrows) for j in range(nsplit)]
    cases = (((0, 1), need[0] & need[1]), ((0,), need[0] & jnp.logical_not(need[1])),
             ((1,), need[1] & jnp.logical_not(need[0])))
    for groups, cond in cases:
        pl.when(nonempty & cond & first)(functools.partial(expert_rows, groups, False))
        pl.when(nonempty & cond & jnp.logical_not(first))(functools.partial(expert_rows, groups, True))


def _experts(work, xs, l, w1, w3, w2):
    tile = pl.BlockSpec(_tiled(TE), lambda w, wt, *_: (wt[w], 0))
    any_spec = pl.BlockSpec(memory_space=pl.ANY)
    return pl.pallas_call(
        functools.partial(_experts_kernel, layer=l),
        grid_spec=pltpu.PrefetchScalarGridSpec(
            num_scalar_prefetch=7, grid=(N_WORK,),
            in_specs=[tile, any_spec, any_spec, any_spec],
            out_specs=tile,
            scratch_shapes=[pltpu.VMEM((D_MODEL, EXPERT_FF), BF16), pltpu.VMEM((D_MODEL, EXPERT_FF), BF16),
                            pltpu.VMEM((EXPERT_FF, D_MODEL), BF16),
                            pltpu.VMEM((2, D_MODEL, EXPERT_FF), F32), pltpu.VMEM((2, D_MODEL, EXPERT_FF), F32),
                            pltpu.VMEM((2, EXPERT_FF, D_MODEL), F32), pltpu.SemaphoreType.DMA((2,))]),
        out_shape=jax.ShapeDtypeStruct(_tiled(N_ASSIGN), F32),
        compiler_params=_cparams(("arbitrary",)),
        name="moe_experts",
    )(*work, xs, w1, w3, w2)


def _moe_residual(pos_s, x_ref, route_ref, mod_ref, ys_hbm, y_s, sem):
    step = pl.program_id(0)
    nsteps = pl.num_programs(0)

    def copy(t, i, k):
        slot = t % 2
        src = _row_tile(ys_hbm, pos_s[2 * (t * TM + i) + k])
        return pltpu.make_async_copy(src, _row_tile(y_s.at[slot, k], i), sem.at[slot])

    def gather(t):
        for i in range(TM):
            copy(t, i, 0).start(priority=0)
            copy(t, i, 1).start(priority=1)

    def wait(i, carry):
        copy(step, i, 0).wait()
        copy(step, i, 1).wait()
        return carry

    pl.when(step == 0)(lambda: gather(step))
    pl.when(step + 1 < nsteps)(lambda: gather(step + 1))
    lax.fori_loop(0, TM, wait, 0, unroll=8)
    route = route_ref[...]
    lane = lax.broadcasted_iota(jnp.int32, route.shape, 1)
    slot = step % 2
    moe = (_lane_col(route, lane, 2) * _tiles_to_rows(y_s.at[slot, 0])
           + _lane_col(route, lane, 3) * _tiles_to_rows(y_s.at[slot, 1]))
    return x_ref[...] + mod_ref[:, 5 * D_MODEL:6 * D_MODEL] * moe


def _combine_inproj_kernel(pos_s, x_ref, route_ref, modp_ref, ys_hbm, mod_ref, nw_ref, w_ref,
                           xo_ref, p_ref, y_s, sem):
    x = _moe_residual(pos_s, x_ref, route_ref, modp_ref, ys_hbm, y_s, sem)
    xo_ref[...] = x
    mod = mod_ref[...]
    h = _rms(x, nw_ref[...]) * (1.0 + mod[:, D_MODEL:2 * D_MODEL]) + mod[:, 0:D_MODEL]
    p_ref[...] = _dot(h.astype(BF16), w_ref[...])


def _combine_inproj(pos, x, route, mods4, l, ys, norm_w, w_in_r):
    row = lambda w: pl.BlockSpec((TM, w), lambda i, p: (i, 0))
    mod = lambda layer: pl.BlockSpec((None, None, 1, 6 * D_MODEL), lambda i, p: (layer, _mod_index(i, TM), 0, 0))
    return pl.pallas_call(
        _combine_inproj_kernel,
        grid_spec=pltpu.PrefetchScalarGridSpec(
            num_scalar_prefetch=1, grid=(N_TOK // TM,),
            in_specs=[row(D_MODEL), row(128), mod(l - 1), pl.BlockSpec(memory_space=pl.ANY), mod(l),
                      pl.BlockSpec((1, D_MODEL), lambda i, p: (0, 0)),
                      pl.BlockSpec((None, D_MODEL, P_COLS), lambda i, p: (l, 0, 0))],
            out_specs=[row(D_MODEL), row(P_COLS)],
            scratch_shapes=[pltpu.VMEM((2, 2) + _tiled(TM), F32), pltpu.SemaphoreType.DMA((2,))]),
        out_shape=[jax.ShapeDtypeStruct((N_TOK, D_MODEL), F32), jax.ShapeDtypeStruct((N_TOK, P_COLS), F32)],
        compiler_params=_cparams(("arbitrary",)),
        name="moe_combine_inproj",
    )(pos, x, route, mods4, ys, mods4, norm_w, w_in_r)


def _combine_kernel(pos_s, x_ref, route_ref, mod_ref, fw_ref, ys_hbm, *refs, final):
    if final:
        oa_ref, ob_ref, y_s, sem = refs
    else:
        o_ref, y_s, sem = refs
    x = _moe_residual(pos_s, x_ref, route_ref, mod_ref, ys_hbm, y_s, sem)
    if final:
        y = _rms(x, fw_ref[...])
        is_ctx = pl.program_id(0) < N_CTX // TM

        @pl.when(is_ctx)
        def _():
            oa_ref[...] = y

        @pl.when(jnp.logical_not(is_ctx))
        def _():
            ob_ref[...] = y
    else:
        o_ref[...] = x


def _combine(pos, x, route, mods4, l, fw, ys, final):
    row = lambda w: pl.BlockSpec((TM, w), lambda i, p: (i, 0))
    if final:
        out_specs = _pair_specs(TM, D_MODEL, True)
        out_shape = [jax.ShapeDtypeStruct((N_CTX, D_MODEL), F32), jax.ShapeDtypeStruct((N_SMP, D_MODEL), F32)]
    else:
        out_specs = row(D_MODEL)
        out_shape = jax.ShapeDtypeStruct((N_TOK, D_MODEL), F32)
    return pl.pallas_call(
        functools.partial(_combine_kernel, final=final),
        grid_spec=pltpu.PrefetchScalarGridSpec(
            num_scalar_prefetch=1, grid=(N_TOK // TM,),
            in_specs=[row(D_MODEL), row(128),
                      pl.BlockSpec((None, None, 1, 6 * D_MODEL), lambda i, p: (l, _mod_index(i, TM), 0, 0)),
                      pl.BlockSpec((1, D_MODEL), lambda i, p: (0, 0)),
                      pl.BlockSpec(memory_space=pl.ANY)],
            out_specs=out_specs,
            scratch_shapes=[pltpu.VMEM((2, 2) + _tiled(TM), F32), pltpu.SemaphoreType.DMA((2,))]),
        out_shape=out_shape,
        compiler_params=_cparams(("arbitrary",)),
        name="moe_combine_final" if final else "moe_combine",
    )(pos, x, route, mods4, fw, ys)


def _rope_swap(w):
    nf = MLA_ROPE // 4
    g = w.reshape(w.shape[:-1] + (2, 2, nf))
    return jnp.stack([-g[..., 1, :], g[..., 0, :]], axis=-2).reshape(w.shape)


def _rope_tables():
    nf = MLA_ROPE // 4
    pos = np.arange(DEC_SEQ)
    row = (pos // GRID_W).astype(np.float32)
    col = (pos % GRID_W).astype(np.float32)
    inv = (np.float32(ROPE_THETA) ** (-np.arange(nf, dtype=np.float32) / np.float32(nf))).astype(np.float32)
    ar = (row[:, None] * inv).astype(np.float32)
    ac = (col[:, None] * inv).astype(np.float32)
    cos = np.concatenate([np.cos(ar), np.cos(ar), np.cos(ac), np.cos(ac)], axis=1)
    sin = np.concatenate([np.sin(ar), np.sin(ar), np.sin(ac), np.sin(ac)], axis=1)
    cos = np.concatenate([np.ones((TM, MLA_ROPE)), cos], axis=0).astype(np.float32)
    sin = np.concatenate([np.zeros((TM, MLA_ROPE)), sin], axis=0).astype(np.float32)
    n = cos.shape[0]
    z32 = np.zeros((n, 32), np.float32)
    tqc = np.concatenate([np.ones((n, MLA_NOPE), np.float32), cos, z32], axis=1)
    tqs = np.concatenate([np.zeros((n, MLA_NOPE), np.float32), sin, z32], axis=1)
    tk = np.concatenate([z32, cos, sin, z32], axis=1)
    return jnp.asarray(tqc), jnp.asarray(tqs), jnp.asarray(tk)


W_IN_COLS = 1984
W_IN_ROWS = 256


def _w_in_relayout_kernel(w_ref, sh_ref, s1_ref, s2_ref, o_ref):
    gate_cols = 4 * GLA_W
    o_ref[:, 0:gate_cols] = w_ref[:, 0:gate_cols].astype(BF16)
    for j in range(gate_cols // 128, MISC_BLK):
        hi = min(128 * j + 256, W_IN_COLS)
        src = w_ref[:, 128 * j:hi].astype(BF16)
        o_ref[:, 128 * j:128 * (j + 1)] = _dot(src, sh_ref[0:hi - 128 * j, :]).astype(BF16)
    head = w_ref[:, gate_cols:gate_cols + 128].astype(BF16)
    tail = w_ref[:, 128 * MISC_BLK:W_IN_COLS].astype(BF16)
    o_ref[:, 128 * MISC_BLK:P_COLS] = (_dot(head, s1_ref[...]) + _dot(tail, s2_ref[...])).astype(BF16)


def _w_in_relayout(w_in):
    sh = np.zeros((256, 128), np.float32)
    sh[32 + np.arange(128), np.arange(128)] = 1.0
    s1 = np.zeros((128, 128), np.float32)
    s1[np.arange(32), np.arange(32)] = 1.0
    s2 = np.zeros((W_IN_COLS - 128 * MISC_BLK, 128), np.float32)
    nf = MLA_ROPE // 4
    for j in range(MLA_ROPE):
        s2[32 + j, 32 + j] = 1.0
        half = (j // nf) % 2
        src = j + nf if half == 0 else j - nf
        s2[32 + src, 64 + j] = -1.0 if half == 0 else 1.0
    consts = [jnp.asarray(a).astype(BF16) for a in (sh, s1, s2)]
    full = lambda a: pl.BlockSpec(a.shape, lambda l, i: (0, 0))
    return pl.pallas_call(
        _w_in_relayout_kernel,
        grid=(DEPTH, D_MODEL // W_IN_ROWS),
        in_specs=[pl.BlockSpec((None, W_IN_ROWS, W_IN_COLS), lambda l, i: (l, i, 0))] + [full(a) for a in consts],
        out_specs=pl.BlockSpec((None, W_IN_ROWS, P_COLS), lambda l, i: (l, i, 0)),
        out_shape=jax.ShapeDtypeStruct((DEPTH, D_MODEL, P_COLS), BF16),
        compiler_params=_cparams(("parallel", "parallel")),
        name="w_in_relayout",
    )(w_in, *consts)


def _prep_layer(l, gla_wg_f, gla_wg_b, mla_wuq, mla_wukv, w_out, moe_wg, moe_bg, moe_we, moe_be):
    wgf = jnp.zeros((128, GLA_W), F32).at[0:16].set(gla_wg_f[l])
    wgb = jnp.zeros((128, GLA_W), F32).at[16:32].set(gla_wg_b[l])
    wq = mla_wuq[l].reshape(MLA_Q_RANK, MLA_HEADS, MLA_NOPE + MLA_ROPE)
    nope, rope = wq[..., :MLA_NOPE], wq[..., MLA_NOPE:]
    z = lambda n: jnp.zeros((MLA_Q_RANK, MLA_HEADS, n), F32)
    hw = MLA_HEADS * HEAD_BLK
    wa = jnp.concatenate([nope, rope, z(32)], axis=-1).reshape(MLA_Q_RANK, hw).astype(BF16)
    wb = jnp.concatenate([z(MLA_NOPE), _rope_swap(rope), z(32)], axis=-1).reshape(MLA_Q_RANK, hw).astype(BF16)
    wkv = mla_wukv[l].reshape(MLA_KV_RANK, MLA_HEADS, MLA_NOPE + MLA_V)
    wk = jnp.concatenate([wkv[..., :MLA_NOPE], jnp.zeros((MLA_KV_RANK, MLA_HEADS, 64), F32)],
                         axis=-1).reshape(MLA_KV_RANK, hw).astype(BF16)
    wv = wkv[..., MLA_NOPE:].reshape(MLA_KV_RANK, MLA_HEADS * MLA_V).astype(BF16)
    wp_np = np.zeros((128, MLA_HEADS, HEAD_BLK), np.float32)
    for j in range(MLA_ROPE):
        wp_np[32 + j, :, MLA_NOPE + j] = 1.0
        wp_np[64 + j, :, MLA_NOPE + j] = 1.0
    wp = jnp.asarray(wp_np.reshape(128, hw)).astype(BF16)
    wo = w_out[l].astype(BF16)
    wr = jnp.concatenate([moe_we[l], moe_wg[l], jnp.zeros((D_MODEL, 128 - N_EXPERTS - N_GROUPS), F32)], axis=1)
    br = jnp.concatenate([moe_be[l], moe_bg[l], jnp.zeros((128 - N_EXPERTS - N_GROUPS,), F32)]).reshape(1, 128)
    return dict(wgf=wgf, wgb=wgb, wa=wa, wb=wb, wk=wk, wv=wv, wp=wp, wo=wo, wr=wr, br=br)


def kernel(x_prompt, x_sample, cache_ckv, cache_krope, state_gla_fwd, state_gla_bwd, c, c_ctx, ada_w, ada_b,
           norm1_w, norm2_w, w_in, gla_wg_f, gla_bg_f, gla_wg_b, gla_bg_b, gla_norm_w, conv_w, conv_b,
           conv_ln_w, conv_ln_b, mla_qnorm_w, mla_wuq, mla_kvnorm_w, mla_wukv, w_out, moe_wg, moe_bg, moe_we,
           moe_be, moe_w1, moe_w3, moe_w2, final_norm_w):
    x = (x_prompt.reshape(N_CTX, D_MODEL), x_sample.reshape(N_SMP, D_MODEL))
    cond8 = jnp.concatenate([c_ctx[None, :], c, jnp.zeros((N_MOD - 1 - DEC_BATCH, D_MODEL), F32)], axis=0)
    mods4 = _ada_table(cond8, ada_w, ada_b).reshape(DEPTH, N_MOD, 1, 6 * D_MODEL)
    tabs = _rope_tables()
    w_in_r = _w_in_relayout(w_in)
    row1 = lambda a: a.reshape(1, -1)

    ckv_new, kr_new, sf_new, sb_new = [], [], [], []
    for l in range(DEPTH):
        w = _prep_layer(l, gla_wg_f, gla_wg_b, mla_wuq, mla_wukv, w_out, moe_wg, moe_bg, moe_we, moe_be)
        if l == 0:
            p = _inproj(x, mods4, l, row1(norm1_w[l]), w_in_r)
        else:
            x, p = _combine_inproj(pos, xn, route, mods4, l, ys, row1(norm1_w[l]), w_in_r)

        gla_args = (p, w["wgf"], w["wgb"], row1(gla_bg_f[l]), row1(gla_bg_b[l]), row1(gla_norm_w[l]))
        og_c, sf, sb = _gla(*gla_args, SEQ, BATCH, 0)
        og_s = _gla(*gla_args, DEC_SEQ, DEC_BATCH, N_CTX // DEC_SEQ, (state_gla_fwd, l), (state_gla_bwd, l))[0]
        og = (og_c, og_s)

        cw = jnp.concatenate([conv_w[l], jnp.zeros((1, CONV_W), F32)], axis=0)
        conv_args = (p, cw, row1(conv_b[l]), row1(conv_ln_w[l]), row1(conv_ln_b[l]))
        oc = (_conv(*conv_args, SEQ, BATCH, 0), _conv(*conv_args, DEC_SEQ, DEC_BATCH, N_CTX // DEC_SEQ))

        q, k, v, ckvn, kr = _mla_prep(p, tabs, row1(mla_qnorm_w[l]), row1(mla_kvnorm_w[l]),
                                      w["wa"], w["wb"], w["wk"], w["wp"], w["wv"])
        cc = cache_ckv[:, l].reshape(DEC_BATCH * PAST_LEN, MLA_KV_RANK)
        ckr = cache_krope[:, l].reshape(DEC_BATCH * PAST_LEN, MLA_ROPE)
        ckr = jnp.pad(ckr, ((0, 0), (32, 64)))
        kc, vc = _cache_kv(cc, ckr, w["wk"], w["wp"], w["wv"])
        om = (_attention(q, k, v, SEQ, BATCH, 0, SEQ, MLA_HEADS // 2),
              _attention(q, k, v, DEC_SEQ, DEC_BATCH, N_CTX, DEC_SEQ, 1, kc, vc))

        xn, h2, route, cnt = _outproj(x, og, oc, om, mods4, l, row1(norm2_w[l]), w["wo"], w["wr"], w["br"])
        posf, *work = _plan(cnt[0, :N_EXPERTS].astype(jnp.int32), cnt, route)
        pos = posf[:, 0:2].astype(jnp.int32).reshape(N_ASSIGN)
        ys = _experts(work, _dispatch(pos, h2), l, moe_w1, moe_w3, moe_w2)
        if l == DEPTH - 1:
            x = _combine(pos, xn, route, mods4, l, row1(final_norm_w), ys, True)

        ckv_new.append(ckvn[:N_CTX].reshape(BATCH, SEQ, MLA_KV_RANK))
        kr_new.append(kr[:N_CTX, 32:32 + MLA_ROPE].reshape(BATCH, SEQ, MLA_ROPE))
        sf_new.append(sf)
        sb_new.append(sb)

    y_ctx, y_smp = x
    return (y_ctx.reshape(BATCH, SEQ, D_MODEL), y_smp.reshape(DEC_BATCH, DEC_SEQ, D_MODEL),
            jnp.stack(ckv_new, axis=1), jnp.stack(kr_new, axis=1),
            jnp.stack(sf_new, axis=1), jnp.stack(sb_new, axis=1))
```
